```python
import jax, jax.numpy as jnp
from jax import lax
import numpy as np

D_MODEL = 1024
BATCH = 2
SEQ = 16384
DEPTH = 2

HEAD_DIM = 64
ATTN_GROUPS = ((128, 1), (512, 4), (2048, 16))
N_ATTN_GROUPS = 3
HEADS_PER_GROUP = 8
ATTN_WIDTH = N_ATTN_GROUPS * HEADS_PER_GROUP * HEAD_DIM
ATTN_OUT = HEADS_PER_GROUP * HEAD_DIM
BLK = 128
REL_BUCKETS = 32
REL_MAX_DIST = 2048
NEG_INF = -1e30
SSM_CH = 16
SSM_WIDTH = D_MODEL // 2
SSM_GROUPS = SSM_WIDTH // SSM_CH
SSM_STATE = 64
PROJ_WIDTH = 3 * ATTN_WIDTH + SSM_WIDTH + 2 * D_MODEL
D_FF = 11 * D_MODEL // 4
N_EXPERTS = 8
TOP_K = 2
D_FF_EXPERT = 7 * D_MODEL // 2
MOE_BLOCK = 512
N_DENSE = (DEPTH + 1) // 2
N_MOE = DEPTH // 2
EPS = 1e-6

kernel_name = "hybrid_dilated_attn_s5_moe_trunk"


def rmsnorm(x, g):
    x32 = x.astype(jnp.float32)
    y = x32 * lax.rsqrt(jnp.mean(x32 * x32, axis=-1, keepdims=True) + EPS)
    return (y * g.astype(jnp.float32)).astype(x.dtype)


def _t5_bucket(dist):
    max_exact = REL_BUCKETS // 2
    d = np.maximum(dist, 1).astype(np.float64)
    large = max_exact + (np.log(d / max_exact) / np.log(REL_MAX_DIST / max_exact)
                         * (REL_BUCKETS - max_exact)).astype(np.int32)
    large = np.minimum(large, REL_BUCKETS - 1)
    return np.where(dist < max_exact, dist, large).astype(np.int32)


def _dilated_group(q, k, v, table, window, dilation):
    B, S, H, Dh = q.shape
    r = dilation
    steps = window // dilation
    L = S // r
    nb = -(-L // BLK)
    Lp = nb * BLK

    def to_blocks(t):
        t = t.reshape(B, L, r, H, Dh).transpose(0, 2, 1, 3, 4)
        t = jnp.pad(t, ((0, 0), (0, 0), (0, Lp - L), (0, 0), (0, 0)))
        return t.reshape(B, r, nb, BLK, H, Dh)

    qb, kb, vb = to_blocks(q), to_blocks(k), to_blocks(v)

    def with_prev(t):
        prev = jnp.pad(t[:, :, :-1], ((0, 0), (0, 0), (1, 0), (0, 0), (0, 0), (0, 0)))
        return jnp.concatenate([prev, t], axis=3)

    kw, vw = with_prev(kb), with_prev(vb)

    qi = np.arange(BLK)[:, None]
    kj = np.arange(2 * BLK)[None, :]
    delta = BLK + qi - kj
    band = (delta >= 0) & (delta <= steps)
    first = (np.arange(nb) == 0)[:, None, None] & (kj < BLK)[None]
    valid = band[None] & ~first
    bucket = _t5_bucket(np.clip(delta, 0, steps) * r)
    bias = jnp.transpose(table[bucket], (2, 0, 1)).astype(jnp.float32)

    logits = jnp.einsum('bcnqhd,bcnkhd->bcnhqk', qb, kw).astype(jnp.float32) * (HEAD_DIM ** -0.5) + bias
    logits = jnp.where(valid[None, None, :, None], logits, NEG_INF)
    m = jnp.max(logits, axis=-1, keepdims=True)
    p = jnp.exp(logits - m)
    s = jnp.sum(p, axis=-1, keepdims=True)
    o = jnp.einsum('bcnhqk,bcnkhd->bcnqhd', (p / s).astype(v.dtype), vw)
    lse = (m + jnp.log(s))[..., 0]

    o = o.reshape(B, r, Lp, H, Dh)[:, :, :L].transpose(0, 2, 1, 3, 4).reshape(B, S, H, Dh)
    lse = lse.transpose(0, 1, 2, 4, 3).reshape(B, r, Lp, H)[:, :, :L].transpose(0, 2, 1, 3).reshape(B, S, H)
    return o, lse


def dilated_mixture(q, k, v, rel_bias):
    B, S, _ = q.shape
    shp = (B, S, N_ATTN_GROUPS, HEADS_PER_GROUP, HEAD_DIM)
    q, k, v = q.reshape(shp), k.reshape(shp), v.reshape(shp)
    outs, lses = [], []
    for g, (window, dilation) in enumerate(ATTN_GROUPS):
        table = rel_bias[:, g * HEADS_PER_GROUP:(g + 1) * HEADS_PER_GROUP]
        o, l = _dilated_group(q[:, :, g], k[:, :, g], v[:, :, g], table, window, dilation)
        outs.append(o)
        lses.append(l)
    alpha = jax.nn.softmax(jnp.stack(lses), axis=0)
    o = jnp.einsum('gbsh,gbshd->bshd', alpha, jnp.stack(outs).astype(jnp.float32))
    return o.reshape(B, S, ATTN_OUT).astype(q.dtype)


def s5_layer(u, lam_re, lam_im, log_dt, b_re, b_im, c_re, c_im, d_skip, w_glu, b_glu):
    Bsz, S, _ = u.shape
    f32 = jnp.float32
    ug = u.astype(f32).reshape(Bsz, S, SSM_GROUPS, SSM_CH)
    lam = lax.complex(lam_re.astype(f32), lam_im.astype(f32))
    dt = jnp.exp(log_dt.astype(f32))[:, None]
    lam_bar = jnp.exp(lam * dt)
    b = lax.complex(b_re.astype(f32), b_im.astype(f32))
    c = lax.complex(c_re.astype(f32), c_im.astype(f32))
    b_bar = ((lam_bar - 1.0) / lam)[..., None] * b
    bu = jnp.einsum('gpc,bsgc->bsgp', b_bar, ug.astype(jnp.complex64))
    a = jnp.broadcast_to(lam_bar, bu.shape)

    def combine(e1, e2):
        a1, x1 = e1
        a2, x2 = e2
        return a2 * a1, a2 * x1 + x2

    _, states = lax.associative_scan(combine, (a, bu), axis=1)
    y = jnp.einsum('gcp,bsgp->bsgc', c, states).real.reshape(Bsz, S, SSM_WIDTH)
    y = y + d_skip.astype(f32) * u.astype(f32)
    z = jax.nn.gelu(y).astype(u.dtype)
    return z * jax.nn.sigmoid(z @ w_glu + b_glu)


def hybrid_mixer(h, rel_bias, w_in, lam_re, lam_im, log_dt, b_re, b_im, c_re, c_im,
                 d_skip, w_glu, b_glu, w_attn_br, w_ssm_br, w_out):
    proj = h @ w_in
    cuts = np.cumsum([ATTN_WIDTH, ATTN_WIDTH, ATTN_WIDTH, SSM_WIDTH, D_MODEL])
    q, k, v, u, g_attn, g_ssm = jnp.split(proj, cuts, axis=-1)
    y_attn = dilated_mixture(q, k, v, rel_bias) @ w_attn_br
    y_ssm = s5_layer(u, lam_re, lam_im, log_dt, b_re, b_im, c_re, c_im, d_skip, w_glu, b_glu) @ w_ssm_br
    merged = jax.nn.sigmoid(g_attn) * y_attn + jax.nn.sigmoid(g_ssm) * y_ssm
    return merged @ w_out


def swiglu(h, w_gate, w_up, w_down):
    return (jax.nn.silu(h @ w_gate) * (h @ w_up)) @ w_down


def moe_swiglu(h, w_router, w_gate, w_up, w_down):
    N, D = h.shape
    logits = (h @ w_router).astype(jnp.float32)
    top_val, top_idx = lax.top_k(logits, TOP_K)
    gates = jax.nn.softmax(top_val, axis=-1)
    A = N * TOP_K
    flat_e = top_idx.reshape(-1)
    flat_tok = jnp.arange(A, dtype=jnp.int32) // TOP_K
    flat_g = gates.reshape(-1)
    order = jnp.argsort(flat_e)
    sorted_e = flat_e[order]
    counts = jnp.bincount(flat_e, length=N_EXPERTS)
    padded = ((counts + MOE_BLOCK - 1) // MOE_BLOCK) * MOE_BLOCK
    pad_end = jnp.cumsum(padded)
    pad_start = pad_end - padded
    start = jnp.cumsum(counts) - counts
    slot = pad_start[sorted_e] + (jnp.arange(A) - start[sorted_e])
    n_blocks = -(-A // MOE_BLOCK) + N_EXPERTS
    n_slots = n_blocks * MOE_BLOCK
    slot_tok = jnp.full((n_slots,), N, jnp.int32).at[slot].set(flat_tok[order])
    slot_gate = jnp.zeros((n_slots,), jnp.float32).at[slot].set(flat_g[order])
    block_e = jnp.minimum(jnp.searchsorted(pad_end, jnp.arange(n_blocks) * MOE_BLOCK, side='right'),
                          N_EXPERTS - 1)
    h_pad = jnp.concatenate([h, jnp.zeros((1, D), h.dtype)], axis=0)
    xs = h_pad[slot_tok].reshape(n_blocks, MOE_BLOCK, D)

    def block_fn(args):
        xb, e = args
        return (jax.nn.silu(xb @ w_gate[e]) * (xb @ w_up[e])) @ w_down[e]

    ys = lax.map(block_fn, (xs, block_e)).reshape(n_slots, D)
    ys = ys.astype(jnp.float32) * slot_gate[:, None]
    out = jnp.zeros((N + 1, D), jnp.float32).at[slot_tok].add(ys)[:N]
    return out.astype(h.dtype)


def setup_inputs(seed: int = 0) -> dict:
    key = jax.random.key(seed)
    ks = iter(jax.random.split(key, 32))
    nrm = lambda shape, scale: jax.random.normal(next(ks), shape, jnp.float32) * scale
    G, P, Hc = SSM_GROUPS, SSM_STATE, SSM_CH
    n_idx = jnp.arange(P, dtype=jnp.float32)
    return {
        "x": nrm((BATCH, SEQ, D_MODEL), 1.0),
        "rel_bias": nrm((REL_BUCKETS, N_ATTN_GROUPS * HEADS_PER_GROUP), 0.1),
        "norm1_g": 1.0 + nrm((DEPTH, D_MODEL), 0.02),
        "w_in": nrm((DEPTH, D_MODEL, PROJ_WIDTH), D_MODEL ** -0.5),
        "ssm_lam_re": -0.5 + nrm((DEPTH, G, P), 0.01),
        "ssm_lam_im": jnp.pi * n_idx + nrm((DEPTH, G, P), 0.01),
        "ssm_log_dt": jax.random.uniform(next(ks), (DEPTH, G), jnp.float32,
                                         float(np.log(1e-3)), float(np.log(1e-1))),
        "ssm_b_re": nrm((DEPTH, G, P, Hc), (2 * Hc) ** -0.5),
        "ssm_b_im": nrm((DEPTH, G, P, Hc), (2 * Hc) ** -0.5),
        "ssm_c_re": nrm((DEPTH, G, Hc, P), (2 * P) ** -0.5),
        "ssm_c_im": nrm((DEPTH, G, Hc, P), (2 * P) ** -0.5),
        "ssm_d": nrm((DEPTH, SSM_WIDTH), 1.0),
        "w_glu": nrm((DEPTH, SSM_WIDTH, SSM_WIDTH), SSM_WIDTH ** -0.5),
        "b_glu": nrm((DEPTH, SSM_WIDTH), 0.01),
        "w_attn_br": nrm((DEPTH, ATTN_OUT, D_MODEL), ATTN_OUT ** -0.5),
        "w_ssm_br": nrm((DEPTH, SSM_WIDTH, D_MODEL), SSM_WIDTH ** -0.5),
        "w_out": nrm((DEPTH, D_MODEL, D_MODEL), D_MODEL ** -0.5),
        "norm2_g": 1.0 + nrm((DEPTH, D_MODEL), 0.02),
        "ffn_w_gate": nrm((N_DENSE, D_MODEL, D_FF), D_MODEL ** -0.5),
        "ffn_w_up": nrm((N_DENSE, D_MODEL, D_FF), D_MODEL ** -0.5),
        "ffn_w_down": nrm((N_DENSE, D_FF, D_MODEL), D_FF ** -0.5),
        "moe_router": nrm((N_MOE, D_MODEL, N_EXPERTS), D_MODEL ** -0.5),
        "moe_w_gate": nrm((N_MOE, N_EXPERTS, D_MODEL, D_FF_EXPERT), D_MODEL ** -0.5),
        "moe_w_up": nrm((N_MOE, N_EXPERTS, D_MODEL, D_FF_EXPERT), D_MODEL ** -0.5),
        "moe_w_down": nrm((N_MOE, N_EXPERTS, D_FF_EXPERT, D_MODEL), D_FF_EXPERT ** -0.5),
        "final_norm_g": 1.0 + nrm((D_MODEL,), 0.02),
    }


def reference(x, rel_bias, norm1_g, w_in, ssm_lam_re, ssm_lam_im, ssm_log_dt, ssm_b_re, ssm_b_im,
              ssm_c_re, ssm_c_im, ssm_d, w_glu, b_glu, w_attn_br, w_ssm_br, w_out, norm2_g,
              ffn_w_gate, ffn_w_up, ffn_w_down, moe_router, moe_w_gate, moe_w_up, moe_w_down,
              final_norm_g):
    B, S, D = x.shape
    for l in range(DEPTH):
        h = rmsnorm(x, norm1_g[l])
        x = x + hybrid_mixer(h, rel_bias, w_in[l], ssm_lam_re[l], ssm_lam_im[l], ssm_log_dt[l],
                             ssm_b_re[l], ssm_b_im[l], ssm_c_re[l], ssm_c_im[l], ssm_d[l],
                             w_glu[l], b_glu[l], w_attn_br[l], w_ssm_br[l], w_out[l])
        h = rmsnorm(x, norm2_g[l])
        if l % 2 == 0:
            i = l // 2
            x = x + swiglu(h, ffn_w_gate[i], ffn_w_up[i], ffn_w_down[i])
        else:
            i = l // 2
            x = x + moe_swiglu(h.reshape(B * S, D), moe_router[i], moe_w_gate[i], moe_w_up[i],
                               moe_w_down[i]).reshape(B, S, D)
    return rmsnorm(x, final_norm_g)
```

```python
import functools

import numpy as np
import jax
import jax.numpy as jnp
from jax import lax
from jax.experimental import pallas as pl
from jax.experimental.pallas import tpu as pltpu

F32 = jnp.float32
BF16 = jnp.bfloat16

D_MODEL = 1024
HEAD_DIM = 64
ATTN_GROUPS = ((128, 1), (512, 4), (2048, 16))
N_GROUPS = 3
HEADS = 8
GROUP_W = HEADS * HEAD_DIM
ATTN_W = N_GROUPS * GROUP_W
BLK = 128
REL_BUCKETS = 32
REL_MAX_DIST = 2048
NEG_INF = -1e30
SSM_CH = 16
SSM_W = D_MODEL // 2
SSM_G = SSM_W // SSM_CH
SSM_P = 64
PROJ_W = 3 * ATTN_W + SSM_W + 2 * D_MODEL
N_EXPERTS = 8
MOE_BM = 512
EPS = 1e-6
CHUNK = 128

LANES = 128
VMEM_LIMIT = 56 * 1024 * 1024


def _cparams(sem):
    return pltpu.CompilerParams(dimension_semantics=sem, vmem_limit_bytes=VMEM_LIMIT)


def _rms(x, g):
    return x * lax.rsqrt(jnp.mean(x * x, axis=-1, keepdims=True) + EPS) * g


def _proj_body(x_ref, g_ref, w_ref, wut_ref, proj_ref, ut_ref, h_scr):
    @pl.when(pl.program_id(1) == 0)
    def _():
        hb = _rms(x_ref[...], g_ref[...]).astype(BF16)
        h_scr[...] = hb
        ut_ref[...] = lax.dot_general(
            wut_ref[...], hb, (((1,), (1,)), ((), ())), preferred_element_type=F32
        ).astype(BF16)

    proj_ref[...] = jnp.dot(h_scr[...], w_ref[...], preferred_element_type=F32).astype(BF16)


def _in_projection(x2, g, w_bf, wut_bf, tm=1024, tn=1024):
    n = x2.shape[0]
    return pl.pallas_call(
        _proj_body,
        grid=(n // tm, PROJ_W // tn),
        in_specs=[
            pl.BlockSpec((tm, D_MODEL), lambda i, j: (i, 0)),
            pl.BlockSpec((1, D_MODEL), lambda i, j: (0, 0)),
            pl.BlockSpec((D_MODEL, tn), lambda i, j: (0, j)),
            pl.BlockSpec((SSM_W, D_MODEL), lambda i, j: (0, 0)),
        ],
        out_specs=[
            pl.BlockSpec((tm, tn), lambda i, j: (i, j)),
            pl.BlockSpec((SSM_W, tm), lambda i, j: (0, i)),
        ],
        out_shape=[
            jax.ShapeDtypeStruct((n, PROJ_W), BF16),
            jax.ShapeDtypeStruct((SSM_W, n), BF16),
        ],
        scratch_shapes=[pltpu.VMEM((tm, D_MODEL), BF16)],
        compiler_params=_cparams(("parallel", "arbitrary")),
        name="in_projection",
    )(x2, g, w_bf, wut_bf)


def _t5_bucket(dist):
    max_exact = REL_BUCKETS // 2
    d = np.maximum(dist, 1).astype(np.float64)
    large = max_exact + (
        np.log(d / max_exact) / np.log(REL_MAX_DIST / max_exact) * (REL_BUCKETS - max_exact)
    ).astype(np.int32)
    large = np.minimum(large, REL_BUCKETS - 1)
    return np.where(dist < max_exact, dist, large).astype(np.int32)


def _band_bias(table, window, dilation):
    steps = window // dilation
    qi = np.arange(BLK)[:, None]
    kj = np.arange(2 * BLK)[None, :]
    delta = BLK + qi - kj
    band = (delta >= 0) & (delta <= steps)
    bucket = _t5_bucket(np.clip(delta, 0, steps) * dilation)
    bias = jnp.transpose(table[bucket], (2, 0, 1)).astype(F32)
    return jnp.where(band[None], bias, NEG_INF)


def _attn_body(q_ref, kp_ref, kc_ref, vp_ref, vc_ref, bias_ref, o_ref, l_ref, *, nsub):
    lane = lax.broadcasted_iota(jnp.int32, (BLK, LANES), 1)
    lo = lane < HEAD_DIM
    keep_lo = jnp.where(lo, 1.0, 0.0).astype(BF16)
    keep_hi = jnp.where(lo, 0.0, 1.0).astype(BF16)
    col = lax.broadcasted_iota(jnp.int32, (BLK, 2 * BLK), 1)
    first_pen = jnp.where(col < BLK, jnp.where(pl.program_id(2) == 0, NEG_INF, 0.0), 0.0)
    for i in range(nsub):
        rows = slice(i * BLK, (i + 1) * BLK)
        q = q_ref[rows, :] * jnp.asarray(HEAD_DIM ** -0.5, BF16)
        if i == 0:
            kw = jnp.concatenate([kp_ref[...], kc_ref[0:BLK, :]], axis=0)
            vw = jnp.concatenate([vp_ref[...], vc_ref[0:BLK, :]], axis=0)
        else:
            kw = kc_ref[(i - 1) * BLK:(i + 1) * BLK, :]
            vw = vc_ref[(i - 1) * BLK:(i + 1) * BLK, :]
        for hp in range(HEADS // 2):
            cols = slice(hp * LANES, (hp + 1) * LANES)
            q2, k2, v2 = q[:, cols], kw[:, cols], vw[:, cols]
            outs, lses = [], []
            for half in range(2):
                qm = q2 * (keep_lo if half == 0 else keep_hi)
                s = lax.dot_general(qm, k2, (((1,), (1,)), ((), ())), preferred_element_type=F32)
                s = s + bias_ref[2 * hp + half]
                if i == 0:
                    s = s + first_pen
                m = jnp.max(s, axis=-1, keepdims=True)
                p = jnp.exp(s - m)
                den = jnp.sum(p, axis=-1, keepdims=True)
                pv = jnp.dot(p.astype(BF16), v2, preferred_element_type=F32)
                outs.append(pv / den)
                lses.append(jnp.broadcast_to(m + jnp.log(den), (BLK, LANES)))
            o_ref[rows, cols] = jnp.where(lo, outs[0], outs[1]).astype(BF16)
            l_ref[rows, cols] = jnp.where(lo, lses[0], lses[1])


def _attention_group(proj, bias, g, dilation, batch, seq):
    r = dilation
    length = seq // r
    qb = min(512, length)
    nsub = qb // BLK
    pv = proj.reshape(batch, length, r * PROJ_W)
    cpr = PROJ_W // GROUP_W
    ng = N_GROUPS

    def qmap(off):
        return lambda b, c, n: (b, n, c * cpr + off)

    def pmap(off):
        return lambda b, c, n: (b, jnp.maximum(n * nsub - 1, 0), c * cpr + off)

    o, l = pl.pallas_call(
        functools.partial(_attn_body, nsub=nsub),
        grid=(batch, r, length // qb),
        in_specs=[
            pl.BlockSpec((None, qb, GROUP_W), qmap(g)),
            pl.BlockSpec((None, BLK, GROUP_W), pmap(ng + g)),
            pl.BlockSpec((None, qb, GROUP_W), qmap(ng + g)),
            pl.BlockSpec((None, BLK, GROUP_W), pmap(2 * ng + g)),
            pl.BlockSpec((None, qb, GROUP_W), qmap(2 * ng + g)),
            pl.BlockSpec((HEADS, BLK, 2 * BLK), lambda b, c, n: (0, 0, 0)),
        ],
        out_specs=[
            pl.BlockSpec((None, qb, GROUP_W), lambda b, c, n: (b, n, c)),
            pl.BlockSpec((None, qb, GROUP_W), lambda b, c, n: (b, n, c)),
        ],
        out_shape=[
            jax.ShapeDtypeStruct((batch, length, r * GROUP_W), BF16),
            jax.ShapeDtypeStruct((batch, length, r * GROUP_W), F32),
        ],
        compiler_params=_cparams(("parallel", "parallel", "arbitrary")),
        name=f"attention_g{g}",
    )(pv, pv, pv, pv, pv, bias)
    return o.reshape(batch * seq, GROUP_W), l.reshape(batch * seq, GROUP_W)


def _ssm_tables(lam_re, lam_im, log_dt, b_re, b_im, c_re, c_im):
    lam = lax.complex(lam_re.astype(F32), lam_im.astype(F32))
    dt = jnp.exp(log_dt.astype(F32))[:, None]
    lam_dt = lam * dt
    lam_bar = jnp.exp(lam_dt)
    b = lax.complex(b_re.astype(F32), b_im.astype(F32))
    c = lax.complex(c_re.astype(F32), c_im.astype(F32))
    b_bar = ((lam_bar - 1.0) / lam)[..., None] * b
    t = jnp.arange(CHUNK, dtype=F32)
    half = CHUNK // 2

    def power(k):
        return jnp.exp(lam_dt[:, None, :] * k[None, :, None])

    b_cp = jnp.transpose(b_bar, (0, 2, 1))

    def rows(pw):
        v = b_cp[:, :, None, :] * pw[:, None, :, :]
        v = jnp.concatenate([v.real, v.imag], axis=-1)
        return v.reshape(SSM_G, SSM_CH * CHUNK, 2 * SSM_P)

    def cols(pw):
        v = c[:, :, None, :] * pw[:, None, :, :]
        v = jnp.concatenate([v.real, -v.imag], axis=-1)
        return jnp.transpose(v, (0, 3, 1, 2)).reshape(SSM_G, 2 * SSM_P, SSM_CH * CHUNK)

    a_tab = rows(power(half - t))
    d_tab = cols(power(t - half))
    w_tab = rows(power(CHUNK - 1.0 - t))
    v_tab = cols(power(t + 1.0))
    lc = jnp.exp(lam_dt * float(CHUNK))
    l1 = jnp.concatenate([lc.real, lc.real], axis=-1)[:, None, :]
    l2 = jnp.concatenate([-lc.imag, lc.imag], axis=-1)[:, None, :]
    return a_tab, d_tab, w_tab.astype(BF16), v_tab.astype(BF16), l1, l2


def _gelu_tanh(y):
    return 0.5 * y * (1.0 + jnp.tanh(0.7978845608028654 * (y + 0.044715 * (y * y * y))))


def _ssm_body(d_ref, u_ref, a_ref, dt_ref, w_ref, v_ref, l1_ref, l2_ref, z_ref,
              m_scr, s_scr, xp_scr, *, nbatch, ncb):
    g = pl.program_id(0)
    width = SSM_CH * CHUNK
    cb = 512
    s_idx = lax.broadcasted_iota(jnp.int32, (width, cb), 0) & (CHUNK - 1)
    t_idx = lax.broadcasted_iota(jnp.int32, (width, cb), 1) & (CHUNK - 1)
    causal = t_idx >= s_idx
    for k in range(width // cb):
        mk = jnp.dot(a_ref[...], dt_ref[:, k * cb:(k + 1) * cb],
                     precision=lax.Precision.HIGHEST, preferred_element_type=F32)
        m_scr[:, k * cb:(k + 1) * cb] = jnp.where(causal, mk, 0.0).astype(BF16)

    x = jnp.concatenate([u_ref[c] for c in range(SSM_CH)], axis=1)
    s_scr[...] = jnp.dot(x, w_ref[...], preferred_element_type=F32)

    l1 = l1_ref[...]
    l2 = l2_ref[...]
    for b in range(nbatch):
        def step(j, st):
            xp_scr[pl.ds(b * ncb + j, 1), :] = st
            inj = s_scr[pl.ds(b * ncb + j, 1), :]
            return st * l1 + pltpu.roll(st, SSM_P, 1) * l2 + inj

        lax.fori_loop(0, ncb, step, jnp.zeros((1, 2 * SSM_P), F32))

    y = jnp.dot(x, m_scr[...], preferred_element_type=F32)
    y = y + jnp.dot(xp_scr[...].astype(BF16), v_ref[...], preferred_element_type=F32)
    for c in range(SSM_CH):
        yc = y[:, c * CHUNK:(c + 1) * CHUNK] + d_ref[g * SSM_CH + c] * u_ref[c].astype(F32)
        z_ref[c] = _gelu_tanh(yc).astype(BF16)


def _ssm_scan(ut3, d_skip, tables, nbatch):
    a_tab, d_tab, w_tab, v_tab, l1, l2 = tables
    nc = ut3.shape[1]
    width = SSM_CH * CHUNK
    grid_spec = pltpu.PrefetchScalarGridSpec(
        num_scalar_prefetch=1,
        grid=(SSM_G,),
        in_specs=[
            pl.BlockSpec((SSM_CH, nc, CHUNK), lambda g, d: (g, 0, 0)),
            pl.BlockSpec((None, width, 2 * SSM_P), lambda g, d: (g, 0, 0)),
            pl.BlockSpec((None, 2 * SSM_P, width), lambda g, d: (g, 0, 0)),
            pl.BlockSpec((None, width, 2 * SSM_P), lambda g, d: (g, 0, 0)),
            pl.BlockSpec((None, 2 * SSM_P, width), lambda g, d: (g, 0, 0)),
            pl.BlockSpec((None, 1, 2 * SSM_P), lambda g, d: (g, 0, 0)),
            pl.BlockSpec((None, 1, 2 * SSM_P), lambda g, d: (g, 0, 0)),
        ],
        out_specs=pl.BlockSpec((SSM_CH, nc, CHUNK), lambda g, d: (g, 0, 0)),
        scratch_shapes=[
            pltpu.VMEM((width, width), BF16),
            pltpu.VMEM((nc, 2 * SSM_P), F32),
            pltpu.VMEM((nc, 2 * SSM_P), F32),
        ],
    )
    return pl.pallas_call(
        functools.partial(_ssm_body, nbatch=nbatch, ncb=nc // nbatch),
        grid_spec=grid_spec,
        out_shape=jax.ShapeDtypeStruct((SSM_W, nc, CHUNK), BF16),
        compiler_params=_cparams(("arbitrary",)),
        name="ssm_scan",
    )(d_skip.astype(F32), ut3, a_tab, d_tab, w_tab, v_tab, l1, l2)


def _merge_body(o0, o1, o2, l0, l1, l2, zt_ref, ga_ref, gs_ref, x_ref,
                wglu_ref, bglu_ref, wab_ref, wsb_ref, wout_ref, out_ref):
    a0, a1, a2 = l0[...], l1[...], l2[...]
    mx = jnp.maximum(jnp.maximum(a0, a1), a2)
    e0, e1, e2 = jnp.exp(a0 - mx), jnp.exp(a1 - mx), jnp.exp(a2 - mx)
    mix = (e0 * o0[...].astype(F32) + e1 * o1[...].astype(F32) + e2 * o2[...].astype(F32)) / (e0 + e1 + e2)
    y_attn = jnp.dot(mix.astype(BF16), wab_ref[...], preferred_element_type=F32)

    z = zt_ref[...].astype(F32).T.astype(BF16)
    gl = jnp.dot(z, wglu_ref[...], preferred_element_type=F32) + bglu_ref[...]
    sg = z.astype(F32) * jax.nn.sigmoid(gl)
    y_ssm = jnp.dot(sg.astype(BF16), wsb_ref[...], preferred_element_type=F32)

    merged = (jax.nn.sigmoid(ga_ref[...].astype(F32)) * y_attn
              + jax.nn.sigmoid(gs_ref[...].astype(F32)) * y_ssm)
    out_ref[...] = x_ref[...] + jnp.dot(merged.astype(BF16), wout_ref[...], preferred_element_type=F32)


def _merge(os_, ls_, zt, proj, x2, wglu, bglu, wab, wsb, wout, tm=512):
    n = x2.shape[0]
    row = lambda i: (i, 0)
    const = lambda i: (0, 0)
    gate_blk = (3 * ATTN_W + SSM_W) // D_MODEL
    in_specs = (
        [pl.BlockSpec((tm, GROUP_W), row)] * 6
        + [
            pl.BlockSpec((SSM_W, tm), lambda i: (0, i)),
            pl.BlockSpec((tm, D_MODEL), lambda i: (i, gate_blk)),
            pl.BlockSpec((tm, D_MODEL), lambda i: (i, gate_blk + 1)),
            pl.BlockSpec((tm, D_MODEL), row),
            pl.BlockSpec((SSM_W, SSM_W), const),
            pl.BlockSpec((1, SSM_W), const),
            pl.BlockSpec((GROUP_W, D_MODEL), const),
            pl.BlockSpec((SSM_W, D_MODEL), const),
            pl.BlockSpec((D_MODEL, D_MODEL), const),
        ]
    )
    return pl.pallas_call(
        _merge_body,
        grid=(n // tm,),
        in_specs=in_specs,
        out_specs=pl.BlockSpec((tm, D_MODEL), row),
        out_shape=jax.ShapeDtypeStruct((n, D_MODEL), F32),
        compiler_params=_cparams(("parallel",)),
        name="merge",
    )(*os_, *ls_, zt, proj, proj, x2, wglu, bglu, wab, wsb, wout)


def _ffn_body(x_ref, g_ref, wg_ref, wu_ref, wd_ref, o_ref, h_scr, acc_scr):
    j = pl.program_id(1)

    @pl.when(j == 0)
    def _():
        h_scr[...] = _rms(x_ref[...], g_ref[...]).astype(BF16)
        acc_scr[...] = jnp.zeros_like(acc_scr)

    h = h_scr[...]
    a = jnp.dot(h, wg_ref[...], preferred_element_type=F32)
    b = jnp.dot(h, wu_ref[...], preferred_element_type=F32)
    act = (a * jax.nn.sigmoid(a) * b).astype(BF16)
    acc_scr[...] += jnp.dot(act, wd_ref[...], preferred_element_type=F32)

    @pl.when(j == pl.num_programs(1) - 1)
    def _():
        o_ref[...] = x_ref[...] + acc_scr[...]


def _dense_ffn(x2, g, wg, wu, wd, tm=1024):
    n = x2.shape[0]
    dff = wg.shape[1]
    tf = dff // 2 if (dff // 2) % LANES == 0 else dff
    return pl.pallas_call(
        _ffn_body,
        grid=(n // tm, dff // tf),
        in_specs=[
            pl.BlockSpec((tm, D_MODEL), lambda i, j: (i, 0)),
            pl.BlockSpec((1, D_MODEL), lambda i, j: (0, 0)),
            pl.BlockSpec((D_MODEL, tf), lambda i, j: (0, j)),
            pl.BlockSpec((D_MODEL, tf), lambda i, j: (0, j)),
            pl.BlockSpec((tf, D_MODEL), lambda i, j: (j, 0)),
        ],
        out_specs=pl.BlockSpec((tm, D_MODEL), lambda i, j: (i, 0)),
        out_shape=jax.ShapeDtypeStruct((n, D_MODEL), F32),
        scratch_shapes=[pltpu.VMEM((tm, D_MODEL), BF16), pltpu.VMEM((tm, D_MODEL), F32)],
        compiler_params=_cparams(("parallel", "arbitrary")),
        name="dense_ffn",
    )(x2, g, wg, wu, wd)


def _router_body(x_ref, g_ref, wr_ref, idx_ref, gate_ref):
    h = _rms(x_ref[...], g_ref[...])
    logits = jnp.dot(h, wr_ref[...], precision=lax.Precision.HIGHEST, preferred_element_type=F32)
    lane = lax.broadcasted_iota(jnp.int32, logits.shape, 1)
    lane_f = lane.astype(F32)
    logits = jnp.where(lane < N_EXPERTS, logits, -jnp.inf)
    v1 = jnp.max(logits, axis=-1, keepdims=True)
    i1 = jnp.min(jnp.where(logits == v1, lane_f, float(LANES)), axis=-1, keepdims=True)
    rest = jnp.where(lane_f == i1, -jnp.inf, logits)
    v2 = jnp.max(rest, axis=-1, keepdims=True)
    i2 = jnp.min(jnp.where(rest == v2, lane_f, float(LANES)), axis=-1, keepdims=True)
    e = jnp.exp(v2 - v1)
    g1 = 1.0 / (1.0 + e)
    g2 = e / (1.0 + e)
    idx_ref[...] = jnp.where(lane == 0, i1, jnp.where(lane == 1, i2, 0.0)).astype(jnp.int32)
    gate_ref[...] = jnp.where(lane == 0, g1, jnp.where(lane == 1, g2, 0.0))


def _router(x2, g, wr_pad, tm=1024):
    n = x2.shape[0]
    return pl.pallas_call(
        _router_body,
        grid=(n // tm,),
        in_specs=[
            pl.BlockSpec((tm, D_MODEL), lambda i: (i, 0)),
            pl.BlockSpec((1, D_MODEL), lambda i: (0, 0)),
            pl.BlockSpec((D_MODEL, LANES), lambda i: (0, 0)),
        ],
        out_specs=[
            pl.BlockSpec((tm, LANES), lambda i: (i, 0)),
            pl.BlockSpec((tm, LANES), lambda i: (i, 0)),
        ],
        out_shape=[
            jax.ShapeDtypeStruct((n, LANES), jnp.int32),
            jax.ShapeDtypeStruct((n, LANES), F32),
        ],
        compiler_params=_cparams(("parallel",)),
        name="router",
    )(x2, g, wr_pad)


def _pack_bf16_pairs(hb):
    half = hb.shape[1] // 2
    lo = lax.bitcast_convert_type(hb[:, :half].astype(F32), jnp.uint32)
    hi = lax.bitcast_convert_type(hb[:, half:].astype(F32), jnp.uint32)
    return (hi & jnp.uint32(0xFFFF0000)) | (lo >> 16)


def _unpack_bf16_pairs(xu):
    lo = lax.bitcast_convert_type(xu << 16, F32).astype(BF16)
    hi = lax.bitcast_convert_type(xu & jnp.uint32(0xFFFF0000), F32).astype(BF16)
    return lo, hi


def _dispatch_body(slot_ref, x_ref, g_ref, xs_in_ref, xs_ref, pack_scr, sem, *, tm):
    del xs_in_ref
    i = pl.program_id(0)
    pack_scr[...] = _pack_bf16_pairs(_rms(x_ref[...], g_ref[...]).astype(BF16))

    def row_copy(t, k):
        dst = slot_ref[(i * tm + t) * 2 + k]
        return pltpu.make_async_copy(pack_scr.at[pl.ds(t, 1)], xs_ref.at[pl.ds(dst, 1)], sem)

    def issue(t, carry):
        row_copy(t, 0).start()
        row_copy(t, 1).start()
        return carry

    lax.fori_loop(0, tm, issue, 0)

    def drain(t, carry):
        row_copy(t, 0).wait()
        row_copy(t, 1).wait()
        return carry

    lax.fori_loop(0, tm, drain, 0)


def _dispatch(slot, x2, g, n_slots, tm=256):
    n = x2.shape[0]
    xs0 = jnp.zeros((n_slots, D_MODEL // 2), jnp.uint32)
    grid_spec = pltpu.PrefetchScalarGridSpec(
        num_scalar_prefetch=1,
        grid=(n // tm,),
        in_specs=[
            pl.BlockSpec((tm, D_MODEL), lambda i, s: (i, 0)),
            pl.BlockSpec((1, D_MODEL), lambda i, s: (0, 0)),
            pl.BlockSpec(memory_space=pl.ANY),
        ],
        out_specs=pl.BlockSpec(memory_space=pl.ANY),
        scratch_shapes=[pltpu.VMEM((tm, D_MODEL // 2), jnp.uint32), pltpu.SemaphoreType.DMA(())],
    )
    return pl.pallas_call(
        functools.partial(_dispatch_body, tm=tm),
        grid_spec=grid_spec,
        out_shape=jax.ShapeDtypeStruct((n_slots, D_MODEL // 2), jnp.uint32),
        input_output_aliases={3: 0},
        compiler_params=_cparams(("arbitrary",)),
        name="moe_dispatch",
    )(slot, x2, g, xs0)


def _experts_body(be_ref, nu_ref, xs_ref, wg_ref, wu_ref, wd_ref, ys_ref, lo_scr, hi_scr):
    del be_ref
    i = pl.program_id(0)
    j = pl.program_id(1)
    half = D_MODEL // 2

    @pl.when(i < nu_ref[0])
    def _():
        @pl.when(j == 0)
        def _():
            lo, hi = _unpack_bf16_pairs(xs_ref[...])
            lo_scr[...] = lo
            hi_scr[...] = hi
            ys_ref[...] = jnp.zeros_like(ys_ref)

        lo, hi = lo_scr[...], hi_scr[...]
        a = (jnp.dot(lo, wg_ref[:half, :], preferred_element_type=F32)
             + jnp.dot(hi, wg_ref[half:, :], preferred_element_type=F32))
        b = (jnp.dot(lo, wu_ref[:half, :], preferred_element_type=F32)
             + jnp.dot(hi, wu_ref[half:, :], preferred_element_type=F32))
        act = (a * jax.nn.sigmoid(a) * b).astype(BF16)
        ys_ref[...] += jnp.dot(act, wd_ref[...], preferred_element_type=F32)

    @pl.when(jnp.logical_and(i >= nu_ref[0], j == 0))
    def _():
        ys_ref[...] = jnp.zeros_like(ys_ref)


def _experts(block_e, n_used, xs, wg, wu, wd, tf=512):
    n_slots = xs.shape[0]
    n_blocks = n_slots // MOE_BM
    dff = wg.shape[2]
    nf = dff // tf

    def blk(i, nu):
        return jnp.minimum(i, nu[0] - 1)

    def fj(i, j, nu):
        return jnp.where(i < nu[0], j, nf - 1)

    grid_spec = pltpu.PrefetchScalarGridSpec(
        num_scalar_prefetch=2,
        grid=(n_blocks, nf),
        in_specs=[
            pl.BlockSpec((MOE_BM, D_MODEL // 2), lambda i, j, be, nu: (blk(i, nu), 0)),
            pl.BlockSpec((None, D_MODEL, tf), lambda i, j, be, nu: (be[blk(i, nu)], 0, fj(i, j, nu))),
            pl.BlockSpec((None, D_MODEL, tf), lambda i, j, be, nu: (be[blk(i, nu)], 0, fj(i, j, nu))),
            pl.BlockSpec((None, tf, D_MODEL), lambda i, j, be, nu: (be[blk(i, nu)], fj(i, j, nu), 0)),
        ],
        out_specs=pl.BlockSpec((MOE_BM, D_MODEL), lambda i, j, be, nu: (i, 0)),
        scratch_shapes=[pltpu.VMEM((MOE_BM, D_MODEL // 2), BF16)] * 2,
    )
    return pl.pallas_call(
        _experts_body,
        grid_spec=grid_spec,
        out_shape=jax.ShapeDtypeStruct((n_slots, D_MODEL), F32),
        compiler_params=_cparams(("arbitrary", "arbitrary")),
        name="moe_experts",
    )(block_e, n_used, xs, wg, wu, wd)


def _combine_body(slot_ref, x_ref, gate_ref, g_ref, ys_ref, o_ref, y0_scr, y1_scr, sem, *, tm):
    i = pl.program_id(0)

    def row_copy(t, k, dst):
        src = slot_ref[(i * tm + t) * 2 + k]
        return pltpu.make_async_copy(ys_ref.at[pl.ds(src, 1)], dst.at[pl.ds(t, 1)], sem)

    def issue(t, carry):
        row_copy(t, 0, y0_scr).start()
        row_copy(t, 1, y1_scr).start()
        return carry

    lax.fori_loop(0, tm, issue, 0)

    def drain(t, carry):
        row_copy(t, 0, y0_scr).wait()
        row_copy(t, 1, y1_scr).wait()
        return carry

    lax.fori_loop(0, tm, drain, 0)

    gt = gate_ref[...]
    y = x_ref[...] + (gt[:, 0:1] * y0_scr[...] + gt[:, 1:2] * y1_scr[...])
    o_ref[...] = _rms(y, g_ref[...])


def _combine(slot, x2, gates, g_final, ys, tm=256):
    n = x2.shape[0]
    grid_spec = pltpu.PrefetchScalarGridSpec(
        num_scalar_prefetch=1,
        grid=(n // tm,),
        in_specs=[
            pl.BlockSpec((tm, D_MODEL), lambda i, s: (i, 0)),
            pl.BlockSpec((tm, LANES), lambda i, s: (i, 0)),
            pl.BlockSpec((1, D_MODEL), lambda i, s: (0, 0)),
            pl.BlockSpec(memory_space=pl.ANY),
        ],
        out_specs=pl.BlockSpec((tm, D_MODEL), lambda i, s: (i, 0)),
        scratch_shapes=[
            pltpu.VMEM((tm, D_MODEL), F32),
            pltpu.VMEM((tm, D_MODEL), F32),
            pltpu.SemaphoreType.DMA(()),
        ],
    )
    return pl.pallas_call(
        functools.partial(_combine_body, tm=tm),
        grid_spec=grid_spec,
        out_shape=jax.ShapeDtypeStruct((n, D_MODEL), F32),
        compiler_params=_cparams(("arbitrary",)),
        name="moe_combine",
    )(slot, x2, gates, g_final, ys)


def _route_slots(idx, n):
    flat_e = idx[:, :2].reshape(-1)
    onehot = (flat_e[:, None] == jnp.arange(N_EXPERTS, dtype=jnp.int32)[None, :]).astype(jnp.int32)
    incl = jnp.cumsum(onehot, axis=0)
    counts = incl[-1]
    rank = jnp.sum((incl - onehot) * onehot, axis=1)
    padded = ((counts + MOE_BM - 1) // MOE_BM) * MOE_BM
    pad_end = jnp.cumsum(padded)
    pad_start = pad_end - padded
    slot = (jnp.sum(pad_start[None, :] * onehot, axis=1) + rank).astype(jnp.int32)
    n_blocks = (2 * n) // MOE_BM + N_EXPERTS
    starts = jnp.arange(n_blocks, dtype=jnp.int32) * MOE_BM
    block_e = jnp.sum((starts[:, None] >= pad_end[None, :]).astype(jnp.int32), axis=1)
    block_e = jnp.minimum(block_e, N_EXPERTS - 1).astype(jnp.int32)
    n_used = (pad_end[-1] // MOE_BM).astype(jnp.int32).reshape(1)
    return slot, block_e, n_used, n_blocks * MOE_BM


def _mixer_layer(x2, batch, seq, rel_bias, norm_g, w_in, ssm, d_skip, w_glu, b_glu,
                 w_attn_br, w_ssm_br, w_out):
    n = x2.shape[0]
    u_lo = 3 * ATTN_W
    w_bf = w_in.astype(BF16)
    wut_bf = jnp.transpose(w_in[:, u_lo:u_lo + SSM_W]).astype(BF16)
    proj, ut = _in_projection(x2, norm_g.reshape(1, D_MODEL), w_bf, wut_bf)

    os_, ls_ = [], []
    for g, (window, dilation) in enumerate(ATTN_GROUPS):
        bias = _band_bias(rel_bias[:, g * HEADS:(g + 1) * HEADS], window, dilation)
        o, l = _attention_group(proj, bias, g, dilation, batch, seq)
        os_.append(o)
        ls_.append(l)

    ut3 = ut.reshape(SSM_W, n // CHUNK, CHUNK)
    zt = _ssm_scan(ut3, d_skip, _ssm_tables(*ssm), batch).reshape(SSM_W, n)

    return _merge(os_, ls_, zt, proj, x2, w_glu.astype(BF16), b_glu.reshape(1, SSM_W).astype(F32),
                  w_attn_br.astype(BF16), w_ssm_br.astype(BF16), w_out.astype(BF16))


def kernel(x, rel_bias, norm1_g, w_in, ssm_lam_re, ssm_lam_im, ssm_log_dt, ssm_b_re, ssm_b_im, ssm_c_re, ssm_c_im, ssm_d, w_glu, b_glu, w_attn_br, w_ssm_br, w_out, norm2_g, ffn_w_gate, ffn_w_up, ffn_w_down, moe_router, moe_w_gate, moe_w_up, moe_w_down, final_norm_g):
    batch, seq, d = x.shape
    assert d == D_MODEL and norm1_g.shape[0] == 2 and seq % (16 * BLK) == 0
    n = batch * seq
    x2 = x.reshape(n, d)

    def mixer(x2, l):
        ssm = (ssm_lam_re[l], ssm_lam_im[l], ssm_log_dt[l], ssm_b_re[l], ssm_b_im[l],
               ssm_c_re[l], ssm_c_im[l])
        return _mixer_layer(x2, batch, seq, rel_bias, norm1_g[l], w_in[l], ssm, ssm_d[l], w_glu[l],
                            b_glu[l], w_attn_br[l], w_ssm_br[l], w_out[l])

    x2 = mixer(x2, 0)
    x2 = _dense_ffn(x2, norm2_g[0].reshape(1, d), ffn_w_gate[0].astype(BF16),
                    ffn_w_up[0].astype(BF16), ffn_w_down[0].astype(BF16))

    x2 = mixer(x2, 1)
    g2 = norm2_g[1].reshape(1, d)
    wr_pad = jnp.zeros((d, LANES), F32).at[:, :N_EXPERTS].set(moe_router[0].astype(F32))
    idx, gates = _router(x2, g2, wr_pad)
    slot, block_e, n_used, n_slots = _route_slots(idx, n)
    xs = _dispatch(slot, x2, g2, n_slots)
    ys = _experts(block_e, n_used, xs, moe_w_gate[0].astype(BF16), moe_w_up[0].astype(BF16),
                  moe_w_down[0].astype(BF16))
    out = _combine(slot, x2, gates, final_norm_g.reshape(1, d), ys)
    return out.reshape(batch, seq, d)
```

```python
import functools

import numpy as np
import jax
import jax.numpy as jnp
from jax import lax
from jax.experimental import pallas as pl
from jax.experimental.pallas import tpu as pltpu

F32 = jnp.float32
BF16 = jnp.bfloat16

D_MODEL = 1024
HEAD_DIM = 64
ATTN_GROUPS = ((128, 1), (512, 4), (2048, 16))
N_GROUPS = 3
HEADS = 8
GROUP_W = HEADS * HEAD_DIM
ATTN_W = N_GROUPS * GROUP_W
BLK = 128
REL_BUCKETS = 32
REL_MAX_DIST = 2048
NEG_INF = -1e30
SSM_CH = 16
SSM_W = D_MODEL // 2
SSM_G = SSM_W // SSM_CH
SSM_P = 64
PROJ_W = 3 * ATTN_W + SSM_W + 2 * D_MODEL
N_EXPERTS = 8
MOE_BM = 512
EPS = 1e-6
CHUNK = 128
SCAN_LEVELS = 8

LANES = 128
VMEM_LIMIT = 56 * 1024 * 1024


def _cparams(sem):
    return pltpu.CompilerParams(dimension_semantics=sem, vmem_limit_bytes=VMEM_LIMIT)


def _rms(x, g):
    return x * lax.rsqrt(jnp.mean(x * x, axis=-1, keepdims=True) + EPS) * g


def _proj_body(x_ref, g_ref, w_ref, wut_ref, proj_ref, ut_ref, h_scr):
    @pl.when(pl.program_id(1) == 0)
    def _():
        hb = _rms(x_ref[...], g_ref[...]).astype(BF16)
        h_scr[...] = hb
        ut = lax.dot_general(wut_ref[...], hb, (((1,), (1,)), ((), ())), preferred_element_type=F32)
        for k in range(ut_ref.shape[0]):
            ut_ref[k] = ut[:, k * CHUNK:(k + 1) * CHUNK]

    proj_ref[...] = jnp.dot(h_scr[...], w_ref[...], preferred_element_type=F32).astype(BF16)


def _in_projection(x2, g, w_bf, wut_bf, tm=1024, tn=1024):
    n = x2.shape[0]
    return pl.pallas_call(
        _proj_body,
        grid=(n // tm, PROJ_W // tn),
        in_specs=[
            pl.BlockSpec((tm, D_MODEL), lambda i, j: (i, 0)),
            pl.BlockSpec((1, D_MODEL), lambda i, j: (0, 0)),
            pl.BlockSpec((D_MODEL, tn), lambda i, j: (0, j)),
            pl.BlockSpec((SSM_W, D_MODEL), lambda i, j: (0, 0)),
        ],
        out_specs=[
            pl.BlockSpec((tm, tn), lambda i, j: (i, j)),
            pl.BlockSpec((tm // CHUNK, SSM_W, CHUNK), lambda i, j: (i, 0, 0)),
        ],
        out_shape=[
            jax.ShapeDtypeStruct((n, PROJ_W), BF16),
            jax.ShapeDtypeStruct((n // CHUNK, SSM_W, CHUNK), F32),
        ],
        scratch_shapes=[pltpu.VMEM((tm, D_MODEL), BF16)],
        compiler_params=_cparams(("parallel", "arbitrary")),
        name="in_projection",
    )(x2, g, w_bf, wut_bf)


def _t5_bucket(dist):
    max_exact = REL_BUCKETS // 2
    d = np.maximum(dist, 1).astype(np.float64)
    large = max_exact + (
        np.log(d / max_exact) / np.log(REL_MAX_DIST / max_exact) * (REL_BUCKETS - max_exact)
    ).astype(np.int32)
    large = np.minimum(large, REL_BUCKETS - 1)
    return np.where(dist < max_exact, dist, large).astype(np.int32)


def _band_bias(table, window, dilation):
    steps = window // dilation
    qi = np.arange(BLK)[:, None]
    kj = np.arange(2 * BLK)[None, :]
    delta = BLK + qi - kj
    band = (delta >= 0) & (delta <= steps)
    bucket = _t5_bucket(np.clip(delta, 0, steps) * dilation)
    onehot = np.eye(REL_BUCKETS, dtype=np.float32)[bucket]
    bias = jnp.einsum("qkb,bh->hqk", onehot, table.astype(F32), precision=lax.Precision.HIGHEST)
    return jnp.where(band[None], bias, NEG_INF)


def _attn_body(q_ref, kp_ref, kc_ref, vp_ref, vc_ref, bias_ref, o_ref, l_ref, *, nsub):
    lane = lax.broadcasted_iota(jnp.int32, (BLK, LANES), 1)
    lo = lane < HEAD_DIM
    keep_lo = jnp.where(lo, 1.0, 0.0).astype(BF16)
    keep_hi = jnp.where(lo, 0.0, 1.0).astype(BF16)
    col = lax.broadcasted_iota(jnp.int32, (BLK, 2 * BLK), 1)
    first_pen = jnp.where(col < BLK, jnp.where(pl.program_id(2) == 0, NEG_INF, 0.0), 0.0)
    for i in range(nsub):
        rows = slice(i * BLK, (i + 1) * BLK)
        q = q_ref[rows, :] * jnp.asarray(HEAD_DIM ** -0.5, BF16)
        if i == 0:
            kw = jnp.concatenate([kp_ref[...], kc_ref[0:BLK, :]], axis=0)
            vw = jnp.concatenate([vp_ref[...], vc_ref[0:BLK, :]], axis=0)
        else:
            kw = kc_ref[(i - 1) * BLK:(i + 1) * BLK, :]
            vw = vc_ref[(i - 1) * BLK:(i + 1) * BLK, :]
        for hp in range(HEADS // 2):
            cols = slice(hp * LANES, (hp + 1) * LANES)
            q2, k2, v2 = q[:, cols], kw[:, cols], vw[:, cols]
            outs, lses = [], []
            for half in range(2):
                qm = q2 * (keep_lo if half == 0 else keep_hi)
                s = lax.dot_general(qm, k2, (((1,), (1,)), ((), ())), preferred_element_type=F32)
                s = s + bias_ref[2 * hp + half]
                if i == 0:
                    s = s + first_pen
                m = jnp.max(s, axis=-1, keepdims=True)
                p = jnp.exp(s - m)
                den = jnp.sum(p, axis=-1, keepdims=True)
                pv = jnp.dot(p.astype(BF16), v2, preferred_element_type=F32)
                outs.append(pv / den)
                lses.append(jnp.broadcast_to(m + jnp.log(den), (BLK, LANES)))
            o_ref[rows, cols] = jnp.where(lo, outs[0], outs[1]).astype(BF16)
            l_ref[rows, cols] = jnp.where(lo, lses[0], lses[1])


def _attention_group(proj, bias, g, dilation, batch, seq):
    r = dilation
    length = seq // r
    qb = min(512, length)
    nsub = qb // BLK
    pv = proj.reshape(batch, length, r * PROJ_W)
    cpr = PROJ_W // GROUP_W
    ng = N_GROUPS

    def qmap(off):
        return lambda b, c, n: (b, n, c * cpr + off)

    def pmap(off):
        return lambda b, c, n: (b, jnp.maximum(n * nsub - 1, 0), c * cpr + off)

    o, l = pl.pallas_call(
        functools.partial(_attn_body, nsub=nsub),
        grid=(batch, r, length // qb),
        in_specs=[
            pl.BlockSpec((None, qb, GROUP_W), qmap(g)),
            pl.BlockSpec((None, BLK, GROUP_W), pmap(ng + g)),
            pl.BlockSpec((None, qb, GROUP_W), qmap(ng + g)),
            pl.BlockSpec((None, BLK, GROUP_W), pmap(2 * ng + g)),
            pl.BlockSpec((None, qb, GROUP_W), qmap(2 * ng + g)),
            pl.BlockSpec((HEADS, BLK, 2 * BLK), lambda b, c, n: (0, 0, 0)),
        ],
        out_specs=[
            pl.BlockSpec((None, qb, GROUP_W), lambda b, c, n: (b, n, c)),
            pl.BlockSpec((None, qb, GROUP_W), lambda b, c, n: (b, n, c)),
        ],
        out_shape=[
            jax.ShapeDtypeStruct((batch, length, r * GROUP_W), BF16),
            jax.ShapeDtypeStruct((batch, length, r * GROUP_W), F32),
        ],
        compiler_params=_cparams(("parallel", "parallel", "arbitrary")),
        name=f"attention_g{g}",
    )(pv, pv, pv, pv, pv, bias)
    return o.reshape(batch * seq, GROUP_W), l.reshape(batch * seq, GROUP_W)


def _ssm_tables(lam_re, lam_im, log_dt, b_re, b_im, c_re, c_im):
    lam = lax.complex(lam_re.astype(F32), lam_im.astype(F32))
    dt = jnp.exp(log_dt.astype(F32))[:, None]
    lam_dt = lam * dt
    lam_bar = jnp.exp(lam_dt)
    b = lax.complex(b_re.astype(F32), b_im.astype(F32))
    c = lax.complex(c_re.astype(F32), c_im.astype(F32))
    b_bar = ((lam_bar - 1.0) / lam)[..., None] * b
    t = jnp.arange(CHUNK, dtype=F32)
    half = CHUNK // 2

    def power(k):
        return jnp.exp(lam_dt[:, None, :] * k[None, :, None])

    b_cp = jnp.transpose(b_bar, (0, 2, 1))

    def rows(pw):
        v = b_cp[:, :, None, :] * pw[:, None, :, :]
        v = jnp.concatenate([v.real, v.imag], axis=-1)
        return v.reshape(SSM_G, SSM_CH * CHUNK, 2 * SSM_P)

    def cols(pw):
        v = c[:, :, None, :] * pw[:, None, :, :]
        v = jnp.concatenate([v.real, -v.imag], axis=-1)
        return jnp.transpose(v, (0, 3, 1, 2)).reshape(SSM_G, 2 * SSM_P, SSM_CH * CHUNK)

    a_tab = rows(power(half - t))
    d_tab = cols(power(t - half))
    w_tab = rows(power(CHUNK - 1.0 - t))
    v_tab = cols(power(t + 1.0))
    lev = jnp.asarray([float(CHUNK * 2 ** k) for k in range(SCAN_LEVELS)], F32)
    lc = jnp.exp(lam_dt[:, None, :] * lev[None, :, None])
    l1 = jnp.concatenate([lc.real, lc.real], axis=-1)
    l2 = jnp.concatenate([-lc.imag, lc.imag], axis=-1)
    return (a_tab.astype(BF16), d_tab.astype(BF16), w_tab.astype(BF16), v_tab.astype(BF16), l1, l2)


def _gelu_tanh(y):
    return 0.5 * y * (1.0 + jnp.tanh(0.7978845608028654 * (y + 0.044715 * (y * y * y))))


def _ssm_body(d_ref, u_ref, a_ref, dt_ref, w_ref, v_ref, l1_ref, l2_ref, z_ref, m_scr, *, ncb):
    g = pl.program_id(0)
    nc = u_ref.shape[0]
    width = SSM_CH * CHUNK
    cb = 512
    s_idx = lax.broadcasted_iota(jnp.int32, (width, cb), 0) & (CHUNK - 1)
    t_idx = lax.broadcasted_iota(jnp.int32, (width, cb), 1) & (CHUNK - 1)
    causal = t_idx >= s_idx
    for k in range(width // cb):
        mk = jnp.dot(a_ref[...], dt_ref[:, k * cb:(k + 1) * cb], preferred_element_type=F32)
        m_scr[:, k * cb:(k + 1) * cb] = jnp.where(causal, mk, 0.0).astype(BF16)

    us = [u_ref[:, c, :] for c in range(SSM_CH)]
    x = jnp.concatenate(us, axis=1).astype(BF16)

    acc = jnp.dot(x, w_ref[...], preferred_element_type=F32)
    rmod = lax.broadcasted_iota(jnp.int32, (nc, 2 * SSM_P), 0) & (ncb - 1)
    for k in range(ncb.bit_length() - 1):
        d = 1 << k
        sh = jnp.where(rmod >= d, pltpu.roll(acc, d, 0), 0.0)
        acc = acc + sh * l1_ref[k:k + 1, :] + pltpu.roll(sh, SSM_P, 1) * l2_ref[k:k + 1, :]
    x_in = jnp.where(rmod >= 1, pltpu.roll(acc, 1, 0), 0.0)

    y = jnp.dot(x, m_scr[...], preferred_element_type=F32)
    y = y + jnp.dot(x_in.astype(BF16), v_ref[...], preferred_element_type=F32)
    for c in range(SSM_CH):
        yc = y[:, c * CHUNK:(c + 1) * CHUNK] + d_ref[g * SSM_CH + c] * us[c]
        z_ref[:, c, :] = _gelu_tanh(yc)


def _ssm_scan(u3, d_skip, tables, nbatch):
    a_tab, d_tab, w_tab, v_tab, l1, l2 = tables
    nc = u3.shape[0]
    ncb = nc // nbatch
    assert ncb & (ncb - 1) == 0 and ncb <= 2 ** SCAN_LEVELS
    width = SSM_CH * CHUNK
    grid_spec = pltpu.PrefetchScalarGridSpec(
        num_scalar_prefetch=1,
        grid=(SSM_G,),
        in_specs=[
            pl.BlockSpec((nc, SSM_CH, CHUNK), lambda g, d: (0, g, 0)),
            pl.BlockSpec((None, width, 2 * SSM_P), lambda g, d: (g, 0, 0)),
            pl.BlockSpec((None, 2 * SSM_P, width), lambda g, d: (g, 0, 0)),
            pl.BlockSpec((None, width, 2 * SSM_P), lambda g, d: (g, 0, 0)),
            pl.BlockSpec((None, 2 * SSM_P, width), lambda g, d: (g, 0, 0)),
            pl.BlockSpec((None, SCAN_LEVELS, 2 * SSM_P), lambda g, d: (g, 0, 0)),
            pl.BlockSpec((None, SCAN_LEVELS, 2 * SSM_P), lambda g, d: (g, 0, 0)),
        ],
        out_specs=pl.BlockSpec((nc, SSM_CH, CHUNK), lambda g, d: (0, g, 0)),
        scratch_shapes=[pltpu.VMEM((width, width), BF16)],
    )
    return pl.pallas_call(
        functools.partial(_ssm_body, ncb=ncb),
        grid_spec=grid_spec,
        out_shape=jax.ShapeDtypeStruct((nc, SSM_W, CHUNK), F32),
        compiler_params=_cparams(("arbitrary",)),
        name="ssm_scan",
    )(d_skip.astype(F32), u3, a_tab, d_tab, w_tab, v_tab, l1, l2)


def _merge_body(o0, o1, o2, l0, l1, l2, zt_ref, ga_ref, gs_ref, x_ref,
                wglu_ref, bglu_ref, wab_ref, wsb_ref, wout_ref, out_ref):
    a0, a1, a2 = l0[...], l1[...], l2[...]
    mx = jnp.maximum(jnp.maximum(a0, a1), a2)
    e0, e1, e2 = jnp.exp(a0 - mx), jnp.exp(a1 - mx), jnp.exp(a2 - mx)
    mix = (e0 * o0[...].astype(F32) + e1 * o1[...].astype(F32) + e2 * o2[...].astype(F32)) / (e0 + e1 + e2)
    y_attn = jnp.dot(mix.astype(BF16), wab_ref[...], preferred_element_type=F32)

    z = jnp.concatenate([zt_ref[k].T for k in range(zt_ref.shape[0])], axis=0).astype(BF16)
    gl = jnp.dot(z, wglu_ref[...], preferred_element_type=F32) + bglu_ref[...]
    sg = z.astype(F32) * jax.nn.sigmoid(gl)
    y_ssm = jnp.dot(sg.astype(BF16), wsb_ref[...], preferred_element_type=F32)

    merged = (jax.nn.sigmoid(ga_ref[...].astype(F32)) * y_attn
              + jax.nn.sigmoid(gs_ref[...].astype(F32)) * y_ssm)
    out_ref[...] = x_ref[...] + jnp.dot(merged.astype(BF16), wout_ref[...], preferred_element_type=F32)


def _merge(os_, ls_, zt, proj, x2, wglu, bglu, wab, wsb, wout, tm=512):
    n = x2.shape[0]
    row = lambda i: (i, 0)
    const = lambda i: (0, 0)
    gate_blk = (3 * ATTN_W + SSM_W) // D_MODEL
    in_specs = (
        [pl.BlockSpec((tm, GROUP_W), row)] * 6
        + [
            pl.BlockSpec((tm // CHUNK, SSM_W, CHUNK), lambda i: (i, 0, 0)),
            pl.BlockSpec((tm, D_MODEL), lambda i: (i, gate_blk)),
            pl.BlockSpec((tm, D_MODEL), lambda i: (i, gate_blk + 1)),
            pl.BlockSpec((tm, D_MODEL), row),
            pl.BlockSpec((SSM_W, SSM_W), const),
            pl.BlockSpec((1, SSM_W), const),
            pl.BlockSpec((GROUP_W, D_MODEL), const),
            pl.BlockSpec((SSM_W, D_MODEL), const),
            pl.BlockSpec((D_MODEL, D_MODEL), const),
        ]
    )
    return pl.pallas_call(
        _merge_body,
        grid=(n // tm,),
        in_specs=in_specs,
        out_specs=pl.BlockSpec((tm, D_MODEL), row),
        out_shape=jax.ShapeDtypeStruct((n, D_MODEL), F32),
        compiler_params=_cparams(("parallel",)),
        name="merge",
    )(*os_, *ls_, zt, proj, proj, x2, wglu, bglu, wab, wsb, wout)


def _ffn_body(x_ref, g_ref, wg_ref, wu_ref, wd_ref, o_ref, h_scr, acc_scr):
    j = pl.program_id(1)

    @pl.when(j == 0)
    def _():
        h_scr[...] = _rms(x_ref[...], g_ref[...]).astype(BF16)
        acc_scr[...] = jnp.zeros_like(acc_scr)

    h = h_scr[...]
    a = jnp.dot(h, wg_ref[...], preferred_element_type=F32)
    b = jnp.dot(h, wu_ref[...], preferred_element_type=F32)
    act = (a * jax.nn.sigmoid(a) * b).astype(BF16)
    acc_scr[...] += jnp.dot(act, wd_ref[...], preferred_element_type=F32)

    @pl.when(j == pl.num_programs(1) - 1)
    def _():
        o_ref[...] = x_ref[...] + acc_scr[...]


def _dense_ffn(x2, g, wg, wu, wd, tm=1024):
    n = x2.shape[0]
    dff = wg.shape[1]
    tf = dff // 2 if (dff // 2) % LANES == 0 else dff
    return pl.pallas_call(
        _ffn_body,
        grid=(n // tm, dff // tf),
        in_specs=[
            pl.BlockSpec((tm, D_MODEL), lambda i, j: (i, 0)),
            pl.BlockSpec((1, D_MODEL), lambda i, j: (0, 0)),
            pl.BlockSpec((D_MODEL, tf), lambda i, j: (0, j)),
            pl.BlockSpec((D_MODEL, tf), lambda i, j: (0, j)),
            pl.BlockSpec((tf, D_MODEL), lambda i, j: (j, 0)),
        ],
        out_specs=pl.BlockSpec((tm, D_MODEL), lambda i, j: (i, 0)),
        out_shape=jax.ShapeDtypeStruct((n, D_MODEL), F32),
        scratch_shapes=[pltpu.VMEM((tm, D_MODEL), BF16), pltpu.VMEM((tm, D_MODEL), F32)],
        compiler_params=_cparams(("parallel", "arbitrary")),
        name="dense_ffn",
    )(x2, g, wg, wu, wd)


def _router_body(x_ref, g_ref, wr_ref, idx_ref, gate_ref):
    h = _rms(x_ref[...], g_ref[...])
    logits = jnp.dot(h, wr_ref[...], precision=lax.Precision.HIGHEST, preferred_element_type=F32)
    lane = lax.broadcasted_iota(jnp.int32, logits.shape, 1)
    lane_f = lane.astype(F32)
    logits = jnp.where(lane < N_EXPERTS, logits, -jnp.inf)
    v1 = jnp.max(logits, axis=-1, keepdims=True)
    i1 = jnp.min(jnp.where(logits == v1, lane_f, float(LANES)), axis=-1, keepdims=True)
    rest = jnp.where(lane_f == i1, -jnp.inf, logits)
    v2 = jnp.max(rest, axis=-1, keepdims=True)
    i2 = jnp.min(jnp.where(rest == v2, lane_f, float(LANES)), axis=-1, keepdims=True)
    e = jnp.exp(v2 - v1)
    g1 = 1.0 / (1.0 + e)
    g2 = e / (1.0 + e)
    idx_ref[...] = jnp.where(lane == 0, i1, jnp.where(lane == 1, i2, 0.0)).astype(jnp.int32)
    gate_ref[...] = jnp.where(lane == 0, g1, jnp.where(lane == 1, g2, 0.0))


def _router(x2, g, wr_pad, tm=1024):
    n = x2.shape[0]
    return pl.pallas_call(
        _router_body,
        grid=(n // tm,),
        in_specs=[
            pl.BlockSpec((tm, D_MODEL), lambda i: (i, 0)),
            pl.BlockSpec((1, D_MODEL), lambda i: (0, 0)),
            pl.BlockSpec((D_MODEL, LANES), lambda i: (0, 0)),
        ],
        out_specs=[
            pl.BlockSpec((tm, LANES), lambda i: (i, 0)),
            pl.BlockSpec((tm, LANES), lambda i: (i, 0)),
        ],
        out_shape=[
            jax.ShapeDtypeStruct((n, LANES), jnp.int32),
            jax.ShapeDtypeStruct((n, LANES), F32),
        ],
        compiler_params=_cparams(("parallel",)),
        name="router",
    )(x2, g, wr_pad)


def _pack_bf16_pairs(hb):
    half = hb.shape[1] // 2
    lo = lax.bitcast_convert_type(hb[:, :half].astype(F32), jnp.uint32)
    hi = lax.bitcast_convert_type(hb[:, half:].astype(F32), jnp.uint32)
    return (hi & jnp.uint32(0xFFFF0000)) | (lo >> 16)


def _unpack_bf16_pairs(xu):
    lo = lax.bitcast_convert_type(xu << 16, F32).astype(BF16)
    hi = lax.bitcast_convert_type(xu & jnp.uint32(0xFFFF0000), F32).astype(BF16)
    return lo, hi


def _dispatch_body(slot_ref, x_ref, g_ref, xs_in_ref, xs_ref, pack_scr, sem, *, tm):
    del xs_in_ref
    i = pl.program_id(0)
    pack_scr[...] = _pack_bf16_pairs(_rms(x_ref[...], g_ref[...]).astype(BF16))

    def row_copy(t, k):
        dst = slot_ref[(i * tm + t) * 2 + k]
        return pltpu.make_async_copy(pack_scr.at[pl.ds(t, 1)], xs_ref.at[pl.ds(dst, 1)], sem)

    def issue(t, carry):
        row_copy(t, 0).start()
        row_copy(t, 1).start()
        return carry

    lax.fori_loop(0, tm, issue, 0)

    def drain(t, carry):
        row_copy(t, 0).wait()
        row_copy(t, 1).wait()
        return carry

    lax.fori_loop(0, tm, drain, 0)


def _dispatch(slot, x2, g, n_slots, tm=256):
    n = x2.shape[0]
    xs0 = jnp.zeros((n_slots, D_MODEL // 2), jnp.uint32)
    grid_spec = pltpu.PrefetchScalarGridSpec(
        num_scalar_prefetch=1,
        grid=(n // tm,),
        in_specs=[
            pl.BlockSpec((tm, D_MODEL), lambda i, s: (i, 0)),
            pl.BlockSpec((1, D_MODEL), lambda i, s: (0, 0)),
            pl.BlockSpec(memory_space=pl.ANY),
        ],
        out_specs=pl.BlockSpec(memory_space=pl.ANY),
        scratch_shapes=[pltpu.VMEM((tm, D_MODEL // 2), jnp.uint32), pltpu.SemaphoreType.DMA(())],
    )
    return pl.pallas_call(
        functools.partial(_dispatch_body, tm=tm),
        grid_spec=grid_spec,
        out_shape=jax.ShapeDtypeStruct((n_slots, D_MODEL // 2), jnp.uint32),
        input_output_aliases={3: 0},
        compiler_params=_cparams(("arbitrary",)),
        name="moe_dispatch",
    )(slot, x2, g, xs0)


def _experts_body(be_ref, nu_ref, xs_ref, wg_ref, wu_ref, wd_ref, ys_ref, lo_scr, hi_scr):
    del be_ref
    i = pl.program_id(0)
    j = pl.program_id(1)
    half = D_MODEL // 2

    @pl.when(i < nu_ref[0])
    def _():
        @pl.when(j == 0)
        def _():
            lo, hi = _unpack_bf16_pairs(xs_ref[...])
            lo_scr[...] = lo
            hi_scr[...] = hi
            ys_ref[...] = jnp.zeros_like(ys_ref)

        lo, hi = lo_scr[...], hi_scr[...]
        a = (jnp.dot(lo, wg_ref[:half, :], preferred_element_type=F32)
             + jnp.dot(hi, wg_ref[half:, :], preferred_element_type=F32))
        b = (jnp.dot(lo, wu_ref[:half, :], preferred_element_type=F32)
             + jnp.dot(hi, wu_ref[half:, :], preferred_element_type=F32))
        act = (a * jax.nn.sigmoid(a) * b).astype(BF16)
        ys_ref[...] += jnp.dot(act, wd_ref[...], preferred_element_type=F32)

    @pl.when(jnp.logical_and(i >= nu_ref[0], j == 0))
    def _():
        ys_ref[...] = jnp.zeros_like(ys_ref)


def _experts(block_e, n_used, xs, wg, wu, wd, tf=512):
    n_slots = xs.shape[0]
    n_blocks = n_slots // MOE_BM
    dff = wg.shape[2]
    nf = dff // tf

    def blk(i, nu):
        return jnp.minimum(i, nu[0] - 1)

    def fj(i, j, nu):
        return jnp.where(i < nu[0], j, nf - 1)

    grid_spec = pltpu.PrefetchScalarGridSpec(
        num_scalar_prefetch=2,
        grid=(n_blocks, nf),
        in_specs=[
            pl.BlockSpec((MOE_BM, D_MODEL // 2), lambda i, j, be, nu: (blk(i, nu), 0)),
            pl.BlockSpec((None, D_MODEL, tf), lambda i, j, be, nu: (be[blk(i, nu)], 0, fj(i, j, nu))),
            pl.BlockSpec((None, D_MODEL, tf), lambda i, j, be, nu: (be[blk(i, nu)], 0, fj(i, j, nu))),
            pl.BlockSpec((None, tf, D_MODEL), lambda i, j, be, nu: (be[blk(i, nu)], fj(i, j, nu), 0)),
        ],
        out_specs=pl.BlockSpec((MOE_BM, D_MODEL), lambda i, j, be, nu: (i, 0)),
        scratch_shapes=[pltpu.VMEM((MOE_BM, D_MODEL // 2), BF16)] * 2,
    )
    return pl.pallas_call(
        _experts_body,
        grid_spec=grid_spec,
        out_shape=jax.ShapeDtypeStruct((n_slots, D_MODEL), F32),
        compiler_params=_cparams(("arbitrary", "arbitrary")),
        name="moe_experts",
    )(block_e, n_used, xs, wg, wu, wd)


def _combine_body(slot_ref, x_ref, gate_ref, g_ref, ys_ref, o_ref, y0_scr, y1_scr, sem, *, tm):
    i = pl.program_id(0)

    def row_copy(t, k, dst):
        src = slot_ref[(i * tm + t) * 2 + k]
        return pltpu.make_async_copy(ys_ref.at[pl.ds(src, 1)], dst.at[pl.ds(t, 1)], sem)

    def issue(t, carry):
        row_copy(t, 0, y0_scr).start()
        row_copy(t, 1, y1_scr).start()
        return carry

    lax.fori_loop(0, tm, issue, 0)

    def drain(t, carry):
        row_copy(t, 0, y0_scr).wait()
        row_copy(t, 1, y1_scr).wait()
        return carry

    lax.fori_loop(0, tm, drain, 0)

    gt = gate_ref[...]
    y = x_ref[...] + (gt[:, 0:1] * y0_scr[...] + gt[:, 1:2] * y1_scr[...])
    o_ref[...] = _rms(y, g_ref[...])


def _combine(slot, x2, gates, g_final, ys, tm=256):
    n = x2.shape[0]
    grid_spec = pltpu.PrefetchScalarGridSpec(
        num_scalar_prefetch=1,
        grid=(n // tm,),
        in_specs=[
            pl.BlockSpec((tm, D_MODEL), lambda i, s: (i, 0)),
            pl.BlockSpec((tm, LANES), lambda i, s: (i, 0)),
            pl.BlockSpec((1, D_MODEL), lambda i, s: (0, 0)),
            pl.BlockSpec(memory_space=pl.ANY),
        ],
        out_specs=pl.BlockSpec((tm, D_MODEL), lambda i, s: (i, 0)),
        scratch_shapes=[
            pltpu.VMEM((tm, D_MODEL), F32),
            pltpu.VMEM((tm, D_MODEL), F32),
            pltpu.SemaphoreType.DMA(()),
        ],
    )
    return pl.pallas_call(
        functools.partial(_combine_body, tm=tm),
        grid_spec=grid_spec,
        out_shape=jax.ShapeDtypeStruct((n, D_MODEL), F32),
        compiler_params=_cparams(("arbitrary",)),
        name="moe_combine",
    )(slot, x2, gates, g_final, ys)


def _route_slots(idx, n):
    flat_e = idx[:, :2].reshape(-1)
    onehot = (flat_e[:, None] == jnp.arange(N_EXPERTS, dtype=jnp.int32)[None, :]).astype(jnp.int32)
    incl = jnp.cumsum(onehot, axis=0)
    counts = incl[-1]
    rank = jnp.sum((incl - onehot) * onehot, axis=1)
    padded = ((counts + MOE_BM - 1) // MOE_BM) * MOE_BM
    pad_end = jnp.cumsum(padded)
    pad_start = pad_end - padded
    slot = (jnp.sum(pad_start[None, :] * onehot, axis=1) + rank).astype(jnp.int32)
    n_blocks = (2 * n) // MOE_BM + N_EXPERTS
    starts = jnp.arange(n_blocks, dtype=jnp.int32) * MOE_BM
    block_e = jnp.sum((starts[:, None] >= pad_end[None, :]).astype(jnp.int32), axis=1)
    block_e = jnp.minimum(block_e, N_EXPERTS - 1).astype(jnp.int32)
    n_used = (pad_end[-1] // MOE_BM).astype(jnp.int32).reshape(1)
    return slot, block_e, n_used, n_blocks * MOE_BM


def _mixer_layer(x2, batch, seq, rel_bias, norm_g, w_in, ssm, d_skip, w_glu, b_glu,
                 w_attn_br, w_ssm_br, w_out):
    n = x2.shape[0]
    u_lo = 3 * ATTN_W
    w_bf = w_in.astype(BF16)
    wut_bf = jnp.transpose(w_in[:, u_lo:u_lo + SSM_W]).astype(BF16)
    proj, ut = _in_projection(x2, norm_g.reshape(1, D_MODEL), w_bf, wut_bf)

    os_, ls_ = [], []
    for g, (window, dilation) in enumerate(ATTN_GROUPS):
        bias = _band_bias(rel_bias[:, g * HEADS:(g + 1) * HEADS], window, dilation)
        o, l = _attention_group(proj, bias, g, dilation, batch, seq)
        os_.append(o)
        ls_.append(l)

    zt = _ssm_scan(ut, d_skip, _ssm_tables(*ssm), batch)

    return _merge(os_, ls_, zt, proj, x2, w_glu.astype(BF16), b_glu.reshape(1, SSM_W).astype(F32),
                  w_attn_br.astype(BF16), w_ssm_br.astype(BF16), w_out.astype(BF16))


def kernel(x, rel_bias, norm1_g, w_in, ssm_lam_re, ssm_lam_im, ssm_log_dt, ssm_b_re, ssm_b_im, ssm_c_re, ssm_c_im, ssm_d, w_glu, b_glu, w_attn_br, w_ssm_br, w_out, norm2_g, ffn_w_gate, ffn_w_up, ffn_w_down, moe_router, moe_w_gate, moe_w_up, moe_w_down, final_norm_g):
    batch, seq, d = x.shape
    assert d == D_MODEL and norm1_g.shape[0] == 2 and seq % (16 * BLK) == 0
    n = batch * seq
    x2 = x.reshape(n, d)

    def mixer(x2, l):
        ssm = (ssm_lam_re[l], ssm_lam_im[l], ssm_log_dt[l], ssm_b_re[l], ssm_b_im[l],
               ssm_c_re[l], ssm_c_im[l])
        return _mixer_layer(x2, batch, seq, rel_bias, norm1_g[l], w_in[l], ssm, ssm_d[l], w_glu[l],
                            b_glu[l], w_attn_br[l], w_ssm_br[l], w_out[l])

    x2 = mixer(x2, 0)
    x2 = _dense_ffn(x2, norm2_g[0].reshape(1, d), ffn_w_gate[0].astype(BF16),
                    ffn_w_up[0].astype(BF16), ffn_w_down[0].astype(BF16))

    x2 = mixer(x2, 1)
    g2 = norm2_g[1].reshape(1, d)
    wr_pad = jnp.zeros((d, LANES), F32).at[:, :N_EXPERTS].set(moe_router[0].astype(F32))
    idx, gates = _router(x2, g2, wr_pad)
    slot, block_e, n_used, n_slots = _route_slots(idx, n)
    xs = _dispatch(slot, x2, g2, n_slots)
    ys = _experts(block_e, n_used, xs, moe_w_gate[0].astype(BF16), moe_w_up[0].astype(BF16),
                  moe_w_down[0].astype(BF16))
    out = _combine(slot, x2, gates, final_norm_g.reshape(1, d), ys)
    return out.reshape(batch, seq, d)
```

```python
import functools

import numpy as np
import jax
import jax.numpy as jnp
from jax import lax
from jax.experimental import pallas as pl
from jax.experimental.pallas import tpu as pltpu

F32 = jnp.float32
BF16 = jnp.bfloat16

D_MODEL = 1024
HEAD_DIM = 64
ATTN_GROUPS = ((128, 1), (512, 4), (2048, 16))
N_GROUPS = 3
HEADS = 8
GROUP_W = HEADS * HEAD_DIM
ATTN_W = N_GROUPS * GROUP_W
BLK = 128
REL_BUCKETS = 32
REL_MAX_DIST = 2048
NEG_INF = -1e30
SSM_CH = 16
SSM_W = D_MODEL // 2
SSM_G = SSM_W // SSM_CH
SSM_P = 64
PROJ_W = 3 * ATTN_W + SSM_W + 2 * D_MODEL
N_EXPERTS = 8
MOE_BM = 512
EPS = 1e-6
CHUNK = 128
SCAN_LEVELS = 8

LANES = 128
VMEM_LIMIT = 56 * 1024 * 1024


def _cparams(sem):
    return pltpu.CompilerParams(dimension_semantics=sem, vmem_limit_bytes=VMEM_LIMIT)


def _rms(x, g):
    return x * lax.rsqrt(jnp.mean(x * x, axis=-1, keepdims=True) + EPS) * g


def _proj_body(x_ref, g_ref, w_ref, wut_ref, qkv0_ref, qkv1_ref, qkv2_ref, gate_ref, ut_ref, d_scr):
    tm = x_ref.shape[0]
    hb = _rms(x_ref[...], g_ref[...]).astype(BF16)
    ut = lax.dot_general(wut_ref[...], hb, (((1,), (1,)), ((), ())), preferred_element_type=F32)
    for k in range(ut_ref.shape[0]):
        ut_ref[k] = ut[:, k * CHUNK:(k + 1) * CHUNK]

    def col_block(k):
        return jnp.dot(hb, w_ref[:, k * GROUP_W:(k + 1) * GROUP_W], preferred_element_type=F32)

    uses = 0
    for g, (out_ref, (_, r)) in enumerate(zip((qkv0_ref, qkv1_ref, qkv2_ref), ATTN_GROUPS)):
        for which in range(3):
            res = col_block(3 * g + which)
            if r == 1:
                out_ref[which, 0] = res.astype(BF16)
            else:
                nl = GROUP_W // LANES
                scrs = [d_scr.at[(uses % 2) * nl + k] for k in range(nl)]
                uses += 1
                for k in range(nl):
                    scrs[k][...] = res[:, k * LANES:(k + 1) * LANES]
                for c in range(r):
                    sub = [s[pl.ds(c, tm // r, stride=r), :] for s in scrs]
                    out_ref[which, c] = jnp.concatenate(sub, axis=1).astype(BF16)
    for k in range(2 * D_MODEL // GROUP_W):
        gate_ref[:, k * GROUP_W:(k + 1) * GROUP_W] = col_block(3 * N_GROUPS + k).astype(BF16)


def _in_projection(x2, g, w_bf, wut_bf, batch, seq, tm=512):
    n = x2.shape[0]
    tiles_per_seq = seq // tm
    wcols = w_bf.shape[1]

    def qkv_spec(r):
        return pl.BlockSpec((3, None, r, tm // r, GROUP_W),
                            lambda i: (0, i // tiles_per_seq, 0, i % tiles_per_seq, 0))

    return pl.pallas_call(
        _proj_body,
        grid=(n // tm,),
        in_specs=[
            pl.BlockSpec((tm, D_MODEL), lambda i: (i, 0)),
            pl.BlockSpec((1, D_MODEL), lambda i: (0, 0)),
            pl.BlockSpec((D_MODEL, wcols), lambda i: (0, 0), pipeline_mode=pl.Buffered(1)),
            pl.BlockSpec((SSM_W, D_MODEL), lambda i: (0, 0), pipeline_mode=pl.Buffered(1)),
        ],
        out_specs=[qkv_spec(r) for _, r in ATTN_GROUPS] + [
            pl.BlockSpec((tm, 2 * D_MODEL), lambda i: (i, 0)),
            pl.BlockSpec((tm // CHUNK, SSM_W, CHUNK), lambda i: (i, 0, 0)),
        ],
        out_shape=[jax.ShapeDtypeStruct((3, batch, r, seq // r, GROUP_W), BF16) for _, r in ATTN_GROUPS] + [
            jax.ShapeDtypeStruct((n, 2 * D_MODEL), BF16),
            jax.ShapeDtypeStruct((n // CHUNK, SSM_W, CHUNK), F32),
        ],
        scratch_shapes=[pltpu.VMEM((2 * GROUP_W // LANES, tm, LANES), F32)],
        compiler_params=_cparams(("parallel",)),
        name="in_projection",
    )(x2, g, w_bf, wut_bf)


def _t5_bucket(dist):
    max_exact = REL_BUCKETS // 2
    d = np.maximum(dist, 1).astype(np.float64)
    large = max_exact + (
        np.log(d / max_exact) / np.log(REL_MAX_DIST / max_exact) * (REL_BUCKETS - max_exact)
    ).astype(np.int32)
    large = np.minimum(large, REL_BUCKETS - 1)
    return np.where(dist < max_exact, dist, large).astype(np.int32)


def _band_bias(table, window, dilation):
    steps = window // dilation
    qi = np.arange(BLK)[:, None]
    kj = np.arange(2 * BLK)[None, :]
    delta = BLK + qi - kj
    band = (delta >= 0) & (delta <= steps)
    bucket = _t5_bucket(np.clip(delta, 0, steps) * dilation)
    onehot = np.eye(REL_BUCKETS, dtype=np.float32)[bucket]
    bias = jnp.einsum("qkb,bh->hqk", onehot, table.astype(F32), precision=lax.Precision.HIGHEST)
    return jnp.where(band[None], bias, NEG_INF)


def _attn_body(q_ref, kp_ref, kc_ref, vp_ref, vc_ref, bias_ref, o_ref, l_ref, *, nsub):
    lane = lax.broadcasted_iota(jnp.int32, (BLK, LANES), 1)
    lo = lane < HEAD_DIM
    keep_lo = jnp.where(lo, 1.0, 0.0).astype(BF16)
    keep_hi = jnp.where(lo, 0.0, 1.0).astype(BF16)
    col = lax.broadcasted_iota(jnp.int32, (BLK, 2 * BLK), 1)
    first_pen = jnp.where(col < BLK, jnp.where(pl.program_id(2) == 0, NEG_INF, 0.0), 0.0)
    for i in range(nsub):
        rows = slice(i * BLK, (i + 1) * BLK)
        q = q_ref[rows, :] * jnp.asarray(HEAD_DIM ** -0.5, BF16)
        if i == 0:
            kw = jnp.concatenate([kp_ref[...], kc_ref[0:BLK, :]], axis=0)
            vw = jnp.concatenate([vp_ref[...], vc_ref[0:BLK, :]], axis=0)
        else:
            kw = kc_ref[(i - 1) * BLK:(i + 1) * BLK, :]
            vw = vc_ref[(i - 1) * BLK:(i + 1) * BLK, :]
        for hp in range(HEADS // 2):
            cols = slice(hp * LANES, (hp + 1) * LANES)
            q2, k2, v2 = q[:, cols], kw[:, cols], vw[:, cols]
            outs, lses = [], []
            for half in range(2):
                qm = q2 * (keep_lo if half == 0 else keep_hi)
                s = lax.dot_general(qm, k2, (((1,), (1,)), ((), ())), preferred_element_type=F32)
                s = s + bias_ref[2 * hp + half]
                if i == 0:
                    s = s + first_pen
                m = jnp.max(s, axis=-1, keepdims=True)
                p = jnp.exp(s - m)
                den = jnp.sum(p, axis=-1, keepdims=True)
                pv = jnp.dot(p.astype(BF16), v2, preferred_element_type=F32)
                outs.append(pv / den)
                lses.append(jnp.broadcast_to(m + jnp.log(den), (BLK, LANES)))
            o_ref[rows, cols] = jnp.where(lo, outs[0], outs[1]).astype(BF16)
            l_ref[rows, cols] = jnp.where(lo, lses[0], lses[1])


def _attention_group(qkv, bias, g):
    _, batch, r, length, _ = qkv.shape
    qb = min(512, length)
    nsub = qb // BLK

    def cur(which):
        return pl.BlockSpec((None, None, None, qb, GROUP_W), lambda b, c, n: (which, b, c, n, 0))

    def prev(which):
        return pl.BlockSpec((None, None, None, BLK, GROUP_W),
                            lambda b, c, n: (which, b, c, jnp.maximum(n * nsub - 1, 0), 0))

    out_spec = pl.BlockSpec((None, None, qb, GROUP_W), lambda b, c, n: (b, c, n, 0))
    return pl.pallas_call(
        functools.partial(_attn_body, nsub=nsub),
        grid=(batch, r, length // qb),
        in_specs=[cur(0), prev(1), cur(1), prev(2), cur(2),
                  pl.BlockSpec((HEADS, BLK, 2 * BLK), lambda b, c, n: (0, 0, 0))],
        out_specs=[out_spec, out_spec],
        out_shape=[
            jax.ShapeDtypeStruct((batch, r, length, GROUP_W), BF16),
            jax.ShapeDtypeStruct((batch, r, length, GROUP_W), F32),
        ],
        compiler_params=_cparams(("parallel", "parallel", "arbitrary")),
        name=f"attention_g{g}",
    )(qkv, qkv, qkv, qkv, qkv, bias)


def _ssm_tables(lam_re, lam_im, log_dt, b_re, b_im, c_re, c_im):
    lam = lax.complex(lam_re.astype(F32), lam_im.astype(F32))
    dt = jnp.exp(log_dt.astype(F32))[:, None]
    lam_dt = lam * dt
    lam_bar = jnp.exp(lam_dt)
    b = lax.complex(b_re.astype(F32), b_im.astype(F32))
    c = lax.complex(c_re.astype(F32), c_im.astype(F32))
    b_bar = ((lam_bar - 1.0) / lam)[..., None] * b
    t = jnp.arange(CHUNK, dtype=F32)
    half = CHUNK // 2

    def power(k):
        return jnp.exp(lam_dt[:, None, :] * k[None, :, None])

    b_cp = jnp.transpose(b_bar, (0, 2, 1))

    def rows(pw):
        v = b_cp[:, :, None, :] * pw[:, None, :, :]
        v = jnp.concatenate([v.real, v.imag], axis=-1)
        return v.reshape(SSM_G, SSM_CH * CHUNK, 2 * SSM_P)

    def cols(pw):
        v = c[:, :, None, :] * pw[:, None, :, :]
        v = jnp.concatenate([v.real, -v.imag], axis=-1)
        return jnp.transpose(v, (0, 3, 1, 2)).reshape(SSM_G, 2 * SSM_P, SSM_CH * CHUNK)

    a_tab = rows(power(half - t))
    d_tab = cols(power(t - half))
    w_tab = rows(power(CHUNK - 1.0 - t))
    v_tab = cols(power(t + 1.0))
    lev = jnp.asarray([float(CHUNK * 2 ** k) for k in range(SCAN_LEVELS)], F32)
    lc = jnp.exp(lam_dt[:, None, :] * lev[None, :, None])
    l1 = jnp.concatenate([lc.real, lc.real], axis=-1)
    l2 = jnp.concatenate([-lc.imag, lc.imag], axis=-1)
    return (a_tab.astype(BF16), d_tab.astype(BF16), w_tab.astype(BF16), v_tab.astype(BF16), l1, l2)


def _gelu_tanh(y):
    return 0.5 * y * (1.0 + jnp.tanh(0.7978845608028654 * (y + 0.044715 * (y * y * y))))


def _ssm_body(d_ref, u_ref, a_ref, dt_ref, w_ref, v_ref, l1_ref, l2_ref, z_ref, m_scr, *, ncb):
    g = pl.program_id(0)
    nc = u_ref.shape[0]
    width = SSM_CH * CHUNK
    cb = 512
    s_idx = lax.broadcasted_iota(jnp.int32, (width, cb), 0) & (CHUNK - 1)
    t_idx = lax.broadcasted_iota(jnp.int32, (width, cb), 1) & (CHUNK - 1)
    causal = t_idx >= s_idx
    for k in range(width // cb):
        mk = jnp.dot(a_ref[...], dt_ref[:, k * cb:(k + 1) * cb], preferred_element_type=F32)
        m_scr[:, k * cb:(k + 1) * cb] = jnp.where(causal, mk, 0.0).astype(BF16)

    us = [u_ref[:, c, :] for c in range(SSM_CH)]
    x = jnp.concatenate(us, axis=1).astype(BF16)

    acc = jnp.dot(x, w_ref[...], preferred_element_type=F32)
    rmod = lax.broadcasted_iota(jnp.int32, (nc, 2 * SSM_P), 0) & (ncb - 1)
    for k in range(ncb.bit_length() - 1):
        d = 1 << k
        sh = jnp.where(rmod >= d, pltpu.roll(acc, d, 0), 0.0)
        acc = acc + sh * l1_ref[k:k + 1, :] + pltpu.roll(sh, SSM_P, 1) * l2_ref[k:k + 1, :]
    x_in = jnp.where(rmod >= 1, pltpu.roll(acc, 1, 0), 0.0)

    y = jnp.dot(x, m_scr[...], preferred_element_type=F32)
    y = y + jnp.dot(x_in.astype(BF16), v_ref[...], preferred_element_type=F32)
    for c in range(SSM_CH):
        yc = y[:, c * CHUNK:(c + 1) * CHUNK] + d_ref[g * SSM_CH + c] * us[c]
        z_ref[:, c, :] = _gelu_tanh(yc)


def _ssm_scan(u3, d_skip, tables, nbatch):
    a_tab, d_tab, w_tab, v_tab, l1, l2 = tables
    nc = u3.shape[0]
    ncb = nc // nbatch
    assert ncb & (ncb - 1) == 0 and ncb <= 2 ** SCAN_LEVELS
    width = SSM_CH * CHUNK
    grid_spec = pltpu.PrefetchScalarGridSpec(
        num_scalar_prefetch=1,
        grid=(SSM_G,),
        in_specs=[
            pl.BlockSpec((nc, SSM_CH, CHUNK), lambda g, d: (0, g, 0)),
            pl.BlockSpec((None, width, 2 * SSM_P), lambda g, d: (g, 0, 0)),
            pl.BlockSpec((None, 2 * SSM_P, width), lambda g, d: (g, 0, 0)),
            pl.BlockSpec((None, width, 2 * SSM_P), lambda g, d: (g, 0, 0)),
            pl.BlockSpec((None, 2 * SSM_P, width), lambda g, d: (g, 0, 0)),
            pl.BlockSpec((None, SCAN_LEVELS, 2 * SSM_P), lambda g, d: (g, 0, 0)),
            pl.BlockSpec((None, SCAN_LEVELS, 2 * SSM_P), lambda g, d: (g, 0, 0)),
        ],
        out_specs=pl.BlockSpec((nc, SSM_CH, CHUNK), lambda g, d: (0, g, 0)),
        scratch_shapes=[pltpu.VMEM((width, width), BF16)],
    )
    return pl.pallas_call(
        functools.partial(_ssm_body, ncb=ncb),
        grid_spec=grid_spec,
        out_shape=jax.ShapeDtypeStruct((nc, SSM_W, CHUNK), F32),
        compiler_params=_cparams(("arbitrary",)),
        name="ssm_scan",
    )(d_skip.astype(F32), u3, a_tab, d_tab, w_tab, v_tab, l1, l2)


def _merge_body(o0, o1, o2, l0, l1, l2, zt_ref, ga_ref, gs_ref, x_ref,
                wglu_ref, bglu_ref, wab_ref, wsb_ref, wout_ref, out_ref, tok_scr):
    def token_major(ref, slot):
        r, rows, _ = ref.shape
        if r == 1:
            return ref[0].astype(F32)
        nl = GROUP_W // LANES
        scrs = [tok_scr.at[slot * nl + k] for k in range(nl)]
        for c in range(r):
            sub = ref[c].astype(F32)
            for k in range(nl):
                scrs[k][pl.ds(c, rows, stride=r), :] = sub[:, k * LANES:(k + 1) * LANES]
        return jnp.concatenate([s[...] for s in scrs], axis=1)

    a0, a1, a2 = token_major(l0, 0), token_major(l1, 0), token_major(l2, 1)
    v0, v1, v2 = token_major(o0, 2), token_major(o1, 2), token_major(o2, 3)
    mx = jnp.maximum(jnp.maximum(a0, a1), a2)
    e0, e1, e2 = jnp.exp(a0 - mx), jnp.exp(a1 - mx), jnp.exp(a2 - mx)
    mix = (e0 * v0 + e1 * v1 + e2 * v2) / (e0 + e1 + e2)
    y_attn = jnp.dot(mix.astype(BF16), wab_ref[...], preferred_element_type=F32)

    z = jnp.concatenate([zt_ref[k].T for k in range(zt_ref.shape[0])], axis=0).astype(BF16)
    gl = jnp.dot(z, wglu_ref[...], preferred_element_type=F32) + bglu_ref[...]
    sg = z.astype(F32) * jax.nn.sigmoid(gl)
    y_ssm = jnp.dot(sg.astype(BF16), wsb_ref[...], preferred_element_type=F32)

    merged = (jax.nn.sigmoid(ga_ref[...].astype(F32)) * y_attn
              + jax.nn.sigmoid(gs_ref[...].astype(F32)) * y_ssm)
    out_ref[...] = x_ref[...] + jnp.dot(merged.astype(BF16), wout_ref[...], preferred_element_type=F32)


def _merge(os_, ls_, zt, gates, x2, wglu, bglu, wab, wsb, wout, tm=512):
    n = x2.shape[0]
    tiles_per_seq = os_[0].shape[2] // tm
    row = lambda i: (i, 0)
    const = lambda i: (0, 0)

    def group_spec(a):
        r = a.shape[1]
        return pl.BlockSpec((None, r, tm // r, GROUP_W),
                            lambda i: (i // tiles_per_seq, 0, i % tiles_per_seq, 0))

    in_specs = (
        [group_spec(a) for a in os_] + [group_spec(a) for a in ls_]
        + [
            pl.BlockSpec((tm // CHUNK, SSM_W, CHUNK), lambda i: (i, 0, 0)),
            pl.BlockSpec((tm, D_MODEL), lambda i: (i, 0)),
            pl.BlockSpec((tm, D_MODEL), lambda i: (i, 1)),
            pl.BlockSpec((tm, D_MODEL), row),
            pl.BlockSpec((SSM_W, SSM_W), const),
            pl.BlockSpec((1, SSM_W), const),
            pl.BlockSpec((GROUP_W, D_MODEL), const),
            pl.BlockSpec((SSM_W, D_MODEL), const),
            pl.BlockSpec((D_MODEL, D_MODEL), const),
        ]
    )
    return pl.pallas_call(
        _merge_body,
        grid=(n // tm,),
        in_specs=in_specs,
        out_specs=pl.BlockSpec((tm, D_MODEL), row),
        out_shape=jax.ShapeDtypeStruct((n, D_MODEL), F32),
        scratch_shapes=[pltpu.VMEM((4 * GROUP_W // LANES, tm, LANES), F32)],
        compiler_params=_cparams(("parallel",)),
        name="merge",
    )(*os_, *ls_, zt, gates, gates, x2, wglu, bglu, wab, wsb, wout)


def _ffn_body(x_ref, g_ref, wg_ref, wu_ref, wd_ref, o_ref, h_scr, acc_scr):
    j = pl.program_id(1)

    @pl.when(j == 0)
    def _():
        h_scr[...] = _rms(x_ref[...], g_ref[...]).astype(BF16)
        acc_scr[...] = jnp.zeros_like(acc_scr)

    h = h_scr[...]
    a = jnp.dot(h, wg_ref[...], preferred_element_type=F32)
    b = jnp.dot(h, wu_ref[...], preferred_element_type=F32)
    act = (a * jax.nn.sigmoid(a) * b).astype(BF16)
    acc_scr[...] += jnp.dot(act, wd_ref[...], preferred_element_type=F32)

    @pl.when(j == pl.num_programs(1) - 1)
    def _():
        o_ref[...] = x_ref[...] + acc_scr[...]


def _dense_ffn(x2, g, wg, wu, wd, tm=1024):
    n = x2.shape[0]
    dff = wg.shape[1]
    tf = dff // 2 if (dff // 2) % LANES == 0 else dff
    return pl.pallas_call(
        _ffn_body,
        grid=(n // tm, dff // tf),
        in_specs=[
            pl.BlockSpec((tm, D_MODEL), lambda i, j: (i, 0)),
            pl.BlockSpec((1, D_MODEL), lambda i, j: (0, 0)),
            pl.BlockSpec((D_MODEL, tf), lambda i, j: (0, j)),
            pl.BlockSpec((D_MODEL, tf), lambda i, j: (0, j)),
            pl.BlockSpec((tf, D_MODEL), lambda i, j: (j, 0)),
        ],
        out_specs=pl.BlockSpec((tm, D_MODEL), lambda i, j: (i, 0)),
        out_shape=jax.ShapeDtypeStruct((n, D_MODEL), F32),
        scratch_shapes=[pltpu.VMEM((tm, D_MODEL), BF16), pltpu.VMEM((tm, D_MODEL), F32)],
        compiler_params=_cparams(("parallel", "arbitrary")),
        name="dense_ffn",
    )(x2, g, wg, wu, wd)


def _router_body(x_ref, g_ref, wr_ref, idx_ref, gate_ref):
    h = _rms(x_ref[...], g_ref[...])
    logits = jnp.dot(h, wr_ref[...], precision=lax.Precision.HIGHEST, preferred_element_type=F32)
    lane = lax.broadcasted_iota(jnp.int32, logits.shape, 1)
    lane_f = lane.astype(F32)
    logits = jnp.where(lane < N_EXPERTS, logits, -jnp.inf)
    v1 = jnp.max(logits, axis=-1, keepdims=True)
    i1 = jnp.min(jnp.where(logits == v1, lane_f, float(LANES)), axis=-1, keepdims=True)
    rest = jnp.where(lane_f == i1, -jnp.inf, logits)
    v2 = jnp.max(rest, axis=-1, keepdims=True)
    i2 = jnp.min(jnp.where(rest == v2, lane_f, float(LANES)), axis=-1, keepdims=True)
    e = jnp.exp(v2 - v1)
    g1 = 1.0 / (1.0 + e)
    g2 = e / (1.0 + e)
    idx_ref[...] = jnp.where(lane == 0, i1, jnp.where(lane == 1, i2, 0.0)).astype(jnp.int32)
    gate_ref[...] = jnp.where(lane == 0, g1, jnp.where(lane == 1, g2, 0.0))


def _router(x2, g, wr_pad, tm=1024):
    n = x2.shape[0]
    return pl.pallas_call(
        _router_body,
        grid=(n // tm,),
        in_specs=[
            pl.BlockSpec((tm, D_MODEL), lambda i: (i, 0)),
            pl.BlockSpec((1, D_MODEL), lambda i: (0, 0)),
            pl.BlockSpec((D_MODEL, LANES), lambda i: (0, 0)),
        ],
        out_specs=[
            pl.BlockSpec((tm, LANES), lambda i: (i, 0)),
            pl.BlockSpec((tm, LANES), lambda i: (i, 0)),
        ],
        out_shape=[
            jax.ShapeDtypeStruct((n, LANES), jnp.int32),
            jax.ShapeDtypeStruct((n, LANES), F32),
        ],
        compiler_params=_cparams(("parallel",)),
        name="router",
    )(x2, g, wr_pad)


def _pack_bf16_pairs(hb):
    half = hb.shape[1] // 2
    lo = lax.bitcast_convert_type(hb[:, :half].astype(F32), jnp.uint32)
    hi = lax.bitcast_convert_type(hb[:, half:].astype(F32), jnp.uint32)
    return (hi & jnp.uint32(0xFFFF0000)) | (lo >> 16)


def _unpack_bf16_pairs(xu):
    lo = lax.bitcast_convert_type(xu << 16, F32).astype(BF16)
    hi = lax.bitcast_convert_type(xu & jnp.uint32(0xFFFF0000), F32).astype(BF16)
    return lo, hi


def _dispatch_body(slot_ref, x_ref, g_ref, xs_in_ref, xs_ref, pack_scr, sem, *, tm):
    del xs_in_ref
    i = pl.program_id(0)
    pack_scr[...] = _pack_bf16_pairs(_rms(x_ref[...], g_ref[...]).astype(BF16))

    def row_copy(t, k):
        dst = slot_ref[(i * tm + t) * 2 + k]
        return pltpu.make_async_copy(pack_scr.at[pl.ds(t, 1)], xs_ref.at[pl.ds(dst, 1)], sem)

    def issue(t, carry):
        row_copy(t, 0).start()
        row_copy(t, 1).start()
        return carry

    lax.fori_loop(0, tm, issue, 0, unroll=8)
    pltpu.make_async_copy(xs_ref.at[pl.ds(0, 2 * tm)], xs_ref.at[pl.ds(0, 2 * tm)], sem).wait()


def _dispatch(slot, x2, g, n_slots, tm=256):
    n = x2.shape[0]
    xs0 = jnp.zeros((n_slots, D_MODEL // 2), jnp.uint32)
    grid_spec = pltpu.PrefetchScalarGridSpec(
        num_scalar_prefetch=1,
        grid=(n // tm,),
        in_specs=[
            pl.BlockSpec((tm, D_MODEL), lambda i, s: (i, 0)),
            pl.BlockSpec((1, D_MODEL), lambda i, s: (0, 0)),
            pl.BlockSpec(memory_space=pl.ANY),
        ],
        out_specs=pl.BlockSpec(memory_space=pl.ANY),
        scratch_shapes=[pltpu.VMEM((tm, D_MODEL // 2), jnp.uint32), pltpu.SemaphoreType.DMA(())],
    )
    return pl.pallas_call(
        functools.partial(_dispatch_body, tm=tm),
        grid_spec=grid_spec,
        out_shape=jax.ShapeDtypeStruct((n_slots, D_MODEL // 2), jnp.uint32),
        input_output_aliases={3: 0},
        compiler_params=_cparams(("arbitrary",)),
        name="moe_dispatch",
    )(slot, x2, g, xs0)


def _experts_body(be_ref, nu_ref, xs_ref, wg_ref, wu_ref, wd_ref, ys_ref, *, tf):
    del be_ref
    i = pl.program_id(0)
    half = D_MODEL // 2

    @pl.when(i < nu_ref[0])
    def _():
        lo, hi = _unpack_bf16_pairs(xs_ref[...])
        for f in range(wg_ref.shape[1] // tf):
            cols = slice(f * tf, (f + 1) * tf)
            a = (jnp.dot(lo, wg_ref[:half, cols], preferred_element_type=F32)
                 + jnp.dot(hi, wg_ref[half:, cols], preferred_element_type=F32))
            b = (jnp.dot(lo, wu_ref[:half, cols], preferred_element_type=F32)
                 + jnp.dot(hi, wu_ref[half:, cols], preferred_element_type=F32))
            act = (a * jax.nn.sigmoid(a) * b).astype(BF16)
            part = jnp.dot(act, wd_ref[cols, :], preferred_element_type=F32)
            if f == 0:
                ys_ref[...] = part
            else:
                ys_ref[...] += part

    @pl.when(i >= nu_ref[0])
    def _():
        ys_ref[...] = jnp.zeros_like(ys_ref)


def _experts(block_e, n_used, xs, wg, wu, wd, tf=512):
    n_slots = xs.shape[0]
    n_blocks = n_slots // MOE_BM
    dff = wg.shape[2]

    def blk(i, nu):
        return jnp.minimum(i, nu[0] - 1)

    def wspec(rows, cols):
        return pl.BlockSpec((None, rows, cols), lambda i, be, nu: (be[blk(i, nu)], 0, 0),
                            pipeline_mode=pl.Buffered(1))

    grid_spec = pltpu.PrefetchScalarGridSpec(
        num_scalar_prefetch=2,
        grid=(n_blocks,),
        in_specs=[
            pl.BlockSpec((MOE_BM, D_MODEL // 2), lambda i, be, nu: (blk(i, nu), 0)),
            wspec(D_MODEL, dff),
            wspec(D_MODEL, dff),
            wspec(dff, D_MODEL),
        ],
        out_specs=pl.BlockSpec((MOE_BM, D_MODEL), lambda i, be, nu: (i, 0)),
    )
    return pl.pallas_call(
        functools.partial(_experts_body, tf=tf),
        grid_spec=grid_spec,
        out_shape=jax.ShapeDtypeStruct((n_slots, D_MODEL), F32),
        compiler_params=_cparams(("arbitrary",)),
        name="moe_experts",
    )(block_e, n_used, xs, wg, wu, wd)


def _combine_body(slot_ref, x_ref, gate_ref, g_ref, ys_ref, o_ref, y0_scr, y1_scr, sem, *, tm):
    i = pl.program_id(0)

    def row_copy(t, k, dst):
        src = slot_ref[(i * tm + t) * 2 + k]
        return pltpu.make_async_copy(ys_ref.at[pl.ds(src, 1)], dst.at[pl.ds(t, 1)], sem)

    def issue(t, carry):
        row_copy(t, 0, y0_scr).start()
        row_copy(t, 1, y1_scr).start()
        return carry

    lax.fori_loop(0, tm, issue, 0, unroll=8)
    pltpu.make_async_copy(ys_ref.at[pl.ds(0, tm)], y0_scr, sem).wait()
    pltpu.make_async_copy(ys_ref.at[pl.ds(0, tm)], y1_scr, sem).wait()

    gt = gate_ref[...]
    y = x_ref[...] + (gt[:, 0:1] * y0_scr[...] + gt[:, 1:2] * y1_scr[...])
    o_ref[...] = _rms(y, g_ref[...])


def _combine(slot, x2, gates, g_final, ys, tm=256):
    n = x2.shape[0]
    grid_spec = pltpu.PrefetchScalarGridSpec(
        num_scalar_prefetch=1,
        grid=(n // tm,),
        in_specs=[
            pl.BlockSpec((tm, D_MODEL), lambda i, s: (i, 0)),
            pl.BlockSpec((tm, LANES), lambda i, s: (i, 0)),
            pl.BlockSpec((1, D_MODEL), lambda i, s: (0, 0)),
            pl.BlockSpec(memory_space=pl.ANY),
        ],
        out_specs=pl.BlockSpec((tm, D_MODEL), lambda i, s: (i, 0)),
        scratch_shapes=[
            pltpu.VMEM((tm, D_MODEL), F32),
            pltpu.VMEM((tm, D_MODEL), F32),
            pltpu.SemaphoreType.DMA(()),
        ],
    )
    return pl.pallas_call(
        functools.partial(_combine_body, tm=tm),
        grid_spec=grid_spec,
        out_shape=jax.ShapeDtypeStruct((n, D_MODEL), F32),
        compiler_params=_cparams(("arbitrary",)),
        name="moe_combine",
    )(slot, x2, gates, g_final, ys)


def _route_slots(idx, n):
    flat_e = idx[:, :2].reshape(-1)
    onehot = (flat_e[:, None] == jnp.arange(N_EXPERTS, dtype=jnp.int32)[None, :]).astype(jnp.int32)
    incl = jnp.cumsum(onehot, axis=0)
    counts = incl[-1]
    rank = jnp.sum((incl - onehot) * onehot, axis=1)
    padded = ((counts + MOE_BM - 1) // MOE_BM) * MOE_BM
    pad_end = jnp.cumsum(padded)
    pad_start = pad_end - padded
    slot = (jnp.sum(pad_start[None, :] * onehot, axis=1) + rank).astype(jnp.int32)
    n_blocks = (2 * n) // MOE_BM + N_EXPERTS
    starts = jnp.arange(n_blocks, dtype=jnp.int32) * MOE_BM
    block_e = jnp.sum((starts[:, None] >= pad_end[None, :]).astype(jnp.int32), axis=1)
    block_e = jnp.minimum(block_e, N_EXPERTS - 1).astype(jnp.int32)
    n_used = (pad_end[-1] // MOE_BM).astype(jnp.int32).reshape(1)
    return slot, block_e, n_used, n_blocks * MOE_BM


def _mixer_layer(x2, batch, seq, rel_bias, norm_g, w_in, ssm, d_skip, w_glu, b_glu,
                 w_attn_br, w_ssm_br, w_out):
    u_lo = 3 * ATTN_W
    blocks = [w_in[:, which * ATTN_W + g * GROUP_W: which * ATTN_W + (g + 1) * GROUP_W]
              for g in range(N_GROUPS) for which in range(3)]
    w_bf = jnp.concatenate(blocks + [w_in[:, u_lo + SSM_W:]], axis=1).astype(BF16)
    wut_bf = jnp.transpose(w_in[:, u_lo:u_lo + SSM_W]).astype(BF16)
    *qkvs, gates, ut = _in_projection(x2, norm_g.reshape(1, D_MODEL), w_bf, wut_bf, batch, seq)

    os_, ls_ = [], []
    for g, (window, dilation) in enumerate(ATTN_GROUPS):
        bias = _band_bias(rel_bias[:, g * HEADS:(g + 1) * HEADS], window, dilation)
        o, l = _attention_group(qkvs[g], bias, g)
        os_.append(o)
        ls_.append(l)

    zt = _ssm_scan(ut, d_skip, _ssm_tables(*ssm), batch)

    return _merge(os_, ls_, zt, gates, x2, w_glu.astype(BF16), b_glu.reshape(1, SSM_W).astype(F32),
                  w_attn_br.astype(BF16), w_ssm_br.astype(BF16), w_out.astype(BF16))


def kernel(x, rel_bias, norm1_g, w_in, ssm_lam_re, ssm_lam_im, ssm_log_dt, ssm_b_re, ssm_b_im, ssm_c_re, ssm_c_im, ssm_d, w_glu, b_glu, w_attn_br, w_ssm_br, w_out, norm2_g, ffn_w_gate, ffn_w_up, ffn_w_down, moe_router, moe_w_gate, moe_w_up, moe_w_down, final_norm_g):
    batch, seq, d = x.shape
    assert d == D_MODEL and norm1_g.shape[0] == 2 and seq % (16 * BLK) == 0
    n = batch * seq
    x2 = x.reshape(n, d)

    def mixer(x2, l):
        ssm = (ssm_lam_re[l], ssm_lam_im[l], ssm_log_dt[l], ssm_b_re[l], ssm_b_im[l],
               ssm_c_re[l], ssm_c_im[l])
        return _mixer_layer(x2, batch, seq, rel_bias, norm1_g[l], w_in[l], ssm, ssm_d[l], w_glu[l],
                            b_glu[l], w_attn_br[l], w_ssm_br[l], w_out[l])

    x2 = mixer(x2, 0)
    x2 = _dense_ffn(x2, norm2_g[0].reshape(1, d), ffn_w_gate[0].astype(BF16),
                    ffn_w_up[0].astype(BF16), ffn_w_down[0].astype(BF16))

    x2 = mixer(x2, 1)
    g2 = norm2_g[1].reshape(1, d)
    wr_pad = jnp.zeros((d, LANES), F32).at[:, :N_EXPERTS].set(moe_router[0].astype(F32))
    idx, gates = _router(x2, g2, wr_pad)
    slot, block_e, n_used, n_slots = _route_slots(idx, n)
    xs = _dispatch(slot, x2, g2, n_slots)
    ys = _experts(block_e, n_used, xs, moe_w_gate[0].astype(BF16), moe_w_up[0].astype(BF16),
                  moe_w_down[0].astype(BF16))
    out = _combine(slot, x2, gates, final_norm_g.reshape(1, d), ys)
    return out.reshape(batch, seq, d)
```

```python
import functools

import numpy as np
import jax
import jax.numpy as jnp
from jax import lax
from jax.experimental import pallas as pl
from jax.experimental.pallas import tpu as pltpu

F32 = jnp.float32
BF16 = jnp.bfloat16

D_MODEL = 1024
HEAD_DIM = 64
ATTN_GROUPS = ((128, 1), (512, 4), (2048, 16))
N_GROUPS = 3
HEADS = 8
GROUP_W = HEADS * HEAD_DIM
ATTN_W = N_GROUPS * GROUP_W
BLK = 128
REL_BUCKETS = 32
REL_MAX_DIST = 2048
NEG_INF = -1e30
SSM_CH = 16
SSM_W = D_MODEL // 2
SSM_G = SSM_W // SSM_CH
SSM_P = 64
PROJ_W = 3 * ATTN_W + SSM_W + 2 * D_MODEL
N_EXPERTS = 8
MOE_BM = 512
EPS = 1e-6
CHUNK = 128
SCAN_LEVELS = 8

LANES = 128
VMEM_LIMIT = 56 * 1024 * 1024


def _cparams(sem):
    return pltpu.CompilerParams(dimension_semantics=sem, vmem_limit_bytes=VMEM_LIMIT)


def _rms(x, g):
    return x * lax.rsqrt(jnp.mean(x * x, axis=-1, keepdims=True) + EPS) * g


def _proj_body(x_ref, g_ref, w_ref, wut_ref, qkv0_ref, qkv1_ref, qkv2_ref, gate_ref, ut_ref, d_scr):
    tm = x_ref.shape[0]
    hb = _rms(x_ref[...], g_ref[...]).astype(BF16)
    ut = lax.dot_general(wut_ref[...], hb, (((1,), (1,)), ((), ())), preferred_element_type=F32)
    for k in range(ut_ref.shape[0]):
        ut_ref[k] = ut[:, k * CHUNK:(k + 1) * CHUNK]

    def col_block(k):
        return jnp.dot(hb, w_ref[:, k * GROUP_W:(k + 1) * GROUP_W], preferred_element_type=F32)

    uses = 0
    for g, (out_ref, (_, r)) in enumerate(zip((qkv0_ref, qkv1_ref, qkv2_ref), ATTN_GROUPS)):
        for which in range(3):
            res = col_block(3 * g + which)
            if r == 1:
                out_ref[which, 0] = res.astype(BF16)
            else:
                nl = GROUP_W // LANES
                scrs = [d_scr.at[(uses % 2) * nl + k] for k in range(nl)]
                uses += 1
                for k in range(nl):
                    scrs[k][...] = res[:, k * LANES:(k + 1) * LANES]
                for c in range(r):
                    sub = [s[pl.ds(c, tm // r, stride=r), :] for s in scrs]
                    out_ref[which, c] = jnp.concatenate(sub, axis=1).astype(BF16)
    for k in range(2 * D_MODEL // GROUP_W):
        gate_ref[:, k * GROUP_W:(k + 1) * GROUP_W] = col_block(3 * N_GROUPS + k).astype(BF16)


def _in_projection(x2, g, w_bf, wut_bf, batch, seq, tm=512):
    n = x2.shape[0]
    tiles_per_seq = seq // tm
    wcols = w_bf.shape[1]

    def qkv_spec(r):
        return pl.BlockSpec((3, None, r, tm // r, GROUP_W),
                            lambda i: (0, i // tiles_per_seq, 0, i % tiles_per_seq, 0))

    return pl.pallas_call(
        _proj_body,
        grid=(n // tm,),
        in_specs=[
            pl.BlockSpec((tm, D_MODEL), lambda i: (i, 0)),
            pl.BlockSpec((1, D_MODEL), lambda i: (0, 0)),
            pl.BlockSpec((D_MODEL, wcols), lambda i: (0, 0), pipeline_mode=pl.Buffered(1)),
            pl.BlockSpec((SSM_W, D_MODEL), lambda i: (0, 0), pipeline_mode=pl.Buffered(1)),
        ],
        out_specs=[qkv_spec(r) for _, r in ATTN_GROUPS] + [
            pl.BlockSpec((tm, 2 * D_MODEL), lambda i: (i, 0)),
            pl.BlockSpec((tm // CHUNK, SSM_W, CHUNK), lambda i: (i, 0, 0)),
        ],
        out_shape=[jax.ShapeDtypeStruct((3, batch, r, seq // r, GROUP_W), BF16) for _, r in ATTN_GROUPS] + [
            jax.ShapeDtypeStruct((n, 2 * D_MODEL), BF16),
            jax.ShapeDtypeStruct((n // CHUNK, SSM_W, CHUNK), F32),
        ],
        scratch_shapes=[pltpu.VMEM((2 * GROUP_W // LANES, tm, LANES), F32)],
        compiler_params=_cparams(("parallel",)),
        name="in_projection",
    )(x2, g, w_bf, wut_bf)


def _t5_bucket(dist):
    max_exact = REL_BUCKETS // 2
    d = np.maximum(dist, 1).astype(np.float64)
    large = max_exact + (
        np.log(d / max_exact) / np.log(REL_MAX_DIST / max_exact) * (REL_BUCKETS - max_exact)
    ).astype(np.int32)
    large = np.minimum(large, REL_BUCKETS - 1)
    return np.where(dist < max_exact, dist, large).astype(np.int32)


def _band_bias(table, window, dilation):
    steps = window // dilation
    qi = np.arange(BLK)[:, None]
    kj = np.arange(2 * BLK)[None, :]
    delta = BLK + qi - kj
    band = (delta >= 0) & (delta <= steps)
    bucket = _t5_bucket(np.clip(delta, 0, steps) * dilation)
    onehot = np.eye(REL_BUCKETS, dtype=np.float32)[bucket]
    bias = jnp.einsum("qkb,bh->hqk", onehot, table.astype(F32), precision=lax.Precision.HIGHEST)
    return jnp.where(band[None], bias, NEG_INF)


def _attn_body(q_ref, kp_ref, kc_ref, vp_ref, vc_ref, bias_ref, o_ref, l_ref, *, nsub):
    lane = lax.broadcasted_iota(jnp.int32, (BLK, LANES), 1)
    lo = lane < HEAD_DIM
    keep_lo = jnp.where(lo, 1.0, 0.0).astype(BF16)
    keep_hi = jnp.where(lo, 0.0, 1.0).astype(BF16)
    col = lax.broadcasted_iota(jnp.int32, (BLK, 2 * BLK), 1)
    first_pen = jnp.where(col < BLK, jnp.where(pl.program_id(2) == 0, NEG_INF, 0.0), 0.0)
    for i in range(nsub):
        rows = slice(i * BLK, (i + 1) * BLK)
        q = q_ref[rows, :] * jnp.asarray(HEAD_DIM ** -0.5, BF16)
        if i == 0:
            kw = jnp.concatenate([kp_ref[...], kc_ref[0:BLK, :]], axis=0)
            vw = jnp.concatenate([vp_ref[...], vc_ref[0:BLK, :]], axis=0)
        else:
            kw = kc_ref[(i - 1) * BLK:(i + 1) * BLK, :]
            vw = vc_ref[(i - 1) * BLK:(i + 1) * BLK, :]
        for hp in range(HEADS // 2):
            cols = slice(hp * LANES, (hp + 1) * LANES)
            q2, k2, v2 = q[:, cols], kw[:, cols], vw[:, cols]
            outs, lses = [], []
            for half in range(2):
                qm = q2 * (keep_lo, keep_hi)[half]
                s = lax.dot_general(qm, k2, (((1,), (1,)), ((), ())), preferred_element_type=F32)
                s = s + bias_ref[2 * hp + half]
                if i == 0:
                    s = s + first_pen
                m = jnp.max(s, axis=-1, keepdims=True)
                p = jnp.exp(s - m)
                den = jnp.sum(p, axis=-1, keepdims=True)
                pv = jnp.dot(p.astype(BF16), v2, preferred_element_type=F32)
                outs.append(pv / den)
                lses.append(jnp.broadcast_to(m + jnp.log(den), (BLK, LANES)))
            o_ref[rows, cols] = jnp.where(lo, outs[0], outs[1]).astype(BF16)
            l_ref[rows, cols] = jnp.where(lo, lses[0], lses[1])


def _attention_group(qkv, bias, g):
    _, batch, r, length, _ = qkv.shape
    qb = min(512, length)
    nsub = qb // BLK

    def cur(which):
        return pl.BlockSpec((None, None, None, qb, GROUP_W), lambda b, c, n: (which, b, c, n, 0))

    def prev(which):
        return pl.BlockSpec((None, None, None, BLK, GROUP_W),
                            lambda b, c, n: (which, b, c, jnp.maximum(n * nsub - 1, 0), 0))

    out_spec = pl.BlockSpec((None, None, qb, GROUP_W), lambda b, c, n: (b, c, n, 0))
    return pl.pallas_call(
        functools.partial(_attn_body, nsub=nsub),
        grid=(batch, r, length // qb),
        in_specs=[cur(0), prev(1), cur(1), prev(2), cur(2),
                  pl.BlockSpec((HEADS, BLK, 2 * BLK), lambda b, c, n: (0, 0, 0))],
        out_specs=[out_spec, out_spec],
        out_shape=[
            jax.ShapeDtypeStruct((batch, r, length, GROUP_W), BF16),
            jax.ShapeDtypeStruct((batch, r, length, GROUP_W), F32),
        ],
        compiler_params=_cparams(("parallel", "parallel", "arbitrary")),
        name=f"attention_g{g}",
    )(qkv, qkv, qkv, qkv, qkv, bias)


def _ssm_tables(lam_re, lam_im, log_dt, b_re, b_im, c_re, c_im):
    lam = lax.complex(lam_re.astype(F32), lam_im.astype(F32))
    dt = jnp.exp(log_dt.astype(F32))[:, None]
    lam_dt = lam * dt
    lam_bar = jnp.exp(lam_dt)
    b = lax.complex(b_re.astype(F32), b_im.astype(F32))
    c = lax.complex(c_re.astype(F32), c_im.astype(F32))
    b_bar = ((lam_bar - 1.0) / lam)[..., None] * b
    t = jnp.arange(CHUNK, dtype=F32)
    half = CHUNK // 2

    def power(k):
        return jnp.exp(lam_dt[:, None, :] * k[None, :, None])

    b_cp = jnp.transpose(b_bar, (0, 2, 1))

    def rows(pw):
        v = b_cp[:, :, None, :] * pw[:, None, :, :]
        v = jnp.concatenate([v.real, v.imag], axis=-1)
        return v.reshape(SSM_G, SSM_CH * CHUNK, 2 * SSM_P)

    def cols(pw):
        v = c[:, :, None, :] * pw[:, None, :, :]
        v = jnp.concatenate([v.real, -v.imag], axis=-1)
        return jnp.transpose(v, (0, 3, 1, 2)).reshape(SSM_G, 2 * SSM_P, SSM_CH * CHUNK)

    a_tab = rows(power(half - t))
    d_tab = cols(power(t - half))
    w_tab = rows(power(CHUNK - 1.0 - t))
    v_tab = cols(power(t + 1.0))
    lev = jnp.asarray([float(CHUNK * 2 ** k) for k in range(SCAN_LEVELS)], F32)
    lc = jnp.exp(lam_dt[:, None, :] * lev[None, :, None])
    l1 = jnp.concatenate([lc.real, lc.real], axis=-1)
    l2 = jnp.concatenate([-lc.imag, lc.imag], axis=-1)
    return (a_tab.astype(BF16), d_tab.astype(BF16), w_tab.astype(BF16), v_tab.astype(BF16), l1, l2)


def _gelu_tanh(y):
    return 0.5 * y * (1.0 + jnp.tanh(0.7978845608028654 * (y + 0.044715 * (y * y * y))))


def _ssm_body(d_ref, u_ref, a_ref, dt_ref, w_ref, v_ref, l1_ref, l2_ref, z_ref, m_scr, *, ncb):
    g = pl.program_id(0)
    nc = u_ref.shape[0]
    width = SSM_CH * CHUNK
    cb = 512
    s_idx = lax.broadcasted_iota(jnp.int32, (width, cb), 0) & (CHUNK - 1)
    t_idx = lax.broadcasted_iota(jnp.int32, (width, cb), 1) & (CHUNK - 1)
    causal = t_idx >= s_idx
    for k in range(width // cb):
        mk = jnp.dot(a_ref[...], dt_ref[:, k * cb:(k + 1) * cb], preferred_element_type=F32)
        m_scr[:, k * cb:(k + 1) * cb] = jnp.where(causal, mk, 0.0).astype(BF16)

    us = [u_ref[:, c, :] for c in range(SSM_CH)]
    x = jnp.concatenate(us, axis=1).astype(BF16)

    acc = jnp.dot(x, w_ref[...], preferred_element_type=F32)
    rmod = lax.broadcasted_iota(jnp.int32, (nc, 2 * SSM_P), 0) & (ncb - 1)
    for k in range(ncb.bit_length() - 1):
        d = 1 << k
        sh = jnp.where(rmod >= d, pltpu.roll(acc, d, 0), 0.0)
        acc = acc + sh * l1_ref[k:k + 1, :] + pltpu.roll(sh, SSM_P, 1) * l2_ref[k:k + 1, :]
    x_in = jnp.where(rmod >= 1, pltpu.roll(acc, 1, 0), 0.0)

    y = jnp.dot(x, m_scr[...], preferred_element_type=F32)
    y = y + jnp.dot(x_in.astype(BF16), v_ref[...], preferred_element_type=F32)
    for c in range(SSM_CH):
        yc = y[:, c * CHUNK:(c + 1) * CHUNK] + d_ref[g * SSM_CH + c] * us[c]
        z_ref[:, c, :] = _gelu_tanh(yc)


def _ssm_scan(u3, d_skip, tables, nbatch):
    a_tab, d_tab, w_tab, v_tab, l1, l2 = tables
    nc = u3.shape[0]
    ncb = nc // nbatch
    assert ncb & (ncb - 1) == 0 and ncb <= 2 ** SCAN_LEVELS
    width = SSM_CH * CHUNK
    grid_spec = pltpu.PrefetchScalarGridSpec(
        num_scalar_prefetch=1,
        grid=(SSM_G,),
        in_specs=[
            pl.BlockSpec((nc, SSM_CH, CHUNK), lambda g, d: (0, g, 0)),
            pl.BlockSpec((None, width, 2 * SSM_P), lambda g, d: (g, 0, 0)),
            pl.BlockSpec((None, 2 * SSM_P, width), lambda g, d: (g, 0, 0)),
            pl.BlockSpec((None, width, 2 * SSM_P), lambda g, d: (g, 0, 0)),
            pl.BlockSpec((None, 2 * SSM_P, width), lambda g, d: (g, 0, 0)),
            pl.BlockSpec((None, SCAN_LEVELS, 2 * SSM_P), lambda g, d: (g, 0, 0)),
            pl.BlockSpec((None, SCAN_LEVELS, 2 * SSM_P), lambda g, d: (g, 0, 0)),
        ],
        out_specs=pl.BlockSpec((nc, SSM_CH, CHUNK), lambda g, d: (0, g, 0)),
        scratch_shapes=[pltpu.VMEM((width, width), BF16)],
    )
    return pl.pallas_call(
        functools.partial(_ssm_body, ncb=ncb),
        grid_spec=grid_spec,
        out_shape=jax.ShapeDtypeStruct((nc, SSM_W, CHUNK), F32),
        compiler_params=_cparams(("arbitrary",)),
        name="ssm_scan",
    )(d_skip.astype(F32), u3, a_tab, d_tab, w_tab, v_tab, l1, l2)


def _merge_body(o0, o1, o2, l0, l1, l2, zt_ref, ga_ref, gs_ref, x_ref,
                wglu_ref, bglu_ref, wab_ref, wsb_ref, wout_ref, out_ref, tok_scr):
    def token_major(ref, slot):
        r, rows, _ = ref.shape
        if r == 1:
            return ref[0].astype(F32)
        nl = GROUP_W // LANES
        scrs = [tok_scr.at[slot * nl + k] for k in range(nl)]
        for c in range(r):
            sub = ref[c].astype(F32)
            for k in range(nl):
                scrs[k][pl.ds(c, rows, stride=r), :] = sub[:, k * LANES:(k + 1) * LANES]
        return jnp.concatenate([s[...] for s in scrs], axis=1)

    a0, a1, a2 = token_major(l0, 0), token_major(l1, 0), token_major(l2, 1)
    v0, v1, v2 = token_major(o0, 2), token_major(o1, 2), token_major(o2, 3)
    mx = jnp.maximum(jnp.maximum(a0, a1), a2)
    e0, e1, e2 = jnp.exp(a0 - mx), jnp.exp(a1 - mx), jnp.exp(a2 - mx)
    mix = (e0 * v0 + e1 * v1 + e2 * v2) / (e0 + e1 + e2)
    y_attn = jnp.dot(mix.astype(BF16), wab_ref[...], preferred_element_type=F32)

    z = jnp.concatenate([zt_ref[k].T for k in range(zt_ref.shape[0])], axis=0).astype(BF16)
    gl = jnp.dot(z, wglu_ref[...], preferred_element_type=F32) + bglu_ref[...]
    sg = z.astype(F32) * jax.nn.sigmoid(gl)
    y_ssm = jnp.dot(sg.astype(BF16), wsb_ref[...], preferred_element_type=F32)

    merged = (jax.nn.sigmoid(ga_ref[...].astype(F32)) * y_attn
              + jax.nn.sigmoid(gs_ref[...].astype(F32)) * y_ssm)
    out_ref[...] = x_ref[...] + jnp.dot(merged.astype(BF16), wout_ref[...], preferred_element_type=F32)


def _merge(os_, ls_, zt, gates, x2, wglu, bglu, wab, wsb, wout, tm=512):
    n = x2.shape[0]
    tiles_per_seq = os_[0].shape[2] // tm
    row = lambda i: (i, 0)
    const = lambda i: (0, 0)

    def group_spec(a):
        r = a.shape[1]
        return pl.BlockSpec((None, r, tm // r, GROUP_W),
                            lambda i: (i // tiles_per_seq, 0, i % tiles_per_seq, 0))

    in_specs = (
        [group_spec(a) for a in os_] + [group_spec(a) for a in ls_]
        + [
            pl.BlockSpec((tm // CHUNK, SSM_W, CHUNK), lambda i: (i, 0, 0)),
            pl.BlockSpec((tm, D_MODEL), lambda i: (i, 0)),
            pl.BlockSpec((tm, D_MODEL), lambda i: (i, 1)),
            pl.BlockSpec((tm, D_MODEL), row),
            pl.BlockSpec((SSM_W, SSM_W), const),
            pl.BlockSpec((1, SSM_W), const),
            pl.BlockSpec((GROUP_W, D_MODEL), const),
            pl.BlockSpec((SSM_W, D_MODEL), const),
            pl.BlockSpec((D_MODEL, D_MODEL), const),
        ]
    )
    return pl.pallas_call(
        _merge_body,
        grid=(n // tm,),
        in_specs=in_specs,
        out_specs=pl.BlockSpec((tm, D_MODEL), row),
        out_shape=jax.ShapeDtypeStruct((n, D_MODEL), F32),
        scratch_shapes=[pltpu.VMEM((4 * GROUP_W // LANES, tm, LANES), F32)],
        compiler_params=_cparams(("parallel",)),
        name="merge",
    )(*os_, *ls_, zt, gates, gates, x2, wglu, bglu, wab, wsb, wout)


def _ffn_body(x_ref, g_ref, wg_ref, wu_ref, wd_ref, o_ref, *, tf):
    h = _rms(x_ref[...], g_ref[...]).astype(BF16)
    for f in range(wg_ref.shape[1] // tf):
        cols = slice(f * tf, (f + 1) * tf)
        a = jnp.dot(h, wg_ref[:, cols], preferred_element_type=F32)
        b = jnp.dot(h, wu_ref[:, cols], preferred_element_type=F32)
        act = (a * jax.nn.sigmoid(a) * b).astype(BF16)
        part = jnp.dot(act, wd_ref[cols, :], preferred_element_type=F32)
        if f == 0:
            o_ref[...] = x_ref[...] + part
        else:
            o_ref[...] += part


def _dense_ffn(x2, g, wg, wu, wd, tm=512, tf=256):
    n = x2.shape[0]
    dff = wg.shape[1]
    assert dff % tf == 0
    resident = lambda shape: pl.BlockSpec(shape, lambda i: (0, 0), pipeline_mode=pl.Buffered(1))
    return pl.pallas_call(
        functools.partial(_ffn_body, tf=tf),
        grid=(n // tm,),
        in_specs=[
            pl.BlockSpec((tm, D_MODEL), lambda i: (i, 0)),
            pl.BlockSpec((1, D_MODEL), lambda i: (0, 0)),
            resident((D_MODEL, dff)),
            resident((D_MODEL, dff)),
            resident((dff, D_MODEL)),
        ],
        out_specs=pl.BlockSpec((tm, D_MODEL), lambda i: (i, 0)),
        out_shape=jax.ShapeDtypeStruct((n, D_MODEL), F32),
        compiler_params=_cparams(("parallel",)),
        name="dense_ffn",
    )(x2, g, wg, wu, wd)


def _router_body(x_ref, g_ref, wr_ref, idx_ref, gate_ref):
    h = _rms(x_ref[...], g_ref[...])
    logits = jnp.dot(h, wr_ref[...], precision=lax.Precision.HIGHEST, preferred_element_type=F32)
    lane = lax.broadcasted_iota(jnp.int32, logits.shape, 1)
    lane_f = lane.astype(F32)
    logits = jnp.where(lane < N_EXPERTS, logits, -jnp.inf)
    v1 = jnp.max(logits, axis=-1, keepdims=True)
    i1 = jnp.min(jnp.where(logits == v1, lane_f, float(LANES)), axis=-1, keepdims=True)
    rest = jnp.where(lane_f == i1, -jnp.inf, logits)
    v2 = jnp.max(rest, axis=-1, keepdims=True)
    i2 = jnp.min(jnp.where(rest == v2, lane_f, float(LANES)), axis=-1, keepdims=True)
    e = jnp.exp(v2 - v1)
    g1 = 1.0 / (1.0 + e)
    g2 = e / (1.0 + e)
    idx_ref[...] = jnp.where(lane == 0, i1, jnp.where(lane == 1, i2, 0.0)).astype(jnp.int32)
    gate_ref[...] = jnp.where(lane == 0, g1, jnp.where(lane == 1, g2, 0.0))


def _router(x2, g, wr_pad, tm=1024):
    n = x2.shape[0]
    return pl.pallas_call(
        _router_body,
        grid=(n // tm,),
        in_specs=[
            pl.BlockSpec((tm, D_MODEL), lambda i: (i, 0)),
            pl.BlockSpec((1, D_MODEL), lambda i: (0, 0)),
            pl.BlockSpec((D_MODEL, LANES), lambda i: (0, 0)),
        ],
        out_specs=[
            pl.BlockSpec((tm, LANES), lambda i: (i, 0)),
            pl.BlockSpec((tm, LANES), lambda i: (i, 0)),
        ],
        out_shape=[
            jax.ShapeDtypeStruct((n, LANES), jnp.int32),
            jax.ShapeDtypeStruct((n, LANES), F32),
        ],
        compiler_params=_cparams(("parallel",)),
        name="router",
    )(x2, g, wr_pad)


def _pack_bf16_pairs(hb):
    half = hb.shape[1] // 2
    lo = lax.bitcast_convert_type(hb[:, :half].astype(F32), jnp.uint32)
    hi = lax.bitcast_convert_type(hb[:, half:].astype(F32), jnp.uint32)
    return (hi & jnp.uint32(0xFFFF0000)) | (lo >> 16)


def _unpack_bf16_pairs(xu):
    lo = lax.bitcast_convert_type(xu << 16, F32).astype(BF16)
    hi = lax.bitcast_convert_type(xu & jnp.uint32(0xFFFF0000), F32).astype(BF16)
    return lo, hi


def _dispatch_body(slot_ref, tail_ref, x_ref, g_ref, xs_ref, pack_scr, zero_scr, sem, zsem, *, tm):
    i = pl.program_id(0)

    @pl.when(i == 0)
    def _():
        zero_scr[...] = jnp.zeros_like(zero_scr)
        def fill(e):
            row = pl.multiple_of(jnp.maximum(tail_ref[e], 0), MOE_BM)
            return pltpu.make_async_copy(zero_scr, xs_ref.at[pl.ds(row, MOE_BM)], zsem)

        for e in range(tail_ref.shape[0]):
            pl.when(tail_ref[e] >= 0)(lambda e=e: fill(e).start())
        for e in range(tail_ref.shape[0]):
            pl.when(tail_ref[e] >= 0)(lambda e=e: fill(e).wait())

    pack_scr[...] = _pack_bf16_pairs(_rms(x_ref[...], g_ref[...]).astype(BF16))

    def row_copy(t, k):
        dst = slot_ref[(i * tm + t) * 2 + k]
        return pltpu.make_async_copy(pack_scr.at[pl.ds(t, 1)], xs_ref.at[pl.ds(dst, 1)], sem)

    def issue(t, carry):
        row_copy(t, 0).start()
        row_copy(t, 1).start()
        return carry

    lax.fori_loop(0, tm, issue, 0, unroll=8)
    pltpu.make_async_copy(xs_ref.at[pl.ds(0, 2 * tm)], xs_ref.at[pl.ds(0, 2 * tm)], sem).wait()


def _dispatch(slot, tail_rows, x2, g, n_slots, tm=256):
    n = x2.shape[0]
    grid_spec = pltpu.PrefetchScalarGridSpec(
        num_scalar_prefetch=2,
        grid=(n // tm,),
        in_specs=[
            pl.BlockSpec((tm, D_MODEL), lambda i, s, z: (i, 0)),
            pl.BlockSpec((1, D_MODEL), lambda i, s, z: (0, 0)),
        ],
        out_specs=pl.BlockSpec(memory_space=pl.ANY),
        scratch_shapes=[
            pltpu.VMEM((tm, D_MODEL // 2), jnp.uint32),
            pltpu.VMEM((MOE_BM, D_MODEL // 2), jnp.uint32),
            pltpu.SemaphoreType.DMA(()),
            pltpu.SemaphoreType.DMA(()),
        ],
    )
    return pl.pallas_call(
        functools.partial(_dispatch_body, tm=tm),
        grid_spec=grid_spec,
        out_shape=jax.ShapeDtypeStruct((n_slots, D_MODEL // 2), jnp.uint32),
        compiler_params=_cparams(("arbitrary",)),
        name="moe_dispatch",
    )(slot, tail_rows, x2, g)


def _experts_body(be_ref, nu_ref, xs_ref, wg_ref, wu_ref, wd_ref, ys_ref, *, tf):
    del be_ref
    i = pl.program_id(0)
    half = D_MODEL // 2

    @pl.when(i < nu_ref[0])
    def _():
        lo, hi = _unpack_bf16_pairs(xs_ref[...])
        for f in range(wg_ref.shape[1] // tf):
            cols = slice(f * tf, (f + 1) * tf)
            a = (jnp.dot(lo, wg_ref[:half, cols], preferred_element_type=F32)
                 + jnp.dot(hi, wg_ref[half:, cols], preferred_element_type=F32))
            b = (jnp.dot(lo, wu_ref[:half, cols], preferred_element_type=F32)
                 + jnp.dot(hi, wu_ref[half:, cols], preferred_element_type=F32))
            act = (a * jax.nn.sigmoid(a) * b).astype(BF16)
            part = jnp.dot(act, wd_ref[cols, :], preferred_element_type=F32)
            if f == 0:
                ys_ref[...] = part
            else:
                ys_ref[...] += part

    @pl.when(i >= nu_ref[0])
    def _():
        ys_ref[...] = jnp.zeros_like(ys_ref)


def _experts(block_e, n_used, xs, wg, wu, wd, tf=512):
    n_slots = xs.shape[0]
    n_blocks = n_slots // MOE_BM
    dff = wg.shape[2]

    def blk(i, nu):
        return jnp.minimum(i, nu[0] - 1)

    def wspec(rows, cols):
        return pl.BlockSpec((None, rows, cols), lambda i, be, nu: (be[blk(i, nu)], 0, 0),
                            pipeline_mode=pl.Buffered(1))

    grid_spec = pltpu.PrefetchScalarGridSpec(
        num_scalar_prefetch=2,
        grid=(n_blocks,),
        in_specs=[
            pl.BlockSpec((MOE_BM, D_MODEL // 2), lambda i, be, nu: (blk(i, nu), 0)),
            wspec(D_MODEL, dff),
            wspec(D_MODEL, dff),
            wspec(dff, D_MODEL),
        ],
        out_specs=pl.BlockSpec((MOE_BM, D_MODEL), lambda i, be, nu: (i, 0)),
    )
    return pl.pallas_call(
        functools.partial(_experts_body, tf=tf),
        grid_spec=grid_spec,
        out_shape=jax.ShapeDtypeStruct((n_slots, D_MODEL), F32),
        compiler_params=_cparams(("arbitrary",)),
        name="moe_experts",
    )(block_e, n_used, xs, wg, wu, wd)


def _combine_body(slot_ref, x_ref, gate_ref, g_ref, ys_ref, o_ref, y0_scr, y1_scr, sem, *, tm):
    i = pl.program_id(0)

    def row_copy(t, k, dst):
        src = slot_ref[(i * tm + t) * 2 + k]
        return pltpu.make_async_copy(ys_ref.at[pl.ds(src, 1)], dst.at[pl.ds(t, 1)], sem)

    def issue(t, carry):
        row_copy(t, 0, y0_scr).start()
        row_copy(t, 1, y1_scr).start()
        return carry

    lax.fori_loop(0, tm, issue, 0, unroll=8)
    pltpu.make_async_copy(ys_ref.at[pl.ds(0, tm)], y0_scr, sem).wait()
    pltpu.make_async_copy(ys_ref.at[pl.ds(0, tm)], y1_scr, sem).wait()

    gt = gate_ref[...]
    y = x_ref[...] + (gt[:, 0:1] * y0_scr[...] + gt[:, 1:2] * y1_scr[...])
    o_ref[...] = _rms(y, g_ref[...])


def _combine(slot, x2, gates, g_final, ys, tm=256):
    n = x2.shape[0]
    grid_spec = pltpu.PrefetchScalarGridSpec(
        num_scalar_prefetch=1,
        grid=(n // tm,),
        in_specs=[
            pl.BlockSpec((tm, D_MODEL), lambda i, s: (i, 0)),
            pl.BlockSpec((tm, LANES), lambda i, s: (i, 0)),
            pl.BlockSpec((1, D_MODEL), lambda i, s: (0, 0)),
            pl.BlockSpec(memory_space=pl.ANY),
        ],
        out_specs=pl.BlockSpec((tm, D_MODEL), lambda i, s: (i, 0)),
        scratch_shapes=[
            pltpu.VMEM((tm, D_MODEL), F32),
            pltpu.VMEM((tm, D_MODEL), F32),
            pltpu.SemaphoreType.DMA(()),
        ],
    )
    return pl.pallas_call(
        functools.partial(_combine_body, tm=tm),
        grid_spec=grid_spec,
        out_shape=jax.ShapeDtypeStruct((n, D_MODEL), F32),
        compiler_params=_cparams(("arbitrary",)),
        name="moe_combine",
    )(slot, x2, gates, g_final, ys)


def _route_slots(idx, n):
    flat_e = idx[:, :2].reshape(-1)
    onehot = (flat_e[:, None] == jnp.arange(N_EXPERTS, dtype=jnp.int32)[None, :]).astype(jnp.int32)
    incl = jnp.cumsum(onehot, axis=0)
    counts = incl[-1]
    rank = jnp.sum((incl - onehot) * onehot, axis=1)
    padded = ((counts + MOE_BM - 1) // MOE_BM) * MOE_BM
    pad_end = jnp.cumsum(padded)
    pad_start = pad_end - padded
    slot = (jnp.sum(pad_start[None, :] * onehot, axis=1) + rank).astype(jnp.int32)
    n_blocks = (2 * n) // MOE_BM + N_EXPERTS
    starts = jnp.arange(n_blocks, dtype=jnp.int32) * MOE_BM
    block_e = jnp.sum((starts[:, None] >= pad_end[None, :]).astype(jnp.int32), axis=1)
    block_e = jnp.minimum(block_e, N_EXPERTS - 1).astype(jnp.int32)
    n_used = (pad_end[-1] // MOE_BM).astype(jnp.int32).reshape(1)
    tails = jnp.where(padded > 0, pad_end - MOE_BM, -1)
    spare = pad_end[-1] + jnp.arange(N_EXPERTS, dtype=pad_end.dtype) * MOE_BM
    spare = jnp.where(spare < n_blocks * MOE_BM, spare, -1)
    tail_rows = jnp.concatenate([tails, spare]).astype(jnp.int32)
    return slot, block_e, n_used, tail_rows, n_blocks * MOE_BM


def _mixer_layer(x2, batch, seq, rel_bias, norm_g, w_in, ssm, d_skip, w_glu, b_glu,
                 w_attn_br, w_ssm_br, w_out):
    u_lo = 3 * ATTN_W
    blocks = [w_in[:, which * ATTN_W + g * GROUP_W: which * ATTN_W + (g + 1) * GROUP_W]
              for g in range(N_GROUPS) for which in range(3)]
    w_bf = jnp.concatenate(blocks + [w_in[:, u_lo + SSM_W:]], axis=1).astype(BF16)
    wut_bf = jnp.transpose(w_in[:, u_lo:u_lo + SSM_W]).astype(BF16)
    *qkvs, gates, ut = _in_projection(x2, norm_g.reshape(1, D_MODEL), w_bf, wut_bf, batch, seq)

    os_, ls_ = [], []
    for g, (window, dilation) in enumerate(ATTN_GROUPS):
        bias = _band_bias(rel_bias[:, g * HEADS:(g + 1) * HEADS], window, dilation)
        o, l = _attention_group(qkvs[g], bias, g)
        os_.append(o)
        ls_.append(l)

    zt = _ssm_scan(ut, d_skip, _ssm_tables(*ssm), batch)

    return _merge(os_, ls_, zt, gates, x2, w_glu.astype(BF16), b_glu.reshape(1, SSM_W).astype(F32),
                  w_attn_br.astype(BF16), w_ssm_br.astype(BF16), w_out.astype(BF16))


def kernel(x, rel_bias, norm1_g, w_in, ssm_lam_re, ssm_lam_im, ssm_log_dt, ssm_b_re, ssm_b_im, ssm_c_re, ssm_c_im, ssm_d, w_glu, b_glu, w_attn_br, w_ssm_br, w_out, norm2_g, ffn_w_gate, ffn_w_up, ffn_w_down, moe_router, moe_w_gate, moe_w_up, moe_w_down, final_norm_g):
    batch, seq, d = x.shape
    assert d == D_MODEL and norm1_g.shape[0] == 2 and seq % (16 * BLK) == 0
    n = batch * seq
    x2 = x.reshape(n, d)

    def mixer(x2, l):
        ssm = (ssm_lam_re[l], ssm_lam_im[l], ssm_log_dt[l], ssm_b_re[l], ssm_b_im[l],
               ssm_c_re[l], ssm_c_im[l])
        return _mixer_layer(x2, batch, seq, rel_bias, norm1_g[l], w_in[l], ssm, ssm_d[l], w_glu[l],
                            b_glu[l], w_attn_br[l], w_ssm_br[l], w_out[l])

    x2 = mixer(x2, 0)
    x2 = _dense_ffn(x2, norm2_g[0].reshape(1, d), ffn_w_gate[0].astype(BF16),
                    ffn_w_up[0].astype(BF16), ffn_w_down[0].astype(BF16))

    x2 = mixer(x2, 1)
    g2 = norm2_g[1].reshape(1, d)
    wr_pad = jnp.zeros((d, LANES), F32).at[:, :N_EXPERTS].set(moe_router[0].astype(F32))
    idx, gates = _router(x2, g2, wr_pad)
    slot, block_e, n_used, tail_rows, n_slots = _route_slots(idx, n)
    xs = _dispatch(slot, tail_rows, x2, g2, n_slots)
    ys = _experts(block_e, n_used, xs, moe_w_gate[0].astype(BF16), moe_w_up[0].astype(BF16),
                  moe_w_down[0].astype(BF16))
    out = _combine(slot, x2, gates, final_norm_g.reshape(1, d), ys)
    return out.reshape(batch, seq, d)
```

```python
import functools

import numpy as np
import jax
import jax.numpy as jnp
from jax import lax
from jax.experimental import pallas as pl
from jax.experimental.pallas import tpu as pltpu

F32 = jnp.float32
BF16 = jnp.bfloat16

D_MODEL = 1024
HEAD_DIM = 64
ATTN_GROUPS = ((128, 1), (512, 4), (2048, 16))
N_GROUPS = 3
HEADS = 8
GROUP_W = HEADS * HEAD_DIM
ATTN_W = N_GROUPS * GROUP_W
BLK = 128
REL_BUCKETS = 32
REL_MAX_DIST = 2048
NEG_INF = -1e30
SSM_CH = 16
SSM_W = D_MODEL // 2
SSM_G = SSM_W // SSM_CH
SSM_P = 64
PROJ_W = 3 * ATTN_W + SSM_W + 2 * D_MODEL
N_EXPERTS = 8
MOE_BM = 512
EPS = 1e-6
CHUNK = 128
SCAN_LEVELS = 8

LANES = 128
VMEM_LIMIT = 56 * 1024 * 1024


def _cparams(sem):
    return pltpu.CompilerParams(dimension_semantics=sem, vmem_limit_bytes=VMEM_LIMIT)


def _rms(x, g):
    return x * lax.rsqrt(jnp.mean(x * x, axis=-1, keepdims=True) + EPS) * g


def _proj_body(x_ref, g_ref, w_ref, wut_ref, qkv0_ref, qkv1_ref, qkv2_ref, gate_ref, ut_ref, d_scr):
    tm = x_ref.shape[0]
    hb = _rms(x_ref[...], g_ref[...]).astype(BF16)
    ut = lax.dot_general(wut_ref[...], hb, (((1,), (1,)), ((), ())), preferred_element_type=F32)
    for k in range(ut_ref.shape[0]):
        ut_ref[k] = ut[:, k * CHUNK:(k + 1) * CHUNK]

    cw = 2 * LANES
    nl = cw // LANES

    def slab(col):
        return jnp.dot(hb, w_ref[:, col:col + cw], preferred_element_type=F32)

    uses = 0
    for g, (out_ref, (_, r)) in enumerate(zip((qkv0_ref, qkv1_ref, qkv2_ref), ATTN_GROUPS)):
        for which in range(3):
            for lo in range(0, GROUP_W, cw):
                res = slab((3 * g + which) * GROUP_W + lo)
                if r == 1:
                    out_ref[which, 0, :, lo:lo + cw] = res.astype(BF16)
                    continue
                scrs = [d_scr.at[(uses % 2) * nl + k] for k in range(nl)]
                uses += 1
                for k in range(nl):
                    scrs[k][...] = res[:, k * LANES:(k + 1) * LANES]
                for c in range(r):
                    sub = [s[pl.ds(c, tm // r, stride=r), :] for s in scrs]
                    out_ref[which, c, :, lo:lo + cw] = jnp.concatenate(sub, axis=1).astype(BF16)
    for lo in range(0, 2 * D_MODEL, cw):
        gate_ref[:, lo:lo + cw] = slab(3 * N_GROUPS * GROUP_W + lo).astype(BF16)


def _in_projection(x2, g, w_bf, wut_bf, batch, seq, tm=512):
    n = x2.shape[0]
    tiles_per_seq = seq // tm
    wcols = w_bf.shape[1]

    def qkv_spec(r):
        return pl.BlockSpec((3, None, r, tm // r, GROUP_W),
                            lambda i: (0, i // tiles_per_seq, 0, i % tiles_per_seq, 0))

    return pl.pallas_call(
        _proj_body,
        grid=(n // tm,),
        in_specs=[
            pl.BlockSpec((tm, D_MODEL), lambda i: (i, 0)),
            pl.BlockSpec((1, D_MODEL), lambda i: (0, 0)),
            pl.BlockSpec((D_MODEL, wcols), lambda i: (0, 0), pipeline_mode=pl.Buffered(1)),
            pl.BlockSpec((SSM_W, D_MODEL), lambda i: (0, 0), pipeline_mode=pl.Buffered(1)),
        ],
        out_specs=[qkv_spec(r) for _, r in ATTN_GROUPS] + [
            pl.BlockSpec((tm, 2 * D_MODEL), lambda i: (i, 0)),
            pl.BlockSpec((tm // CHUNK, SSM_W, CHUNK), lambda i: (i, 0, 0)),
        ],
        out_shape=[jax.ShapeDtypeStruct((3, batch, r, seq // r, GROUP_W), BF16) for _, r in ATTN_GROUPS] + [
            jax.ShapeDtypeStruct((n, 2 * D_MODEL), BF16),
            jax.ShapeDtypeStruct((n // CHUNK, SSM_W, CHUNK), F32),
        ],
        scratch_shapes=[pltpu.VMEM((4, tm, LANES), F32)],
        compiler_params=_cparams(("parallel",)),
        name="in_projection",
    )(x2, g, w_bf, wut_bf)


def _t5_bucket(dist):
    max_exact = REL_BUCKETS // 2
    d = np.maximum(dist, 1).astype(np.float64)
    large = max_exact + (
        np.log(d / max_exact) / np.log(REL_MAX_DIST / max_exact) * (REL_BUCKETS - max_exact)
    ).astype(np.int32)
    large = np.minimum(large, REL_BUCKETS - 1)
    return np.where(dist < max_exact, dist, large).astype(np.int32)


def _band_bias(table, window, dilation):
    steps = window // dilation
    qi = np.arange(BLK)[:, None]
    kj = np.arange(2 * BLK)[None, :]
    delta = BLK + qi - kj
    band = (delta >= 0) & (delta <= steps)
    bucket = _t5_bucket(np.clip(delta, 0, steps) * dilation)
    onehot = np.eye(REL_BUCKETS, dtype=np.float32)[bucket]
    bias = jnp.einsum("qkb,bh->hqk", onehot, table.astype(F32), precision=lax.Precision.HIGHEST)
    return jnp.where(band[None], bias, NEG_INF)


def _attn_body(q_ref, kp_ref, kc_ref, vp_ref, vc_ref, bias_ref, o_ref, l_ref, *, nsub):
    lane = lax.broadcasted_iota(jnp.int32, (BLK, LANES), 1)
    lo = lane < HEAD_DIM
    keep_lo = jnp.where(lo, 1.0, 0.0).astype(BF16)
    keep_hi = jnp.where(lo, 0.0, 1.0).astype(BF16)
    col = lax.broadcasted_iota(jnp.int32, (BLK, 2 * BLK), 1)
    first_pen = jnp.where(col < BLK, jnp.where(pl.program_id(2) == 0, NEG_INF, 0.0), 0.0)
    for i in range(nsub):
        rows = slice(i * BLK, (i + 1) * BLK)
        q = q_ref[rows, :] * jnp.asarray(HEAD_DIM ** -0.5, BF16)
        if i == 0:
            kw = jnp.concatenate([kp_ref[...], kc_ref[0:BLK, :]], axis=0)
            vw = jnp.concatenate([vp_ref[...], vc_ref[0:BLK, :]], axis=0)
        else:
            kw = kc_ref[(i - 1) * BLK:(i + 1) * BLK, :]
            vw = vc_ref[(i - 1) * BLK:(i + 1) * BLK, :]
        for hp in range(HEADS // 2):
            cols = slice(hp * LANES, (hp + 1) * LANES)
            q2, k2, v2 = q[:, cols], kw[:, cols], vw[:, cols]
            outs, lses = [], []
            for half in range(2):
                qm = q2 * (keep_lo, keep_hi)[half]
                s = lax.dot_general(qm, k2, (((1,), (1,)), ((), ())), preferred_element_type=F32)
                s = s + bias_ref[2 * hp + half]
                if i == 0:
                    s = s + first_pen
                m = jnp.max(s, axis=-1, keepdims=True)
                p = jnp.exp(s - m)
                den = jnp.sum(p, axis=-1, keepdims=True)
                pv = jnp.dot(p.astype(BF16), v2, preferred_element_type=F32)
                outs.append(pv / den)
                lses.append(jnp.broadcast_to(m + jnp.log(den), (BLK, LANES)))
            o_ref[rows, cols] = jnp.where(lo, outs[0], outs[1]).astype(BF16)
            l_ref[rows, cols] = jnp.where(lo, lses[0], lses[1])


def _attention_group(qkv, bias, g):
    _, batch, r, length, _ = qkv.shape
    qb = min(512, length)
    nsub = qb // BLK

    def cur(which):
        return pl.BlockSpec((None, None, None, qb, GROUP_W), lambda b, c, n: (which, b, c, n, 0))

    def prev(which):
        return pl.BlockSpec((None, None, None, BLK, GROUP_W),
                            lambda b, c, n: (which, b, c, jnp.maximum(n * nsub - 1, 0), 0))

    out_spec = pl.BlockSpec((None, None, qb, GROUP_W), lambda b, c, n: (b, c, n, 0))
    return pl.pallas_call(
        functools.partial(_attn_body, nsub=nsub),
        grid=(batch, r, length // qb),
        in_specs=[cur(0), prev(1), cur(1), prev(2), cur(2),
                  pl.BlockSpec((HEADS, BLK, 2 * BLK), lambda b, c, n: (0, 0, 0))],
        out_specs=[out_spec, out_spec],
        out_shape=[
            jax.ShapeDtypeStruct((batch, r, length, GROUP_W), BF16),
            jax.ShapeDtypeStruct((batch, r, length, GROUP_W), F32),
        ],
        compiler_params=_cparams(("parallel", "parallel", "arbitrary")),
        name=f"attention_g{g}",
    )(qkv, qkv, qkv, qkv, qkv, bias)


def _ssm_tables(lam_re, lam_im, log_dt, b_re, b_im, c_re, c_im):
    lam = lax.complex(lam_re.astype(F32), lam_im.astype(F32))
    dt = jnp.exp(log_dt.astype(F32))[:, None]
    lam_dt = lam * dt
    lam_bar = jnp.exp(lam_dt)
    b = lax.complex(b_re.astype(F32), b_im.astype(F32))
    c = lax.complex(c_re.astype(F32), c_im.astype(F32))
    b_bar = ((lam_bar - 1.0) / lam)[..., None] * b
    t = jnp.arange(CHUNK, dtype=F32)
    half = CHUNK // 2

    def power(k):
        return jnp.exp(lam_dt[:, None, :] * k[None, :, None])

    b_cp = jnp.transpose(b_bar, (0, 2, 1))

    def rows(pw):
        v = b_cp[:, :, None, :] * pw[:, None, :, :]
        v = jnp.concatenate([v.real, v.imag], axis=-1)
        return v.reshape(SSM_G, SSM_CH * CHUNK, 2 * SSM_P)

    def cols(pw):
        v = c[:, :, None, :] * pw[:, None, :, :]
        v = jnp.concatenate([v.real, -v.imag], axis=-1)
        return jnp.transpose(v, (0, 3, 1, 2)).reshape(SSM_G, 2 * SSM_P, SSM_CH * CHUNK)

    a_tab = rows(power(half - t))
    d_tab = cols(power(t - half))
    w_tab = rows(power(CHUNK - 1.0 - t))
    v_tab = cols(power(t + 1.0))
    lev = jnp.asarray([float(CHUNK * 2 ** k) for k in range(SCAN_LEVELS)], F32)
    lc = jnp.exp(lam_dt[:, None, :] * lev[None, :, None])
    l1 = jnp.concatenate([lc.real, lc.real], axis=-1)
    l2 = jnp.concatenate([-lc.imag, lc.imag], axis=-1)
    return (a_tab.astype(BF16), d_tab.astype(BF16), w_tab.astype(BF16), v_tab.astype(BF16), l1, l2)


def _gelu_tanh(y):
    return y * jax.nn.sigmoid(1.5957691216057308 * (y + 0.044715 * (y * y * y)))


def _ssm_body(d_ref, u_ref, a_ref, dt_ref, w_ref, v_ref, l1_ref, l2_ref, z_ref, m_scr, mask_scr, z_scr, *,
              ncb):
    g = pl.program_id(0)
    nc = u_ref.shape[0]
    width = SSM_CH * CHUNK
    cb = mask_scr.shape[1]

    @pl.when(g == 0)
    def _():
        s_idx = lax.broadcasted_iota(jnp.int32, (width, cb), 0) & (CHUNK - 1)
        t_idx = lax.broadcasted_iota(jnp.int32, (width, cb), 1) & (CHUNK - 1)
        mask_scr[...] = jnp.where(t_idx >= s_idx, -1, 0).astype(jnp.int32)

    for k in range(width // cb):
        mk = jnp.dot(a_ref[...], dt_ref[:, k * cb:(k + 1) * cb], preferred_element_type=F32)
        kept = lax.bitcast_convert_type(mk, jnp.int32) & mask_scr[...]
        m_scr[:, k * cb:(k + 1) * cb] = lax.bitcast_convert_type(kept, F32).astype(BF16)

    u2 = u_ref.reshape(nc * SSM_CH, CHUNK)
    us =[u2[pl.ds(c, nc, stride=SSM_CH), :] for c in range(SSM_CH)]
    x = jnp.concatenate(us, axis=1).astype(BF16)

    acc = jnp.dot(x, w_ref[...], preferred_element_type=F32)
    rmod = lax.broadcasted_iota(jnp.int32, (nc, 2 * SSM_P), 0) & (ncb - 1)
    for k in range(ncb.bit_length() - 1):
        d = 1 << k
        sh = jnp.where(rmod >= d, pltpu.roll(acc, d, 0), 0.0)
        acc = acc + sh * l1_ref[k:k + 1, :] + pltpu.roll(sh, SSM_P, 1) * l2_ref[k:k + 1, :]
    x_in = jnp.where(rmod >= 1, pltpu.roll(acc, 1, 0), 0.0)

    y = jnp.dot(x, m_scr[...], preferred_element_type=F32)
    y = y + jnp.dot(x_in.astype(BF16), v_ref[...], preferred_element_type=F32)
    for c in range(SSM_CH):
        yc = y[:, c * CHUNK:(c + 1) * CHUNK] + d_ref[g * SSM_CH + c] * us[c]
        z_scr[pl.ds(c, nc, stride=SSM_CH), :] = _gelu_tanh(yc)
    z_ref[...] = z_scr[...].reshape(nc, SSM_CH, CHUNK)


def _ssm_scan(u3, d_skip, tables, nbatch):
    a_tab, d_tab, w_tab, v_tab, l1, l2 = tables
    nc = u3.shape[0]
    ncb = nc // nbatch
    assert ncb & (ncb - 1) == 0 and ncb <= 2 ** SCAN_LEVELS
    width = SSM_CH * CHUNK
    grid_spec = pltpu.PrefetchScalarGridSpec(
        num_scalar_prefetch=1,
        grid=(SSM_G,),
        in_specs=[
            pl.BlockSpec((nc, SSM_CH, CHUNK), lambda g, d: (0, g, 0)),
            pl.BlockSpec((None, width, 2 * SSM_P), lambda g, d: (g, 0, 0)),
            pl.BlockSpec((None, 2 * SSM_P, width), lambda g, d: (g, 0, 0)),
            pl.BlockSpec((None, width, 2 * SSM_P), lambda g, d: (g, 0, 0)),
            pl.BlockSpec((None, 2 * SSM_P, width), lambda g, d: (g, 0, 0)),
            pl.BlockSpec((None, SCAN_LEVELS, 2 * SSM_P), lambda g, d: (g, 0, 0)),
            pl.BlockSpec((None, SCAN_LEVELS, 2 * SSM_P), lambda g, d: (g, 0, 0)),
        ],
        out_specs=pl.BlockSpec((nc, SSM_CH, CHUNK), lambda g, d: (0, g, 0)),
        scratch_shapes=[pltpu.VMEM((width, width), BF16), pltpu.VMEM((width, 512), jnp.int32),
                        pltpu.VMEM((nc * SSM_CH, CHUNK), F32)],
    )
    return pl.pallas_call(
        functools.partial(_ssm_body, ncb=ncb),
        grid_spec=grid_spec,
        out_shape=jax.ShapeDtypeStruct((nc, SSM_W, CHUNK), F32),
        compiler_params=_cparams(("arbitrary",)),
        name="ssm_scan",
    )(d_skip.astype(F32), u3, a_tab, d_tab, w_tab, v_tab, l1, l2)


def _merge_body(o0, o1, o2, l0, l1, l2, zt_ref, ga_ref, gs_ref, x_ref,
                wglu_ref, bglu_ref, wab_ref, wsb_ref, wout_ref, out_ref, tok_scr):
    def token_major(ref, slot):
        r, rows, _ = ref.shape
        if r == 1:
            return ref[0].astype(F32)
        nl = GROUP_W // LANES
        scrs = [tok_scr.at[slot * nl + k] for k in range(nl)]
        for c in range(r):
            sub = ref[c].astype(F32)
            for k in range(nl):
                scrs[k][pl.ds(c, rows, stride=r), :] = sub[:, k * LANES:(k + 1) * LANES]
        return jnp.concatenate([s[...] for s in scrs], axis=1)

    a0, a1, a2 = token_major(l0, 0), token_major(l1, 0), token_major(l2, 1)
    v0, v1, v2 = token_major(o0, 2), token_major(o1, 2), token_major(o2, 3)
    mx = jnp.maximum(jnp.maximum(a0, a1), a2)
    e0, e1, e2 = jnp.exp(a0 - mx), jnp.exp(a1 - mx), jnp.exp(a2 - mx)
    mix = (e0 * v0 + e1 * v1 + e2 * v2) / (e0 + e1 + e2)
    y_attn = jnp.dot(mix.astype(BF16), wab_ref[...], preferred_element_type=F32)

    z = jnp.concatenate([zt_ref[k].T for k in range(zt_ref.shape[0])], axis=0).astype(BF16)
    gl = jnp.dot(z, wglu_ref[...], preferred_element_type=F32) + bglu_ref[...]
    sg = z.astype(F32) * jax.nn.sigmoid(gl)
    y_ssm = jnp.dot(sg.astype(BF16), wsb_ref[...], preferred_element_type=F32)

    merged = (jax.nn.sigmoid(ga_ref[...].astype(F32)) * y_attn
              + jax.nn.sigmoid(gs_ref[...].astype(F32)) * y_ssm)
    out_ref[...] = x_ref[...] + jnp.dot(merged.astype(BF16), wout_ref[...], preferred_element_type=F32)


def _merge(os_, ls_, zt, gates, x2, wglu, bglu, wab, wsb, wout, tm=512):
    n = x2.shape[0]
    tiles_per_seq = os_[0].shape[2] // tm
    row = lambda i: (i, 0)
    const = lambda i: (0, 0)

    def group_spec(a):
        r = a.shape[1]
        return pl.BlockSpec((None, r, tm // r, GROUP_W),
                            lambda i: (i // tiles_per_seq, 0, i % tiles_per_seq, 0))

    in_specs = (
        [group_spec(a) for a in os_] + [group_spec(a) for a in ls_]
        + [
            pl.BlockSpec((tm // CHUNK, SSM_W, CHUNK), lambda i: (i, 0, 0)),
            pl.BlockSpec((tm, D_MODEL), lambda i: (i, 0)),
            pl.BlockSpec((tm, D_MODEL), lambda i: (i, 1)),
            pl.BlockSpec((tm, D_MODEL), row),
            pl.BlockSpec((SSM_W, SSM_W), const),
            pl.BlockSpec((1, SSM_W), const),
            pl.BlockSpec((GROUP_W, D_MODEL), const),
            pl.BlockSpec((SSM_W, D_MODEL), const),
            pl.BlockSpec((D_MODEL, D_MODEL), const),
        ]
    )
    return pl.pallas_call(
        _merge_body,
        grid=(n // tm,),
        in_specs=in_specs,
        out_specs=pl.BlockSpec((tm, D_MODEL), row),
        out_shape=jax.ShapeDtypeStruct((n, D_MODEL), F32),
        scratch_shapes=[pltpu.VMEM((4 * GROUP_W // LANES, tm, LANES), F32)],
        compiler_params=_cparams(("parallel",)),
        name="merge",
    )(*os_, *ls_, zt, gates, gates, x2, wglu, bglu, wab, wsb, wout)


def _ffn_body(x_ref, g_ref, wg_ref, wu_ref, wd_ref, o_ref, *, tf):
    h = _rms(x_ref[...], g_ref[...]).astype(BF16)
    for f in range(wg_ref.shape[1] // tf):
        cols = slice(f * tf, (f + 1) * tf)
        a = jnp.dot(h, wg_ref[:, cols], preferred_element_type=F32)
        b = jnp.dot(h, wu_ref[:, cols], preferred_element_type=F32)
        act = (a * jax.nn.sigmoid(a) * b).astype(BF16)
        part = jnp.dot(act, wd_ref[cols, :], preferred_element_type=F32)
        if f == 0:
            o_ref[...] = x_ref[...] + part
        else:
            o_ref[...] += part


def _dense_ffn(x2, g, wg, wu, wd, tm=512, tf=256):
    n = x2.shape[0]
    dff = wg.shape[1]
    assert dff % tf == 0
    resident = lambda shape: pl.BlockSpec(shape, lambda i: (0, 0), pipeline_mode=pl.Buffered(1))
    return pl.pallas_call(
        functools.partial(_ffn_body, tf=tf),
        grid=(n // tm,),
        in_specs=[
            pl.BlockSpec((tm, D_MODEL), lambda i: (i, 0)),
            pl.BlockSpec((1, D_MODEL), lambda i: (0, 0)),
            resident((D_MODEL, dff)),
            resident((D_MODEL, dff)),
            resident((dff, D_MODEL)),
        ],
        out_specs=pl.BlockSpec((tm, D_MODEL), lambda i: (i, 0)),
        out_shape=jax.ShapeDtypeStruct((n, D_MODEL), F32),
        compiler_params=_cparams(("parallel",)),
        name="dense_ffn",
    )(x2, g, wg, wu, wd)


def _router_body(x_ref, g_ref, wr_ref, idx_ref, gate_ref):
    h = _rms(x_ref[...], g_ref[...])
    w = wr_ref[...]
    h_hi, w_hi = h.astype(BF16), w.astype(BF16)
    h_lo = (h - h_hi.astype(F32)).astype(BF16)
    w_lo = (w - w_hi.astype(F32)).astype(BF16)
    logits = (jnp.dot(h_hi, w_hi, preferred_element_type=F32)
              + (jnp.dot(h_hi, w_lo, preferred_element_type=F32)
                 + jnp.dot(h_lo, w_hi, preferred_element_type=F32)))
    lane = lax.broadcasted_iota(jnp.int32, logits.shape, 1)
    lane_f = lane.astype(F32)
    logits = jnp.where(lane < N_EXPERTS, logits, -jnp.inf)
    v1 = jnp.max(logits, axis=-1, keepdims=True)
    i1 = jnp.min(jnp.where(logits == v1, lane_f, float(LANES)), axis=-1, keepdims=True)
    rest = jnp.where(lane_f == i1, -jnp.inf, logits)
    v2 = jnp.max(rest, axis=-1, keepdims=True)
    i2 = jnp.min(jnp.where(rest == v2, lane_f, float(LANES)), axis=-1, keepdims=True)
    e = jnp.exp(v2 - v1)
    g1 = 1.0 / (1.0 + e)
    g2 = e / (1.0 + e)
    idx_ref[...] = jnp.where(lane == 0, i1, jnp.where(lane == 1, i2, 0.0)).astype(jnp.int32)
    gate_ref[...] = jnp.where(lane == 0, g1, jnp.where(lane == 1, g2, 0.0))


def _router(x2, g, wr_pad, tm=1024):
    n = x2.shape[0]
    return pl.pallas_call(
        _router_body,
        grid=(n // tm,),
        in_specs=[
            pl.BlockSpec((tm, D_MODEL), lambda i: (i, 0)),
            pl.BlockSpec((1, D_MODEL), lambda i: (0, 0)),
            pl.BlockSpec((D_MODEL, LANES), lambda i: (0, 0)),
        ],
        out_specs=[
            pl.BlockSpec((tm, LANES), lambda i: (i, 0)),
            pl.BlockSpec((tm, LANES), lambda i: (i, 0)),
        ],
        out_shape=[
            jax.ShapeDtypeStruct((n, LANES), jnp.int32),
            jax.ShapeDtypeStruct((n, LANES), F32),
        ],
        compiler_params=_cparams(("parallel",)),
        name="router",
    )(x2, g, wr_pad)


def _pack_bf16_pairs(hb):
    half = hb.shape[1] // 2
    lo = lax.bitcast_convert_type(hb[:, :half].astype(F32), jnp.uint32)
    hi = lax.bitcast_convert_type(hb[:, half:].astype(F32), jnp.uint32)
    return (hi & jnp.uint32(0xFFFF0000)) | (lo >> 16)


def _unpack_bf16_pairs(xu):
    lo = lax.bitcast_convert_type(xu << 16, F32).astype(BF16)
    hi = lax.bitcast_convert_type(xu & jnp.uint32(0xFFFF0000), F32).astype(BF16)
    return lo, hi


def _dispatch_body(slot_ref, tail_ref, x_ref, g_ref, xs_ref, pack_scr, zero_scr, sem, zsem, *, tm):
    i = pl.program_id(0)

    @pl.when(i == 0)
    def _():
        zero_scr[...] = jnp.zeros_like(zero_scr)
        def fill(e):
            row = pl.multiple_of(jnp.maximum(tail_ref[e], 0), MOE_BM)
            return pltpu.make_async_copy(zero_scr, xs_ref.at[pl.ds(row, MOE_BM)], zsem)

        for e in range(tail_ref.shape[0]):
            pl.when(tail_ref[e] >= 0)(lambda e=e: fill(e).start())
        for e in range(tail_ref.shape[0]):
            pl.when(tail_ref[e] >= 0)(lambda e=e: fill(e).wait())

    pack_scr[...] = _pack_bf16_pairs(_rms(x_ref[...], g_ref[...]).astype(BF16))

    def row_copy(t, k):
        dst = slot_ref[(i * tm + t) * 2 + k]
        return pltpu.make_async_copy(pack_scr.at[pl.ds(t, 1)], xs_ref.at[pl.ds(dst, 1)], sem)

    def issue(t, carry):
        row_copy(t, 0).start()
        row_copy(t, 1).start()
        return carry

    lax.fori_loop(0, tm, issue, 0, unroll=8)
    pltpu.make_async_copy(xs_ref.at[pl.ds(0, 2 * tm)], xs_ref.at[pl.ds(0, 2 * tm)], sem).wait()


def _dispatch(slot, tail_rows, x2, g, n_slots, tm=256):
    n = x2.shape[0]
    grid_spec = pltpu.PrefetchScalarGridSpec(
        num_scalar_prefetch=2,
        grid=(n // tm,),
        in_specs=[
            pl.BlockSpec((tm, D_MODEL), lambda i, s, z: (i, 0)),
            pl.BlockSpec((1, D_MODEL), lambda i, s, z: (0, 0)),
        ],
        out_specs=pl.BlockSpec(memory_space=pl.ANY),
        scratch_shapes=[
            pltpu.VMEM((tm, D_MODEL // 2), jnp.uint32),
            pltpu.VMEM((MOE_BM, D_MODEL // 2), jnp.uint32),
            pltpu.SemaphoreType.DMA(()),
            pltpu.SemaphoreType.DMA(()),
        ],
    )
    return pl.pallas_call(
        functools.partial(_dispatch_body, tm=tm),
        grid_spec=grid_spec,
        out_shape=jax.ShapeDtypeStruct((n_slots, D_MODEL // 2), jnp.uint32),
        compiler_params=_cparams(("arbitrary",)),
        name="moe_dispatch",
    )(slot, tail_rows, x2, g)


def _experts_body(be_ref, nu_ref, xs_ref, wg_ref, wu_ref, wd_ref, ys_ref, *, tf):
    del be_ref
    i = pl.program_id(0)
    half = D_MODEL // 2

    @pl.when(i < nu_ref[0])
    def _():
        lo, hi = _unpack_bf16_pairs(xs_ref[...])
        for f in range(wg_ref.shape[1] // tf):
            cols = slice(f * tf, (f + 1) * tf)
            a = (jnp.dot(lo, wg_ref[:half, cols], preferred_element_type=F32)
                 + jnp.dot(hi, wg_ref[half:, cols], preferred_element_type=F32))
            b = (jnp.dot(lo, wu_ref[:half, cols], preferred_element_type=F32)
                 + jnp.dot(hi, wu_ref[half:, cols], preferred_element_type=F32))
            act = (a * jax.nn.sigmoid(a) * b).astype(BF16)
            part = jnp.dot(act, wd_ref[cols, :], preferred_element_type=F32)
            if f == 0:
                ys_ref[...] = part
            else:
                ys_ref[...] += part

    @pl.when(i >= nu_ref[0])
    def _():
        ys_ref[...] = jnp.zeros_like(ys_ref)


def _experts(block_e, n_used, xs, wg, wu, wd, tf=512):
    n_slots = xs.shape[0]
    n_blocks = n_slots // MOE_BM
    dff = wg.shape[2]

    def blk(i, nu):
        return jnp.minimum(i, nu[0] - 1)

    def wspec(rows, cols):
        return pl.BlockSpec((None, rows, cols), lambda i, be, nu: (be[blk(i, nu)], 0, 0),
                            pipeline_mode=pl.Buffered(1))

    grid_spec = pltpu.PrefetchScalarGridSpec(
        num_scalar_prefetch=2,
        grid=(n_blocks,),
        in_specs=[
            pl.BlockSpec((MOE_BM, D_MODEL // 2), lambda i, be, nu: (blk(i, nu), 0)),
            wspec(D_MODEL, dff),
            wspec(D_MODEL, dff),
            wspec(dff, D_MODEL),
        ],
        out_specs=pl.BlockSpec((MOE_BM, D_MODEL), lambda i, be, nu: (i, 0)),
    )
    return pl.pallas_call(
        functools.partial(_experts_body, tf=tf),
        grid_spec=grid_spec,
        out_shape=jax.ShapeDtypeStruct((n_slots, D_MODEL), F32),
        compiler_params=_cparams(("arbitrary",)),
        name="moe_experts",
    )(block_e, n_used, xs, wg, wu, wd)


def _combine_body(slot_ref, x_ref, gate_ref, g_ref, ys_ref, o_ref, y0_scr, y1_scr, sem, *, tm):
    i = pl.program_id(0)

    def row_copy(t, k, dst):
        src = slot_ref[(i * tm + t) * 2 + k]
        return pltpu.make_async_copy(ys_ref.at[pl.ds(src, 1)], dst.at[pl.ds(t, 1)], sem)

    def issue(t, carry):
        row_copy(t, 0, y0_scr).start()
        row_copy(t, 1, y1_scr).start()
        return carry

    lax.fori_loop(0, tm, issue, 0, unroll=8)
    pltpu.make_async_copy(ys_ref.at[pl.ds(0, tm)], y0_scr, sem).wait()
    pltpu.make_async_copy(ys_ref.at[pl.ds(0, tm)], y1_scr, sem).wait()

    gt = gate_ref[...]
    y = x_ref[...] + (gt[:, 0:1] * y0_scr[...] + gt[:, 1:2] * y1_scr[...])
    o_ref[...] = _rms(y, g_ref[...])


def _combine(slot, x2, gates, g_final, ys, tm=256):
    n = x2.shape[0]
    grid_spec = pltpu.PrefetchScalarGridSpec(
        num_scalar_prefetch=1,
        grid=(n // tm,),
        in_specs=[
            pl.BlockSpec((tm, D_MODEL), lambda i, s: (i, 0)),
            pl.BlockSpec((tm, LANES), lambda i, s: (i, 0)),
            pl.BlockSpec((1, D_MODEL), lambda i, s: (0, 0)),
            pl.BlockSpec(memory_space=pl.ANY),
        ],
        out_specs=pl.BlockSpec((tm, D_MODEL), lambda i, s: (i, 0)),
        scratch_shapes=[
            pltpu.VMEM((tm, D_MODEL), F32),
            pltpu.VMEM((tm, D_MODEL), F32),
            pltpu.SemaphoreType.DMA(()),
        ],
    )
    return pl.pallas_call(
        functools.partial(_combine_body, tm=tm),
        grid_spec=grid_spec,
        out_shape=jax.ShapeDtypeStruct((n, D_MODEL), F32),
        compiler_params=_cparams(("arbitrary",)),
        name="moe_combine",
    )(slot, x2, gates, g_final, ys)


def _route_slots(idx, n):
    flat_e = idx[:, :2].reshape(-1)
    onehot = (flat_e[:, None] == jnp.arange(N_EXPERTS, dtype=jnp.int32)[None, :]).astype(jnp.int32)
    grp = 512
    oh3 = onehot.reshape(-1, grp, N_EXPERTS).astype(F32)
    below = jnp.asarray(np.tril(np.ones((grp, grp), np.float32), -1))
    local = jnp.einsum("ts,gse->gte", below, oh3).astype(jnp.int32)
    per_grp = jnp.sum(oh3, axis=1).astype(jnp.int32)
    grp_start = jnp.cumsum(per_grp, axis=0) - per_grp
    counts = jnp.sum(per_grp, axis=0)
    excl = (local + grp_start[:, None, :]).reshape(-1, N_EXPERTS)
    rank = jnp.sum(excl * onehot, axis=1)
    padded = ((counts + MOE_BM - 1) // MOE_BM) * MOE_BM
    pad_end = jnp.cumsum(padded)
    pad_start = pad_end - padded
    slot = (jnp.sum(pad_start[None, :] * onehot, axis=1) + rank).astype(jnp.int32)
    n_blocks = (2 * n) // MOE_BM + N_EXPERTS
    starts = jnp.arange(n_blocks, dtype=jnp.int32) * MOE_BM
    block_e = jnp.sum((starts[:, None] >= pad_end[None, :]).astype(jnp.int32), axis=1)
    block_e = jnp.minimum(block_e, N_EXPERTS - 1).astype(jnp.int32)
    n_used = (pad_end[-1] // MOE_BM).astype(jnp.int32).reshape(1)
    tails = jnp.where(padded > 0, pad_end - MOE_BM, -1)
    spare = pad_end[-1] + jnp.arange(N_EXPERTS, dtype=pad_end.dtype) * MOE_BM
    spare = jnp.where(spare < n_blocks * MOE_BM, spare, -1)
    tail_rows = jnp.concatenate([tails, spare]).astype(jnp.int32)
    return slot, block_e, n_used, tail_rows, n_blocks * MOE_BM


def _mixer_layer(x2, batch, seq, rel_bias, norm_g, w_in, ssm, d_skip, w_glu, b_glu,
                 w_attn_br, w_ssm_br, w_out):
    u_lo = 3 * ATTN_W
    blocks = [w_in[:, which * ATTN_W + g * GROUP_W: which * ATTN_W + (g + 1) * GROUP_W]
              for g in range(N_GROUPS) for which in range(3)]
    w_bf = jnp.concatenate(blocks + [w_in[:, u_lo + SSM_W:]], axis=1).astype(BF16)
    wut_bf = jnp.transpose(w_in[:, u_lo:u_lo + SSM_W]).astype(BF16)
    *qkvs, gates, ut = _in_projection(x2, norm_g.reshape(1, D_MODEL), w_bf, wut_bf, batch, seq)

    os_, ls_ = [], []
    for g, (window, dilation) in enumerate(ATTN_GROUPS):
        bias = _band_bias(rel_bias[:, g * HEADS:(g + 1) * HEADS], window, dilation)
        o, l = _attention_group(qkvs[g], bias, g)
        os_.append(o)
        ls_.append(l)

    zt = _ssm_scan(ut, d_skip, _ssm_tables(*ssm), batch)

    return _merge(os_, ls_, zt, gates, x2, w_glu.astype(BF16), b_glu.reshape(1, SSM_W).astype(F32),
                  w_attn_br.astype(BF16), w_ssm_br.astype(BF16), w_out.astype(BF16))


def kernel(x, rel_bias, norm1_g, w_in, ssm_lam_re, ssm_lam_im, ssm_log_dt, ssm_b_re, ssm_b_im, ssm_c_re, ssm_c_im, ssm_d, w_glu, b_glu, w_attn_br, w_ssm_br, w_out, norm2_g, ffn_w_gate, ffn_w_up, ffn_w_down, moe_router, moe_w_gate, moe_w_up, moe_w_down, final_norm_g):
    batch, seq, d = x.shape
    assert d == D_MODEL and norm1_g.shape[0] == 2 and seq % (16 * BLK) == 0
    n = batch * seq
    x2 = x.reshape(n, d)

    def mixer(x2, l):
        ssm = (ssm_lam_re[l], ssm_lam_im[l], ssm_log_dt[l], ssm_b_re[l], ssm_b_im[l],
               ssm_c_re[l], ssm_c_im[l])
        return _mixer_layer(x2, batch, seq, rel_bias, norm1_g[l], w_in[l], ssm, ssm_d[l], w_glu[l],
                            b_glu[l], w_attn_br[l], w_ssm_br[l], w_out[l])

    x2 = mixer(x2, 0)
    x2 = _dense_ffn(x2, norm2_g[0].reshape(1, d), ffn_w_gate[0].astype(BF16),
                    ffn_w_up[0].astype(BF16), ffn_w_down[0].astype(BF16))

    x2 = mixer(x2, 1)
    g2 = norm2_g[1].reshape(1, d)
    wr_pad = jnp.zeros((d, LANES), F32).at[:, :N_EXPERTS].set(moe_router[0].astype(F32))
    idx, gates = _router(x2, g2, wr_pad)
    slot, block_e, n_used, tail_rows, n_slots = _route_slots(idx, n)
    xs = _dispatch(slot, tail_rows, x2, g2, n_slots)
    ys = _experts(block_e, n_used, xs, moe_w_gate[0].astype(BF16), moe_w_up[0].astype(BF16),
                  moe_w_down[0].astype(BF16))
    out = _combine(slot, x2, gates, final_norm_g.reshape(1, d), ys)
    return out.reshape(batch, seq, d)
```

```python
import functools

import numpy as np
import jax
import jax.numpy as jnp
from jax import lax
from jax.experimental import pallas as pl
from jax.experimental.pallas import tpu as pltpu

F32 = jnp.float32
BF16 = jnp.bfloat16

D_MODEL = 1024
HEAD_DIM = 64
ATTN_GROUPS = ((128, 1), (512, 4), (2048, 16))
N_GROUPS = 3
HEADS = 8
GROUP_W = HEADS * HEAD_DIM
ATTN_W = N_GROUPS * GROUP_W
BLK = 128
REL_BUCKETS = 32
REL_MAX_DIST = 2048
NEG_INF = -1e30
SSM_CH = 16
SSM_W = D_MODEL // 2
SSM_G = SSM_W // SSM_CH
SSM_P = 64
PROJ_W = 3 * ATTN_W + SSM_W + 2 * D_MODEL
N_EXPERTS = 8
MOE_BM = 512
EPS = 1e-6
CHUNK = 128
SCAN_LEVELS = 8

LANES = 128
VMEM_LIMIT = 56 * 1024 * 1024


def _cparams(sem):
    return pltpu.CompilerParams(dimension_semantics=sem, vmem_limit_bytes=VMEM_LIMIT)


def _rms(x, g):
    return x * lax.rsqrt(jnp.mean(x * x, axis=-1, keepdims=True) + EPS) * g


def _proj_body(x_ref, g_ref, w_ref, wut_ref, qkv0_ref, qkv1_ref, qkv2_ref, gate_ref, ut_ref, d_scr):
    tm = x_ref.shape[0]
    hb = _rms(x_ref[...], g_ref[...]).astype(BF16)
    ut = lax.dot_general(wut_ref[...], hb, (((1,), (1,)), ((), ())), preferred_element_type=F32)
    for k in range(ut_ref.shape[0]):
        ut_ref[k] = ut[:, k * CHUNK:(k + 1) * CHUNK]

    cw = 2 * LANES
    nl = cw // LANES

    def slab(col):
        return jnp.dot(hb, w_ref[:, col:col + cw], preferred_element_type=F32)

    uses = 0
    for g, (out_ref, (_, r)) in enumerate(zip((qkv0_ref, qkv1_ref, qkv2_ref), ATTN_GROUPS)):
        for which in range(3):
            for lo in range(0, GROUP_W, cw):
                res = slab((3 * g + which) * GROUP_W + lo)
                if r == 1:
                    out_ref[which, 0, :, lo:lo + cw] = res.astype(BF16)
                    continue
                scrs = [d_scr.at[(uses % 2) * nl + k] for k in range(nl)]
                uses += 1
                for k in range(nl):
                    scrs[k][...] = res[:, k * LANES:(k + 1) * LANES]
                for c in range(r):
                    sub = [s[pl.ds(c, tm // r, stride=r), :] for s in scrs]
                    out_ref[which, c, :, lo:lo + cw] = jnp.concatenate(sub, axis=1).astype(BF16)
    for lo in range(0, 2 * D_MODEL, cw):
        gate_ref[:, lo:lo + cw] = slab(3 * N_GROUPS * GROUP_W + lo).astype(BF16)


def _in_projection(x2, g, w_bf, wut_bf, batch, seq, tm=512):
    n = x2.shape[0]
    tiles_per_seq = seq // tm
    wcols = w_bf.shape[1]

    def qkv_spec(r):
        return pl.BlockSpec((3, None, r, tm // r, GROUP_W),
                            lambda i: (0, i // tiles_per_seq, 0, i % tiles_per_seq, 0))

    return pl.pallas_call(
        _proj_body,
        grid=(n // tm,),
        in_specs=[
            pl.BlockSpec((tm, D_MODEL), lambda i: (i, 0)),
            pl.BlockSpec((1, D_MODEL), lambda i: (0, 0)),
            pl.BlockSpec((D_MODEL, wcols), lambda i: (0, 0), pipeline_mode=pl.Buffered(1)),
            pl.BlockSpec((SSM_W, D_MODEL), lambda i: (0, 0), pipeline_mode=pl.Buffered(1)),
        ],
        out_specs=[qkv_spec(r) for _, r in ATTN_GROUPS] + [
            pl.BlockSpec((tm, 2 * D_MODEL), lambda i: (i, 0)),
            pl.BlockSpec((tm // CHUNK, SSM_W, CHUNK), lambda i: (i, 0, 0)),
        ],
        out_shape=[jax.ShapeDtypeStruct((3, batch, r, seq // r, GROUP_W), BF16) for _, r in ATTN_GROUPS] + [
            jax.ShapeDtypeStruct((n, 2 * D_MODEL), BF16),
            jax.ShapeDtypeStruct((n // CHUNK, SSM_W, CHUNK), F32),
        ],
        scratch_shapes=[pltpu.VMEM((4, tm, LANES), F32)],
        compiler_params=_cparams(("parallel",)),
        name="in_projection",
    )(x2, g, w_bf, wut_bf)


def _t5_bucket(dist):
    max_exact = REL_BUCKETS // 2
    d = np.maximum(dist, 1).astype(np.float64)
    large = max_exact + (
        np.log(d / max_exact) / np.log(REL_MAX_DIST / max_exact) * (REL_BUCKETS - max_exact)
    ).astype(np.int32)
    large = np.minimum(large, REL_BUCKETS - 1)
    return np.where(dist < max_exact, dist, large).astype(np.int32)


def _band_bias(table, window, dilation):
    steps = window // dilation
    qi = np.arange(BLK)[:, None]
    kj = np.arange(2 * BLK)[None, :]
    delta = BLK + qi - kj
    band = (delta >= 0) & (delta <= steps)
    bucket = _t5_bucket(np.clip(delta, 0, steps) * dilation)
    onehot = np.eye(REL_BUCKETS, dtype=np.float32)[bucket]
    bias = jnp.einsum("qkb,bh->hqk", onehot, table.astype(F32), precision=lax.Precision.HIGHEST)
    return jnp.where(band[None], bias, NEG_INF)


def _attn_body(q_ref, kp_ref, kc_ref, vp_ref, vc_ref, bias_ref, o_ref, l_ref, *, nsub):
    lane = lax.broadcasted_iota(jnp.int32, (BLK, LANES), 1)
    lo = lane < HEAD_DIM
    keep_lo = jnp.where(lo, 1.0, 0.0).astype(BF16)
    keep_hi = jnp.where(lo, 0.0, 1.0).astype(BF16)
    col = lax.broadcasted_iota(jnp.int32, (BLK, 2 * BLK), 1)
    first_pen = jnp.where(col < BLK, jnp.where(pl.program_id(2) == 0, NEG_INF, 0.0), 0.0)
    for i in range(nsub):
        rows = slice(i * BLK, (i + 1) * BLK)
        q = q_ref[rows, :] * jnp.asarray(HEAD_DIM ** -0.5, BF16)
        if i == 0:
            kw = jnp.concatenate([kp_ref[...], kc_ref[0:BLK, :]], axis=0)
            vw = jnp.concatenate([vp_ref[...], vc_ref[0:BLK, :]], axis=0)
        else:
            kw = kc_ref[(i - 1) * BLK:(i + 1) * BLK, :]
            vw = vc_ref[(i - 1) * BLK:(i + 1) * BLK, :]
        for hp in range(HEADS // 2):
            cols = slice(hp * LANES, (hp + 1) * LANES)
            q2, k2, v2 = q[:, cols], kw[:, cols], vw[:, cols]
            outs, lses = [], []
            for half in range(2):
                qm = q2 * (keep_lo, keep_hi)[half]
                s = lax.dot_general(qm, k2, (((1,), (1,)), ((), ())), preferred_element_type=F32)
                s = s + bias_ref[2 * hp + half]
                if i == 0:
                    s = s + first_pen
                m = jnp.max(s, axis=-1, keepdims=True)
                p = jnp.exp(s - m)
                den = jnp.sum(p, axis=-1, keepdims=True)
                pv = jnp.dot(p.astype(BF16), v2, preferred_element_type=F32)
                outs.append(pv / den)
                lses.append(jnp.broadcast_to(m + jnp.log(den), (BLK, LANES)))
            o_ref[rows, cols] = jnp.where(lo, outs[0], outs[1]).astype(BF16)
            l_ref[rows, cols] = jnp.where(lo, lses[0], lses[1])


def _attention_group(qkv, bias, g):
    _, batch, r, length, _ = qkv.shape
    qb = min(512, length)
    nsub = qb // BLK

    def cur(which):
        return pl.BlockSpec((None, None, None, qb, GROUP_W), lambda b, c, n: (which, b, c, n, 0))

    def prev(which):
        return pl.BlockSpec((None, None, None, BLK, GROUP_W),
                            lambda b, c, n: (which, b, c, jnp.maximum(n * nsub - 1, 0), 0))

    out_spec = pl.BlockSpec((None, None, qb, GROUP_W), lambda b, c, n: (b, c, n, 0))
    return pl.pallas_call(
        functools.partial(_attn_body, nsub=nsub),
        grid=(batch, r, length // qb),
        in_specs=[cur(0), prev(1), cur(1), prev(2), cur(2),
                  pl.BlockSpec((HEADS, BLK, 2 * BLK), lambda b, c, n: (0, 0, 0))],
        out_specs=[out_spec, out_spec],
        out_shape=[
            jax.ShapeDtypeStruct((batch, r, length, GROUP_W), BF16),
            jax.ShapeDtypeStruct((batch, r, length, GROUP_W), F32),
        ],
        compiler_params=_cparams(("parallel", "parallel", "arbitrary")),
        name=f"attention_g{g}",
    )(qkv, qkv, qkv, qkv, qkv, bias)


def _ssm_tables(lam_re, lam_im, log_dt, b_re, b_im, c_re, c_im):
    lam = lax.complex(lam_re.astype(F32), lam_im.astype(F32))
    dt = jnp.exp(log_dt.astype(F32))[:, None]
    lam_dt = lam * dt
    lam_bar = jnp.exp(lam_dt)
    b = lax.complex(b_re.astype(F32), b_im.astype(F32))
    b_bar = ((lam_bar - 1.0) / lam)[..., None] * b
    half = CHUNK // 2
    t = jnp.arange(CHUNK, dtype=F32)

    def power(k):
        return jnp.exp(lam_dt[:, None, :] * jnp.reshape(jnp.asarray(k, F32), (1, -1, 1)))

    p_fwd = power(t - half)
    p_bwd = power(half - t)

    def rows(pw):
        w1 = jnp.concatenate([pw.real, pw.imag], axis=-1)[:, None, :, :]
        w2 = jnp.concatenate([-pw.imag, pw.real], axis=-1)[:, None, :, :]
        br = jnp.transpose(jnp.concatenate([b_bar.real, b_bar.real], axis=1), (0, 2, 1))[:, :, None, :]
        bi = jnp.transpose(jnp.concatenate([b_bar.imag, b_bar.imag], axis=1), (0, 2, 1))[:, :, None, :]
        return (br * w1 + bi * w2).astype(BF16).reshape(SSM_G, SSM_CH * CHUNK, 2 * SSM_P)

    def cols(pw):
        pt = jnp.transpose(pw, (0, 2, 1))
        w1 = jnp.concatenate([pt.real, -pt.imag], axis=1)[:, :, None, :]
        w2 = jnp.concatenate([-pt.imag, -pt.real], axis=1)[:, :, None, :]
        cr = jnp.transpose(jnp.concatenate([c_re, c_re], axis=2).astype(F32), (0, 2, 1))[:, :, :, None]
        ci = jnp.transpose(jnp.concatenate([c_im, c_im], axis=2).astype(F32), (0, 2, 1))[:, :, :, None]
        return (cr * w1 + ci * w2).astype(BF16).reshape(SSM_G, 2 * SSM_P, SSM_CH * CHUNK)

    a_tab = rows(p_bwd)
    d_tab = cols(p_fwd)
    w_tab = rows(p_bwd * power(CHUNK - 1.0 - half))
    v_tab = cols(p_fwd * power(half + 1.0))
    lc = power([float(CHUNK * 2 ** k) for k in range(SCAN_LEVELS)])
    l1 = jnp.concatenate([lc.real, lc.real], axis=-1)
    l2 = jnp.concatenate([-lc.imag, lc.imag], axis=-1)
    return a_tab, d_tab, w_tab, v_tab, l1, l2


def _gelu_tanh(y):
    return y * jax.nn.sigmoid(1.5957691216057308 * (y + 0.044715 * (y * y * y)))


def _ssm_body(d_ref, u_ref, a_ref, dt_ref, w_ref, v_ref, l1_ref, l2_ref, z_ref, m_scr, mask_scr, z_scr, *,
              ncb):
    g = pl.program_id(0)
    nc = u_ref.shape[0]
    width = SSM_CH * CHUNK
    cb = mask_scr.shape[1]

    @pl.when(g == 0)
    def _():
        s_idx = lax.broadcasted_iota(jnp.int32, (width, cb), 0) & (CHUNK - 1)
        t_idx = lax.broadcasted_iota(jnp.int32, (width, cb), 1) & (CHUNK - 1)
        mask_scr[...] = jnp.where(t_idx >= s_idx, -1, 0).astype(jnp.int32)

    for k in range(width // cb):
        mk = jnp.dot(a_ref[...], dt_ref[:, k * cb:(k + 1) * cb], preferred_element_type=F32)
        kept = lax.bitcast_convert_type(mk, jnp.int32) & mask_scr[...]
        m_scr[:, k * cb:(k + 1) * cb] = lax.bitcast_convert_type(kept, F32).astype(BF16)

    u2 = u_ref.reshape(nc * SSM_CH, CHUNK)
    us =[u2[pl.ds(c, nc, stride=SSM_CH), :] for c in range(SSM_CH)]
    x = jnp.concatenate(us, axis=1).astype(BF16)

    acc = jnp.dot(x, w_ref[...], preferred_element_type=F32)
    rmod = lax.broadcasted_iota(jnp.int32, (nc, 2 * SSM_P), 0) & (ncb - 1)
    for k in range(ncb.bit_length() - 1):
        d = 1 << k
        sh = jnp.where(rmod >= d, pltpu.roll(acc, d, 0), 0.0)
        acc = acc + sh * l1_ref[k:k + 1, :] + pltpu.roll(sh, SSM_P, 1) * l2_ref[k:k + 1, :]
    x_in = jnp.where(rmod >= 1, pltpu.roll(acc, 1, 0), 0.0)

    y = jnp.dot(x, m_scr[...], preferred_element_type=F32)
    y = y + jnp.dot(x_in.astype(BF16), v_ref[...], preferred_element_type=F32)
    for c in range(SSM_CH):
        yc = y[:, c * CHUNK:(c + 1) * CHUNK] + d_ref[g * SSM_CH + c] * us[c]
        z_scr[pl.ds(c, nc, stride=SSM_CH), :] = _gelu_tanh(yc)
    z_ref[...] = z_scr[...].reshape(nc, SSM_CH, CHUNK)


def _ssm_scan(u3, d_skip, tables, nbatch):
    a_tab, d_tab, w_tab, v_tab, l1, l2 = tables
    nc = u3.shape[0]
    ncb = nc // nbatch
    assert ncb & (ncb - 1) == 0 and ncb <= 2 ** SCAN_LEVELS
    width = SSM_CH * CHUNK
    grid_spec = pltpu.PrefetchScalarGridSpec(
        num_scalar_prefetch=1,
        grid=(SSM_G,),
        in_specs=[
            pl.BlockSpec((nc, SSM_CH, CHUNK), lambda g, d: (0, g, 0)),
            pl.BlockSpec((None, width, 2 * SSM_P), lambda g, d: (g, 0, 0)),
            pl.BlockSpec((None, 2 * SSM_P, width), lambda g, d: (g, 0, 0)),
            pl.BlockSpec((None, width, 2 * SSM_P), lambda g, d: (g, 0, 0)),
            pl.BlockSpec((None, 2 * SSM_P, width), lambda g, d: (g, 0, 0)),
            pl.BlockSpec((None, SCAN_LEVELS, 2 * SSM_P), lambda g, d: (g, 0, 0)),
            pl.BlockSpec((None, SCAN_LEVELS, 2 * SSM_P), lambda g, d: (g, 0, 0)),
        ],
        out_specs=pl.BlockSpec((nc, SSM_CH, CHUNK), lambda g, d: (0, g, 0)),
        scratch_shapes=[pltpu.VMEM((width, width), BF16), pltpu.VMEM((width, 512), jnp.int32),
                        pltpu.VMEM((nc * SSM_CH, CHUNK), F32)],
    )
    return pl.pallas_call(
        functools.partial(_ssm_body, ncb=ncb),
        grid_spec=grid_spec,
        out_shape=jax.ShapeDtypeStruct((nc, SSM_W, CHUNK), F32),
        compiler_params=_cparams(("arbitrary",)),
        name="ssm_scan",
    )(d_skip.astype(F32), u3, a_tab, d_tab, w_tab, v_tab, l1, l2)


def _merge_body(o0, o1, o2, l0, l1, l2, zt_ref, ga_ref, gs_ref, x_ref,
                wglu_ref, bglu_ref, wab_ref, wsb_ref, wout_ref, out_ref, tok_scr):
    def token_major(ref, slot):
        r, rows, _ = ref.shape
        if r == 1:
            return ref[0].astype(F32)
        nl = GROUP_W // LANES
        scrs = [tok_scr.at[slot * nl + k] for k in range(nl)]
        for c in range(r):
            sub = ref[c].astype(F32)
            for k in range(nl):
                scrs[k][pl.ds(c, rows, stride=r), :] = sub[:, k * LANES:(k + 1) * LANES]
        return jnp.concatenate([s[...] for s in scrs], axis=1)

    a0, a1, a2 = token_major(l0, 0), token_major(l1, 0), token_major(l2, 1)
    v0, v1, v2 = token_major(o0, 2), token_major(o1, 2), token_major(o2, 3)
    mx = jnp.maximum(jnp.maximum(a0, a1), a2)
    e0, e1, e2 = jnp.exp(a0 - mx), jnp.exp(a1 - mx), jnp.exp(a2 - mx)
    mix = (e0 * v0 + e1 * v1 + e2 * v2) / (e0 + e1 + e2)
    y_attn = jnp.dot(mix.astype(BF16), wab_ref[...], preferred_element_type=F32)

    z = jnp.concatenate([zt_ref[k].T for k in range(zt_ref.shape[0])], axis=0).astype(BF16)
    gl = jnp.dot(z, wglu_ref[...], preferred_element_type=F32) + bglu_ref[...]
    sg = z.astype(F32) * jax.nn.sigmoid(gl)
    y_ssm = jnp.dot(sg.astype(BF16), wsb_ref[...], preferred_element_type=F32)

    merged = (jax.nn.sigmoid(ga_ref[...].astype(F32)) * y_attn
              + jax.nn.sigmoid(gs_ref[...].astype(F32)) * y_ssm)
    out_ref[...] = x_ref[...] + jnp.dot(merged.astype(BF16), wout_ref[...], preferred_element_type=F32)


def _merge(os_, ls_, zt, gates, x2, wglu, bglu, wab, wsb, wout, tm=512):
    n = x2.shape[0]
    tiles_per_seq = os_[0].shape[2] // tm
    row = lambda i: (i, 0)
    const = lambda i: (0, 0)

    def group_spec(a):
        r = a.shape[1]
        return pl.BlockSpec((None, r, tm // r, GROUP_W),
                            lambda i: (i // tiles_per_seq, 0, i % tiles_per_seq, 0))

    in_specs = (
        [group_spec(a) for a in os_] + [group_spec(a) for a in ls_]
        + [
            pl.BlockSpec((tm // CHUNK, SSM_W, CHUNK), lambda i: (i, 0, 0)),
            pl.BlockSpec((tm, D_MODEL), lambda i: (i, 0)),
            pl.BlockSpec((tm, D_MODEL), lambda i: (i, 1)),
            pl.BlockSpec((tm, D_MODEL), row),
            pl.BlockSpec((SSM_W, SSM_W), const),
            pl.BlockSpec((1, SSM_W), const),
            pl.BlockSpec((GROUP_W, D_MODEL), const),
            pl.BlockSpec((SSM_W, D_MODEL), const),
            pl.BlockSpec((D_MODEL, D_MODEL), const),
        ]
    )
    return pl.pallas_call(
        _merge_body,
        grid=(n // tm,),
        in_specs=in_specs,
        out_specs=pl.BlockSpec((tm, D_MODEL), row),
        out_shape=jax.ShapeDtypeStruct((n, D_MODEL), F32),
        scratch_shapes=[pltpu.VMEM((4 * GROUP_W // LANES, tm, LANES), F32)],
        compiler_params=_cparams(("parallel",)),
        name="merge",
    )(*os_, *ls_, zt, gates, gates, x2, wglu, bglu, wab, wsb, wout)


def _ffn_body(x_ref, g_ref, wg_ref, wu_ref, wd_ref, o_ref, *, tf):
    h = _rms(x_ref[...], g_ref[...]).astype(BF16)
    for f in range(wg_ref.shape[1] // tf):
        cols = slice(f * tf, (f + 1) * tf)
        a = jnp.dot(h, wg_ref[:, cols], preferred_element_type=F32)
        b = jnp.dot(h, wu_ref[:, cols], preferred_element_type=F32)
        act = (a * jax.nn.sigmoid(a) * b).astype(BF16)
        part = jnp.dot(act, wd_ref[cols, :], preferred_element_type=F32)
        if f == 0:
            o_ref[...] = x_ref[...] + part
        else:
            o_ref[...] += part


def _dense_ffn(x2, g, wg, wu, wd, tm=512, tf=256):
    n = x2.shape[0]
    dff = wg.shape[1]
    assert dff % tf == 0
    resident = lambda shape: pl.BlockSpec(shape, lambda i: (0, 0), pipeline_mode=pl.Buffered(1))
    return pl.pallas_call(
        functools.partial(_ffn_body, tf=tf),
        grid=(n // tm,),
        in_specs=[
            pl.BlockSpec((tm, D_MODEL), lambda i: (i, 0)),
            pl.BlockSpec((1, D_MODEL), lambda i: (0, 0)),
            resident((D_MODEL, dff)),
            resident((D_MODEL, dff)),
            resident((dff, D_MODEL)),
        ],
        out_specs=pl.BlockSpec((tm, D_MODEL), lambda i: (i, 0)),
        out_shape=jax.ShapeDtypeStruct((n, D_MODEL), F32),
        compiler_params=_cparams(("parallel",)),
        name="dense_ffn",
    )(x2, g, wg, wu, wd)


def _router_body(x_ref, g_ref, wr_ref, idx_ref, gate_ref):
    h = _rms(x_ref[...], g_ref[...])
    w = wr_ref[...]
    h_hi, w_hi = h.astype(BF16), w.astype(BF16)
    h_lo = (h - h_hi.astype(F32)).astype(BF16)
    w_lo = (w - w_hi.astype(F32)).astype(BF16)
    logits = (jnp.dot(h_hi, w_hi, preferred_element_type=F32)
              + (jnp.dot(h_hi, w_lo, preferred_element_type=F32)
                 + jnp.dot(h_lo, w_hi, preferred_element_type=F32)))
    lane = lax.broadcasted_iota(jnp.int32, logits.shape, 1)
    lane_f = lane.astype(F32)
    logits = jnp.where(lane < N_EXPERTS, logits, -jnp.inf)
    v1 = jnp.max(logits, axis=-1, keepdims=True)
    i1 = jnp.min(jnp.where(logits == v1, lane_f, float(LANES)), axis=-1, keepdims=True)
    rest = jnp.where(lane_f == i1, -jnp.inf, logits)
    v2 = jnp.max(rest, axis=-1, keepdims=True)
    i2 = jnp.min(jnp.where(rest == v2, lane_f, float(LANES)), axis=-1, keepdims=True)
    e = jnp.exp(v2 - v1)
    g1 = 1.0 / (1.0 + e)
    g2 = e / (1.0 + e)
    idx_ref[...] = jnp.where(lane == 0, i1, jnp.where(lane == 1, i2, 0.0)).astype(jnp.int32)
    gate_ref[...] = jnp.where(lane == 0, g1, jnp.where(lane == 1, g2, 0.0))


def _router(x2, g, wr_pad, tm=1024):
    n = x2.shape[0]
    return pl.pallas_call(
        _router_body,
        grid=(n // tm,),
        in_specs=[
            pl.BlockSpec((tm, D_MODEL), lambda i: (i, 0)),
            pl.BlockSpec((1, D_MODEL), lambda i: (0, 0)),
            pl.BlockSpec((D_MODEL, LANES), lambda i: (0, 0)),
        ],
        out_specs=[
            pl.BlockSpec((tm, LANES), lambda i: (i, 0)),
            pl.BlockSpec((tm, LANES), lambda i: (i, 0)),
        ],
        out_shape=[
            jax.ShapeDtypeStruct((n, LANES), jnp.int32),
            jax.ShapeDtypeStruct((n, LANES), F32),
        ],
        compiler_params=_cparams(("parallel",)),
        name="router",
    )(x2, g, wr_pad)


def _pack_bf16_pairs(hb):
    half = hb.shape[1] // 2
    lo = lax.bitcast_convert_type(hb[:, :half].astype(F32), jnp.uint32)
    hi = lax.bitcast_convert_type(hb[:, half:].astype(F32), jnp.uint32)
    return (hi & jnp.uint32(0xFFFF0000)) | (lo >> 16)


def _unpack_bf16_pairs(xu):
    lo = lax.bitcast_convert_type(xu << 16, F32).astype(BF16)
    hi = lax.bitcast_convert_type(xu & jnp.uint32(0xFFFF0000), F32).astype(BF16)
    return lo, hi


def _dispatch_body(slot_ref, tail_ref, x_ref, g_ref, xs_ref, pack_scr, zero_scr, sem, zsem, *, tm):
    i = pl.program_id(0)

    @pl.when(i == 0)
    def _():
        zero_scr[...] = jnp.zeros_like(zero_scr)
        def fill(e):
            row = pl.multiple_of(jnp.maximum(tail_ref[e], 0), MOE_BM)
            return pltpu.make_async_copy(zero_scr, xs_ref.at[pl.ds(row, MOE_BM)], zsem)

        for e in range(tail_ref.shape[0]):
            pl.when(tail_ref[e] >= 0)(lambda e=e: fill(e).start())
        for e in range(tail_ref.shape[0]):
            pl.when(tail_ref[e] >= 0)(lambda e=e: fill(e).wait())

    pack_scr[...] = _pack_bf16_pairs(_rms(x_ref[...], g_ref[...]).astype(BF16))

    def row_copy(t, k):
        dst = slot_ref[(i * tm + t) * 2 + k]
        return pltpu.make_async_copy(pack_scr.at[pl.ds(t, 1)], xs_ref.at[pl.ds(dst, 1)], sem)

    def issue(t, carry):
        row_copy(t, 0).start()
        row_copy(t, 1).start()
        return carry

    lax.fori_loop(0, tm, issue, 0, unroll=8)
    pltpu.make_async_copy(xs_ref.at[pl.ds(0, 2 * tm)], xs_ref.at[pl.ds(0, 2 * tm)], sem).wait()


def _dispatch(slot, tail_rows, x2, g, n_slots, tm=256):
    n = x2.shape[0]
    grid_spec = pltpu.PrefetchScalarGridSpec(
        num_scalar_prefetch=2,
        grid=(n // tm,),
        in_specs=[
            pl.BlockSpec((tm, D_MODEL), lambda i, s, z: (i, 0)),
            pl.BlockSpec((1, D_MODEL), lambda i, s, z: (0, 0)),
        ],
        out_specs=pl.BlockSpec(memory_space=pl.ANY),
        scratch_shapes=[
            pltpu.VMEM((tm, D_MODEL // 2), jnp.uint32),
            pltpu.VMEM((MOE_BM, D_MODEL // 2), jnp.uint32),
            pltpu.SemaphoreType.DMA(()),
            pltpu.SemaphoreType.DMA(()),
        ],
    )
    return pl.pallas_call(
        functools.partial(_dispatch_body, tm=tm),
        grid_spec=grid_spec,
        out_shape=jax.ShapeDtypeStruct((n_slots, D_MODEL // 2), jnp.uint32),
        compiler_params=_cparams(("arbitrary",)),
        name="moe_dispatch",
    )(slot, tail_rows, x2, g)


def _experts_body(be_ref, nu_ref, xs_ref, wg_ref, wu_ref, wd_ref, ys_ref, *, tf):
    del be_ref
    i = pl.program_id(0)
    half = D_MODEL // 2

    @pl.when(i < nu_ref[0])
    def _():
        lo, hi = _unpack_bf16_pairs(xs_ref[...])
        for f in range(wg_ref.shape[1] // tf):
            cols = slice(f * tf, (f + 1) * tf)
            a = (jnp.dot(lo, wg_ref[:half, cols], preferred_element_type=F32)
                 + jnp.dot(hi, wg_ref[half:, cols], preferred_element_type=F32))
            b = (jnp.dot(lo, wu_ref[:half, cols], preferred_element_type=F32)
                 + jnp.dot(hi, wu_ref[half:, cols], preferred_element_type=F32))
            act = (a * jax.nn.sigmoid(a) * b).astype(BF16)
            part = jnp.dot(act, wd_ref[cols, :], preferred_element_type=F32)
            if f == 0:
                ys_ref[...] = part
            else:
                ys_ref[...] += part

    @pl.when(i >= nu_ref[0])
    def _():
        ys_ref[...] = jnp.zeros_like(ys_ref)


def _experts(block_e, n_used, xs, wg, wu, wd, tf=512):
    n_slots = xs.shape[0]
    n_blocks = n_slots // MOE_BM
    dff = wg.shape[2]

    def blk(i, nu):
        return jnp.minimum(i, nu[0] - 1)

    def wspec(rows, cols):
        return pl.BlockSpec((None, rows, cols), lambda i, be, nu: (be[blk(i, nu)], 0, 0),
                            pipeline_mode=pl.Buffered(1))

    grid_spec = pltpu.PrefetchScalarGridSpec(
        num_scalar_prefetch=2,
        grid=(n_blocks,),
        in_specs=[
            pl.BlockSpec((MOE_BM, D_MODEL // 2), lambda i, be, nu: (blk(i, nu), 0)),
            wspec(D_MODEL, dff),
            wspec(D_MODEL, dff),
            wspec(dff, D_MODEL),
        ],
        out_specs=pl.BlockSpec((MOE_BM, D_MODEL), lambda i, be, nu: (i, 0)),
    )
    return pl.pallas_call(
        functools.partial(_experts_body, tf=tf),
        grid_spec=grid_spec,
        out_shape=jax.ShapeDtypeStruct((n_slots, D_MODEL), F32),
        compiler_params=_cparams(("arbitrary",)),
        name="moe_experts",
    )(block_e, n_used, xs, wg, wu, wd)


def _combine_body(slot_ref, x_ref, gate_ref, g_ref, ys_ref, o_ref, y0_scr, y1_scr, sem, *, tm):
    i = pl.program_id(0)

    def row_copy(t, k, dst):
        src = slot_ref[(i * tm + t) * 2 + k]
        return pltpu.make_async_copy(ys_ref.at[pl.ds(src, 1)], dst.at[pl.ds(t, 1)], sem)

    def issue(t, carry):
        row_copy(t, 0, y0_scr).start()
        row_copy(t, 1, y1_scr).start()
        return carry

    lax.fori_loop(0, tm, issue, 0, unroll=8)
    pltpu.make_async_copy(ys_ref.at[pl.ds(0, tm)], y0_scr, sem).wait()
    pltpu.make_async_copy(ys_ref.at[pl.ds(0, tm)], y1_scr, sem).wait()

    gt = gate_ref[...]
    y = x_ref[...] + (gt[:, 0:1] * y0_scr[...] + gt[:, 1:2] * y1_scr[...])
    o_ref[...] = _rms(y, g_ref[...])


def _combine(slot, x2, gates, g_final, ys, tm=256):
    n = x2.shape[0]
    grid_spec = pltpu.PrefetchScalarGridSpec(
        num_scalar_prefetch=1,
        grid=(n // tm,),
        in_specs=[
            pl.BlockSpec((tm, D_MODEL), lambda i, s: (i, 0)),
            pl.BlockSpec((tm, LANES), lambda i, s: (i, 0)),
            pl.BlockSpec((1, D_MODEL), lambda i, s: (0, 0)),
            pl.BlockSpec(memory_space=pl.ANY),
        ],
        out_specs=pl.BlockSpec((tm, D_MODEL), lambda i, s: (i, 0)),
        scratch_shapes=[
            pltpu.VMEM((tm, D_MODEL), F32),
            pltpu.VMEM((tm, D_MODEL), F32),
            pltpu.SemaphoreType.DMA(()),
        ],
    )
    return pl.pallas_call(
        functools.partial(_combine_body, tm=tm),
        grid_spec=grid_spec,
        out_shape=jax.ShapeDtypeStruct((n, D_MODEL), F32),
        compiler_params=_cparams(("arbitrary",)),
        name="moe_combine",
    )(slot, x2, gates, g_final, ys)


def _route_slots(idx, n):
    flat_e = idx[:, :2].reshape(-1)
    onehot = (flat_e[:, None] == jnp.arange(N_EXPERTS, dtype=jnp.int32)[None, :]).astype(jnp.int32)
    grp = 512
    oh3 = onehot.reshape(-1, grp, N_EXPERTS).astype(F32)
    below = jnp.asarray(np.tril(np.ones((grp, grp), np.float32), -1))
    local = jnp.einsum("ts,gse->gte", below, oh3).astype(jnp.int32)
    per_grp = jnp.sum(oh3, axis=1).astype(jnp.int32)
    grp_start = jnp.cumsum(per_grp, axis=0) - per_grp
    counts = jnp.sum(per_grp, axis=0)
    excl = (local + grp_start[:, None, :]).reshape(-1, N_EXPERTS)
    rank = jnp.sum(excl * onehot, axis=1)
    padded = ((counts + MOE_BM - 1) // MOE_BM) * MOE_BM
    pad_end = jnp.cumsum(padded)
    pad_start = pad_end - padded
    slot = (jnp.sum(pad_start[None, :] * onehot, axis=1) + rank).astype(jnp.int32)
    n_blocks = (2 * n) // MOE_BM + N_EXPERTS
    starts = jnp.arange(n_blocks, dtype=jnp.int32) * MOE_BM
    block_e = jnp.sum((starts[:, None] >= pad_end[None, :]).astype(jnp.int32), axis=1)
    block_e = jnp.minimum(block_e, N_EXPERTS - 1).astype(jnp.int32)
    n_used = (pad_end[-1] // MOE_BM).astype(jnp.int32).reshape(1)
    tails = jnp.where(padded > 0, pad_end - MOE_BM, -1)
    spare = pad_end[-1] + jnp.arange(N_EXPERTS, dtype=pad_end.dtype) * MOE_BM
    spare = jnp.where(spare < n_blocks * MOE_BM, spare, -1)
    tail_rows = jnp.concatenate([tails, spare]).astype(jnp.int32)
    return slot, block_e, n_used, tail_rows, n_blocks * MOE_BM


def _mixer_layer(x2, batch, seq, rel_bias, norm_g, w_in, ssm, d_skip, w_glu, b_glu,
                 w_attn_br, w_ssm_br, w_out):
    u_lo = 3 * ATTN_W
    blocks = [w_in[:, which * ATTN_W + g * GROUP_W: which * ATTN_W + (g + 1) * GROUP_W]
              for g in range(N_GROUPS) for which in range(3)]
    w_bf = jnp.concatenate(blocks + [w_in[:, u_lo + SSM_W:]], axis=1).astype(BF16)
    wut_bf = jnp.transpose(w_in[:, u_lo:u_lo + SSM_W]).astype(BF16)
    *qkvs, gates, ut = _in_projection(x2, norm_g.reshape(1, D_MODEL), w_bf, wut_bf, batch, seq)

    os_, ls_ = [], []
    for g, (window, dilation) in enumerate(ATTN_GROUPS):
        bias = _band_bias(rel_bias[:, g * HEADS:(g + 1) * HEADS], window, dilation)
        o, l = _attention_group(qkvs[g], bias, g)
        os_.append(o)
        ls_.append(l)

    zt = _ssm_scan(ut, d_skip, _ssm_tables(*ssm), batch)

    return _merge(os_, ls_, zt, gates, x2, w_glu.astype(BF16), b_glu.reshape(1, SSM_W).astype(F32),
                  w_attn_br.astype(BF16), w_ssm_br.astype(BF16), w_out.astype(BF16))


def kernel(x, rel_bias, norm1_g, w_in, ssm_lam_re, ssm_lam_im, ssm_log_dt, ssm_b_re, ssm_b_im, ssm_c_re, ssm_c_im, ssm_d, w_glu, b_glu, w_attn_br, w_ssm_br, w_out, norm2_g, ffn_w_gate, ffn_w_up, ffn_w_down, moe_router, moe_w_gate, moe_w_up, moe_w_down, final_norm_g):
    batch, seq, d = x.shape
    assert d == D_MODEL and norm1_g.shape[0] == 2 and seq % (16 * BLK) == 0
    n = batch * seq
    x2 = x.reshape(n, d)

    def mixer(x2, l):
        ssm = (ssm_lam_re[l], ssm_lam_im[l], ssm_log_dt[l], ssm_b_re[l], ssm_b_im[l],
               ssm_c_re[l], ssm_c_im[l])
        return _mixer_layer(x2, batch, seq, rel_bias, norm1_g[l], w_in[l], ssm, ssm_d[l], w_glu[l],
                            b_glu[l], w_attn_br[l], w_ssm_br[l], w_out[l])

    x2 = mixer(x2, 0)
    x2 = _dense_ffn(x2, norm2_g[0].reshape(1, d), ffn_w_gate[0].astype(BF16),
                    ffn_w_up[0].astype(BF16), ffn_w_down[0].astype(BF16))

    x2 = mixer(x2, 1)
    g2 = norm2_g[1].reshape(1, d)
    wr_pad = jnp.zeros((d, LANES), F32).at[:, :N_EXPERTS].set(moe_router[0].astype(F32))
    idx, gates = _router(x2, g2, wr_pad)
    slot, block_e, n_used, tail_rows, n_slots = _route_slots(idx, n)
    xs = _dispatch(slot, tail_rows, x2, g2, n_slots)
    ys = _experts(block_e, n_used, xs, moe_w_gate[0].astype(BF16), moe_w_up[0].astype(BF16),
                  moe_w_down[0].astype(BF16))
    out = _combine(slot, x2, gates, final_norm_g.reshape(1, d), ys)
    return out.reshape(batch, seq, d)
```

```python
import functools

import numpy as np
import jax
import jax.numpy as jnp
from jax import lax
from jax.experimental import pallas as pl
from jax.experimental.pallas import tpu as pltpu

F32 = jnp.float32
BF16 = jnp.bfloat16

D_MODEL = 1024
HEAD_DIM = 64
ATTN_GROUPS = ((128, 1), (512, 4), (2048, 16))
N_GROUPS = 3
HEADS = 8
GROUP_W = HEADS * HEAD_DIM
ATTN_W = N_GROUPS * GROUP_W
BLK = 128
REL_BUCKETS = 32
REL_MAX_DIST = 2048
NEG_INF = -1e30
SSM_CH = 16
SSM_W = D_MODEL // 2
SSM_G = SSM_W // SSM_CH
SSM_P = 64
PROJ_W = 3 * ATTN_W + SSM_W + 2 * D_MODEL
N_EXPERTS = 8
MOE_BM = 512
MOE_TT = 512
RUN_ALIGN = 8
RUN_PIECES = tuple(1 << b for b in range(MOE_TT.bit_length() - 1, RUN_ALIGN.bit_length() - 2, -1))
CBUF_ROWS = -(-(2 * MOE_TT + N_EXPERTS * (RUN_ALIGN - 1)) // 16) * 16
EPS = 1e-6
CHUNK = 128
SCAN_LEVELS = 8

LANES = 128
VMEM_LIMIT = 56 * 1024 * 1024


def _cparams(sem):
    return pltpu.CompilerParams(dimension_semantics=sem, vmem_limit_bytes=VMEM_LIMIT)


def _rms(x, g):
    return x * lax.rsqrt(jnp.mean(x * x, axis=-1, keepdims=True) + EPS) * g


def _proj_body(x_ref, g_ref, w_ref, wut_ref, qkv0_ref, qkv1_ref, qkv2_ref, gate_ref, ut_ref, d_scr):
    tm = x_ref.shape[0]
    hb = _rms(x_ref[...], g_ref[...]).astype(BF16)
    ut = lax.dot_general(wut_ref[...], hb, (((1,), (1,)), ((), ())), preferred_element_type=F32)
    for k in range(ut_ref.shape[0]):
        ut_ref[k] = ut[:, k * CHUNK:(k + 1) * CHUNK]

    cw = 2 * LANES
    nl = cw // LANES

    def slab(col):
        return jnp.dot(hb, w_ref[:, col:col + cw], preferred_element_type=F32)

    uses = 0
    for g, (out_ref, (_, r)) in enumerate(zip((qkv0_ref, qkv1_ref, qkv2_ref), ATTN_GROUPS)):
        for which in range(3):
            for lo in range(0, GROUP_W, cw):
                res = slab((3 * g + which) * GROUP_W + lo)
                if r == 1:
                    out_ref[which, 0, :, lo:lo + cw] = res.astype(BF16)
                    continue
                scrs = [d_scr.at[(uses % 2) * nl + k] for k in range(nl)]
                uses += 1
                for k in range(nl):
                    scrs[k][...] = res[:, k * LANES:(k + 1) * LANES]
                for c in range(r):
                    sub = [s[pl.ds(c, tm // r, stride=r), :] for s in scrs]
                    out_ref[which, c, :, lo:lo + cw] = jnp.concatenate(sub, axis=1).astype(BF16)
    for lo in range(0, 2 * D_MODEL, cw):
        gate_ref[:, lo:lo + cw] = slab(3 * N_GROUPS * GROUP_W + lo).astype(BF16)


def _in_projection(x2, g, w_bf, wut_bf, batch, seq, tm=512):
    n = x2.shape[0]
    tiles_per_seq = seq // tm
    wcols = w_bf.shape[1]

    def qkv_spec(r):
        return pl.BlockSpec((3, None, r, tm // r, GROUP_W),
                            lambda i: (0, i // tiles_per_seq, 0, i % tiles_per_seq, 0))

    return pl.pallas_call(
        _proj_body,
        grid=(n // tm,),
        in_specs=[
            pl.BlockSpec((tm, D_MODEL), lambda i: (i, 0)),
            pl.BlockSpec((1, D_MODEL), lambda i: (0, 0)),
            pl.BlockSpec((D_MODEL, wcols), lambda i: (0, 0), pipeline_mode=pl.Buffered(1)),
            pl.BlockSpec((SSM_W, D_MODEL), lambda i: (0, 0), pipeline_mode=pl.Buffered(1)),
        ],
        out_specs=[qkv_spec(r) for _, r in ATTN_GROUPS] + [
            pl.BlockSpec((tm, 2 * D_MODEL), lambda i: (i, 0)),
            pl.BlockSpec((tm // CHUNK, SSM_W, CHUNK), lambda i: (i, 0, 0)),
        ],
        out_shape=[jax.ShapeDtypeStruct((3, batch, r, seq // r, GROUP_W), BF16) for _, r in ATTN_GROUPS] + [
            jax.ShapeDtypeStruct((n, 2 * D_MODEL), BF16),
            jax.ShapeDtypeStruct((n // CHUNK, SSM_W, CHUNK), F32),
        ],
        scratch_shapes=[pltpu.VMEM((4, tm, LANES), F32)],
        compiler_params=_cparams(("parallel",)),
        name="in_projection",
    )(x2, g, w_bf, wut_bf)


def _t5_bucket(dist):
    max_exact = REL_BUCKETS // 2
    d = np.maximum(dist, 1).astype(np.float64)
    large = max_exact + (
        np.log(d / max_exact) / np.log(REL_MAX_DIST / max_exact) * (REL_BUCKETS - max_exact)
    ).astype(np.int32)
    large = np.minimum(large, REL_BUCKETS - 1)
    return np.where(dist < max_exact, dist, large).astype(np.int32)


def _band_bias(table, window, dilation):
    steps = window // dilation
    qi = np.arange(BLK)[:, None]
    kj = np.arange(2 * BLK)[None, :]
    delta = BLK + qi - kj
    band = (delta >= 0) & (delta <= steps)
    bucket = _t5_bucket(np.clip(delta, 0, steps) * dilation)
    onehot = np.eye(REL_BUCKETS, dtype=np.float32)[bucket]
    bias = jnp.einsum("qkb,bh->hqk", onehot, table.astype(F32), precision=lax.Precision.HIGHEST)
    return jnp.where(band[None], bias, NEG_INF)


def _attn_body(q_ref, kp_ref, kc_ref, vp_ref, vc_ref, bias_ref, o_ref, l_ref, *, nsub):
    lane = lax.broadcasted_iota(jnp.int32, (BLK, LANES), 1)
    lo = lane < HEAD_DIM
    keep_lo = jnp.where(lo, 1.0, 0.0).astype(BF16)
    keep_hi = jnp.where(lo, 0.0, 1.0).astype(BF16)
    col = lax.broadcasted_iota(jnp.int32, (BLK, 2 * BLK), 1)
    first_pen = jnp.where(col < BLK, jnp.where(pl.program_id(2) == 0, NEG_INF, 0.0), 0.0)
    for i in range(nsub):
        rows = slice(i * BLK, (i + 1) * BLK)
        q = q_ref[rows, :] * jnp.asarray(HEAD_DIM ** -0.5, BF16)
        if i == 0:
            kw = jnp.concatenate([kp_ref[...], kc_ref[0:BLK, :]], axis=0)
            vw = jnp.concatenate([vp_ref[...], vc_ref[0:BLK, :]], axis=0)
        else:
            kw = kc_ref[(i - 1) * BLK:(i + 1) * BLK, :]
            vw = vc_ref[(i - 1) * BLK:(i + 1) * BLK, :]
        for hp in range(HEADS // 2):
            cols = slice(hp * LANES, (hp + 1) * LANES)
            q2, k2, v2 = q[:, cols], kw[:, cols], vw[:, cols]
            outs, lses = [], []
            for half in range(2):
                qm = q2 * (keep_lo, keep_hi)[half]
                s = lax.dot_general(qm, k2, (((1,), (1,)), ((), ())), preferred_element_type=F32)
                s = s + bias_ref[2 * hp + half]
                if i == 0:
                    s = s + first_pen
                m = jnp.max(s, axis=-1, keepdims=True)
                p = jnp.exp(s - m)
                den = jnp.sum(p, axis=-1, keepdims=True)
                pv = jnp.dot(p.astype(BF16), v2, preferred_element_type=F32)
                outs.append(pv / den)
                lses.append(jnp.broadcast_to(m + jnp.log(den), (BLK, LANES)))
            o_ref[rows, cols] = jnp.where(lo, outs[0], outs[1]).astype(BF16)
            l_ref[rows, cols] = jnp.where(lo, lses[0], lses[1])


def _attention_group(qkv, bias, g):
    _, batch, r, length, _ = qkv.shape
    qb = min(512, length)
    nsub = qb // BLK

    def cur(which):
        return pl.BlockSpec((None, None, None, qb, GROUP_W), lambda b, c, n: (which, b, c, n, 0))

    def prev(which):
        return pl.BlockSpec((None, None, None, BLK, GROUP_W),
                            lambda b, c, n: (which, b, c, jnp.maximum(n * nsub - 1, 0), 0))

    out_spec = pl.BlockSpec((None, None, qb, GROUP_W), lambda b, c, n: (b, c, n, 0))
    return pl.pallas_call(
        functools.partial(_attn_body, nsub=nsub),
        grid=(batch, r, length // qb),
        in_specs=[cur(0), prev(1), cur(1), prev(2), cur(2),
                  pl.BlockSpec((HEADS, BLK, 2 * BLK), lambda b, c, n: (0, 0, 0))],
        out_specs=[out_spec, out_spec],
        out_shape=[
            jax.ShapeDtypeStruct((batch, r, length, GROUP_W), BF16),
            jax.ShapeDtypeStruct((batch, r, length, GROUP_W), F32),
        ],
        compiler_params=_cparams(("parallel", "parallel", "arbitrary")),
        name=f"attention_g{g}",
    )(qkv, qkv, qkv, qkv, qkv, bias)


def _ssm_tables(lam_re, lam_im, log_dt, b_re, b_im, c_re, c_im):
    lam = lax.complex(lam_re.astype(F32), lam_im.astype(F32))
    dt = jnp.exp(log_dt.astype(F32))[:, None]
    lam_dt = lam * dt
    lam_bar = jnp.exp(lam_dt)
    b = lax.complex(b_re.astype(F32), b_im.astype(F32))
    b_bar = ((lam_bar - 1.0) / lam)[..., None] * b
    half = CHUNK // 2
    t = jnp.arange(CHUNK, dtype=F32)

    def power(k):
        return jnp.exp(lam_dt[:, None, :] * jnp.reshape(jnp.asarray(k, F32), (1, -1, 1)))

    p_fwd = power(t - half)
    p_bwd = power(half - t)

    def rows(pw):
        w1 = jnp.concatenate([pw.real, pw.imag], axis=-1)[:, None, :, :]
        w2 = jnp.concatenate([-pw.imag, pw.real], axis=-1)[:, None, :, :]
        br = jnp.transpose(jnp.concatenate([b_bar.real, b_bar.real], axis=1), (0, 2, 1))[:, :, None, :]
        bi = jnp.transpose(jnp.concatenate([b_bar.imag, b_bar.imag], axis=1), (0, 2, 1))[:, :, None, :]
        return (br * w1 + bi * w2).astype(BF16).reshape(SSM_G, SSM_CH * CHUNK, 2 * SSM_P)

    def cols(pw):
        pt = jnp.transpose(pw, (0, 2, 1))
        w1 = jnp.concatenate([pt.real, -pt.imag], axis=1)[:, None, :, :]
        w2 = jnp.concatenate([-pt.imag, -pt.real], axis=1)[:, None, :, :]
        cr = jnp.concatenate([c_re, c_re], axis=2).astype(F32)[:, :, :, None]
        ci = jnp.concatenate([c_im, c_im], axis=2).astype(F32)[:, :, :, None]
        return (cr * w1 + ci * w2).astype(BF16)

    a_tab = rows(p_bwd)
    d_tab = cols(p_fwd)
    w_tab = rows(p_bwd * power(CHUNK - 1.0 - half))
    v_tab = cols(p_fwd * power(half + 1.0))
    lc = power([float(CHUNK * 2 ** k) for k in range(SCAN_LEVELS)])
    l1 = jnp.concatenate([lc.real, lc.real], axis=-1)
    l2 = jnp.concatenate([-lc.imag, lc.imag], axis=-1)
    return a_tab, d_tab, w_tab, v_tab, l1, l2


def _gelu_tanh(y):
    return y * jax.nn.sigmoid(1.5957691216057308 * (y + 0.044715 * (y * y * y)))


def _ssm_body(d_ref, u_ref, a_ref, dt_ref, w_ref, v_ref, l1_ref, l2_ref, z_ref, m_scr, mask_scr, z_scr, *,
              ncb):
    g = pl.program_id(0)
    nc = u_ref.shape[0]
    width = SSM_CH * CHUNK
    cb = mask_scr.shape[1]

    @pl.when(g == 0)
    def _():
        s_idx = lax.broadcasted_iota(jnp.int32, (width, cb), 0) & (CHUNK - 1)
        t_idx = lax.broadcasted_iota(jnp.int32, (width, cb), 1) & (CHUNK - 1)
        mask_scr[...] = jnp.where(t_idx >= s_idx, -1, 0).astype(jnp.int32)

    for k in range(width // cb):
        per = cb // CHUNK
        rhs = jnp.concatenate([dt_ref[c] for c in range(k * per, (k + 1) * per)], axis=1)
        mk = jnp.dot(a_ref[...], rhs, preferred_element_type=F32)
        kept = lax.bitcast_convert_type(mk, jnp.int32) & mask_scr[...]
        m_scr[:, k * cb:(k + 1) * cb] = lax.bitcast_convert_type(kept, F32).astype(BF16)

    u2 = u_ref.reshape(nc * SSM_CH, CHUNK)
    us =[u2[pl.ds(c, nc, stride=SSM_CH), :] for c in range(SSM_CH)]
    x = jnp.concatenate(us, axis=1).astype(BF16)

    acc = jnp.dot(x, w_ref[...], preferred_element_type=F32)
    rmod = lax.broadcasted_iota(jnp.int32, (nc, 2 * SSM_P), 0) & (ncb - 1)
    for k in range(ncb.bit_length() - 1):
        d = 1 << k
        sh = jnp.where(rmod >= d, pltpu.roll(acc, d, 0), 0.0)
        acc = acc + sh * l1_ref[k:k + 1, :] + pltpu.roll(sh, SSM_P, 1) * l2_ref[k:k + 1, :]
    x_in = jnp.where(rmod >= 1, pltpu.roll(acc, 1, 0), 0.0)

    y = jnp.dot(x, m_scr[...], preferred_element_type=F32)
    v_all = jnp.concatenate([v_ref[c] for c in range(SSM_CH)], axis=1)
    y = y + jnp.dot(x_in.astype(BF16), v_all, preferred_element_type=F32)
    for c in range(SSM_CH):
        yc = y[:, c * CHUNK:(c + 1) * CHUNK] + d_ref[g * SSM_CH + c] * us[c]
        z_scr[pl.ds(c, nc, stride=SSM_CH), :] = _gelu_tanh(yc)
    z_ref[...] = z_scr[...].reshape(nc, SSM_CH, CHUNK)


def _ssm_scan(u3, d_skip, tables, nbatch):
    a_tab, d_tab, w_tab, v_tab, l1, l2 = tables
    nc = u3.shape[0]
    ncb = nc // nbatch
    assert ncb & (ncb - 1) == 0 and ncb <= 2 ** SCAN_LEVELS
    width = SSM_CH * CHUNK
    grid_spec = pltpu.PrefetchScalarGridSpec(
        num_scalar_prefetch=1,
        grid=(SSM_G,),
        in_specs=[
            pl.BlockSpec((nc, SSM_CH, CHUNK), lambda g, d: (0, g, 0)),
            pl.BlockSpec((None, width, 2 * SSM_P), lambda g, d: (g, 0, 0)),
            pl.BlockSpec((None, SSM_CH, 2 * SSM_P, CHUNK), lambda g, d: (g, 0, 0, 0)),
            pl.BlockSpec((None, width, 2 * SSM_P), lambda g, d: (g, 0, 0)),
            pl.BlockSpec((None, SSM_CH, 2 * SSM_P, CHUNK), lambda g, d: (g, 0, 0, 0)),
            pl.BlockSpec((None, SCAN_LEVELS, 2 * SSM_P), lambda g, d: (g, 0, 0)),
            pl.BlockSpec((None, SCAN_LEVELS, 2 * SSM_P), lambda g, d: (g, 0, 0)),
        ],
        out_specs=pl.BlockSpec((nc, SSM_CH, CHUNK), lambda g, d: (0, g, 0)),
        scratch_shapes=[pltpu.VMEM((width, width), BF16), pltpu.VMEM((width, 512), jnp.int32),
                        pltpu.VMEM((nc * SSM_CH, CHUNK), F32)],
    )
    return pl.pallas_call(
        functools.partial(_ssm_body, ncb=ncb),
        grid_spec=grid_spec,
        out_shape=jax.ShapeDtypeStruct((nc, SSM_W, CHUNK), F32),
        compiler_params=_cparams(("arbitrary",)),
        name="ssm_scan",
    )(d_skip.astype(F32), u3, a_tab, d_tab, w_tab, v_tab, l1, l2)


def _merge_body(o0, o1, o2, l0, l1, l2, zt_ref, ga_ref, gs_ref, x_ref,
                wglu_ref, bglu_ref, wab_ref, wsb_ref, wout_ref, out_ref, tok_scr):
    def token_major(ref, slot):
        r, rows, _ = ref.shape
        if r == 1:
            return ref[0].astype(F32)
        nl = GROUP_W // LANES
        scrs = [tok_scr.at[slot * nl + k] for k in range(nl)]
        for c in range(r):
            sub = ref[c].astype(F32)
            for k in range(nl):
                scrs[k][pl.ds(c, rows, stride=r), :] = sub[:, k * LANES:(k + 1) * LANES]
        return jnp.concatenate([s[...] for s in scrs], axis=1)

    a0, a1, a2 = token_major(l0, 0), token_major(l1, 0), token_major(l2, 1)
    v0, v1, v2 = token_major(o0, 2), token_major(o1, 2), token_major(o2, 3)
    mx = jnp.maximum(jnp.maximum(a0, a1), a2)
    e0, e1, e2 = jnp.exp(a0 - mx), jnp.exp(a1 - mx), jnp.exp(a2 - mx)
    mix = (e0 * v0 + e1 * v1 + e2 * v2) / (e0 + e1 + e2)
    y_attn = jnp.dot(mix.astype(BF16), wab_ref[...], preferred_element_type=F32)

    z = jnp.concatenate([zt_ref[k].T for k in range(zt_ref.shape[0])], axis=0).astype(BF16)
    gl = jnp.dot(z, wglu_ref[...], preferred_element_type=F32) + bglu_ref[...]
    sg = z.astype(F32) * jax.nn.sigmoid(gl)
    y_ssm = jnp.dot(sg.astype(BF16), wsb_ref[...], preferred_element_type=F32)

    merged = (jax.nn.sigmoid(ga_ref[...].astype(F32)) * y_attn
              + jax.nn.sigmoid(gs_ref[...].astype(F32)) * y_ssm)
    out_ref[...] = x_ref[...] + jnp.dot(merged.astype(BF16), wout_ref[...], preferred_element_type=F32)


def _merge(os_, ls_, zt, gates, x2, wglu, bglu, wab, wsb, wout, tm=512):
    n = x2.shape[0]
    tiles_per_seq = os_[0].shape[2] // tm
    row = lambda i: (i, 0)
    const = lambda i: (0, 0)

    def group_spec(a):
        r = a.shape[1]
        return pl.BlockSpec((None, r, tm // r, GROUP_W),
                            lambda i: (i // tiles_per_seq, 0, i % tiles_per_seq, 0))

    in_specs = (
        [group_spec(a) for a in os_] + [group_spec(a) for a in ls_]
        + [
            pl.BlockSpec((tm // CHUNK, SSM_W, CHUNK), lambda i: (i, 0, 0)),
            pl.BlockSpec((tm, D_MODEL), lambda i: (i, 0)),
            pl.BlockSpec((tm, D_MODEL), lambda i: (i, 1)),
            pl.BlockSpec((tm, D_MODEL), row),
            pl.BlockSpec((SSM_W, SSM_W), const),
            pl.BlockSpec((1, SSM_W), const),
            pl.BlockSpec((GROUP_W, D_MODEL), const),
            pl.BlockSpec((SSM_W, D_MODEL), const),
            pl.BlockSpec((D_MODEL, D_MODEL), const),
        ]
    )
    return pl.pallas_call(
        _merge_body,
        grid=(n // tm,),
        in_specs=in_specs,
        out_specs=pl.BlockSpec((tm, D_MODEL), row),
        out_shape=jax.ShapeDtypeStruct((n, D_MODEL), F32),
        scratch_shapes=[pltpu.VMEM((4 * GROUP_W // LANES, tm, LANES), F32)],
        compiler_params=_cparams(("parallel",)),
        name="merge",
    )(*os_, *ls_, zt, gates, gates, x2, wglu, bglu, wab, wsb, wout)


def _ffn_body(x_ref, g_ref, wg_ref, wu_ref, wd_ref, o_ref, *, tf):
    h = _rms(x_ref[...], g_ref[...]).astype(BF16)
    for f in range(wg_ref.shape[1] // tf):
        cols = slice(f * tf, (f + 1) * tf)
        a = jnp.dot(h, wg_ref[:, cols], preferred_element_type=F32)
        b = jnp.dot(h, wu_ref[:, cols], preferred_element_type=F32)
        act = (a * jax.nn.sigmoid(a) * b).astype(BF16)
        part = jnp.dot(act, wd_ref[cols, :], preferred_element_type=F32)
        if f == 0:
            o_ref[...] = x_ref[...] + part
        else:
            o_ref[...] += part


def _dense_ffn(x2, g, wg, wu, wd, tm=512, tf=256):
    n = x2.shape[0]
    dff = wg.shape[1]
    assert dff % tf == 0
    resident = lambda shape: pl.BlockSpec(shape, lambda i: (0, 0), pipeline_mode=pl.Buffered(1))
    return pl.pallas_call(
        functools.partial(_ffn_body, tf=tf),
        grid=(n // tm,),
        in_specs=[
            pl.BlockSpec((tm, D_MODEL), lambda i: (i, 0)),
            pl.BlockSpec((1, D_MODEL), lambda i: (0, 0)),
            resident((D_MODEL, dff)),
            resident((D_MODEL, dff)),
            resident((dff, D_MODEL)),
        ],
        out_specs=pl.BlockSpec((tm, D_MODEL), lambda i: (i, 0)),
        out_shape=jax.ShapeDtypeStruct((n, D_MODEL), F32),
        compiler_params=_cparams(("parallel",)),
        name="dense_ffn",
    )(x2, g, wg, wu, wd)


def _router_body(x_ref, g_ref, wr_ref, idx_ref, gate_ref, idxt_ref):
    h = _rms(x_ref[...], g_ref[...])
    w = wr_ref[...]
    h_hi, w_hi = h.astype(BF16), w.astype(BF16)
    h_lo = (h - h_hi.astype(F32)).astype(BF16)
    w_lo = (w - w_hi.astype(F32)).astype(BF16)
    logits = (jnp.dot(h_hi, w_hi, preferred_element_type=F32)
              + (jnp.dot(h_hi, w_lo, preferred_element_type=F32)
                 + jnp.dot(h_lo, w_hi, preferred_element_type=F32)))
    lane = lax.broadcasted_iota(jnp.int32, logits.shape, 1)
    lane_f = lane.astype(F32)
    logits = jnp.where(lane < N_EXPERTS, logits, -jnp.inf)
    v1 = jnp.max(logits, axis=-1, keepdims=True)
    i1 = jnp.min(jnp.where(logits == v1, lane_f, float(LANES)), axis=-1, keepdims=True)
    rest = jnp.where(lane_f == i1, -jnp.inf, logits)
    v2 = jnp.max(rest, axis=-1, keepdims=True)
    i2 = jnp.min(jnp.where(rest == v2, lane_f, float(LANES)), axis=-1, keepdims=True)
    e = jnp.exp(v2 - v1)
    g1 = 1.0 / (1.0 + e)
    g2 = e / (1.0 + e)
    idx_f = jnp.where(lane == 0, i1, jnp.where(lane == 1, i2, 0.0))
    idx_ref[...] = idx_f.astype(jnp.int32)
    gate_ref[...] = jnp.where(lane == 0, g1, jnp.where(lane == 1, g2, 0.0))
    idxt_ref[...] = idx_f.T[:idxt_ref.shape[0], :].astype(jnp.int32)


def _router(x2, g, wr_pad, tm=1024):
    n = x2.shape[0]
    return pl.pallas_call(
        _router_body,
        grid=(n // tm,),
        in_specs=[
            pl.BlockSpec((tm, D_MODEL), lambda i: (i, 0)),
            pl.BlockSpec((1, D_MODEL), lambda i: (0, 0)),
            pl.BlockSpec((D_MODEL, LANES), lambda i: (0, 0)),
        ],
        out_specs=[
            pl.BlockSpec((tm, LANES), lambda i: (i, 0)),
            pl.BlockSpec((tm, LANES), lambda i: (i, 0)),
            pl.BlockSpec((8, tm), lambda i: (0, i)),
        ],
        out_shape=[
            jax.ShapeDtypeStruct((n, LANES), jnp.int32),
            jax.ShapeDtypeStruct((n, LANES), F32),
            jax.ShapeDtypeStruct((8, n), jnp.int32),
        ],
        compiler_params=_cparams(("parallel",)),
        name="router",
    )(x2, g, wr_pad)


def _pack_bf16_pairs(hb):
    half = hb.shape[1] // 2
    lo = lax.bitcast_convert_type(hb[:, :half].astype(F32), jnp.uint32)
    hi = lax.bitcast_convert_type(hb[:, half:].astype(F32), jnp.uint32)
    return (hi & jnp.uint32(0xFFFF0000)) | (lo >> 16)


def _unpack_bf16_pairs(xu):
    lo = lax.bitcast_convert_type(xu << 16, F32).astype(BF16)
    hi = lax.bitcast_convert_type(xu & jnp.uint32(0xFFFF0000), F32).astype(BF16)
    return lo, hi


def _for_each_run_piece(i, start_ref, loff_ref, len_ref, fn):
    for e in range(N_EXPERTS):
        j = i * N_EXPERTS + e
        length, boff, soff = len_ref[j], loff_ref[j], start_ref[j]
        done = 0
        for p in RUN_PIECES:
            cond = (length & p) != 0
            fn(cond, pl.multiple_of(boff + done, RUN_ALIGN), pl.multiple_of(soff + done, RUN_ALIGN), p)
            done = done + jnp.where(cond, p, 0)


def _dispatch_body(start_ref, loff_ref, len_ref, tail_ref, x_ref, g_ref, idxt_ref, off_ref, cnt0_ref,
                   xs_ref, cbuf, tri_scr, zero_scr, sem, zsem):
    i = pl.program_id(0)
    tt = x_ref.shape[0]

    @pl.when(i == 0)
    def _():
        r = lax.broadcasted_iota(jnp.int32, (tt, tt), 0)
        c = lax.broadcasted_iota(jnp.int32, (tt, tt), 1)
        tri_scr[...] = jnp.where(r < c, 1.0, 0.0).astype(BF16)
        zero_scr[...] = jnp.zeros_like(zero_scr)

        def fill(e):
            row = pl.multiple_of(jnp.maximum(tail_ref[e], 0), MOE_BM)
            return pltpu.make_async_copy(zero_scr, xs_ref.at[pl.ds(row, MOE_BM)], zsem)

        for e in range(tail_ref.shape[0]):
            pl.when(tail_ref[e] >= 0)(lambda e=e: fill(e).start())
        for e in range(tail_ref.shape[0]):
            pl.when(tail_ref[e] >= 0)(lambda e=e: fill(e).wait())

    hb = _rms(x_ref[...], g_ref[...]).astype(BF16)

    sub = lax.broadcasted_iota(jnp.int32, (N_EXPERTS, tt), 0)
    pos = []
    for k in range(2):
        oh = jnp.where(sub == idxt_ref[k:k + 1, :], 1.0, 0.0)
        rank = jnp.dot(oh.astype(BF16), tri_scr[...], preferred_element_type=F32)
        base = off_ref[:, :1] if k == 0 else off_ref[:, :1] + cnt0_ref[:, :1]
        pos.append(jnp.sum(oh * (base + rank), axis=0, keepdims=True))

    rows = lax.broadcasted_iota(jnp.int32, (cbuf.shape[0], tt), 0).astype(F32)
    perm = (jnp.where(rows == pos[0], 1.0, 0.0) + jnp.where(rows == pos[1], 1.0, 0.0)).astype(BF16)
    cbuf[...] = _pack_bf16_pairs(jnp.dot(perm, hb, preferred_element_type=F32).astype(BF16))

    def piece(op):
        def fn(cond, brow, srow, p):
            cp = pltpu.make_async_copy(cbuf.at[pl.ds(brow, p)], xs_ref.at[pl.ds(srow, p)], sem)
            pl.when(cond)(getattr(cp, op))
        return fn

    _for_each_run_piece(i, start_ref, loff_ref, len_ref, piece("start"))
    _for_each_run_piece(i, start_ref, loff_ref, len_ref, piece("wait"))


def _dispatch(plan, x2, g, idxt):
    n = x2.shape[0]
    tt = MOE_TT
    smem = lambda i, *_: (i, 0, 0)
    grid_spec = pltpu.PrefetchScalarGridSpec(
        num_scalar_prefetch=4,
        grid=(n // tt,),
        in_specs=[
            pl.BlockSpec((tt, D_MODEL), lambda i, *_: (i, 0)),
            pl.BlockSpec((1, D_MODEL), lambda i, *_: (0, 0)),
            pl.BlockSpec((8, tt), lambda i, *_: (0, i)),
            pl.BlockSpec((None, N_EXPERTS, LANES), smem),
            pl.BlockSpec((None, N_EXPERTS, LANES), smem),
        ],
        out_specs=pl.BlockSpec(memory_space=pl.ANY),
        scratch_shapes=[
            pltpu.VMEM((CBUF_ROWS, D_MODEL // 2), jnp.uint32),
            pltpu.VMEM((tt, tt), BF16),
            pltpu.VMEM((MOE_BM, D_MODEL // 2), jnp.uint32),
            pltpu.SemaphoreType.DMA(()),
            pltpu.SemaphoreType.DMA(()),
        ],
    )
    return pl.pallas_call(
        _dispatch_body,
        grid_spec=grid_spec,
        out_shape=jax.ShapeDtypeStruct((plan["n_slots"], D_MODEL // 2), jnp.uint32),
        compiler_params=_cparams(("arbitrary",)),
        name="moe_dispatch",
    )(plan["start"], plan["loff"], plan["len8"], plan["tail_rows"], x2, g, idxt,
      plan["off_lanes"], plan["cnt0_lanes"])


def _experts_body(be_ref, nu_ref, xs_ref, wg_ref, wu_ref, wd_ref, ys_ref, acc_scr, *, tf):
    del be_ref
    i = pl.program_id(0)
    half = D_MODEL // 2

    @pl.when(i < nu_ref[0])
    def _():
        lo, hi = _unpack_bf16_pairs(xs_ref[...])
        for f in range(wg_ref.shape[1] // tf):
            cols = slice(f * tf, (f + 1) * tf)
            a = (jnp.dot(lo, wg_ref[:half, cols], preferred_element_type=F32)
                 + jnp.dot(hi, wg_ref[half:, cols], preferred_element_type=F32))
            b = (jnp.dot(lo, wu_ref[:half, cols], preferred_element_type=F32)
                 + jnp.dot(hi, wu_ref[half:, cols], preferred_element_type=F32))
            act = (a * jax.nn.sigmoid(a) * b).astype(BF16)
            part = jnp.dot(act, wd_ref[cols, :], preferred_element_type=F32)
            if f == 0:
                acc_scr[...] = part
            else:
                acc_scr[...] += part
        ys_ref[...] = _pack_bf16_pairs(acc_scr[...].astype(BF16))

    @pl.when(i >= nu_ref[0])
    def _():
        ys_ref[...] = jnp.zeros_like(ys_ref)


def _experts(block_e, n_used, xs, wg, wu, wd, tf=512):
    n_slots = xs.shape[0]
    n_blocks = n_slots // MOE_BM
    dff = wg.shape[2]

    def blk(i, nu):
        return jnp.minimum(i, nu[0] - 1)

    def wspec(rows, cols):
        return pl.BlockSpec((None, rows, cols), lambda i, be, nu: (be[blk(i, nu)], 0, 0),
                            pipeline_mode=pl.Buffered(1))

    grid_spec = pltpu.PrefetchScalarGridSpec(
        num_scalar_prefetch=2,
        grid=(n_blocks,),
        in_specs=[
            pl.BlockSpec((MOE_BM, D_MODEL // 2), lambda i, be, nu: (blk(i, nu), 0)),
            wspec(D_MODEL, dff),
            wspec(D_MODEL, dff),
            wspec(dff, D_MODEL),
        ],
        out_specs=pl.BlockSpec((MOE_BM, D_MODEL // 2), lambda i, be, nu: (i, 0)),
        scratch_shapes=[pltpu.VMEM((MOE_BM, D_MODEL), F32)],
    )
    return pl.pallas_call(
        functools.partial(_experts_body, tf=tf),
        grid_spec=grid_spec,
        out_shape=jax.ShapeDtypeStruct((n_slots, D_MODEL // 2), jnp.uint32),
        compiler_params=_cparams(("arbitrary",)),
        name="moe_experts",
    )(block_e, n_used, xs, wg, wu, wd)


def _combine_body(start_ref, loff_ref, len_ref, x_ref, idx_ref, gate_ref, meta_ref, g_ref, ys_ref, o_ref,
                  ybuf, tri_scr, sem):
    i = pl.program_id(0)
    tt = x_ref.shape[0]

    @pl.when(i == 0)
    def _():
        r = lax.broadcasted_iota(jnp.int32, (tt, tt), 0)
        c = lax.broadcasted_iota(jnp.int32, (tt, tt), 1)
        tri_scr[...] = jnp.where(c < r, 1.0, 0.0).astype(BF16)
        ybuf[...] = jnp.zeros_like(ybuf)

    def piece(op):
        def fn(cond, brow, srow, p):
            cp = pltpu.make_async_copy(ys_ref.at[pl.ds(srow, p)], ybuf.at[pl.ds(brow, p)], sem)
            pl.when(cond)(getattr(cp, op))
        return fn

    _for_each_run_piece(i, start_ref, loff_ref, len_ref, piece("start"))

    lane = lax.broadcasted_iota(jnp.int32, (tt, LANES), 1)
    idx = idx_ref[...]
    gt = gate_ref[...]
    cols = lax.broadcasted_iota(jnp.int32, (tt, ybuf.shape[0]), 1).astype(F32)
    sel = None
    for k in range(2):
        oh = jnp.where(lane == idx[:, k:k + 1], 1.0, 0.0)
        rank = jnp.dot(tri_scr[...], oh.astype(BF16), preferred_element_type=F32)
        base = meta_ref[0:1, :] if k == 0 else meta_ref[0:1, :] + meta_ref[1:2, :]
        pos = jnp.sum(oh * (base + rank), axis=1, keepdims=True)
        term = jnp.where(cols == pos, gt[:, k:k + 1], 0.0)
        sel = term if sel is None else sel + term
    sel = sel.astype(BF16)

    _for_each_run_piece(i, start_ref, loff_ref, len_ref, piece("wait"))

    lo, hi = _unpack_bf16_pairs(ybuf[...])
    y = jnp.concatenate([jnp.dot(sel, lo, preferred_element_type=F32),
                         jnp.dot(sel, hi, preferred_element_type=F32)], axis=1)
    o_ref[...] = _rms(x_ref[...] + y, g_ref[...])


def _combine(plan, x2, idx, gates, g_final, ys):
    n = x2.shape[0]
    tt = MOE_TT
    grid_spec = pltpu.PrefetchScalarGridSpec(
        num_scalar_prefetch=3,
        grid=(n // tt,),
        in_specs=[
            pl.BlockSpec((tt, D_MODEL), lambda i, *_: (i, 0)),
            pl.BlockSpec((tt, LANES), lambda i, *_: (i, 0)),
            pl.BlockSpec((tt, LANES), lambda i, *_: (i, 0)),
            pl.BlockSpec((None, 8, LANES), lambda i, *_: (i, 0, 0)),
            pl.BlockSpec((1, D_MODEL), lambda i, *_: (0, 0)),
            pl.BlockSpec(memory_space=pl.ANY),
        ],
        out_specs=pl.BlockSpec((tt, D_MODEL), lambda i, *_: (i, 0)),
        scratch_shapes=[
            pltpu.VMEM((CBUF_ROWS, D_MODEL // 2), jnp.uint32),
            pltpu.VMEM((tt, tt), BF16),
            pltpu.SemaphoreType.DMA(()),
        ],
    )
    return pl.pallas_call(
        _combine_body,
        grid_spec=grid_spec,
        out_shape=jax.ShapeDtypeStruct((n, D_MODEL), F32),
        compiler_params=_cparams(("arbitrary",)),
        name="moe_combine",
    )(plan["start"], plan["loff"], plan["len8"], x2, idx, gates, plan["meta_rows"], g_final, ys)


def _route_plan(idxt, n):
    nt = n // MOE_TT
    e2 = idxt[:2].reshape(2, nt, MOE_TT)
    oh = (e2[..., None] == jnp.arange(N_EXPERTS, dtype=jnp.int32)).astype(jnp.int32)
    cnt = jnp.sum(oh, axis=2)
    cnt0 = cnt[0]
    len8 = (cnt[0] + cnt[1] + RUN_ALIGN - 1) // RUN_ALIGN * RUN_ALIGN
    loff = jnp.cumsum(len8, axis=1) - len8
    region = jnp.sum(len8, axis=0)
    padded = (region + MOE_BM - 1) // MOE_BM * MOE_BM
    pad_end = jnp.cumsum(padded)
    start = (pad_end - padded)[None, :] + jnp.cumsum(len8, axis=0) - len8
    n_blocks = (2 * n + nt * N_EXPERTS * (RUN_ALIGN - 1) + MOE_BM - 1) // MOE_BM + N_EXPERTS
    starts = jnp.arange(n_blocks, dtype=jnp.int32) * MOE_BM
    block_e = jnp.sum((starts[:, None] >= pad_end[None, :]).astype(jnp.int32), axis=1)
    block_e = jnp.minimum(block_e, N_EXPERTS - 1).astype(jnp.int32)
    n_used = (pad_end[-1] // MOE_BM).astype(jnp.int32).reshape(1)
    tails = jnp.where(padded > 0, pad_end - MOE_BM, -1)
    spare = pad_end[-1] + jnp.arange(n_blocks - (2 * n) // MOE_BM, dtype=pad_end.dtype) * MOE_BM
    spare = jnp.where(spare < n_blocks * MOE_BM, spare, -1)
    lanes = lambda a: jnp.broadcast_to(a.astype(F32)[:, :, None], (nt, N_EXPERTS, LANES))
    meta_rows = jnp.zeros((nt, 8, LANES), F32)
    meta_rows = meta_rows.at[:, 0, :N_EXPERTS].set(loff.astype(F32)).at[:, 1, :N_EXPERTS].set(cnt0.astype(F32))
    flat = lambda a: a.reshape(-1).astype(jnp.int32)
    return dict(start=flat(start), loff=flat(loff), len8=flat(len8), block_e=block_e, n_used=n_used,
                tail_rows=jnp.concatenate([tails, spare]).astype(jnp.int32),
                off_lanes=lanes(loff), cnt0_lanes=lanes(cnt0), meta_rows=meta_rows,
                n_slots=n_blocks * MOE_BM)


def _mixer_layer(x2, batch, seq, rel_bias, norm_g, w_in, ssm, d_skip, w_glu, b_glu,
                 w_attn_br, w_ssm_br, w_out):
    u_lo = 3 * ATTN_W
    blocks = [w_in[:, which * ATTN_W + g * GROUP_W: which * ATTN_W + (g + 1) * GROUP_W]
              for g in range(N_GROUPS) for which in range(3)]
    w_bf = jnp.concatenate(blocks + [w_in[:, u_lo + SSM_W:]], axis=1).astype(BF16)
    wut_bf = jnp.transpose(w_in[:, u_lo:u_lo + SSM_W]).astype(BF16)
    *qkvs, gates, ut = _in_projection(x2, norm_g.reshape(1, D_MODEL), w_bf, wut_bf, batch, seq)

    os_, ls_ = [], []
    for g, (window, dilation) in enumerate(ATTN_GROUPS):
        bias = _band_bias(rel_bias[:, g * HEADS:(g + 1) * HEADS], window, dilation)
        o, l = _attention_group(qkvs[g], bias, g)
        os_.append(o)
        ls_.append(l)

    zt = _ssm_scan(ut, d_skip, _ssm_tables(*ssm), batch)

    return _merge(os_, ls_, zt, gates, x2, w_glu.astype(BF16), b_glu.reshape(1, SSM_W).astype(F32),
                  w_attn_br.astype(BF16), w_ssm_br.astype(BF16), w_out.astype(BF16))


def kernel(x, rel_bias, norm1_g, w_in, ssm_lam_re, ssm_lam_im, ssm_log_dt, ssm_b_re, ssm_b_im, ssm_c_re, ssm_c_im, ssm_d, w_glu, b_glu, w_attn_br, w_ssm_br, w_out, norm2_g, ffn_w_gate, ffn_w_up, ffn_w_down, moe_router, moe_w_gate, moe_w_up, moe_w_down, final_norm_g):
    batch, seq, d = x.shape
    assert d == D_MODEL and norm1_g.shape[0] == 2 and seq % (16 * BLK) == 0
    n = batch * seq
    x2 = x.reshape(n, d)

    def mixer(x2, l):
        ssm = (ssm_lam_re[l], ssm_lam_im[l], ssm_log_dt[l], ssm_b_re[l], ssm_b_im[l],
               ssm_c_re[l], ssm_c_im[l])
        return _mixer_layer(x2, batch, seq, rel_bias, norm1_g[l], w_in[l], ssm, ssm_d[l], w_glu[l],
                            b_glu[l], w_attn_br[l], w_ssm_br[l], w_out[l])

    x2 = mixer(x2, 0)
    x2 = _dense_ffn(x2, norm2_g[0].reshape(1, d), ffn_w_gate[0].astype(BF16),
                    ffn_w_up[0].astype(BF16), ffn_w_down[0].astype(BF16))

    x2 = mixer(x2, 1)
    g2 = norm2_g[1].reshape(1, d)
    wr_pad = jnp.zeros((d, LANES), F32).at[:, :N_EXPERTS].set(moe_router[0].astype(F32))
    idx, gates, idxt = _router(x2, g2, wr_pad)
    plan = _route_plan(idxt, n)
    xs = _dispatch(plan, x2, g2, idxt)
    ys = _experts(plan["block_e"], plan["n_used"], xs, moe_w_gate[0].astype(BF16),
                  moe_w_up[0].astype(BF16), moe_w_down[0].astype(BF16))
    out = _combine(plan, x2, idx, gates, final_norm_g.reshape(1, d), ys)
    return out.reshape(batch, seq, d)
```

```python
import functools

import numpy as np
import jax
import jax.numpy as jnp
from jax import lax
from jax.experimental import pallas as pl
from jax.experimental.pallas import tpu as pltpu

F32 = jnp.float32
BF16 = jnp.bfloat16

D_MODEL = 1024
HEAD_DIM = 64
ATTN_GROUPS = ((128, 1), (512, 4), (2048, 16))
N_GROUPS = 3
HEADS = 8
GROUP_W = HEADS * HEAD_DIM
ATTN_W = N_GROUPS * GROUP_W
BLK = 128
REL_BUCKETS = 32
REL_MAX_DIST = 2048
NEG_INF = -1e30
SSM_CH = 16
SSM_W = D_MODEL // 2
SSM_G = SSM_W // SSM_CH
SSM_P = 64
PROJ_W = 3 * ATTN_W + SSM_W + 2 * D_MODEL
N_EXPERTS = 8
MOE_BM = 512
MOE_TT = 512
RUN_ALIGN = 8
RUN_PIECES = tuple(1 << b for b in range(MOE_TT.bit_length() - 1, RUN_ALIGN.bit_length() - 2, -1))
CBUF_ROWS = -(-(2 * MOE_TT + N_EXPERTS * (RUN_ALIGN - 1)) // 16) * 16
EPS = 1e-6
CHUNK = 128
SCAN_LEVELS = 8

MERGE_SPLIT = 1
LANES = 128
VMEM_LIMIT = 56 * 1024 * 1024


def _cparams(sem):
    return pltpu.CompilerParams(dimension_semantics=sem, vmem_limit_bytes=VMEM_LIMIT)


def _rms(x, g):
    return x * lax.rsqrt(jnp.mean(x * x, axis=-1, keepdims=True) + EPS) * g


def _proj_body(x_ref, g_ref, w_ref, wut_ref, qkv0_ref, qkv1_ref, qkv2_ref, gate_ref, ut_ref, d_scr):
    tm = x_ref.shape[0]
    hb = _rms(x_ref[...], g_ref[...]).astype(BF16)
    ut = lax.dot_general(wut_ref[...], hb, (((1,), (1,)), ((), ())), preferred_element_type=F32)
    for k in range(ut_ref.shape[0]):
        ut_ref[k] = ut[:, k * CHUNK:(k + 1) * CHUNK]

    cw = 2 * LANES
    nl = cw // LANES

    def slab(col):
        return jnp.dot(hb, w_ref[:, col:col + cw], preferred_element_type=F32)

    uses = 0
    for g, (out_ref, (_, r)) in enumerate(zip((qkv0_ref, qkv1_ref, qkv2_ref), ATTN_GROUPS)):
        for which in range(3):
            for lo in range(0, GROUP_W, cw):
                res = slab((3 * g + which) * GROUP_W + lo)
                if r == 1:
                    out_ref[which, 0, :, lo:lo + cw] = res.astype(BF16)
                    continue
                scrs = [d_scr.at[(uses % 2) * nl + k] for k in range(nl)]
                uses += 1
                for k in range(nl):
                    scrs[k][...] = res[:, k * LANES:(k + 1) * LANES]
                for c in range(r):
                    sub = [s[pl.ds(c, tm // r, stride=r), :] for s in scrs]
                    out_ref[which, c, :, lo:lo + cw] = jnp.concatenate(sub, axis=1).astype(BF16)
    for lo in range(0, 2 * D_MODEL, cw):
        gate_ref[:, lo:lo + cw] = slab(3 * N_GROUPS * GROUP_W + lo).astype(BF16)


def _in_projection(x2, g, w_bf, wut_bf, batch, seq, tm=512):
    n = x2.shape[0]
    tiles_per_seq = seq // tm
    wcols = w_bf.shape[1]

    def qkv_spec(r):
        return pl.BlockSpec((3, None, r, tm // r, GROUP_W),
                            lambda i: (0, i // tiles_per_seq, 0, i % tiles_per_seq, 0))

    return pl.pallas_call(
        _proj_body,
        grid=(n // tm,),
        in_specs=[
            pl.BlockSpec((tm, D_MODEL), lambda i: (i, 0)),
            pl.BlockSpec((1, D_MODEL), lambda i: (0, 0)),
            pl.BlockSpec((D_MODEL, wcols), lambda i: (0, 0), pipeline_mode=pl.Buffered(1)),
            pl.BlockSpec((SSM_W, D_MODEL), lambda i: (0, 0), pipeline_mode=pl.Buffered(1)),
        ],
        out_specs=[qkv_spec(r) for _, r in ATTN_GROUPS] + [
            pl.BlockSpec((tm, 2 * D_MODEL), lambda i: (i, 0)),
            pl.BlockSpec((tm // CHUNK, SSM_W, CHUNK), lambda i: (i, 0, 0)),
        ],
        out_shape=[jax.ShapeDtypeStruct((3, batch, r, seq // r, GROUP_W), BF16) for _, r in ATTN_GROUPS] + [
            jax.ShapeDtypeStruct((n, 2 * D_MODEL), BF16),
            jax.ShapeDtypeStruct((n // CHUNK, SSM_W, CHUNK), F32),
        ],
        scratch_shapes=[pltpu.VMEM((4, tm, LANES), F32)],
        compiler_params=_cparams(("parallel",)),
        name="in_projection",
    )(x2, g, w_bf, wut_bf)


def _t5_bucket(dist):
    max_exact = REL_BUCKETS // 2
    d = np.maximum(dist, 1).astype(np.float64)
    large = max_exact + (
        np.log(d / max_exact) / np.log(REL_MAX_DIST / max_exact) * (REL_BUCKETS - max_exact)
    ).astype(np.int32)
    large = np.minimum(large, REL_BUCKETS - 1)
    return np.where(dist < max_exact, dist, large).astype(np.int32)


def _band_bias(table, window, dilation):
    steps = window // dilation
    qi = np.arange(BLK)[:, None]
    kj = np.arange(2 * BLK)[None, :]
    delta = BLK + qi - kj
    band = (delta >= 0) & (delta <= steps)
    bucket = _t5_bucket(np.clip(delta, 0, steps) * dilation)
    onehot = np.eye(REL_BUCKETS, dtype=np.float32)[bucket]
    bias = jnp.einsum("qkb,bh->hqk", onehot, table.astype(F32), precision=lax.Precision.HIGHEST)
    return jnp.where(band[None], bias, NEG_INF)


def _attn_body(q_ref, kp_ref, kc_ref, vp_ref, vc_ref, bias_ref, o_ref, l_ref, *, nsub):
    lane = lax.broadcasted_iota(jnp.int32, (BLK, LANES), 1)
    lo = lane < HEAD_DIM
    keep_lo = jnp.where(lo, 1.0, 0.0).astype(BF16)
    keep_hi = jnp.where(lo, 0.0, 1.0).astype(BF16)
    col = lax.broadcasted_iota(jnp.int32, (BLK, 2 * BLK), 1)
    first_pen = jnp.where(col < BLK, jnp.where(pl.program_id(2) == 0, NEG_INF, 0.0), 0.0)
    for i in range(nsub):
        rows = slice(i * BLK, (i + 1) * BLK)
        q = q_ref[rows, :] * jnp.asarray(HEAD_DIM ** -0.5, BF16)
        if i == 0:
            kw = jnp.concatenate([kp_ref[...], kc_ref[0:BLK, :]], axis=0)
            vw = jnp.concatenate([vp_ref[...], vc_ref[0:BLK, :]], axis=0)
        else:
            kw = kc_ref[(i - 1) * BLK:(i + 1) * BLK, :]
            vw = vc_ref[(i - 1) * BLK:(i + 1) * BLK, :]
        for hp in range(HEADS // 2):
            cols = slice(hp * LANES, (hp + 1) * LANES)
            q2, k2, v2 = q[:, cols], kw[:, cols], vw[:, cols]
            outs, lses = [], []
            for half in range(2):
                qm = q2 * (keep_lo, keep_hi)[half]
                s = lax.dot_general(qm, k2, (((1,), (1,)), ((), ())), preferred_element_type=F32)
                s = s + bias_ref[2 * hp + half]
                if i == 0:
                    s = s + first_pen
                m = jnp.max(s, axis=-1, keepdims=True)
                p = jnp.exp(s - m)
                den = jnp.sum(p, axis=-1, keepdims=True)
                pv = jnp.dot(p.astype(BF16), v2, preferred_element_type=F32)
                outs.append(pv / den)
                lses.append(jnp.broadcast_to(m + jnp.log(den), (BLK, LANES)))
            o_ref[rows, cols] = jnp.where(lo, outs[0], outs[1]).astype(BF16)
            l_ref[rows, cols] = jnp.where(lo, lses[0], lses[1])


def _attention_group(qkv, bias, g):
    _, batch, r, length, _ = qkv.shape
    qb = min(512, length)
    nsub = qb // BLK

    def cur(which):
        return pl.BlockSpec((None, None, None, qb, GROUP_W), lambda b, c, n: (which, b, c, n, 0))

    def prev(which):
        return pl.BlockSpec((None, None, None, BLK, GROUP_W),
                            lambda b, c, n: (which, b, c, jnp.maximum(n * nsub - 1, 0), 0))

    out_spec = pl.BlockSpec((None, None, qb, GROUP_W), lambda b, c, n: (b, c, n, 0))
    return pl.pallas_call(
        functools.partial(_attn_body, nsub=nsub),
        grid=(batch, r, length // qb),
        in_specs=[cur(0), prev(1), cur(1), prev(2), cur(2),
                  pl.BlockSpec((HEADS, BLK, 2 * BLK), lambda b, c, n: (0, 0, 0))],
        out_specs=[out_spec, out_spec],
        out_shape=[
            jax.ShapeDtypeStruct((batch, r, length, GROUP_W), BF16),
            jax.ShapeDtypeStruct((batch, r, length, GROUP_W), F32),
        ],
        compiler_params=_cparams(("parallel", "parallel", "arbitrary")),
        name=f"attention_g{g}",
    )(qkv, qkv, qkv, qkv, qkv, bias)


def _ssm_tables(lam_re, lam_im, log_dt, b_re, b_im, c_re, c_im):
    lam = lax.complex(lam_re.astype(F32), lam_im.astype(F32))
    dt = jnp.exp(log_dt.astype(F32))[:, None]
    lam_dt = lam * dt
    lam_bar = jnp.exp(lam_dt)
    b = lax.complex(b_re.astype(F32), b_im.astype(F32))
    b_bar = ((lam_bar - 1.0) / lam)[..., None] * b
    half = CHUNK // 2
    t = jnp.arange(CHUNK, dtype=F32)

    def power(k):
        return jnp.exp(lam_dt[:, None, :] * jnp.reshape(jnp.asarray(k, F32), (1, -1, 1)))

    p_fwd = power(t - half)
    p_bwd = power(half - t)

    def in_pair(pw):
        return jnp.stack([jnp.concatenate([pw.real, pw.imag], axis=-1),
                          jnp.concatenate([-pw.imag, pw.real], axis=-1)], axis=1)

    def out_pair(pw):
        return jnp.stack([jnp.concatenate([pw.real, -pw.imag], axis=-1),
                          jnp.concatenate([-pw.imag, -pw.real], axis=-1)], axis=1)

    powers = jnp.stack([
        in_pair(p_bwd),
        in_pair(p_bwd * power(CHUNK - 1.0 - half)),
        out_pair(p_fwd),
        out_pair(p_fwd * power(half + 1.0)),
    ], axis=1)
    dup = lambda a: jnp.concatenate([a, a], axis=-1)
    b_cp = jnp.transpose(b_bar, (0, 2, 1))
    coefs = jnp.stack([dup(b_cp.real), dup(b_cp.imag), dup(c_re.astype(F32)), dup(c_im.astype(F32))],
                      axis=1)
    lc = power([float(CHUNK * 2 ** k) for k in range(SCAN_LEVELS)])
    l1 = jnp.concatenate([lc.real, lc.real], axis=-1)
    l2 = jnp.concatenate([-lc.imag, lc.imag], axis=-1)
    return powers, coefs, l1, l2


def _gelu_tanh(y):
    return y * jax.nn.sigmoid(1.5957691216057308 * (y + 0.044715 * (y * y * y)))


def _ssm_body(d_ref, u_ref, pw_ref, cf_ref, l1_ref, l2_ref, z_ref, m_scr, mask_scr, z_scr, a_scr, w_scr, *,
              ncb):
    g = pl.program_id(0)
    nc = u_ref.shape[0]
    width = SSM_CH * CHUNK
    cb = mask_scr.shape[1]

    @pl.when(g == 0)
    def _():
        s_idx = lax.broadcasted_iota(jnp.int32, (width, cb), 0) & (CHUNK - 1)
        t_idx = lax.broadcasted_iota(jnp.int32, (width, cb), 1) & (CHUNK - 1)
        mask_scr[...] = jnp.where(t_idx >= s_idx, -1, 0).astype(jnp.int32)

    def expand(kind, c1, c2, ch):
        return (cf_ref[c1, ch:ch + 1, :] * pw_ref[kind, 0] + cf_ref[c2, ch:ch + 1, :] * pw_ref[kind, 1])

    for ch in range(SSM_CH):
        rows = slice(ch * CHUNK, (ch + 1) * CHUNK)
        a_scr[rows, :] = expand(0, 0, 1, ch).astype(BF16)
        w_scr[rows, :] = expand(1, 0, 1, ch).astype(BF16)
    d_tiles = [expand(2, 2, 3, ch).T.astype(BF16) for ch in range(SSM_CH)]
    v_tiles = [expand(3, 2, 3, ch).T.astype(BF16) for ch in range(SSM_CH)]

    for k in range(width // cb):
        per = cb // CHUNK
        rhs = jnp.concatenate(d_tiles[k * per:(k + 1) * per], axis=1)
        mk = jnp.dot(a_scr[...], rhs, preferred_element_type=F32)
        kept = lax.bitcast_convert_type(mk, jnp.int32) & mask_scr[...]
        m_scr[:, k * cb:(k + 1) * cb] = lax.bitcast_convert_type(kept, F32).astype(BF16)

    u2 = u_ref.reshape(nc * SSM_CH, CHUNK)
    us =[u2[pl.ds(c, nc, stride=SSM_CH), :] for c in range(SSM_CH)]
    x = jnp.concatenate(us, axis=1).astype(BF16)

    acc = jnp.dot(x, w_scr[...], preferred_element_type=F32)
    rmod = lax.broadcasted_iota(jnp.int32, (nc, 2 * SSM_P), 0) & (ncb - 1)
    for k in range(ncb.bit_length() - 1):
        d = 1 << k
        sh = jnp.where(rmod >= d, pltpu.roll(acc, d, 0), 0.0)
        acc = acc + sh * l1_ref[k:k + 1, :] + pltpu.roll(sh, SSM_P, 1) * l2_ref[k:k + 1, :]
    x_in = jnp.where(rmod >= 1, pltpu.roll(acc, 1, 0), 0.0)

    y = jnp.dot(x, m_scr[...], preferred_element_type=F32)
    v_all = jnp.concatenate(v_tiles, axis=1)
    y = y + jnp.dot(x_in.astype(BF16), v_all, preferred_element_type=F32)
    for c in range(SSM_CH):
        yc = y[:, c * CHUNK:(c + 1) * CHUNK] + d_ref[g * SSM_CH + c] * us[c]
        z_scr[pl.ds(c, nc, stride=SSM_CH), :] = _gelu_tanh(yc)
    z_ref[...] = z_scr[...].reshape(nc, SSM_CH, CHUNK)


def _ssm_scan(u3, d_skip, tables, nbatch):
    powers, coefs, l1, l2 = tables
    nc = u3.shape[0]
    ncb = nc // nbatch
    assert ncb & (ncb - 1) == 0 and ncb <= 2 ** SCAN_LEVELS
    width = SSM_CH * CHUNK
    grid_spec = pltpu.PrefetchScalarGridSpec(
        num_scalar_prefetch=1,
        grid=(SSM_G,),
        in_specs=[
            pl.BlockSpec((nc, SSM_CH, CHUNK), lambda g, d: (0, g, 0)),
            pl.BlockSpec((None,) + powers.shape[1:], lambda g, d: (g, 0, 0, 0, 0)),
            pl.BlockSpec((None,) + coefs.shape[1:], lambda g, d: (g, 0, 0, 0)),
            pl.BlockSpec((None, SCAN_LEVELS, 2 * SSM_P), lambda g, d: (g, 0, 0)),
            pl.BlockSpec((None, SCAN_LEVELS, 2 * SSM_P), lambda g, d: (g, 0, 0)),
        ],
        out_specs=pl.BlockSpec((nc, SSM_CH, CHUNK), lambda g, d: (0, g, 0)),
        scratch_shapes=[pltpu.VMEM((width, width), BF16), pltpu.VMEM((width, 512), jnp.int32),
                        pltpu.VMEM((nc * SSM_CH, CHUNK), F32),
                        pltpu.VMEM((width, 2 * SSM_P), BF16), pltpu.VMEM((width, 2 * SSM_P), BF16)],
    )
    return pl.pallas_call(
        functools.partial(_ssm_body, ncb=ncb),
        grid_spec=grid_spec,
        out_shape=jax.ShapeDtypeStruct((nc, SSM_W, CHUNK), F32),
        compiler_params=_cparams(("arbitrary",)),
        name="ssm_scan",
    )(d_skip.astype(F32), u3, powers, coefs, l1, l2)


def _merge_body(o0, o1, o2, l0, l1, l2, zt_ref, ga_ref, gs_ref, x_ref,
                wglu_ref, bglu_ref, wab_ref, wsb_ref, wout_ref, out_ref, tok_scr):
    def token_major(ref, slot):
        r, rows, _ = ref.shape
        if r == 1:
            return lambda rs: ref[0, rs, :].astype(F32)
        nl = GROUP_W // LANES
        scrs = [tok_scr.at[slot * nl + k] for k in range(nl)]
        for c in range(r):
            sub = ref[c].astype(F32)
            for k in range(nl):
                scrs[k][pl.ds(c, rows, stride=r), :] = sub[:, k * LANES:(k + 1) * LANES]
        return lambda rs: jnp.concatenate([s[rs, :] for s in scrs], axis=1)

    lse = [token_major(l0, 0), token_major(l1, 0), token_major(l2, 1)]
    val = [token_major(o0, 0), token_major(o1, 2), token_major(o2, 3)]

    tm = x_ref.shape[0]
    rows = tm // MERGE_SPLIT
    for h in range(MERGE_SPLIT):
        rs = slice(h * rows, (h + 1) * rows)
        a0, a1, a2 = (f(rs) for f in lse)
        v0, v1, v2 = (f(rs) for f in val)
        mx = jnp.maximum(jnp.maximum(a0, a1), a2)
        e0, e1, e2 = jnp.exp(a0 - mx), jnp.exp(a1 - mx), jnp.exp(a2 - mx)
        mix = (e0 * v0 + e1 * v1 + e2 * v2) / (e0 + e1 + e2)
        y_attn = jnp.dot(mix.astype(BF16), wab_ref[...], preferred_element_type=F32)

        chunks = range(h * rows // CHUNK, (h + 1) * rows // CHUNK)
        z = jnp.concatenate([zt_ref[k].T for k in chunks], axis=0).astype(BF16)
        gl = jnp.dot(z, wglu_ref[...], preferred_element_type=F32) + bglu_ref[...]
        sg = z.astype(F32) * jax.nn.sigmoid(gl)
        y_ssm = jnp.dot(sg.astype(BF16), wsb_ref[...], preferred_element_type=F32)

        merged = (jax.nn.sigmoid(ga_ref[rs, :].astype(F32)) * y_attn
                  + jax.nn.sigmoid(gs_ref[rs, :].astype(F32)) * y_ssm)
        out_ref[rs, :] = x_ref[rs, :] + jnp.dot(merged.astype(BF16), wout_ref[...],
                                                 preferred_element_type=F32)


def _merge(os_, ls_, zt, gates, x2, wglu, bglu, wab, wsb, wout, tm=512):
    n = x2.shape[0]
    tiles_per_seq = os_[0].shape[2] // tm
    row = lambda i: (i, 0)
    const = lambda i: (0, 0)

    def group_spec(a):
        r = a.shape[1]
        return pl.BlockSpec((None, r, tm // r, GROUP_W),
                            lambda i: (i // tiles_per_seq, 0, i % tiles_per_seq, 0))

    in_specs = (
        [group_spec(a) for a in os_] + [group_spec(a) for a in ls_]
        + [
            pl.BlockSpec((tm // CHUNK, SSM_W, CHUNK), lambda i: (i, 0, 0)),
            pl.BlockSpec((tm, D_MODEL), lambda i: (i, 0)),
            pl.BlockSpec((tm, D_MODEL), lambda i: (i, 1)),
            pl.BlockSpec((tm, D_MODEL), row),
            pl.BlockSpec((SSM_W, SSM_W), const),
            pl.BlockSpec((1, SSM_W), const),
            pl.BlockSpec((GROUP_W, D_MODEL), const),
            pl.BlockSpec((SSM_W, D_MODEL), const),
            pl.BlockSpec((D_MODEL, D_MODEL), const),
        ]
    )
    return pl.pallas_call(
        _merge_body,
        grid=(n // tm,),
        in_specs=in_specs,
        out_specs=pl.BlockSpec((tm, D_MODEL), row),
        out_shape=jax.ShapeDtypeStruct((n, D_MODEL), F32),
        scratch_shapes=[pltpu.VMEM((4 * GROUP_W // LANES, tm, LANES), F32)],
        compiler_params=_cparams(("parallel",)),
        name="merge",
    )(*os_, *ls_, zt, gates, gates, x2, wglu, bglu, wab, wsb, wout)


def _ffn_body(x_ref, g_ref, wg_ref, wu_ref, wd_ref, *rest, tf):
    ncast = (len(rest) - 1) // 2
    o_ref = rest[ncast]
    for src, dst in zip(rest[:ncast], rest[ncast + 1:]):
        dst[...] = src[...].astype(BF16)
    h = _rms(x_ref[...], g_ref[...]).astype(BF16)
    for f in range(wg_ref.shape[1] // tf):
        cols = slice(f * tf, (f + 1) * tf)
        a = jnp.dot(h, wg_ref[:, cols], preferred_element_type=F32)
        b = jnp.dot(h, wu_ref[:, cols], preferred_element_type=F32)
        act = (a * jax.nn.sigmoid(a) * b).astype(BF16)
        part = jnp.dot(act, wd_ref[cols, :], preferred_element_type=F32)
        if f == 0:
            o_ref[...] = x_ref[...] + part
        else:
            o_ref[...] += part


def _dense_ffn(x2, g, wg, wu, wd, to_cast=(), tm=512, tf=256):
    n = x2.shape[0]
    dff = wg.shape[1]
    steps = n // tm
    assert dff % tf == 0 and all(a.shape[0] % (16 * steps) == 0 for a in to_cast)
    resident = lambda shape: pl.BlockSpec(shape, lambda i: (0, 0), pipeline_mode=pl.Buffered(1))
    slabs = [pl.BlockSpec((a.shape[0] // steps, a.shape[1]), lambda i: (i, 0)) for a in to_cast]
    out, *casts = pl.pallas_call(
        functools.partial(_ffn_body, tf=tf),
        grid=(steps,),
        in_specs=[
            pl.BlockSpec((tm, D_MODEL), lambda i: (i, 0)),
            pl.BlockSpec((1, D_MODEL), lambda i: (0, 0)),
            resident((D_MODEL, dff)),
            resident((D_MODEL, dff)),
            resident((dff, D_MODEL)),
        ] + slabs,
        out_specs=[pl.BlockSpec((tm, D_MODEL), lambda i: (i, 0))] + slabs,
        out_shape=[jax.ShapeDtypeStruct((n, D_MODEL), F32)]
        + [jax.ShapeDtypeStruct(a.shape, BF16) for a in to_cast],
        compiler_params=_cparams(("parallel",)),
        name="dense_ffn",
    )(x2, g, wg, wu, wd, *to_cast)
    return out, casts


def _router_body(x_ref, g_ref, wr_ref, idx_ref, gate_ref, idxt_ref):
    h = _rms(x_ref[...], g_ref[...])
    w = wr_ref[...]
    h_hi, w_hi = h.astype(BF16), w.astype(BF16)
    h_lo = (h - h_hi.astype(F32)).astype(BF16)
    w_lo = (w - w_hi.astype(F32)).astype(BF16)
    logits = (jnp.dot(h_hi, w_hi, preferred_element_type=F32)
              + (jnp.dot(h_hi, w_lo, preferred_element_type=F32)
                 + jnp.dot(h_lo, w_hi, preferred_element_type=F32)))
    lane = lax.broadcasted_iota(jnp.int32, logits.shape, 1)
    lane_f = lane.astype(F32)
    logits = jnp.where(lane < N_EXPERTS, logits, -jnp.inf)
    v1 = jnp.max(logits, axis=-1, keepdims=True)
    i1 = jnp.min(jnp.where(logits == v1, lane_f, float(LANES)), axis=-1, keepdims=True)
    rest = jnp.where(lane_f == i1, -jnp.inf, logits)
    v2 = jnp.max(rest, axis=-1, keepdims=True)
    i2 = jnp.min(jnp.where(rest == v2, lane_f, float(LANES)), axis=-1, keepdims=True)
    e = jnp.exp(v2 - v1)
    g1 = 1.0 / (1.0 + e)
    g2 = e / (1.0 + e)
    idx_f = jnp.where(lane == 0, i1, jnp.where(lane == 1, i2, 0.0))
    idx_ref[...] = idx_f.astype(jnp.int32)
    gate_ref[...] = jnp.where(lane == 0, g1, jnp.where(lane == 1, g2, 0.0))
    idxt_ref[...] = idx_f.T[:idxt_ref.shape[0], :].astype(jnp.int32)


def _router(x2, g, wr_pad, tm=1024):
    n = x2.shape[0]
    return pl.pallas_call(
        _router_body,
        grid=(n // tm,),
        in_specs=[
            pl.BlockSpec((tm, D_MODEL), lambda i: (i, 0)),
            pl.BlockSpec((1, D_MODEL), lambda i: (0, 0)),
            pl.BlockSpec((D_MODEL, LANES), lambda i: (0, 0)),
        ],
        out_specs=[
            pl.BlockSpec((tm, LANES), lambda i: (i, 0)),
            pl.BlockSpec((tm, LANES), lambda i: (i, 0)),
            pl.BlockSpec((8, tm), lambda i: (0, i)),
        ],
        out_shape=[
            jax.ShapeDtypeStruct((n, LANES), jnp.int32),
            jax.ShapeDtypeStruct((n, LANES), F32),
            jax.ShapeDtypeStruct((8, n), jnp.int32),
        ],
        compiler_params=_cparams(("parallel",)),
        name="router",
    )(x2, g, wr_pad)


def _pack_bf16_pairs(hb):
    half = hb.shape[1] // 2
    lo = lax.bitcast_convert_type(hb[:, :half].astype(F32), jnp.uint32)
    hi = lax.bitcast_convert_type(hb[:, half:].astype(F32), jnp.uint32)
    return (hi & jnp.uint32(0xFFFF0000)) | (lo >> 16)


def _unpack_bf16_pairs(xu):
    lo = lax.bitcast_convert_type(xu << 16, F32).astype(BF16)
    hi = lax.bitcast_convert_type(xu & jnp.uint32(0xFFFF0000), F32).astype(BF16)
    return lo, hi


def _for_each_run_piece(i, start_ref, loff_ref, len_ref, fn):
    for e in range(N_EXPERTS):
        j = i * N_EXPERTS + e
        length, boff, soff = len_ref[j], loff_ref[j], start_ref[j]
        done = 0
        for p in RUN_PIECES:
            cond = (length & p) != 0
            fn(cond, pl.multiple_of(boff + done, RUN_ALIGN), pl.multiple_of(soff + done, RUN_ALIGN), p)
            done = done + jnp.where(cond, p, 0)


def _dispatch_body(start_ref, loff_ref, len_ref, tail_ref, x_ref, g_ref, idxt_ref, off_ref, cnt0_ref,
                   xs_ref, cbuf, tri_scr, zero_scr, sem, zsem):
    i = pl.program_id(0)
    tt = x_ref.shape[0]

    @pl.when(i == 0)
    def _():
        r = lax.broadcasted_iota(jnp.int32, (tt, tt), 0)
        c = lax.broadcasted_iota(jnp.int32, (tt, tt), 1)
        tri_scr[...] = jnp.where(r < c, 1.0, 0.0).astype(BF16)
        zero_scr[...] = jnp.zeros_like(zero_scr)

        def fill(e):
            row = pl.multiple_of(jnp.maximum(tail_ref[e], 0), MOE_BM)
            return pltpu.make_async_copy(zero_scr, xs_ref.at[pl.ds(row, MOE_BM)], zsem)

        for e in range(tail_ref.shape[0]):
            pl.when(tail_ref[e] >= 0)(lambda e=e: fill(e).start())
        for e in range(tail_ref.shape[0]):
            pl.when(tail_ref[e] >= 0)(lambda e=e: fill(e).wait())

    hb = _rms(x_ref[...], g_ref[...]).astype(BF16)

    sub = lax.broadcasted_iota(jnp.int32, (N_EXPERTS, tt), 0)
    pos = []
    for k in range(2):
        oh = jnp.where(sub == idxt_ref[k:k + 1, :], 1.0, 0.0)
        rank = jnp.dot(oh.astype(BF16), tri_scr[...], preferred_element_type=F32)
        base = off_ref[:, :1] if k == 0 else off_ref[:, :1] + cnt0_ref[:, :1]
        pos.append(jnp.sum(oh * (base + rank), axis=0, keepdims=True))

    rows = lax.broadcasted_iota(jnp.int32, (cbuf.shape[0], tt), 0).astype(F32)
    perm = (jnp.where(rows == pos[0], 1.0, 0.0) + jnp.where(rows == pos[1], 1.0, 0.0)).astype(BF16)
    cbuf[...] = _pack_bf16_pairs(jnp.dot(perm, hb, preferred_element_type=F32).astype(BF16))

    def piece(op):
        def fn(cond, brow, srow, p):
            cp = pltpu.make_async_copy(cbuf.at[pl.ds(brow, p)], xs_ref.at[pl.ds(srow, p)], sem)
            pl.when(cond)(getattr(cp, op))
        return fn

    _for_each_run_piece(i, start_ref, loff_ref, len_ref, piece("start"))
    _for_each_run_piece(i, start_ref, loff_ref, len_ref, piece("wait"))


def _dispatch(plan, x2, g, idxt):
    n = x2.shape[0]
    tt = MOE_TT
    smem = lambda i, *_: (i, 0, 0)
    grid_spec = pltpu.PrefetchScalarGridSpec(
        num_scalar_prefetch=4,
        grid=(n // tt,),
        in_specs=[
            pl.BlockSpec((tt, D_MODEL), lambda i, *_: (i, 0)),
            pl.BlockSpec((1, D_MODEL), lambda i, *_: (0, 0)),
            pl.BlockSpec((8, tt), lambda i, *_: (0, i)),
            pl.BlockSpec((None, N_EXPERTS, LANES), smem),
            pl.BlockSpec((None, N_EXPERTS, LANES), smem),
        ],
        out_specs=pl.BlockSpec(memory_space=pl.ANY),
        scratch_shapes=[
            pltpu.VMEM((CBUF_ROWS, D_MODEL // 2), jnp.uint32),
            pltpu.VMEM((tt, tt), BF16),
            pltpu.VMEM((MOE_BM, D_MODEL // 2), jnp.uint32),
            pltpu.SemaphoreType.DMA(()),
            pltpu.SemaphoreType.DMA(()),
        ],
    )
    return pl.pallas_call(
        _dispatch_body,
        grid_spec=grid_spec,
        out_shape=jax.ShapeDtypeStruct((plan["n_slots"], D_MODEL // 2), jnp.uint32),
        compiler_params=_cparams(("arbitrary",)),
        name="moe_dispatch",
    )(plan["start"], plan["loff"], plan["len8"], plan["tail_rows"], x2, g, idxt,
      plan["off_lanes"], plan["cnt0_lanes"])


def _experts_body(be_ref, nu_ref, xs_ref, wg_ref, wu_ref, wd_ref, ys_ref, acc_scr, *, tf):
    del be_ref
    i = pl.program_id(0)
    half = D_MODEL // 2

    @pl.when(i < nu_ref[0])
    def _():
        lo, hi = _unpack_bf16_pairs(xs_ref[...])
        for f in range(wg_ref.shape[1] // tf):
            cols = slice(f * tf, (f + 1) * tf)
            a = (jnp.dot(lo, wg_ref[:half, cols], preferred_element_type=F32)
                 + jnp.dot(hi, wg_ref[half:, cols], preferred_element_type=F32))
            b = (jnp.dot(lo, wu_ref[:half, cols], preferred_element_type=F32)
                 + jnp.dot(hi, wu_ref[half:, cols], preferred_element_type=F32))
            act = (a * jax.nn.sigmoid(a) * b).astype(BF16)
            part = jnp.dot(act, wd_ref[cols, :], preferred_element_type=F32)
            if f == 0:
                acc_scr[...] = part
            else:
                acc_scr[...] += part
        ys_ref[...] = _pack_bf16_pairs(acc_scr[...].astype(BF16))

    @pl.when(i >= nu_ref[0])
    def _():
        ys_ref[...] = jnp.zeros_like(ys_ref)


def _experts(block_e, n_used, xs, wg, wu, wd, tf=512):
    n_slots = xs.shape[0]
    n_blocks = n_slots // MOE_BM
    dff = wg.shape[2]

    def blk(i, nu):
        return jnp.minimum(i, nu[0] - 1)

    def wspec(rows, cols):
        return pl.BlockSpec((None, rows, cols), lambda i, be, nu: (be[blk(i, nu)], 0, 0),
                            pipeline_mode=pl.Buffered(1))

    grid_spec = pltpu.PrefetchScalarGridSpec(
        num_scalar_prefetch=2,
        grid=(n_blocks,),
        in_specs=[
            pl.BlockSpec((MOE_BM, D_MODEL // 2), lambda i, be, nu: (blk(i, nu), 0)),
            wspec(D_MODEL, dff),
            wspec(D_MODEL, dff),
            wspec(dff, D_MODEL),
        ],
        out_specs=pl.BlockSpec((MOE_BM, D_MODEL // 2), lambda i, be, nu: (i, 0)),
        scratch_shapes=[pltpu.VMEM((MOE_BM, D_MODEL), F32)],
    )
    return pl.pallas_call(
        functools.partial(_experts_body, tf=tf),
        grid_spec=grid_spec,
        out_shape=jax.ShapeDtypeStruct((n_slots, D_MODEL // 2), jnp.uint32),
        compiler_params=_cparams(("arbitrary",)),
        name="moe_experts",
    )(block_e, n_used, xs, wg, wu, wd)


def _combine_body(start_ref, loff_ref, len_ref, x_ref, idx_ref, gate_ref, meta_ref, g_ref, ys_ref, o_ref,
                  ybuf, tri_scr, sem):
    i = pl.program_id(0)
    tt = x_ref.shape[0]

    @pl.when(i == 0)
    def _():
        r = lax.broadcasted_iota(jnp.int32, (tt, tt), 0)
        c = lax.broadcasted_iota(jnp.int32, (tt, tt), 1)
        tri_scr[...] = jnp.where(c < r, 1.0, 0.0).astype(BF16)
        ybuf[...] = jnp.zeros_like(ybuf)

    def piece(op):
        def fn(cond, brow, srow, p):
            cp = pltpu.make_async_copy(ys_ref.at[pl.ds(srow, p)], ybuf.at[pl.ds(brow, p)], sem)
            pl.when(cond)(getattr(cp, op))
        return fn

    _for_each_run_piece(i, start_ref, loff_ref, len_ref, piece("start"))

    lane = lax.broadcasted_iota(jnp.int32, (tt, LANES), 1)
    idx = idx_ref[...]
    gt = gate_ref[...]
    cols = lax.broadcasted_iota(jnp.int32, (tt, ybuf.shape[0]), 1).astype(F32)
    sel = None
    for k in range(2):
        oh = jnp.where(lane == idx[:, k:k + 1], 1.0, 0.0)
        rank = jnp.dot(tri_scr[...], oh.astype(BF16), preferred_element_type=F32)
        base = meta_ref[0:1, :] if k == 0 else meta_ref[0:1, :] + meta_ref[1:2, :]
        pos = jnp.sum(oh * (base + rank), axis=1, keepdims=True)
        term = jnp.where(cols == pos, gt[:, k:k + 1], 0.0)
        sel = term if sel is None else sel + term
    sel = sel.astype(BF16)

    _for_each_run_piece(i, start_ref, loff_ref, len_ref, piece("wait"))

    lo, hi = _unpack_bf16_pairs(ybuf[...])
    y = jnp.concatenate([jnp.dot(sel, lo, preferred_element_type=F32),
                         jnp.dot(sel, hi, preferred_element_type=F32)], axis=1)
    o_ref[...] = _rms(x_ref[...] + y, g_ref[...])


def _combine(plan, x2, idx, gates, g_final, ys):
    n = x2.shape[0]
    tt = MOE_TT
    grid_spec = pltpu.PrefetchScalarGridSpec(
        num_scalar_prefetch=3,
        grid=(n // tt,),
        in_specs=[
            pl.BlockSpec((tt, D_MODEL), lambda i, *_: (i, 0)),
            pl.BlockSpec((tt, LANES), lambda i, *_: (i, 0)),
            pl.BlockSpec((tt, LANES), lambda i, *_: (i, 0)),
            pl.BlockSpec((None, 8, LANES), lambda i, *_: (i, 0, 0)),
            pl.BlockSpec((1, D_MODEL), lambda i, *_: (0, 0)),
            pl.BlockSpec(memory_space=pl.ANY),
        ],
        out_specs=pl.BlockSpec((tt, D_MODEL), lambda i, *_: (i, 0)),
        scratch_shapes=[
            pltpu.VMEM((CBUF_ROWS, D_MODEL // 2), jnp.uint32),
            pltpu.VMEM((tt, tt), BF16),
            pltpu.SemaphoreType.DMA(()),
        ],
    )
    return pl.pallas_call(
        _combine_body,
        grid_spec=grid_spec,
        out_shape=jax.ShapeDtypeStruct((n, D_MODEL), F32),
        compiler_params=_cparams(("arbitrary",)),
        name="moe_combine",
    )(plan["start"], plan["loff"], plan["len8"], x2, idx, gates, plan["meta_rows"], g_final, ys)


def _route_plan(idxt, n):
    nt = n // MOE_TT
    e2 = idxt[:2].reshape(2, nt, MOE_TT)
    oh = (e2[..., None] == jnp.arange(N_EXPERTS, dtype=jnp.int32)).astype(jnp.int32)
    cnt = jnp.sum(oh, axis=2)
    cnt0 = cnt[0]
    len8 = (cnt[0] + cnt[1] + RUN_ALIGN - 1) // RUN_ALIGN * RUN_ALIGN
    loff = jnp.cumsum(len8, axis=1) - len8
    region = jnp.sum(len8, axis=0)
    padded = (region + MOE_BM - 1) // MOE_BM * MOE_BM
    pad_end = jnp.cumsum(padded)
    start = (pad_end - padded)[None, :] + jnp.cumsum(len8, axis=0) - len8
    n_blocks = (2 * n + nt * N_EXPERTS * (RUN_ALIGN - 1) + MOE_BM - 1) // MOE_BM + N_EXPERTS
    starts = jnp.arange(n_blocks, dtype=jnp.int32) * MOE_BM
    block_e = jnp.sum((starts[:, None] >= pad_end[None, :]).astype(jnp.int32), axis=1)
    block_e = jnp.minimum(block_e, N_EXPERTS - 1).astype(jnp.int32)
    n_used = (pad_end[-1] // MOE_BM).astype(jnp.int32).reshape(1)
    tails = jnp.where(padded > 0, pad_end - MOE_BM, -1)
    spare = pad_end[-1] + jnp.arange(n_blocks - (2 * n) // MOE_BM, dtype=pad_end.dtype) * MOE_BM
    spare = jnp.where(spare < n_blocks * MOE_BM, spare, -1)
    lanes = lambda a: jnp.broadcast_to(a.astype(F32)[:, :, None], (nt, N_EXPERTS, LANES))
    meta_rows = jnp.zeros((nt, 8, LANES), F32)
    meta_rows = meta_rows.at[:, 0, :N_EXPERTS].set(loff.astype(F32)).at[:, 1, :N_EXPERTS].set(cnt0.astype(F32))
    flat = lambda a: a.reshape(-1).astype(jnp.int32)
    return dict(start=flat(start), loff=flat(loff), len8=flat(len8), block_e=block_e, n_used=n_used,
                tail_rows=jnp.concatenate([tails, spare]).astype(jnp.int32),
                off_lanes=lanes(loff), cnt0_lanes=lanes(cnt0), meta_rows=meta_rows,
                n_slots=n_blocks * MOE_BM)


def _mixer_layer(x2, batch, seq, rel_bias, norm_g, w_in, ssm, d_skip, w_glu, b_glu,
                 w_attn_br, w_ssm_br, w_out):
    u_lo = 3 * ATTN_W
    blocks = [w_in[:, which * ATTN_W + g * GROUP_W: which * ATTN_W + (g + 1) * GROUP_W]
              for g in range(N_GROUPS) for which in range(3)]
    w_bf = jnp.concatenate(blocks + [w_in[:, u_lo + SSM_W:]], axis=1).astype(BF16)
    wut_bf = jnp.transpose(w_in[:, u_lo:u_lo + SSM_W]).astype(BF16)
    *qkvs, gates, ut = _in_projection(x2, norm_g.reshape(1, D_MODEL), w_bf, wut_bf, batch, seq)

    os_, ls_ = [], []
    for g, (window, dilation) in enumerate(ATTN_GROUPS):
        bias = _band_bias(rel_bias[:, g * HEADS:(g + 1) * HEADS], window, dilation)
        o, l = _attention_group(qkvs[g], bias, g)
        os_.append(o)
        ls_.append(l)

    zt = _ssm_scan(ut, d_skip, _ssm_tables(*ssm), batch)

    return _merge(os_, ls_, zt, gates, x2, w_glu.astype(BF16), b_glu.reshape(1, SSM_W).astype(F32),
                  w_attn_br.astype(BF16), w_ssm_br.astype(BF16), w_out.astype(BF16))


def kernel(x, rel_bias, norm1_g, w_in, ssm_lam_re, ssm_lam_im, ssm_log_dt, ssm_b_re, ssm_b_im, ssm_c_re, ssm_c_im, ssm_d, w_glu, b_glu, w_attn_br, w_ssm_br, w_out, norm2_g, ffn_w_gate, ffn_w_up, ffn_w_down, moe_router, moe_w_gate, moe_w_up, moe_w_down, final_norm_g):
    batch, seq, d = x.shape
    assert d == D_MODEL and norm1_g.shape[0] == 2 and seq % (16 * BLK) == 0
    n = batch * seq
    x2 = x.reshape(n, d)

    def mixer(x2, l):
        ssm = (ssm_lam_re[l], ssm_lam_im[l], ssm_log_dt[l], ssm_b_re[l], ssm_b_im[l],
               ssm_c_re[l], ssm_c_im[l])
        return _mixer_layer(x2, batch, seq, rel_bias, norm1_g[l], w_in[l], ssm, ssm_d[l], w_glu[l],
                            b_glu[l], w_attn_br[l], w_ssm_br[l], w_out[l])

    x2 = mixer(x2, 0)
    moe_w = (moe_w_gate[0], moe_w_up[0], moe_w_down[0])
    x2, moe_bf = _dense_ffn(x2, norm2_g[0].reshape(1, d), ffn_w_gate[0].astype(BF16),
                            ffn_w_up[0].astype(BF16), ffn_w_down[0].astype(BF16),
                            to_cast=[w.reshape(-1, w.shape[2]) for w in moe_w])
    moe_bf = [b.reshape(w.shape) for b, w in zip(moe_bf, moe_w)]

    x2 = mixer(x2, 1)
    g2 = norm2_g[1].reshape(1, d)
    wr_pad = jnp.zeros((d, LANES), F32).at[:, :N_EXPERTS].set(moe_router[0].astype(F32))
    idx, gates, idxt = _router(x2, g2, wr_pad)
    plan = _route_plan(idxt, n)
    xs = _dispatch(plan, x2, g2, idxt)
    ys = _experts(plan["block_e"], plan["n_used"], xs, *moe_bf)
    out = _combine(plan, x2, idx, gates, final_norm_g.reshape(1, d), ys)
    return out.reshape(batch, seq, d)
```

```python
import functools

import numpy as np
import jax
import jax.numpy as jnp
from jax import lax
from jax.experimental import pallas as pl
from jax.experimental.pallas import tpu as pltpu

F32 = jnp.float32
BF16 = jnp.bfloat16

D_MODEL = 1024
HEAD_DIM = 64
ATTN_GROUPS = ((128, 1), (512, 4), (2048, 16))
N_GROUPS = 3
HEADS = 8
GROUP_W = HEADS * HEAD_DIM
ATTN_W = N_GROUPS * GROUP_W
BLK = 128
REL_BUCKETS = 32
REL_MAX_DIST = 2048
NEG_INF = -1e30
SSM_CH = 16
SSM_W = D_MODEL // 2
SSM_G = SSM_W // SSM_CH
SSM_P = 64
PROJ_W = 3 * ATTN_W + SSM_W + 2 * D_MODEL
N_EXPERTS = 8
MOE_BM = 512
MOE_TT = 512
RUN_ALIGN = 8
RUN_PIECES = tuple(1 << b for b in range(MOE_TT.bit_length() - 1, RUN_ALIGN.bit_length() - 2, -1))
CBUF_ROWS = -(-(2 * MOE_TT + N_EXPERTS * (RUN_ALIGN - 1)) // 16) * 16
EPS = 1e-6
CHUNK = 128
SCAN_LEVELS = 8

MERGE_SPLIT = 1
LANES = 128
VMEM_LIMIT = 56 * 1024 * 1024


def _cparams(sem):
    return pltpu.CompilerParams(dimension_semantics=sem, vmem_limit_bytes=VMEM_LIMIT)


def _rms(x, g):
    return x * lax.rsqrt(jnp.mean(x * x, axis=-1, keepdims=True) + EPS) * g


def _proj_body(x_ref, g_ref, w_ref, wu_ref, qkv0_ref, qkv1_ref, qkv2_ref, gate_ref, ut_ref, d_scr):
    tm = x_ref.shape[0]
    hb = _rms(x_ref[...], g_ref[...]).astype(BF16)
    u = jnp.dot(hb, wu_ref[...], preferred_element_type=F32)
    for k in range(ut_ref.shape[0]):
        ut_ref[k] = u[k * CHUNK:(k + 1) * CHUNK, :].T

    cw = 2 * LANES
    nl = cw // LANES

    def slab(col):
        return jnp.dot(hb, w_ref[:, col:col + cw], preferred_element_type=F32)

    uses = 0
    for g, (out_ref, (_, r)) in enumerate(zip((qkv0_ref, qkv1_ref, qkv2_ref), ATTN_GROUPS)):
        for which in range(3):
            for lo in range(0, GROUP_W, cw):
                res = slab((3 * g + which) * GROUP_W + lo)
                if r == 1:
                    out_ref[which, 0, :, lo:lo + cw] = res.astype(BF16)
                    continue
                scrs = [d_scr.at[(uses % 2) * nl + k] for k in range(nl)]
                uses += 1
                for k in range(nl):
                    scrs[k][...] = res[:, k * LANES:(k + 1) * LANES]
                for c in range(r):
                    sub = [s[pl.ds(c, tm // r, stride=r), :] for s in scrs]
                    out_ref[which, c, :, lo:lo + cw] = jnp.concatenate(sub, axis=1).astype(BF16)
    for lo in range(0, 2 * D_MODEL, cw):
        gate_ref[:, lo:lo + cw] = slab(3 * N_GROUPS * GROUP_W + lo).astype(BF16)


def _in_projection(x2, g, w_bf, wu_bf, batch, seq, tm=512):
    n = x2.shape[0]
    tiles_per_seq = seq // tm
    wcols = w_bf.shape[1]

    def qkv_spec(r):
        return pl.BlockSpec((3, None, r, tm // r, GROUP_W),
                            lambda i: (0, i // tiles_per_seq, 0, i % tiles_per_seq, 0))

    return pl.pallas_call(
        _proj_body,
        grid=(n // tm,),
        in_specs=[
            pl.BlockSpec((tm, D_MODEL), lambda i: (i, 0)),
            pl.BlockSpec((1, D_MODEL), lambda i: (0, 0)),
            pl.BlockSpec((D_MODEL, wcols), lambda i: (0, 0), pipeline_mode=pl.Buffered(1)),
            pl.BlockSpec((D_MODEL, SSM_W), lambda i: (0, 0), pipeline_mode=pl.Buffered(1)),
        ],
        out_specs=[qkv_spec(r) for _, r in ATTN_GROUPS] + [
            pl.BlockSpec((tm, 2 * D_MODEL), lambda i: (i, 0)),
            pl.BlockSpec((tm // CHUNK, SSM_W, CHUNK), lambda i: (i, 0, 0)),
        ],
        out_shape=[jax.ShapeDtypeStruct((3, batch, r, seq // r, GROUP_W), BF16) for _, r in ATTN_GROUPS] + [
            jax.ShapeDtypeStruct((n, 2 * D_MODEL), BF16),
            jax.ShapeDtypeStruct((n // CHUNK, SSM_W, CHUNK), F32),
        ],
        scratch_shapes=[pltpu.VMEM((4, tm, LANES), F32)],
        compiler_params=_cparams(("parallel",)),
        name="in_projection",
    )(x2, g, w_bf, wu_bf)


def _t5_bucket(dist):
    max_exact = REL_BUCKETS // 2
    d = np.maximum(dist, 1).astype(np.float64)
    large = max_exact + (
        np.log(d / max_exact) / np.log(REL_MAX_DIST / max_exact) * (REL_BUCKETS - max_exact)
    ).astype(np.int32)
    large = np.minimum(large, REL_BUCKETS - 1)
    return np.where(dist < max_exact, dist, large).astype(np.int32)


def _band_bias(table, window, dilation):
    steps = window // dilation
    qi = np.arange(BLK)[:, None]
    kj = np.arange(2 * BLK)[None, :]
    delta = BLK + qi - kj
    band = (delta >= 0) & (delta <= steps)
    bucket = _t5_bucket(np.clip(delta, 0, steps) * dilation)
    onehot = np.eye(REL_BUCKETS, dtype=np.float32)[bucket]
    bias = jnp.einsum("qkb,bh->hqk", onehot, table.astype(F32), precision=lax.Precision.HIGHEST)
    return jnp.where(band[None], bias, NEG_INF)


def _attn_body(q_ref, kp_ref, kc_ref, vp_ref, vc_ref, bias_ref, o_ref, l_ref, *, nsub):
    lane = lax.broadcasted_iota(jnp.int32, (BLK, LANES), 1)
    lo = lane < HEAD_DIM
    keep_lo = jnp.where(lo, 1.0, 0.0).astype(BF16)
    keep_hi = jnp.where(lo, 0.0, 1.0).astype(BF16)
    col = lax.broadcasted_iota(jnp.int32, (BLK, 2 * BLK), 1)
    first_pen = jnp.where(col < BLK, jnp.where(pl.program_id(2) == 0, NEG_INF, 0.0), 0.0)
    for i in range(nsub):
        rows = slice(i * BLK, (i + 1) * BLK)
        q = q_ref[rows, :] * jnp.asarray(HEAD_DIM ** -0.5, BF16)
        if i == 0:
            kw = jnp.concatenate([kp_ref[...], kc_ref[0:BLK, :]], axis=0)
            vw = jnp.concatenate([vp_ref[...], vc_ref[0:BLK, :]], axis=0)
        else:
            kw = kc_ref[(i - 1) * BLK:(i + 1) * BLK, :]
            vw = vc_ref[(i - 1) * BLK:(i + 1) * BLK, :]
        for hp in range(HEADS // 2):
            cols = slice(hp * LANES, (hp + 1) * LANES)
            q2, k2, v2 = q[:, cols], kw[:, cols], vw[:, cols]
            outs, lses = [], []
            for half in range(2):
                qm = q2 * (keep_lo, keep_hi)[half]
                s = lax.dot_general(qm, k2, (((1,), (1,)), ((), ())), preferred_element_type=F32)
                s = s + bias_ref[2 * hp + half]
                if i == 0:
                    s = s + first_pen
                m = jnp.max(s, axis=-1, keepdims=True)
                p = jnp.exp(s - m)
                den = jnp.sum(p, axis=-1, keepdims=True)
                pv = jnp.dot(p.astype(BF16), v2, preferred_element_type=F32)
                outs.append(pv / den)
                lses.append(jnp.broadcast_to(m + jnp.log(den), (BLK, LANES)))
            o_ref[rows, cols] = jnp.where(lo, outs[0], outs[1]).astype(BF16)
            l_ref[rows, cols] = jnp.where(lo, lses[0], lses[1])


def _attention_group(qkv, bias, g):
    _, batch, r, length, _ = qkv.shape
    qb = min(512, length)
    nsub = qb // BLK

    def cur(which):
        return pl.BlockSpec((None, None, None, qb, GROUP_W), lambda b, c, n: (which, b, c, n, 0))

    def prev(which):
        return pl.BlockSpec((None, None, None, BLK, GROUP_W),
                            lambda b, c, n: (which, b, c, jnp.maximum(n * nsub - 1, 0), 0))

    out_spec = pl.BlockSpec((None, None, qb, GROUP_W), lambda b, c, n: (b, c, n, 0))
    return pl.pallas_call(
        functools.partial(_attn_body, nsub=nsub),
        grid=(batch, r, length // qb),
        in_specs=[cur(0), prev(1), cur(1), prev(2), cur(2),
                  pl.BlockSpec((HEADS, BLK, 2 * BLK), lambda b, c, n: (0, 0, 0))],
        out_specs=[out_spec, out_spec],
        out_shape=[
            jax.ShapeDtypeStruct((batch, r, length, GROUP_W), BF16),
            jax.ShapeDtypeStruct((batch, r, length, GROUP_W), F32),
        ],
        compiler_params=_cparams(("parallel", "parallel", "arbitrary")),
        name=f"attention_g{g}",
    )(qkv, qkv, qkv, qkv, qkv, bias)


def _ssm_tables(lam_re, lam_im, log_dt, b_re, b_im, c_re, c_im):
    lam = lax.complex(lam_re.astype(F32), lam_im.astype(F32))
    dt = jnp.exp(log_dt.astype(F32))[:, None]
    lam_dt = lam * dt
    lam_bar = jnp.exp(lam_dt)
    b = lax.complex(b_re.astype(F32), b_im.astype(F32))
    b_bar = ((lam_bar - 1.0) / lam)[..., None] * b
    half = CHUNK // 2
    t = jnp.arange(CHUNK, dtype=F32)

    def power(k):
        return jnp.exp(lam_dt[:, None, :] * jnp.reshape(jnp.asarray(k, F32), (1, -1, 1)))

    p_fwd = power(t - half)
    p_bwd = power(half - t)

    def in_pair(pw):
        return jnp.stack([jnp.concatenate([pw.real, pw.imag], axis=-1),
                          jnp.concatenate([-pw.imag, pw.real], axis=-1)], axis=1)

    def out_pair(pw):
        return jnp.stack([jnp.concatenate([pw.real, -pw.imag], axis=-1),
                          jnp.concatenate([-pw.imag, -pw.real], axis=-1)], axis=1)

    powers = jnp.stack([
        in_pair(p_bwd),
        in_pair(p_bwd * power(CHUNK - 1.0 - half)),
        out_pair(p_fwd),
        out_pair(p_fwd * power(half + 1.0)),
    ], axis=1)
    dup = lambda a: jnp.concatenate([a, a], axis=-1)
    b_cp = jnp.transpose(b_bar, (0, 2, 1))
    coefs = jnp.stack([dup(b_cp.real), dup(b_cp.imag), dup(c_re.astype(F32)), dup(c_im.astype(F32))],
                      axis=1)
    lc = power([float(CHUNK * 2 ** k) for k in range(SCAN_LEVELS)])
    l1 = jnp.concatenate([lc.real, lc.real], axis=-1)
    l2 = jnp.concatenate([-lc.imag, lc.imag], axis=-1)
    return powers, coefs, l1, l2


def _gelu_tanh(y):
    return y * jax.nn.sigmoid(1.5957691216057308 * (y + 0.044715 * (y * y * y)))


def _ssm_body(d_ref, u_ref, pw_ref, cf_ref, l1_ref, l2_ref, z_ref, m_scr, mask_scr, z_scr, a_scr, w_scr, *,
              ncb):
    g = pl.program_id(0)
    nc = u_ref.shape[0]
    width = SSM_CH * CHUNK
    cb = mask_scr.shape[1]

    @pl.when(g == 0)
    def _():
        s_idx = lax.broadcasted_iota(jnp.int32, (width, cb), 0) & (CHUNK - 1)
        t_idx = lax.broadcasted_iota(jnp.int32, (width, cb), 1) & (CHUNK - 1)
        mask_scr[...] = jnp.where(t_idx >= s_idx, -1, 0).astype(jnp.int32)

    def expand(kind, c1, c2, ch):
        return (cf_ref[c1, ch:ch + 1, :] * pw_ref[kind, 0] + cf_ref[c2, ch:ch + 1, :] * pw_ref[kind, 1])

    for ch in range(SSM_CH):
        rows = slice(ch * CHUNK, (ch + 1) * CHUNK)
        a_scr[rows, :] = expand(0, 0, 1, ch).astype(BF16)
        w_scr[rows, :] = expand(1, 0, 1, ch).astype(BF16)
    d_tiles = [expand(2, 2, 3, ch).T.astype(BF16) for ch in range(SSM_CH)]
    v_tiles = [expand(3, 2, 3, ch).T.astype(BF16) for ch in range(SSM_CH)]

    for k in range(width // cb):
        per = cb // CHUNK
        rhs = jnp.concatenate(d_tiles[k * per:(k + 1) * per], axis=1)
        mk = jnp.dot(a_scr[...], rhs, preferred_element_type=F32)
        kept = lax.bitcast_convert_type(mk, jnp.int32) & mask_scr[...]
        m_scr[:, k * cb:(k + 1) * cb] = lax.bitcast_convert_type(kept, F32).astype(BF16)

    u2 = u_ref.reshape(nc * SSM_CH, CHUNK)
    us =[u2[pl.ds(c, nc, stride=SSM_CH), :] for c in range(SSM_CH)]
    x = jnp.concatenate(us, axis=1).astype(BF16)

    acc = jnp.dot(x, w_scr[...], preferred_element_type=F32)
    rmod = lax.broadcasted_iota(jnp.int32, (nc, 2 * SSM_P), 0) & (ncb - 1)
    for k in range(ncb.bit_length() - 1):
        d = 1 << k
        sh = jnp.where(rmod >= d, pltpu.roll(acc, d, 0), 0.0)
        acc = acc + sh * l1_ref[k:k + 1, :] + pltpu.roll(sh, SSM_P, 1) * l2_ref[k:k + 1, :]
    x_in = jnp.where(rmod >= 1, pltpu.roll(acc, 1, 0), 0.0)

    y = jnp.dot(x, m_scr[...], preferred_element_type=F32)
    v_all = jnp.concatenate(v_tiles, axis=1)
    y = y + jnp.dot(x_in.astype(BF16), v_all, preferred_element_type=F32)
    for c in range(SSM_CH):
        yc = y[:, c * CHUNK:(c + 1) * CHUNK] + d_ref[g * SSM_CH + c] * us[c]
        z_scr[pl.ds(c, nc, stride=SSM_CH), :] = _gelu_tanh(yc)
    z_ref[...] = z_scr[...].reshape(nc, SSM_CH, CHUNK)


def _ssm_scan(u3, d_skip, tables, nbatch):
    powers, coefs, l1, l2 = tables
    nc = u3.shape[0]
    ncb = nc // nbatch
    assert ncb & (ncb - 1) == 0 and ncb <= 2 ** SCAN_LEVELS
    width = SSM_CH * CHUNK
    grid_spec = pltpu.PrefetchScalarGridSpec(
        num_scalar_prefetch=1,
        grid=(SSM_G,),
        in_specs=[
            pl.BlockSpec((nc, SSM_CH, CHUNK), lambda g, d: (0, g, 0)),
            pl.BlockSpec((None,) + powers.shape[1:], lambda g, d: (g, 0, 0, 0, 0)),
            pl.BlockSpec((None,) + coefs.shape[1:], lambda g, d: (g, 0, 0, 0)),
            pl.BlockSpec((None, SCAN_LEVELS, 2 * SSM_P), lambda g, d: (g, 0, 0)),
            pl.BlockSpec((None, SCAN_LEVELS, 2 * SSM_P), lambda g, d: (g, 0, 0)),
        ],
        out_specs=pl.BlockSpec((nc, SSM_CH, CHUNK), lambda g, d: (0, g, 0)),
        scratch_shapes=[pltpu.VMEM((width, width), BF16), pltpu.VMEM((width, 512), jnp.int32),
                        pltpu.VMEM((nc * SSM_CH, CHUNK), F32),
                        pltpu.VMEM((width, 2 * SSM_P), BF16), pltpu.VMEM((width, 2 * SSM_P), BF16)],
    )
    return pl.pallas_call(
        functools.partial(_ssm_body, ncb=ncb),
        grid_spec=grid_spec,
        out_shape=jax.ShapeDtypeStruct((nc, SSM_W, CHUNK), F32),
        compiler_params=_cparams(("arbitrary",)),
        name="ssm_scan",
    )(d_skip.astype(F32), u3, powers, coefs, l1, l2)


def _merge_body(o0, o1, o2, l0, l1, l2, zt_ref, ga_ref, gs_ref, x_ref,
                wglu_ref, bglu_ref, wab_ref, wsb_ref, wout_ref, out_ref, tok_scr):
    def token_major(ref, slot):
        r, rows, _ = ref.shape
        if r == 1:
            return lambda rs: ref[0, rs, :].astype(F32)
        nl = GROUP_W // LANES
        scrs = [tok_scr.at[slot * nl + k] for k in range(nl)]
        for c in range(r):
            sub = ref[c].astype(F32)
            for k in range(nl):
                scrs[k][pl.ds(c, rows, stride=r), :] = sub[:, k * LANES:(k + 1) * LANES]
        return lambda rs: jnp.concatenate([s[rs, :] for s in scrs], axis=1)

    lse = [token_major(l0, 0), token_major(l1, 0), token_major(l2, 1)]
    val = [token_major(o0, 0), token_major(o1, 2), token_major(o2, 3)]

    tm = x_ref.shape[0]
    rows = tm // MERGE_SPLIT
    for h in range(MERGE_SPLIT):
        rs = slice(h * rows, (h + 1) * rows)
        a0, a1, a2 = (f(rs) for f in lse)
        v0, v1, v2 = (f(rs) for f in val)
        mx = jnp.maximum(jnp.maximum(a0, a1), a2)
        e0, e1, e2 = jnp.exp(a0 - mx), jnp.exp(a1 - mx), jnp.exp(a2 - mx)
        mix = (e0 * v0 + e1 * v1 + e2 * v2) / (e0 + e1 + e2)
        y_attn = jnp.dot(mix.astype(BF16), wab_ref[...], preferred_element_type=F32)

        chunks = range(h * rows // CHUNK, (h + 1) * rows // CHUNK)
        z = jnp.concatenate([zt_ref[k].T for k in chunks], axis=0).astype(BF16)
        gl = jnp.dot(z, wglu_ref[...], preferred_element_type=F32) + bglu_ref[...]
        sg = z.astype(F32) * jax.nn.sigmoid(gl)
        y_ssm = jnp.dot(sg.astype(BF16), wsb_ref[...], preferred_element_type=F32)

        merged = (jax.nn.sigmoid(ga_ref[rs, :].astype(F32)) * y_attn
                  + jax.nn.sigmoid(gs_ref[rs, :].astype(F32)) * y_ssm)
        out_ref[rs, :] = x_ref[rs, :] + jnp.dot(merged.astype(BF16), wout_ref[...],
                                                 preferred_element_type=F32)


def _merge(os_, ls_, zt, gates, x2, wglu, bglu, wab, wsb, wout, tm=512):
    n = x2.shape[0]
    tiles_per_seq = os_[0].shape[2] // tm
    row = lambda i: (i, 0)
    const = lambda i: (0, 0)

    def group_spec(a):
        r = a.shape[1]
        return pl.BlockSpec((None, r, tm // r, GROUP_W),
                            lambda i: (i // tiles_per_seq, 0, i % tiles_per_seq, 0))

    in_specs = (
        [group_spec(a) for a in os_] + [group_spec(a) for a in ls_]
        + [
            pl.BlockSpec((tm // CHUNK, SSM_W, CHUNK), lambda i: (i, 0, 0)),
            pl.BlockSpec((tm, D_MODEL), lambda i: (i, 0)),
            pl.BlockSpec((tm, D_MODEL), lambda i: (i, 1)),
            pl.BlockSpec((tm, D_MODEL), row),
            pl.BlockSpec((SSM_W, SSM_W), const),
            pl.BlockSpec((1, SSM_W), const),
            pl.BlockSpec((GROUP_W, D_MODEL), const),
            pl.BlockSpec((SSM_W, D_MODEL), const),
            pl.BlockSpec((D_MODEL, D_MODEL), const),
        ]
    )
    return pl.pallas_call(
        _merge_body,
        grid=(n // tm,),
        in_specs=in_specs,
        out_specs=pl.BlockSpec((tm, D_MODEL), row),
        out_shape=jax.ShapeDtypeStruct((n, D_MODEL), F32),
        scratch_shapes=[pltpu.VMEM((4 * GROUP_W // LANES, tm, LANES), F32)],
        compiler_params=_cparams(("parallel",)),
        name="merge",
    )(*os_, *ls_, zt, gates, gates, x2, wglu, bglu, wab, wsb, wout)


def _ffn_body(x_ref, g_ref, wg_ref, wu_ref, wd_ref, *rest, tf):
    ncast = (len(rest) - 1) // 2
    o_ref = rest[ncast]
    for src, dst in zip(rest[:ncast], rest[ncast + 1:]):
        dst[...] = src[...].astype(BF16)
    h = _rms(x_ref[...], g_ref[...]).astype(BF16)
    for f in range(wg_ref.shape[1] // tf):
        cols = slice(f * tf, (f + 1) * tf)
        a = jnp.dot(h, wg_ref[:, cols], preferred_element_type=F32)
        b = jnp.dot(h, wu_ref[:, cols], preferred_element_type=F32)
        act = (a * jax.nn.sigmoid(a) * b).astype(BF16)
        part = jnp.dot(act, wd_ref[cols, :], preferred_element_type=F32)
        if f == 0:
            o_ref[...] = x_ref[...] + part
        else:
            o_ref[...] += part


def _dense_ffn(x2, g, wg, wu, wd, to_cast=(), tm=512, tf=256):
    n = x2.shape[0]
    dff = wg.shape[1]
    steps = n // tm
    assert dff % tf == 0 and all(a.shape[0] % (16 * steps) == 0 for a in to_cast)
    resident = lambda shape: pl.BlockSpec(shape, lambda i: (0, 0), pipeline_mode=pl.Buffered(1))
    slabs = [pl.BlockSpec((a.shape[0] // steps, a.shape[1]), lambda i: (i, 0)) for a in to_cast]
    out, *casts = pl.pallas_call(
        functools.partial(_ffn_body, tf=tf),
        grid=(steps,),
        in_specs=[
            pl.BlockSpec((tm, D_MODEL), lambda i: (i, 0)),
            pl.BlockSpec((1, D_MODEL), lambda i: (0, 0)),
            resident((D_MODEL, dff)),
            resident((D_MODEL, dff)),
            resident((dff, D_MODEL)),
        ] + slabs,
        out_specs=[pl.BlockSpec((tm, D_MODEL), lambda i: (i, 0))] + slabs,
        out_shape=[jax.ShapeDtypeStruct((n, D_MODEL), F32)]
        + [jax.ShapeDtypeStruct(a.shape, BF16) for a in to_cast],
        compiler_params=_cparams(("parallel",)),
        name="dense_ffn",
    )(x2, g, wg, wu, wd, *to_cast)
    return out, casts


def _router_body(x_ref, g_ref, wr_ref, idx_ref, gate_ref, idxt_ref):
    h = _rms(x_ref[...], g_ref[...])
    w = wr_ref[...]
    h_hi, w_hi = h.astype(BF16), w.astype(BF16)
    h_lo = (h - h_hi.astype(F32)).astype(BF16)
    w_lo = (w - w_hi.astype(F32)).astype(BF16)
    logits = (jnp.dot(h_hi, w_hi, preferred_element_type=F32)
              + (jnp.dot(h_hi, w_lo, preferred_element_type=F32)
                 + jnp.dot(h_lo, w_hi, preferred_element_type=F32)))
    lane = lax.broadcasted_iota(jnp.int32, logits.shape, 1)
    lane_f = lane.astype(F32)
    logits = jnp.where(lane < N_EXPERTS, logits, -jnp.inf)
    v1 = jnp.max(logits, axis=-1, keepdims=True)
    i1 = jnp.min(jnp.where(logits == v1, lane_f, float(LANES)), axis=-1, keepdims=True)
    rest = jnp.where(lane_f == i1, -jnp.inf, logits)
    v2 = jnp.max(rest, axis=-1, keepdims=True)
    i2 = jnp.min(jnp.where(rest == v2, lane_f, float(LANES)), axis=-1, keepdims=True)
    e = jnp.exp(v2 - v1)
    g1 = 1.0 / (1.0 + e)
    g2 = e / (1.0 + e)
    idx_f = jnp.where(lane == 0, i1, jnp.where(lane == 1, i2, 0.0))
    idx_ref[...] = idx_f.astype(jnp.int32)
    gate_ref[...] = jnp.where(lane == 0, g1, jnp.where(lane == 1, g2, 0.0))
    idxt_ref[...] = idx_f.T[:idxt_ref.shape[0], :].astype(jnp.int32)


def _router(x2, g, wr_pad, tm=1024):
    n = x2.shape[0]
    return pl.pallas_call(
        _router_body,
        grid=(n // tm,),
        in_specs=[
            pl.BlockSpec((tm, D_MODEL), lambda i: (i, 0)),
            pl.BlockSpec((1, D_MODEL), lambda i: (0, 0)),
            pl.BlockSpec((D_MODEL, LANES), lambda i: (0, 0)),
        ],
        out_specs=[
            pl.BlockSpec((tm, LANES), lambda i: (i, 0)),
            pl.BlockSpec((tm, LANES), lambda i: (i, 0)),
            pl.BlockSpec((8, tm), lambda i: (0, i)),
        ],
        out_shape=[
            jax.ShapeDtypeStruct((n, LANES), jnp.int32),
            jax.ShapeDtypeStruct((n, LANES), F32),
            jax.ShapeDtypeStruct((8, n), jnp.int32),
        ],
        compiler_params=_cparams(("parallel",)),
        name="router",
    )(x2, g, wr_pad)


def _pack_bf16_pairs(hb):
    half = hb.shape[1] // 2
    lo = lax.bitcast_convert_type(hb[:, :half].astype(F32), jnp.uint32)
    hi = lax.bitcast_convert_type(hb[:, half:].astype(F32), jnp.uint32)
    return (hi & jnp.uint32(0xFFFF0000)) | (lo >> 16)


def _unpack_bf16_pairs(xu):
    lo = lax.bitcast_convert_type(xu << 16, F32).astype(BF16)
    hi = lax.bitcast_convert_type(xu & jnp.uint32(0xFFFF0000), F32).astype(BF16)
    return lo, hi


def _for_each_run_piece(i, start_ref, loff_ref, len_ref, fn):
    for e in range(N_EXPERTS):
        j = i * N_EXPERTS + e
        length, boff, soff = len_ref[j], loff_ref[j], start_ref[j]
        done = 0
        for p in RUN_PIECES:
            cond = (length & p) != 0
            fn(cond, pl.multiple_of(boff + done, RUN_ALIGN), pl.multiple_of(soff + done, RUN_ALIGN), p)
            done = done + jnp.where(cond, p, 0)


def _dispatch_body(start_ref, loff_ref, len_ref, tail_ref, x_ref, g_ref, idxt_ref, off_ref, cnt0_ref,
                   xs_ref, cbuf, tri_scr, zero_scr, sem, zsem):
    i = pl.program_id(0)
    tt = x_ref.shape[0]

    @pl.when(i == 0)
    def _():
        r = lax.broadcasted_iota(jnp.int32, (tt, tt), 0)
        c = lax.broadcasted_iota(jnp.int32, (tt, tt), 1)
        tri_scr[...] = jnp.where(r < c, 1.0, 0.0).astype(BF16)
        zero_scr[...] = jnp.zeros_like(zero_scr)

        def fill(e):
            row = pl.multiple_of(jnp.maximum(tail_ref[e], 0), MOE_BM)
            return pltpu.make_async_copy(zero_scr, xs_ref.at[pl.ds(row, MOE_BM)], zsem)

        for e in range(tail_ref.shape[0]):
            pl.when(tail_ref[e] >= 0)(lambda e=e: fill(e).start())
        for e in range(tail_ref.shape[0]):
            pl.when(tail_ref[e] >= 0)(lambda e=e: fill(e).wait())

    hb = _rms(x_ref[...], g_ref[...]).astype(BF16)

    sub = lax.broadcasted_iota(jnp.int32, (N_EXPERTS, tt), 0)
    pos = []
    for k in range(2):
        oh = jnp.where(sub == idxt_ref[k:k + 1, :], 1.0, 0.0)
        rank = jnp.dot(oh.astype(BF16), tri_scr[...], preferred_element_type=F32)
        base = off_ref[:, :1] if k == 0 else off_ref[:, :1] + cnt0_ref[:, :1]
        pos.append(jnp.sum(oh * (base + rank), axis=0, keepdims=True))

    rows = lax.broadcasted_iota(jnp.int32, (cbuf.shape[1], tt), 0).astype(F32)
    perm = (jnp.where(rows == pos[0], 1.0, 0.0) + jnp.where(rows == pos[1], 1.0, 0.0)).astype(BF16)
    slot = i % 2
    cbuf[slot] = _pack_bf16_pairs(jnp.dot(perm, hb, preferred_element_type=F32).astype(BF16))

    def piece(op, buf):
        def fn(cond, brow, srow, p):
            cp = pltpu.make_async_copy(cbuf.at[buf, pl.ds(brow, p)], xs_ref.at[pl.ds(srow, p)], sem.at[buf])
            pl.when(cond)(getattr(cp, op))
        return fn

    _for_each_run_piece(i, start_ref, loff_ref, len_ref, piece("start", slot))

    @pl.when(i > 0)
    def _():
        _for_each_run_piece(i - 1, start_ref, loff_ref, len_ref, piece("wait", 1 - slot))

    @pl.when(i == pl.num_programs(0) - 1)
    def _():
        _for_each_run_piece(i, start_ref, loff_ref, len_ref, piece("wait", slot))


def _dispatch(plan, x2, g, idxt):
    n = x2.shape[0]
    tt = MOE_TT
    smem = lambda i, *_: (i, 0, 0)
    grid_spec = pltpu.PrefetchScalarGridSpec(
        num_scalar_prefetch=4,
        grid=(n // tt,),
        in_specs=[
            pl.BlockSpec((tt, D_MODEL), lambda i, *_: (i, 0)),
            pl.BlockSpec((1, D_MODEL), lambda i, *_: (0, 0)),
            pl.BlockSpec((8, tt), lambda i, *_: (0, i)),
            pl.BlockSpec((None, N_EXPERTS, LANES), smem),
            pl.BlockSpec((None, N_EXPERTS, LANES), smem),
        ],
        out_specs=pl.BlockSpec(memory_space=pl.ANY),
        scratch_shapes=[
            pltpu.VMEM((2, CBUF_ROWS, D_MODEL // 2), jnp.uint32),
            pltpu.VMEM((tt, tt), BF16),
            pltpu.VMEM((MOE_BM, D_MODEL // 2), jnp.uint32),
            pltpu.SemaphoreType.DMA((2,)),
            pltpu.SemaphoreType.DMA(()),
        ],
    )
    return pl.pallas_call(
        _dispatch_body,
        grid_spec=grid_spec,
        out_shape=jax.ShapeDtypeStruct((plan["n_slots"], D_MODEL // 2), jnp.uint32),
        compiler_params=_cparams(("arbitrary",)),
        name="moe_dispatch",
    )(plan["start"], plan["loff"], plan["len8"], plan["tail_rows"], x2, g, idxt,
      plan["off_lanes"], plan["cnt0_lanes"])


def _experts_body(be_ref, nu_ref, xs_ref, wg_ref, wu_ref, wd_ref, ys_ref, acc_scr, *, tf):
    del be_ref
    i = pl.program_id(0)
    half = D_MODEL // 2

    @pl.when(i < nu_ref[0])
    def _():
        lo, hi = _unpack_bf16_pairs(xs_ref[...])
        for f in range(wg_ref.shape[1] // tf):
            cols = slice(f * tf, (f + 1) * tf)
            a = (jnp.dot(lo, wg_ref[:half, cols], preferred_element_type=F32)
                 + jnp.dot(hi, wg_ref[half:, cols], preferred_element_type=F32))
            b = (jnp.dot(lo, wu_ref[:half, cols], preferred_element_type=F32)
                 + jnp.dot(hi, wu_ref[half:, cols], preferred_element_type=F32))
            act = (a * jax.nn.sigmoid(a) * b).astype(BF16)
            part = jnp.dot(act, wd_ref[cols, :], preferred_element_type=F32)
            if f == 0:
                acc_scr[...] = part
            else:
                acc_scr[...] += part
        ys_ref[...] = _pack_bf16_pairs(acc_scr[...].astype(BF16))

    @pl.when(i >= nu_ref[0])
    def _():
        ys_ref[...] = jnp.zeros_like(ys_ref)


def _experts(block_e, n_used, xs, wg, wu, wd, tf=512):
    n_slots = xs.shape[0]
    n_blocks = n_slots // MOE_BM
    dff = wg.shape[2]

    def blk(i, nu):
        return jnp.minimum(i, nu[0] - 1)

    def wspec(rows, cols):
        return pl.BlockSpec((None, rows, cols), lambda i, be, nu: (be[blk(i, nu)], 0, 0),
                            pipeline_mode=pl.Buffered(1))

    grid_spec = pltpu.PrefetchScalarGridSpec(
        num_scalar_prefetch=2,
        grid=(n_blocks,),
        in_specs=[
            pl.BlockSpec((MOE_BM, D_MODEL // 2), lambda i, be, nu: (blk(i, nu), 0)),
            wspec(D_MODEL, dff),
            wspec(D_MODEL, dff),
            wspec(dff, D_MODEL),
        ],
        out_specs=pl.BlockSpec((MOE_BM, D_MODEL // 2), lambda i, be, nu: (i, 0)),
        scratch_shapes=[pltpu.VMEM((MOE_BM, D_MODEL), F32)],
    )
    return pl.pallas_call(
        functools.partial(_experts_body, tf=tf),
        grid_spec=grid_spec,
        out_shape=jax.ShapeDtypeStruct((n_slots, D_MODEL // 2), jnp.uint32),
        compiler_params=_cparams(("arbitrary",)),
        name="moe_experts",
    )(block_e, n_used, xs, wg, wu, wd)


def _combine_body(start_ref, loff_ref, len_ref, x_ref, idx_ref, gate_ref, meta_ref, g_ref, ys_ref, o_ref,
                  ybuf, tri_scr, sem):
    i = pl.program_id(0)
    tt = x_ref.shape[0]
    slot = i % 2

    def piece(op, buf):
        def fn(cond, brow, srow, p):
            cp = pltpu.make_async_copy(ys_ref.at[pl.ds(srow, p)], ybuf.at[buf, pl.ds(brow, p)], sem.at[buf])
            pl.when(cond)(getattr(cp, op))
        return fn

    @pl.when(i == 0)
    def _():
        r = lax.broadcasted_iota(jnp.int32, (tt, tt), 0)
        c = lax.broadcasted_iota(jnp.int32, (tt, tt), 1)
        tri_scr[...] = jnp.where(c < r, 1.0, 0.0).astype(BF16)
        ybuf[...] = jnp.zeros_like(ybuf)
        _for_each_run_piece(i, start_ref, loff_ref, len_ref, piece("start", slot))

    @pl.when(i + 1 < pl.num_programs(0))
    def _():
        _for_each_run_piece(i + 1, start_ref, loff_ref, len_ref, piece("start", 1 - slot))

    lane = lax.broadcasted_iota(jnp.int32, (tt, LANES), 1)
    idx = idx_ref[...]
    gt = gate_ref[...]
    cols = lax.broadcasted_iota(jnp.int32, (tt, ybuf.shape[1]), 1).astype(F32)
    sel = None
    for k in range(2):
        oh = jnp.where(lane == idx[:, k:k + 1], 1.0, 0.0)
        rank = jnp.dot(tri_scr[...], oh.astype(BF16), preferred_element_type=F32)
        base = meta_ref[0:1, :] if k == 0 else meta_ref[0:1, :] + meta_ref[1:2, :]
        pos = jnp.sum(oh * (base + rank), axis=1, keepdims=True)
        term = jnp.where(cols == pos, gt[:, k:k + 1], 0.0)
        sel = term if sel is None else sel + term
    sel = sel.astype(BF16)

    _for_each_run_piece(i, start_ref, loff_ref, len_ref, piece("wait", slot))

    lo, hi = _unpack_bf16_pairs(ybuf[slot])
    y = jnp.concatenate([jnp.dot(sel, lo, preferred_element_type=F32),
                         jnp.dot(sel, hi, preferred_element_type=F32)], axis=1)
    o_ref[...] = _rms(x_ref[...] + y, g_ref[...])


def _combine(plan, x2, idx, gates, g_final, ys):
    n = x2.shape[0]
    tt = MOE_TT
    grid_spec = pltpu.PrefetchScalarGridSpec(
        num_scalar_prefetch=3,
        grid=(n // tt,),
        in_specs=[
            pl.BlockSpec((tt, D_MODEL), lambda i, *_: (i, 0)),
            pl.BlockSpec((tt, LANES), lambda i, *_: (i, 0)),
            pl.BlockSpec((tt, LANES), lambda i, *_: (i, 0)),
            pl.BlockSpec((None, 8, LANES), lambda i, *_: (i, 0, 0)),
            pl.BlockSpec((1, D_MODEL), lambda i, *_: (0, 0)),
            pl.BlockSpec(memory_space=pl.ANY),
        ],
        out_specs=pl.BlockSpec((tt, D_MODEL), lambda i, *_: (i, 0)),
        scratch_shapes=[
            pltpu.VMEM((2, CBUF_ROWS, D_MODEL // 2), jnp.uint32),
            pltpu.VMEM((tt, tt), BF16),
            pltpu.SemaphoreType.DMA((2,)),
        ],
    )
    return pl.pallas_call(
        _combine_body,
        grid_spec=grid_spec,
        out_shape=jax.ShapeDtypeStruct((n, D_MODEL), F32),
        compiler_params=_cparams(("arbitrary",)),
        name="moe_combine",
    )(plan["start"], plan["loff"], plan["len8"], x2, idx, gates, plan["meta_rows"], g_final, ys)


def _route_plan(idxt, n):
    nt = n // MOE_TT
    e2 = idxt[:2].reshape(2, nt, MOE_TT)
    oh = (e2[..., None] == jnp.arange(N_EXPERTS, dtype=jnp.int32)).astype(jnp.int32)
    cnt = jnp.sum(oh, axis=2)
    cnt0 = cnt[0]
    len8 = (cnt[0] + cnt[1] + RUN_ALIGN - 1) // RUN_ALIGN * RUN_ALIGN
    loff = jnp.cumsum(len8, axis=1) - len8
    region = jnp.sum(len8, axis=0)
    padded = (region + MOE_BM - 1) // MOE_BM * MOE_BM
    pad_end = jnp.cumsum(padded)
    start = (pad_end - padded)[None, :] + jnp.cumsum(len8, axis=0) - len8
    n_blocks = (2 * n + nt * N_EXPERTS * (RUN_ALIGN - 1) + MOE_BM - 1) // MOE_BM + N_EXPERTS
    starts = jnp.arange(n_blocks, dtype=jnp.int32) * MOE_BM
    block_e = jnp.sum((starts[:, None] >= pad_end[None, :]).astype(jnp.int32), axis=1)
    block_e = jnp.minimum(block_e, N_EXPERTS - 1).astype(jnp.int32)
    n_used = (pad_end[-1] // MOE_BM).astype(jnp.int32).reshape(1)
    tails = jnp.where(padded > 0, pad_end - MOE_BM, -1)
    spare = pad_end[-1] + jnp.arange(n_blocks - (2 * n) // MOE_BM, dtype=pad_end.dtype) * MOE_BM
    spare = jnp.where(spare < n_blocks * MOE_BM, spare, -1)
    lanes = lambda a: jnp.broadcast_to(a.astype(F32)[:, :, None], (nt, N_EXPERTS, LANES))
    meta_rows = jnp.zeros((nt, 8, LANES), F32)
    meta_rows = meta_rows.at[:, 0, :N_EXPERTS].set(loff.astype(F32)).at[:, 1, :N_EXPERTS].set(cnt0.astype(F32))
    flat = lambda a: a.reshape(-1).astype(jnp.int32)
    return dict(start=flat(start), loff=flat(loff), len8=flat(len8), block_e=block_e, n_used=n_used,
                tail_rows=jnp.concatenate([tails, spare]).astype(jnp.int32),
                off_lanes=lanes(loff), cnt0_lanes=lanes(cnt0), meta_rows=meta_rows,
                n_slots=n_blocks * MOE_BM)


def _mixer_layer(x2, batch, seq, rel_bias, norm_g, w_in, ssm, d_skip, w_glu, b_glu,
                 w_attn_br, w_ssm_br, w_out):
    u_lo = 3 * ATTN_W
    blocks = [w_in[:, which * ATTN_W + g * GROUP_W: which * ATTN_W + (g + 1) * GROUP_W]
              for g in range(N_GROUPS) for which in range(3)]
    w_bf = jnp.concatenate(blocks + [w_in[:, u_lo + SSM_W:]], axis=1).astype(BF16)
    wu_bf = w_in[:, u_lo:u_lo + SSM_W].astype(BF16)
    *qkvs, gates, ut = _in_projection(x2, norm_g.reshape(1, D_MODEL), w_bf, wu_bf, batch, seq)

    os_, ls_ = [], []
    for g, (window, dilation) in enumerate(ATTN_GROUPS):
        bias = _band_bias(rel_bias[:, g * HEADS:(g + 1) * HEADS], window, dilation)
        o, l = _attention_group(qkvs[g], bias, g)
        os_.append(o)
        ls_.append(l)

    zt = _ssm_scan(ut, d_skip, _ssm_tables(*ssm), batch)

    return _merge(os_, ls_, zt, gates, x2, w_glu.astype(BF16), b_glu.reshape(1, SSM_W).astype(F32),
                  w_attn_br.astype(BF16), w_ssm_br.astype(BF16), w_out.astype(BF16))


def kernel(x, rel_bias, norm1_g, w_in, ssm_lam_re, ssm_lam_im, ssm_log_dt, ssm_b_re, ssm_b_im, ssm_c_re, ssm_c_im, ssm_d, w_glu, b_glu, w_attn_br, w_ssm_br, w_out, norm2_g, ffn_w_gate, ffn_w_up, ffn_w_down, moe_router, moe_w_gate, moe_w_up, moe_w_down, final_norm_g):
    batch, seq, d = x.shape
    assert d == D_MODEL and norm1_g.shape[0] == 2 and seq % (16 * BLK) == 0
    n = batch * seq
    x2 = x.reshape(n, d)

    def mixer(x2, l):
        ssm = (ssm_lam_re[l], ssm_lam_im[l], ssm_log_dt[l], ssm_b_re[l], ssm_b_im[l],
               ssm_c_re[l], ssm_c_im[l])
        return _mixer_layer(x2, batch, seq, rel_bias, norm1_g[l], w_in[l], ssm, ssm_d[l], w_glu[l],
                            b_glu[l], w_attn_br[l], w_ssm_br[l], w_out[l])

    x2 = mixer(x2, 0)
    moe_w = (moe_w_gate[0], moe_w_up[0], moe_w_down[0])
    x2, moe_bf = _dense_ffn(x2, norm2_g[0].reshape(1, d), ffn_w_gate[0].astype(BF16),
                            ffn_w_up[0].astype(BF16), ffn_w_down[0].astype(BF16),
                            to_cast=[w.reshape(-1, w.shape[2]) for w in moe_w])
    moe_bf = [b.reshape(w.shape) for b, w in zip(moe_bf, moe_w)]

    x2 = mixer(x2, 1)
    g2 = norm2_g[1].reshape(1, d)
    wr_pad = jnp.zeros((d, LANES), F32).at[:, :N_EXPERTS].set(moe_router[0].astype(F32))
    idx, gates, idxt = _router(x2, g2, wr_pad)
    plan = _route_plan(idxt, n)
    xs = _dispatch(plan, x2, g2, idxt)
    ys = _experts(plan["block_e"], plan["n_used"], xs, *moe_bf)
    out = _combine(plan, x2, idx, gates, final_norm_g.reshape(1, d), ys)
    return out.reshape(batch, seq, d)
```

```python
import functools

import numpy as np
import jax
import jax.numpy as jnp
from jax import lax
from jax.experimental import pallas as pl
from jax.experimental.pallas import tpu as pltpu

F32 = jnp.float32
BF16 = jnp.bfloat16

D_MODEL = 1024
HEAD_DIM = 64
ATTN_GROUPS = ((128, 1), (512, 4), (2048, 16))
N_GROUPS = 3
HEADS = 8
GROUP_W = HEADS * HEAD_DIM
ATTN_W = N_GROUPS * GROUP_W
BLK = 128
REL_BUCKETS = 32
REL_MAX_DIST = 2048
NEG_INF = -1e30
LOG2E = 1.4426950408889634
SSM_CH = 16
SSM_W = D_MODEL // 2
SSM_G = SSM_W // SSM_CH
SSM_P = 64
PROJ_W = 3 * ATTN_W + SSM_W + 2 * D_MODEL
N_EXPERTS = 8
MOE_BM = 512
MOE_TT = 512
RUN_ALIGN = 8
RUN_PIECES = tuple(1 << b for b in range(MOE_TT.bit_length() - 1, RUN_ALIGN.bit_length() - 2, -1))
CBUF_ROWS = -(-(2 * MOE_TT + N_EXPERTS * (RUN_ALIGN - 1)) // 16) * 16
EPS = 1e-6
CHUNK = 128
SCAN_LEVELS = 8

MERGE_SPLIT = 1
LANES = 128
VMEM_LIMIT = 56 * 1024 * 1024


def _cparams(sem):
    return pltpu.CompilerParams(dimension_semantics=sem, vmem_limit_bytes=VMEM_LIMIT)


def _rms(x, g):
    return x * lax.rsqrt(jnp.mean(x * x, axis=-1, keepdims=True) + EPS) * g


def _proj_body(x_ref, g_ref, w_ref, wu_ref, qkv0_ref, qkv1_ref, qkv2_ref, gate_ref, ut_ref, d_scr):
    tm = x_ref.shape[0]
    hb = _rms(x_ref[...], g_ref[...]).astype(BF16)
    u = jnp.dot(hb, wu_ref[...], preferred_element_type=F32)
    for k in range(ut_ref.shape[0]):
        ut_ref[k] = u[k * CHUNK:(k + 1) * CHUNK, :].T

    cw = 2 * LANES
    nl = cw // LANES

    def slab(col):
        return jnp.dot(hb, w_ref[:, col:col + cw], preferred_element_type=F32)

    uses = 0
    for g, (out_ref, (_, r)) in enumerate(zip((qkv0_ref, qkv1_ref, qkv2_ref), ATTN_GROUPS)):
        for which in range(3):
            for lo in range(0, GROUP_W, cw):
                res = slab((3 * g + which) * GROUP_W + lo)
                if r == 1:
                    out_ref[which, 0, :, lo:lo + cw] = res.astype(BF16)
                    continue
                scrs = [d_scr.at[(uses % 2) * nl + k] for k in range(nl)]
                uses += 1
                for k in range(nl):
                    scrs[k][...] = res[:, k * LANES:(k + 1) * LANES]
                for c in range(r):
                    sub = [s[pl.ds(c, tm // r, stride=r), :] for s in scrs]
                    out_ref[which, c, :, lo:lo + cw] = jnp.concatenate(sub, axis=1).astype(BF16)
    for lo in range(0, 2 * D_MODEL, cw):
        gate_ref[:, lo:lo + cw] = slab(3 * N_GROUPS * GROUP_W + lo).astype(BF16)


def _in_projection(x2, g, w_bf, wu_bf, batch, seq, tm=512):
    n = x2.shape[0]
    tiles_per_seq = seq // tm
    wcols = w_bf.shape[1]

    def qkv_spec(r):
        return pl.BlockSpec((3, None, r, tm // r, GROUP_W),
                            lambda i: (0, i // tiles_per_seq, 0, i % tiles_per_seq, 0))

    return pl.pallas_call(
        _proj_body,
        grid=(n // tm,),
        in_specs=[
            pl.BlockSpec((tm, D_MODEL), lambda i: (i, 0)),
            pl.BlockSpec((1, D_MODEL), lambda i: (0, 0)),
            pl.BlockSpec((D_MODEL, wcols), lambda i: (0, 0), pipeline_mode=pl.Buffered(1)),
            pl.BlockSpec((D_MODEL, SSM_W), lambda i: (0, 0), pipeline_mode=pl.Buffered(1)),
        ],
        out_specs=[qkv_spec(r) for _, r in ATTN_GROUPS] + [
            pl.BlockSpec((tm, 2 * D_MODEL), lambda i: (i, 0)),
            pl.BlockSpec((tm // CHUNK, SSM_W, CHUNK), lambda i: (i, 0, 0)),
        ],
        out_shape=[jax.ShapeDtypeStruct((3, batch, r, seq // r, GROUP_W), BF16) for _, r in ATTN_GROUPS] + [
            jax.ShapeDtypeStruct((n, 2 * D_MODEL), BF16),
            jax.ShapeDtypeStruct((n // CHUNK, SSM_W, CHUNK), F32),
        ],
        scratch_shapes=[pltpu.VMEM((4, tm, LANES), F32)],
        compiler_params=_cparams(("parallel",)),
        name="in_projection",
    )(x2, g, w_bf, wu_bf)


def _t5_bucket(dist):
    max_exact = REL_BUCKETS // 2
    d = np.maximum(dist, 1).astype(np.float64)
    large = max_exact + (
        np.log(d / max_exact) / np.log(REL_MAX_DIST / max_exact) * (REL_BUCKETS - max_exact)
    ).astype(np.int32)
    large = np.minimum(large, REL_BUCKETS - 1)
    return np.where(dist < max_exact, dist, large).astype(np.int32)


def _band_bias(table, window, dilation):
    steps = window // dilation
    qi = np.arange(BLK)[:, None]
    kj = np.arange(2 * BLK)[None, :]
    delta = BLK + qi - kj
    band = (delta >= 0) & (delta <= steps)
    bucket = _t5_bucket(np.clip(delta, 0, steps) * dilation)
    onehot = np.eye(REL_BUCKETS, dtype=np.float32)[bucket]
    bias = jnp.einsum("qkb,bh->hqk", onehot, table.astype(F32), precision=lax.Precision.HIGHEST)
    return jnp.where(band[None], bias * LOG2E, NEG_INF)


def _attn_body(q_ref, kp_ref, kc_ref, vp_ref, vc_ref, bias_ref, o_ref, l_ref, *, nsub):
    lane = lax.broadcasted_iota(jnp.int32, (BLK, LANES), 1)
    lo = lane < HEAD_DIM
    keep_lo = jnp.where(lo, 1.0, 0.0).astype(BF16)
    keep_hi = jnp.where(lo, 0.0, 1.0).astype(BF16)
    col = lax.broadcasted_iota(jnp.int32, (BLK, 2 * BLK), 1)
    first_pen = jnp.where(col < BLK, jnp.where(pl.program_id(2) == 0, NEG_INF, 0.0), 0.0)
    for i in range(nsub):
        rows = slice(i * BLK, (i + 1) * BLK)
        q = q_ref[rows, :]
        if i == 0:
            kw = jnp.concatenate([kp_ref[...], kc_ref[0:BLK, :]], axis=0)
            vw = jnp.concatenate([vp_ref[...], vc_ref[0:BLK, :]], axis=0)
        else:
            kw = kc_ref[(i - 1) * BLK:(i + 1) * BLK, :]
            vw = vc_ref[(i - 1) * BLK:(i + 1) * BLK, :]
        for hp in range(HEADS // 2):
            cols = slice(hp * LANES, (hp + 1) * LANES)
            q2, k2, v2 = q[:, cols], kw[:, cols], vw[:, cols]
            pvs, ms, dens = [], [], []
            for half in range(2):
                qm = q2 * (keep_lo, keep_hi)[half]
                s = lax.dot_general(qm, k2, (((1,), (1,)), ((), ())), preferred_element_type=F32)
                s = s + bias_ref[2 * hp + half]
                if i == 0:
                    s = s + first_pen
                m = jnp.max(s, axis=-1, keepdims=True)
                p = jnp.exp2(s - m)
                dens.append(jnp.broadcast_to(jnp.sum(p, axis=-1, keepdims=True), (BLK, LANES)))
                ms.append(jnp.broadcast_to(m, (BLK, LANES)))
                pvs.append(jnp.dot(p.astype(BF16), v2, preferred_element_type=F32))
            den = jnp.where(lo, dens[0], dens[1])
            o_ref[rows, cols] = (jnp.where(lo, pvs[0], pvs[1]) / den).astype(BF16)
            l_ref[rows, cols] = jnp.where(lo, ms[0], ms[1]) + jnp.log2(den)


def _attention_group(qkv, bias, g):
    _, batch, r, length, _ = qkv.shape
    qb = min(512, length)
    nsub = qb // BLK

    def cur(which):
        return pl.BlockSpec((None, None, None, qb, GROUP_W), lambda b, c, n: (which, b, c, n, 0))

    def prev(which):
        return pl.BlockSpec((None, None, None, BLK, GROUP_W),
                            lambda b, c, n: (which, b, c, jnp.maximum(n * nsub - 1, 0), 0))

    out_spec = pl.BlockSpec((None, None, qb, GROUP_W), lambda b, c, n: (b, c, n, 0))
    return pl.pallas_call(
        functools.partial(_attn_body, nsub=nsub),
        grid=(batch, r, length // qb),
        in_specs=[cur(0), prev(1), cur(1), prev(2), cur(2),
                  pl.BlockSpec((HEADS, BLK, 2 * BLK), lambda b, c, n: (0, 0, 0))],
        out_specs=[out_spec, out_spec],
        out_shape=[
            jax.ShapeDtypeStruct((batch, r, length, GROUP_W), BF16),
            jax.ShapeDtypeStruct((batch, r, length, GROUP_W), F32),
        ],
        compiler_params=_cparams(("parallel", "parallel", "arbitrary")),
        name=f"attention_g{g}",
    )(qkv, qkv, qkv, qkv, qkv, bias)


def _ssm_tables(lam_re, lam_im, log_dt, b_re, b_im, c_re, c_im):
    lam = lax.complex(lam_re.astype(F32), lam_im.astype(F32))
    dt = jnp.exp(log_dt.astype(F32))[:, None]
    lam_dt = lam * dt
    lam_bar = jnp.exp(lam_dt)
    b = lax.complex(b_re.astype(F32), b_im.astype(F32))
    b_bar = ((lam_bar - 1.0) / lam)[..., None] * b
    half = CHUNK // 2
    t = jnp.arange(CHUNK, dtype=F32)

    def power(k):
        return jnp.exp(lam_dt[:, None, :] * jnp.reshape(jnp.asarray(k, F32), (1, -1, 1)))

    p_fwd = power(t - half)
    p_bwd = power(half - t)

    def in_pair(pw):
        return jnp.stack([jnp.concatenate([pw.real, pw.imag], axis=-1),
                          jnp.concatenate([-pw.imag, pw.real], axis=-1)], axis=1)

    def out_pair(pw):
        return jnp.stack([jnp.concatenate([pw.real, -pw.imag], axis=-1),
                          jnp.concatenate([-pw.imag, -pw.real], axis=-1)], axis=1)

    powers = jnp.stack([
        in_pair(p_bwd),
        in_pair(p_bwd * power(CHUNK - 1.0 - half)),
        out_pair(p_fwd),
        out_pair(p_fwd * power(half + 1.0)),
    ], axis=1)
    dup = lambda a: jnp.concatenate([a, a], axis=-1)
    b_cp = jnp.transpose(b_bar, (0, 2, 1))
    coefs = jnp.stack([dup(b_cp.real), dup(b_cp.imag), dup(c_re.astype(F32)), dup(c_im.astype(F32))],
                      axis=1)
    lc = power([float(CHUNK * 2 ** k) for k in range(SCAN_LEVELS)])
    l1 = jnp.concatenate([lc.real, lc.real], axis=-1)
    l2 = jnp.concatenate([-lc.imag, lc.imag], axis=-1)
    return powers, coefs, l1, l2


def _gelu_tanh(y):
    return y * jax.nn.sigmoid(1.5957691216057308 * (y + 0.044715 * (y * y * y)))


def _ssm_body(d_ref, u_ref, pw_ref, cf_ref, l1_ref, l2_ref, z_ref, m_scr, mask_scr, z_scr, a_scr, w_scr, *,
              ncb):
    g = pl.program_id(0)
    nc = u_ref.shape[0]
    width = SSM_CH * CHUNK
    cb = mask_scr.shape[1]

    @pl.when(g == 0)
    def _():
        s_idx = lax.broadcasted_iota(jnp.int32, (width, cb), 0) & (CHUNK - 1)
        t_idx = lax.broadcasted_iota(jnp.int32, (width, cb), 1) & (CHUNK - 1)
        mask_scr[...] = jnp.where(t_idx >= s_idx, -1, 0).astype(jnp.int32)

    def expand(kind, c1, c2, ch):
        return (cf_ref[c1, ch:ch + 1, :] * pw_ref[kind, 0] + cf_ref[c2, ch:ch + 1, :] * pw_ref[kind, 1])

    for ch in range(SSM_CH):
        rows = slice(ch * CHUNK, (ch + 1) * CHUNK)
        a_scr[rows, :] = expand(0, 0, 1, ch).astype(BF16)
        w_scr[rows, :] = expand(1, 0, 1, ch).astype(BF16)
    d_tiles = [expand(2, 2, 3, ch).T.astype(BF16) for ch in range(SSM_CH)]
    v_tiles = [expand(3, 2, 3, ch).T.astype(BF16) for ch in range(SSM_CH)]

    for k in range(width // cb):
        per = cb // CHUNK
        rhs = jnp.concatenate(d_tiles[k * per:(k + 1) * per], axis=1)
        mk = jnp.dot(a_scr[...], rhs, preferred_element_type=F32)
        kept = lax.bitcast_convert_type(mk, jnp.int32) & mask_scr[...]
        m_scr[:, k * cb:(k + 1) * cb] = lax.bitcast_convert_type(kept, F32).astype(BF16)

    u2 = u_ref.reshape(nc * SSM_CH, CHUNK)
    us =[u2[pl.ds(c, nc, stride=SSM_CH), :] for c in range(SSM_CH)]
    x = jnp.concatenate(us, axis=1).astype(BF16)

    acc = jnp.dot(x, w_scr[...], preferred_element_type=F32)
    rmod = lax.broadcasted_iota(jnp.int32, (nc, 2 * SSM_P), 0) & (ncb - 1)
    for k in range(ncb.bit_length() - 1):
        d = 1 << k
        sh = jnp.where(rmod >= d, pltpu.roll(acc, d, 0), 0.0)
        acc = acc + sh * l1_ref[k:k + 1, :] + pltpu.roll(sh, SSM_P, 1) * l2_ref[k:k + 1, :]
    x_in = jnp.where(rmod >= 1, pltpu.roll(acc, 1, 0), 0.0)

    y = jnp.dot(x, m_scr[...], preferred_element_type=F32)
    v_all = jnp.concatenate(v_tiles, axis=1)
    y = y + jnp.dot(x_in.astype(BF16), v_all, preferred_element_type=F32)
    for c in range(SSM_CH):
        yc = y[:, c * CHUNK:(c + 1) * CHUNK] + d_ref[g * SSM_CH + c] * us[c]
        z_scr[pl.ds(c, nc, stride=SSM_CH), :] = _gelu_tanh(yc)
    z_ref[...] = z_scr[...].reshape(nc, SSM_CH, CHUNK)


def _ssm_scan(u3, d_skip, tables, nbatch):
    powers, coefs, l1, l2 = tables
    nc = u3.shape[0]
    ncb = nc // nbatch
    assert ncb & (ncb - 1) == 0 and ncb <= 2 ** SCAN_LEVELS
    width = SSM_CH * CHUNK
    grid_spec = pltpu.PrefetchScalarGridSpec(
        num_scalar_prefetch=1,
        grid=(SSM_G,),
        in_specs=[
            pl.BlockSpec((nc, SSM_CH, CHUNK), lambda g, d: (0, g, 0)),
            pl.BlockSpec((None,) + powers.shape[1:], lambda g, d: (g, 0, 0, 0, 0)),
            pl.BlockSpec((None,) + coefs.shape[1:], lambda g, d: (g, 0, 0, 0)),
            pl.BlockSpec((None, SCAN_LEVELS, 2 * SSM_P), lambda g, d: (g, 0, 0)),
            pl.BlockSpec((None, SCAN_LEVELS, 2 * SSM_P), lambda g, d: (g, 0, 0)),
        ],
        out_specs=pl.BlockSpec((nc, SSM_CH, CHUNK), lambda g, d: (0, g, 0)),
        scratch_shapes=[pltpu.VMEM((width, width), BF16), pltpu.VMEM((width, 512), jnp.int32),
                        pltpu.VMEM((nc * SSM_CH, CHUNK), F32),
                        pltpu.VMEM((width, 2 * SSM_P), BF16), pltpu.VMEM((width, 2 * SSM_P), BF16)],
    )
    return pl.pallas_call(
        functools.partial(_ssm_body, ncb=ncb),
        grid_spec=grid_spec,
        out_shape=jax.ShapeDtypeStruct((nc, SSM_W, CHUNK), F32),
        compiler_params=_cparams(("arbitrary",)),
        name="ssm_scan",
    )(d_skip.astype(F32), u3, powers, coefs, l1, l2)


def _merge_body(o0, o1, o2, l0, l1, l2, zt_ref, ga_ref, gs_ref, x_ref,
                wglu_ref, bglu_ref, wab_ref, wsb_ref, wout_ref, out_ref, tok_scr):
    def token_major(ref, slot):
        r, rows, _ = ref.shape
        if r == 1:
            return lambda rs: ref[0, rs, :].astype(F32)
        nl = GROUP_W // LANES
        scrs = [tok_scr.at[slot * nl + k] for k in range(nl)]
        for c in range(r):
            sub = ref[c].astype(F32)
            for k in range(nl):
                scrs[k][pl.ds(c, rows, stride=r), :] = sub[:, k * LANES:(k + 1) * LANES]
        return lambda rs: jnp.concatenate([s[rs, :] for s in scrs], axis=1)

    lse = [token_major(l0, 0), token_major(l1, 0), token_major(l2, 1)]
    val = [token_major(o0, 0), token_major(o1, 2), token_major(o2, 3)]

    tm = x_ref.shape[0]
    rows = tm // MERGE_SPLIT
    for h in range(MERGE_SPLIT):
        rs = slice(h * rows, (h + 1) * rows)
        a0, a1, a2 = (f(rs) for f in lse)
        v0, v1, v2 = (f(rs) for f in val)
        mx = jnp.maximum(jnp.maximum(a0, a1), a2)
        e0, e1, e2 = jnp.exp2(a0 - mx), jnp.exp2(a1 - mx), jnp.exp2(a2 - mx)
        mix = (e0 * v0 + e1 * v1 + e2 * v2) / (e0 + e1 + e2)
        y_attn = jnp.dot(mix.astype(BF16), wab_ref[...], preferred_element_type=F32)

        chunks = range(h * rows // CHUNK, (h + 1) * rows // CHUNK)
        z = jnp.concatenate([zt_ref[k].T for k in chunks], axis=0).astype(BF16)
        gl = jnp.dot(z, wglu_ref[...], preferred_element_type=F32) + bglu_ref[...]
        sg = z.astype(F32) * jax.nn.sigmoid(gl)
        y_ssm = jnp.dot(sg.astype(BF16), wsb_ref[...], preferred_element_type=F32)

        merged = (jax.nn.sigmoid(ga_ref[rs, :].astype(F32)) * y_attn
                  + jax.nn.sigmoid(gs_ref[rs, :].astype(F32)) * y_ssm)
        out_ref[rs, :] = x_ref[rs, :] + jnp.dot(merged.astype(BF16), wout_ref[...],
                                                 preferred_element_type=F32)


def _merge(os_, ls_, zt, gates, x2, wglu, bglu, wab, wsb, wout, tm=512):
    n = x2.shape[0]
    tiles_per_seq = os_[0].shape[2] // tm
    row = lambda i: (i, 0)
    const = lambda i: (0, 0)

    def group_spec(a):
        r = a.shape[1]
        return pl.BlockSpec((None, r, tm // r, GROUP_W),
                            lambda i: (i // tiles_per_seq, 0, i % tiles_per_seq, 0))

    in_specs = (
        [group_spec(a) for a in os_] + [group_spec(a) for a in ls_]
        + [
            pl.BlockSpec((tm // CHUNK, SSM_W, CHUNK), lambda i: (i, 0, 0)),
            pl.BlockSpec((tm, D_MODEL), lambda i: (i, 0)),
            pl.BlockSpec((tm, D_MODEL), lambda i: (i, 1)),
            pl.BlockSpec((tm, D_MODEL), row),
            pl.BlockSpec((SSM_W, SSM_W), const),
            pl.BlockSpec((1, SSM_W), const),
            pl.BlockSpec((GROUP_W, D_MODEL), const),
            pl.BlockSpec((SSM_W, D_MODEL), const),
            pl.BlockSpec((D_MODEL, D_MODEL), const),
        ]
    )
    return pl.pallas_call(
        _merge_body,
        grid=(n // tm,),
        in_specs=in_specs,
        out_specs=pl.BlockSpec((tm, D_MODEL), row),
        out_shape=jax.ShapeDtypeStruct((n, D_MODEL), F32),
        scratch_shapes=[pltpu.VMEM((4 * GROUP_W // LANES, tm, LANES), F32)],
        compiler_params=_cparams(("parallel",)),
        name="merge",
    )(*os_, *ls_, zt, gates, gates, x2, wglu, bglu, wab, wsb, wout)


def _ffn_body(x_ref, g_ref, wg_ref, wu_ref, wd_ref, *rest, tf):
    ncast = (len(rest) - 1) // 2
    o_ref = rest[ncast]
    for src, dst in zip(rest[:ncast], rest[ncast + 1:]):
        dst[...] = src[...].astype(BF16)
    h = _rms(x_ref[...], g_ref[...]).astype(BF16)
    for f in range(wg_ref.shape[1] // tf):
        cols = slice(f * tf, (f + 1) * tf)
        a = jnp.dot(h, wg_ref[:, cols], preferred_element_type=F32)
        b = jnp.dot(h, wu_ref[:, cols], preferred_element_type=F32)
        act = (a * jax.nn.sigmoid(a) * b).astype(BF16)
        part = jnp.dot(act, wd_ref[cols, :], preferred_element_type=F32)
        if f == 0:
            o_ref[...] = x_ref[...] + part
        else:
            o_ref[...] += part


def _dense_ffn(x2, g, wg, wu, wd, to_cast=(), tm=512, tf=256):
    n = x2.shape[0]
    dff = wg.shape[1]
    steps = n // tm
    assert dff % tf == 0 and all(a.shape[0] % (16 * steps) == 0 for a in to_cast)
    resident = lambda shape: pl.BlockSpec(shape, lambda i: (0, 0), pipeline_mode=pl.Buffered(1))
    slabs = [pl.BlockSpec((a.shape[0] // steps, a.shape[1]), lambda i: (i, 0)) for a in to_cast]
    out, *casts = pl.pallas_call(
        functools.partial(_ffn_body, tf=tf),
        grid=(steps,),
        in_specs=[
            pl.BlockSpec((tm, D_MODEL), lambda i: (i, 0)),
            pl.BlockSpec((1, D_MODEL), lambda i: (0, 0)),
            resident((D_MODEL, dff)),
            resident((D_MODEL, dff)),
            resident((dff, D_MODEL)),
        ] + slabs,
        out_specs=[pl.BlockSpec((tm, D_MODEL), lambda i: (i, 0))] + slabs,
        out_shape=[jax.ShapeDtypeStruct((n, D_MODEL), F32)]
        + [jax.ShapeDtypeStruct(a.shape, BF16) for a in to_cast],
        compiler_params=_cparams(("parallel",)),
        name="dense_ffn",
    )(x2, g, wg, wu, wd, *to_cast)
    return out, casts


def _router_body(x_ref, g_ref, wr_ref, idx_ref, gate_ref, idxt_ref):
    h = _rms(x_ref[...], g_ref[...])
    w = wr_ref[...]
    h_hi, w_hi = h.astype(BF16), w.astype(BF16)
    h_lo = (h - h_hi.astype(F32)).astype(BF16)
    w_lo = (w - w_hi.astype(F32)).astype(BF16)
    logits = (jnp.dot(h_hi, w_hi, preferred_element_type=F32)
              + (jnp.dot(h_hi, w_lo, preferred_element_type=F32)
                 + jnp.dot(h_lo, w_hi, preferred_element_type=F32)))
    lane = lax.broadcasted_iota(jnp.int32, logits.shape, 1)
    lane_f = lane.astype(F32)
    logits = jnp.where(lane < N_EXPERTS, logits, -jnp.inf)
    v1 = jnp.max(logits, axis=-1, keepdims=True)
    i1 = jnp.min(jnp.where(logits == v1, lane_f, float(LANES)), axis=-1, keepdims=True)
    rest = jnp.where(lane_f == i1, -jnp.inf, logits)
    v2 = jnp.max(rest, axis=-1, keepdims=True)
    i2 = jnp.min(jnp.where(rest == v2, lane_f, float(LANES)), axis=-1, keepdims=True)
    e = jnp.exp(v2 - v1)
    g1 = 1.0 / (1.0 + e)
    g2 = e / (1.0 + e)
    idx_f = jnp.where(lane == 0, i1, jnp.where(lane == 1, i2, 0.0))
    idx_ref[...] = idx_f.astype(jnp.int32)
    gate_ref[...] = jnp.where(lane == 0, g1, jnp.where(lane == 1, g2, 0.0))
    idxt_ref[...] = idx_f.T[:idxt_ref.shape[0], :].astype(jnp.int32)


def _router(x2, g, wr_pad, tm=1024):
    n = x2.shape[0]
    return pl.pallas_call(
        _router_body,
        grid=(n // tm,),
        in_specs=[
            pl.BlockSpec((tm, D_MODEL), lambda i: (i, 0)),
            pl.BlockSpec((1, D_MODEL), lambda i: (0, 0)),
            pl.BlockSpec((D_MODEL, LANES), lambda i: (0, 0)),
        ],
        out_specs=[
            pl.BlockSpec((tm, LANES), lambda i: (i, 0)),
            pl.BlockSpec((tm, LANES), lambda i: (i, 0)),
            pl.BlockSpec((8, tm), lambda i: (0, i)),
        ],
        out_shape=[
            jax.ShapeDtypeStruct((n, LANES), jnp.int32),
            jax.ShapeDtypeStruct((n, LANES), F32),
            jax.ShapeDtypeStruct((8, n), jnp.int32),
        ],
        compiler_params=_cparams(("parallel",)),
        name="router",
    )(x2, g, wr_pad)


def _pack_bf16_pairs(hb):
    half = hb.shape[1] // 2
    lo = lax.bitcast_convert_type(hb[:, :half].astype(F32), jnp.uint32)
    hi = lax.bitcast_convert_type(hb[:, half:].astype(F32), jnp.uint32)
    return (hi & jnp.uint32(0xFFFF0000)) | (lo >> 16)


def _unpack_bf16_pairs(xu):
    lo = lax.bitcast_convert_type(xu << 16, F32).astype(BF16)
    hi = lax.bitcast_convert_type(xu & jnp.uint32(0xFFFF0000), F32).astype(BF16)
    return lo, hi


def _for_each_run_piece(i, start_ref, loff_ref, len_ref, fn):
    for e in range(N_EXPERTS):
        j = i * N_EXPERTS + e
        length, boff, soff = len_ref[j], loff_ref[j], start_ref[j]
        done = 0
        for p in RUN_PIECES:
            cond = (length & p) != 0
            fn(cond, pl.multiple_of(boff + done, RUN_ALIGN), pl.multiple_of(soff + done, RUN_ALIGN), p)
            done = done + jnp.where(cond, p, 0)


def _dispatch_body(start_ref, loff_ref, len_ref, tail_ref, x_ref, g_ref, idxt_ref, off_ref, cnt0_ref,
                   xs_ref, cbuf, tri_scr, zero_scr, sem, zsem):
    i = pl.program_id(0)
    tt = x_ref.shape[0]

    @pl.when(i == 0)
    def _():
        r = lax.broadcasted_iota(jnp.int32, (tt, tt), 0)
        c = lax.broadcasted_iota(jnp.int32, (tt, tt), 1)
        tri_scr[...] = jnp.where(r < c, 1.0, 0.0).astype(BF16)
        zero_scr[...] = jnp.zeros_like(zero_scr)

        def fill(e):
            row = pl.multiple_of(jnp.maximum(tail_ref[e], 0), MOE_BM)
            return pltpu.make_async_copy(zero_scr, xs_ref.at[pl.ds(row, MOE_BM)], zsem)

        for e in range(tail_ref.shape[0]):
            pl.when(tail_ref[e] >= 0)(lambda e=e: fill(e).start())
        for e in range(tail_ref.shape[0]):
            pl.when(tail_ref[e] >= 0)(lambda e=e: fill(e).wait())

    hb = _rms(x_ref[...], g_ref[...]).astype(BF16)

    sub = lax.broadcasted_iota(jnp.int32, (N_EXPERTS, tt), 0)
    pos = []
    for k in range(2):
        oh = jnp.where(sub == idxt_ref[k:k + 1, :], 1.0, 0.0)
        rank = jnp.dot(oh.astype(BF16), tri_scr[...], preferred_element_type=F32)
        base = off_ref[:, :1] if k == 0 else off_ref[:, :1] + cnt0_ref[:, :1]
        pos.append(jnp.sum(oh * (base + rank), axis=0, keepdims=True))

    rows = lax.broadcasted_iota(jnp.int32, (cbuf.shape[1], tt), 0).astype(F32)
    perm = (jnp.where(rows == pos[0], 1.0, 0.0) + jnp.where(rows == pos[1], 1.0, 0.0)).astype(BF16)
    slot = i % 2
    cbuf[slot] = _pack_bf16_pairs(jnp.dot(perm, hb, preferred_element_type=F32).astype(BF16))

    def piece(op, buf):
        def fn(cond, brow, srow, p):
            cp = pltpu.make_async_copy(cbuf.at[buf, pl.ds(brow, p)], xs_ref.at[pl.ds(srow, p)], sem.at[buf])
            pl.when(cond)(getattr(cp, op))
        return fn

    _for_each_run_piece(i, start_ref, loff_ref, len_ref, piece("start", slot))

    @pl.when(i > 0)
    def _():
        _for_each_run_piece(i - 1, start_ref, loff_ref, len_ref, piece("wait", 1 - slot))

    @pl.when(i == pl.num_programs(0) - 1)
    def _():
        _for_each_run_piece(i, start_ref, loff_ref, len_ref, piece("wait", slot))


def _dispatch(plan, x2, g, idxt):
    n = x2.shape[0]
    tt = MOE_TT
    smem = lambda i, *_: (i, 0, 0)
    grid_spec = pltpu.PrefetchScalarGridSpec(
        num_scalar_prefetch=4,
        grid=(n // tt,),
        in_specs=[
            pl.BlockSpec((tt, D_MODEL), lambda i, *_: (i, 0)),
            pl.BlockSpec((1, D_MODEL), lambda i, *_: (0, 0)),
            pl.BlockSpec((8, tt), lambda i, *_: (0, i)),
            pl.BlockSpec((None, N_EXPERTS, LANES), smem),
            pl.BlockSpec((None, N_EXPERTS, LANES), smem),
        ],
        out_specs=pl.BlockSpec(memory_space=pl.ANY),
        scratch_shapes=[
            pltpu.VMEM((2, CBUF_ROWS, D_MODEL // 2), jnp.uint32),
            pltpu.VMEM((tt, tt), BF16),
            pltpu.VMEM((MOE_BM, D_MODEL // 2), jnp.uint32),
            pltpu.SemaphoreType.DMA((2,)),
            pltpu.SemaphoreType.DMA(()),
        ],
    )
    return pl.pallas_call(
        _dispatch_body,
        grid_spec=grid_spec,
        out_shape=jax.ShapeDtypeStruct((plan["n_slots"], D_MODEL // 2), jnp.uint32),
        compiler_params=_cparams(("arbitrary",)),
        name="moe_dispatch",
    )(plan["start"], plan["loff"], plan["len8"], plan["tail_rows"], x2, g, idxt,
      plan["off_lanes"], plan["cnt0_lanes"])


def _experts_body(be_ref, nu_ref, xs_ref, wg_ref, wu_ref, wd_ref, ys_ref, acc_scr, *, tf):
    del be_ref
    i = pl.program_id(0)
    half = D_MODEL // 2

    @pl.when(i < nu_ref[0])
    def _():
        lo, hi = _unpack_bf16_pairs(xs_ref[...])
        for f in range(wg_ref.shape[1] // tf):
            cols = slice(f * tf, (f + 1) * tf)
            a = (jnp.dot(lo, wg_ref[:half, cols], preferred_element_type=F32)
                 + jnp.dot(hi, wg_ref[half:, cols], preferred_element_type=F32))
            b = (jnp.dot(lo, wu_ref[:half, cols], preferred_element_type=F32)
                 + jnp.dot(hi, wu_ref[half:, cols], preferred_element_type=F32))
            act = (a * jax.nn.sigmoid(a) * b).astype(BF16)
            part = jnp.dot(act, wd_ref[cols, :], preferred_element_type=F32)
            if f == 0:
                acc_scr[...] = part
            else:
                acc_scr[...] += part
        ys_ref[...] = _pack_bf16_pairs(acc_scr[...].astype(BF16))

    @pl.when(i >= nu_ref[0])
    def _():
        ys_ref[...] = jnp.zeros_like(ys_ref)


def _experts(block_e, n_used, xs, wg, wu, wd, tf=512):
    n_slots = xs.shape[0]
    n_blocks = n_slots // MOE_BM
    dff = wg.shape[2]

    def blk(i, nu):
        return jnp.minimum(i, nu[0] - 1)

    def wspec(rows, cols):
        return pl.BlockSpec((None, rows, cols), lambda i, be, nu: (be[blk(i, nu)], 0, 0),
                            pipeline_mode=pl.Buffered(1))

    grid_spec = pltpu.PrefetchScalarGridSpec(
        num_scalar_prefetch=2,
        grid=(n_blocks,),
        in_specs=[
            pl.BlockSpec((MOE_BM, D_MODEL // 2), lambda i, be, nu: (blk(i, nu), 0)),
            wspec(D_MODEL, dff),
            wspec(D_MODEL, dff),
            wspec(dff, D_MODEL),
        ],
        out_specs=pl.BlockSpec((MOE_BM, D_MODEL // 2), lambda i, be, nu: (i, 0)),
        scratch_shapes=[pltpu.VMEM((MOE_BM, D_MODEL), F32)],
    )
    return pl.pallas_call(
        functools.partial(_experts_body, tf=tf),
        grid_spec=grid_spec,
        out_shape=jax.ShapeDtypeStruct((n_slots, D_MODEL // 2), jnp.uint32),
        compiler_params=_cparams(("arbitrary",)),
        name="moe_experts",
    )(block_e, n_used, xs, wg, wu, wd)


def _combine_body(start_ref, loff_ref, len_ref, x_ref, idx_ref, gate_ref, meta_ref, g_ref, ys_ref, o_ref,
                  ybuf, tri_scr, sem):
    i = pl.program_id(0)
    tt = x_ref.shape[0]
    slot = i % 2

    def piece(op, buf):
        def fn(cond, brow, srow, p):
            cp = pltpu.make_async_copy(ys_ref.at[pl.ds(srow, p)], ybuf.at[buf, pl.ds(brow, p)], sem.at[buf])
            pl.when(cond)(getattr(cp, op))
        return fn

    @pl.when(i == 0)
    def _():
        r = lax.broadcasted_iota(jnp.int32, (tt, tt), 0)
        c = lax.broadcasted_iota(jnp.int32, (tt, tt), 1)
        tri_scr[...] = jnp.where(c < r, 1.0, 0.0).astype(BF16)
        ybuf[...] = jnp.zeros_like(ybuf)
        _for_each_run_piece(i, start_ref, loff_ref, len_ref, piece("start", slot))

    @pl.when(i + 1 < pl.num_programs(0))
    def _():
        _for_each_run_piece(i + 1, start_ref, loff_ref, len_ref, piece("start", 1 - slot))

    lane = lax.broadcasted_iota(jnp.int32, (tt, LANES), 1)
    idx = idx_ref[...]
    gt = gate_ref[...]
    cols = lax.broadcasted_iota(jnp.int32, (tt, ybuf.shape[1]), 1).astype(F32)
    sel = None
    for k in range(2):
        oh = jnp.where(lane == idx[:, k:k + 1], 1.0, 0.0)
        rank = jnp.dot(tri_scr[...], oh.astype(BF16), preferred_element_type=F32)
        base = meta_ref[0:1, :] if k == 0 else meta_ref[0:1, :] + meta_ref[1:2, :]
        pos = jnp.sum(oh * (base + rank), axis=1, keepdims=True)
        term = jnp.where(cols == pos, gt[:, k:k + 1], 0.0)
        sel = term if sel is None else sel + term
    sel = sel.astype(BF16)

    _for_each_run_piece(i, start_ref, loff_ref, len_ref, piece("wait", slot))

    lo, hi = _unpack_bf16_pairs(ybuf[slot])
    y = jnp.concatenate([jnp.dot(sel, lo, preferred_element_type=F32),
                         jnp.dot(sel, hi, preferred_element_type=F32)], axis=1)
    o_ref[...] = _rms(x_ref[...] + y, g_ref[...])


def _combine(plan, x2, idx, gates, g_final, ys):
    n = x2.shape[0]
    tt = MOE_TT
    grid_spec = pltpu.PrefetchScalarGridSpec(
        num_scalar_prefetch=3,
        grid=(n // tt,),
        in_specs=[
            pl.BlockSpec((tt, D_MODEL), lambda i, *_: (i, 0)),
            pl.BlockSpec((tt, LANES), lambda i, *_: (i, 0)),
            pl.BlockSpec((tt, LANES), lambda i, *_: (i, 0)),
            pl.BlockSpec((None, 8, LANES), lambda i, *_: (i, 0, 0)),
            pl.BlockSpec((1, D_MODEL), lambda i, *_: (0, 0)),
            pl.BlockSpec(memory_space=pl.ANY),
        ],
        out_specs=pl.BlockSpec((tt, D_MODEL), lambda i, *_: (i, 0)),
        scratch_shapes=[
            pltpu.VMEM((2, CBUF_ROWS, D_MODEL // 2), jnp.uint32),
            pltpu.VMEM((tt, tt), BF16),
            pltpu.SemaphoreType.DMA((2,)),
        ],
    )
    return pl.pallas_call(
        _combine_body,
        grid_spec=grid_spec,
        out_shape=jax.ShapeDtypeStruct((n, D_MODEL), F32),
        compiler_params=_cparams(("arbitrary",)),
        name="moe_combine",
    )(plan["start"], plan["loff"], plan["len8"], x2, idx, gates, plan["meta_rows"], g_final, ys)


def _route_plan(idxt, n):
    nt = n // MOE_TT
    e2 = idxt[:2].reshape(2, nt, MOE_TT)
    oh = (e2[..., None] == jnp.arange(N_EXPERTS, dtype=jnp.int32)).astype(jnp.int32)
    cnt = jnp.sum(oh, axis=2)
    cnt0 = cnt[0]
    len8 = (cnt[0] + cnt[1] + RUN_ALIGN - 1) // RUN_ALIGN * RUN_ALIGN
    loff = jnp.cumsum(len8, axis=1) - len8
    region = jnp.sum(len8, axis=0)
    padded = (region + MOE_BM - 1) // MOE_BM * MOE_BM
    pad_end = jnp.cumsum(padded)
    start = (pad_end - padded)[None, :] + jnp.cumsum(len8, axis=0) - len8
    n_blocks = (2 * n + nt * N_EXPERTS * (RUN_ALIGN - 1) + MOE_BM - 1) // MOE_BM + N_EXPERTS
    starts = jnp.arange(n_blocks, dtype=jnp.int32) * MOE_BM
    block_e = jnp.sum((starts[:, None] >= pad_end[None, :]).astype(jnp.int32), axis=1)
    block_e = jnp.minimum(block_e, N_EXPERTS - 1).astype(jnp.int32)
    n_used = (pad_end[-1] // MOE_BM).astype(jnp.int32).reshape(1)
    tails = jnp.where(padded > 0, pad_end - MOE_BM, -1)
    spare = pad_end[-1] + jnp.arange(n_blocks - (2 * n) // MOE_BM, dtype=pad_end.dtype) * MOE_BM
    spare = jnp.where(spare < n_blocks * MOE_BM, spare, -1)
    lanes = lambda a: jnp.broadcast_to(a.astype(F32)[:, :, None], (nt, N_EXPERTS, LANES))
    meta_rows = jnp.zeros((nt, 8, LANES), F32)
    meta_rows = meta_rows.at[:, 0, :N_EXPERTS].set(loff.astype(F32)).at[:, 1, :N_EXPERTS].set(cnt0.astype(F32))
    flat = lambda a: a.reshape(-1).astype(jnp.int32)
    return dict(start=flat(start), loff=flat(loff), len8=flat(len8), block_e=block_e, n_used=n_used,
                tail_rows=jnp.concatenate([tails, spare]).astype(jnp.int32),
                off_lanes=lanes(loff), cnt0_lanes=lanes(cnt0), meta_rows=meta_rows,
                n_slots=n_blocks * MOE_BM)


def _mixer_layer(x2, batch, seq, rel_bias, norm_g, w_in, ssm, d_skip, w_glu, b_glu,
                 w_attn_br, w_ssm_br, w_out):
    u_lo = 3 * ATTN_W
    qscale = (1.0, HEAD_DIM ** -0.5 * LOG2E)
    blocks = [w_in[:, which * ATTN_W + g * GROUP_W: which * ATTN_W + (g + 1) * GROUP_W] * qscale[which == 0]
              for g in range(N_GROUPS) for which in range(3)]
    w_bf = jnp.concatenate(blocks + [w_in[:, u_lo + SSM_W:]], axis=1).astype(BF16)
    wu_bf = w_in[:, u_lo:u_lo + SSM_W].astype(BF16)
    *qkvs, gates, ut = _in_projection(x2, norm_g.reshape(1, D_MODEL), w_bf, wu_bf, batch, seq)

    os_, ls_ = [], []
    for g, (window, dilation) in enumerate(ATTN_GROUPS):
        bias = _band_bias(rel_bias[:, g * HEADS:(g + 1) * HEADS], window, dilation)
        o, l = _attention_group(qkvs[g], bias, g)
        os_.append(o)
        ls_.append(l)

    zt = _ssm_scan(ut, d_skip, _ssm_tables(*ssm), batch)

    return _merge(os_, ls_, zt, gates, x2, w_glu.astype(BF16), b_glu.reshape(1, SSM_W).astype(F32),
                  w_attn_br.astype(BF16), w_ssm_br.astype(BF16), w_out.astype(BF16))


def kernel(x, rel_bias, norm1_g, w_in, ssm_lam_re, ssm_lam_im, ssm_log_dt, ssm_b_re, ssm_b_im, ssm_c_re, ssm_c_im, ssm_d, w_glu, b_glu, w_attn_br, w_ssm_br, w_out, norm2_g, ffn_w_gate, ffn_w_up, ffn_w_down, moe_router, moe_w_gate, moe_w_up, moe_w_down, final_norm_g):
    batch, seq, d = x.shape
    assert d == D_MODEL and norm1_g.shape[0] == 2 and seq % (16 * BLK) == 0
    n = batch * seq
    x2 = x.reshape(n, d)

    def mixer(x2, l):
        ssm = (ssm_lam_re[l], ssm_lam_im[l], ssm_log_dt[l], ssm_b_re[l], ssm_b_im[l],
               ssm_c_re[l], ssm_c_im[l])
        return _mixer_layer(x2, batch, seq, rel_bias, norm1_g[l], w_in[l], ssm, ssm_d[l], w_glu[l],
                            b_glu[l], w_attn_br[l], w_ssm_br[l], w_out[l])

    x2 = mixer(x2, 0)
    moe_w = (moe_w_gate[0], moe_w_up[0], moe_w_down[0])
    x2, moe_bf = _dense_ffn(x2, norm2_g[0].reshape(1, d), ffn_w_gate[0].astype(BF16),
                            ffn_w_up[0].astype(BF16), ffn_w_down[0].astype(BF16),
                            to_cast=[w.reshape(-1, w.shape[2]) for w in moe_w])
    moe_bf = [b.reshape(w.shape) for b, w in zip(moe_bf, moe_w)]

    x2 = mixer(x2, 1)
    g2 = norm2_g[1].reshape(1, d)
    wr_pad = jnp.zeros((d, LANES), F32).at[:, :N_EXPERTS].set(moe_router[0].astype(F32))
    idx, gates, idxt = _router(x2, g2, wr_pad)
    plan = _route_plan(idxt, n)
    xs = _dispatch(plan, x2, g2, idxt)
    ys = _experts(plan["block_e"], plan["n_used"], xs, *moe_bf)
    out = _combine(plan, x2, idx, gates, final_norm_g.reshape(1, d), ys)
    return out.reshape(batch, seq, d)
```

```python
import functools

import numpy as np
import jax
import jax.numpy as jnp
from jax import lax
from jax.experimental import pallas as pl
from jax.experimental.pallas import tpu as pltpu

F32 = jnp.float32
BF16 = jnp.bfloat16

D_MODEL = 1024
HEAD_DIM = 64
ATTN_GROUPS = ((128, 1), (512, 4), (2048, 16))
N_GROUPS = 3
HEADS = 8
GROUP_W = HEADS * HEAD_DIM
ATTN_W = N_GROUPS * GROUP_W
BLK = 128
REL_BUCKETS = 32
REL_MAX_DIST = 2048
NEG_INF = -1e30
LOG2E = 1.4426950408889634
SSM_CH = 16
SSM_W = D_MODEL // 2
SSM_G = SSM_W // SSM_CH
SSM_P = 64
PROJ_W = 3 * ATTN_W + SSM_W + 2 * D_MODEL
N_EXPERTS = 8
MOE_BM = 512
MOE_TT = 512
RUN_ALIGN = 8
RUN_PIECES = tuple(1 << b for b in range(MOE_TT.bit_length() - 1, RUN_ALIGN.bit_length() - 2, -1))
CBUF_ROWS = -(-(2 * MOE_TT + N_EXPERTS * (RUN_ALIGN - 1)) // 16) * 16
EPS = 1e-6
CHUNK = 128
SCAN_LEVELS = 8

MERGE_SPLIT = 1
LANES = 128
VMEM_LIMIT = 56 * 1024 * 1024


def _cparams(sem):
    return pltpu.CompilerParams(dimension_semantics=sem, vmem_limit_bytes=VMEM_LIMIT)


def _rms(x, g):
    return x * lax.rsqrt(jnp.mean(x * x, axis=-1, keepdims=True) + EPS) * g


def _proj_body(x_ref, g_ref, w_ref, wu_ref, qkv0_ref, qkv1_ref, qkv2_ref, gate_ref, ut_ref, d_scr):
    tm = x_ref.shape[0]
    h = _rms(x_ref[...], g_ref[...])
    hb = h.astype(BF16)
    u = jnp.dot(hb, wu_ref[...], preferred_element_type=F32)
    for k in range(ut_ref.shape[0]):
        ut_ref[k] = u[k * CHUNK:(k + 1) * CHUNK, :].T

    nl = D_MODEL // LANES
    for k in range(nl):
        d_scr[k] = h[:, k * LANES:(k + 1) * LANES]

    def by_subsequence(r):
        blocks = [jnp.concatenate([d_scr.at[k][pl.ds(c, tm // r, stride=r), :] for k in range(nl)], axis=1)
                  for c in range(r)]
        return jnp.concatenate(blocks, axis=0).astype(BF16)

    cw = 2 * LANES
    for g, (out_ref, (_, r)) in enumerate(zip((qkv0_ref, qkv1_ref, qkv2_ref), ATTN_GROUPS)):
        lhs = hb if r == 1 else by_subsequence(r)
        for which in range(3):
            for lo in range(0, GROUP_W, cw):
                col = (3 * g + which) * GROUP_W + lo
                res = jnp.dot(lhs, w_ref[:, col:col + cw], preferred_element_type=F32).astype(BF16)
                for c in range(r):
                    out_ref[which, c, :, lo:lo + cw] = res[c * (tm // r):(c + 1) * (tm // r), :]
    for lo in range(0, 2 * D_MODEL, cw):
        col = 3 * N_GROUPS * GROUP_W + lo
        gate_ref[:, lo:lo + cw] = jnp.dot(hb, w_ref[:, col:col + cw], preferred_element_type=F32).astype(BF16)


def _in_projection(x2, g, w_bf, wu_bf, batch, seq, tm=512):
    n = x2.shape[0]
    tiles_per_seq = seq // tm
    wcols = w_bf.shape[1]

    def qkv_spec(r):
        return pl.BlockSpec((3, None, r, tm // r, GROUP_W),
                            lambda i: (0, i // tiles_per_seq, 0, i % tiles_per_seq, 0))

    return pl.pallas_call(
        _proj_body,
        grid=(n // tm,),
        in_specs=[
            pl.BlockSpec((tm, D_MODEL), lambda i: (i, 0)),
            pl.BlockSpec((1, D_MODEL), lambda i: (0, 0)),
            pl.BlockSpec((D_MODEL, wcols), lambda i: (0, 0), pipeline_mode=pl.Buffered(1)),
            pl.BlockSpec((D_MODEL, SSM_W), lambda i: (0, 0), pipeline_mode=pl.Buffered(1)),
        ],
        out_specs=[qkv_spec(r) for _, r in ATTN_GROUPS] + [
            pl.BlockSpec((tm, 2 * D_MODEL), lambda i: (i, 0)),
            pl.BlockSpec((tm // CHUNK, SSM_W, CHUNK), lambda i: (i, 0, 0)),
        ],
        out_shape=[jax.ShapeDtypeStruct((3, batch, r, seq // r, GROUP_W), BF16) for _, r in ATTN_GROUPS] + [
            jax.ShapeDtypeStruct((n, 2 * D_MODEL), BF16),
            jax.ShapeDtypeStruct((n // CHUNK, SSM_W, CHUNK), F32),
        ],
        scratch_shapes=[pltpu.VMEM((D_MODEL // LANES, tm, LANES), F32)],
        compiler_params=_cparams(("parallel",)),
        name="in_projection",
    )(x2, g, w_bf, wu_bf)


def _t5_bucket(dist):
    max_exact = REL_BUCKETS // 2
    d = np.maximum(dist, 1).astype(np.float64)
    large = max_exact + (
        np.log(d / max_exact) / np.log(REL_MAX_DIST / max_exact) * (REL_BUCKETS - max_exact)
    ).astype(np.int32)
    large = np.minimum(large, REL_BUCKETS - 1)
    return np.where(dist < max_exact, dist, large).astype(np.int32)


def _band_bias(table, window, dilation):
    steps = window // dilation
    qi = np.arange(BLK)[:, None]
    kj = np.arange(2 * BLK)[None, :]
    delta = BLK + qi - kj
    band = (delta >= 0) & (delta <= steps)
    bucket = _t5_bucket(np.clip(delta, 0, steps) * dilation)
    onehot = np.eye(REL_BUCKETS, dtype=np.float32)[bucket]
    bias = jnp.einsum("qkb,bh->hqk", onehot, table.astype(F32), precision=lax.Precision.HIGHEST)
    return jnp.where(band[None], bias * LOG2E, NEG_INF)


def _attn_body(q_ref, kp_ref, kc_ref, vp_ref, vc_ref, bias_ref, o_ref, l_ref, *, nsub):
    lane = lax.broadcasted_iota(jnp.int32, (BLK, LANES), 1)
    lo = lane < HEAD_DIM
    keep_lo = jnp.where(lo, 1.0, 0.0).astype(BF16)
    keep_hi = jnp.where(lo, 0.0, 1.0).astype(BF16)
    col = lax.broadcasted_iota(jnp.int32, (BLK, 2 * BLK), 1)
    first_pen = jnp.where(col < BLK, jnp.where(pl.program_id(2) == 0, NEG_INF, 0.0), 0.0)
    for i in range(nsub):
        rows = slice(i * BLK, (i + 1) * BLK)
        q = q_ref[rows, :]
        if i == 0:
            kw = jnp.concatenate([kp_ref[...], kc_ref[0:BLK, :]], axis=0)
            vw = jnp.concatenate([vp_ref[...], vc_ref[0:BLK, :]], axis=0)
        else:
            kw = kc_ref[(i - 1) * BLK:(i + 1) * BLK, :]
            vw = vc_ref[(i - 1) * BLK:(i + 1) * BLK, :]
        for hp in range(HEADS // 2):
            cols = slice(hp * LANES, (hp + 1) * LANES)
            q2, k2, v2 = q[:, cols], kw[:, cols], vw[:, cols]
            pvs, ms, dens = [], [], []
            for half in range(2):
                qm = q2 * (keep_lo, keep_hi)[half]
                s = lax.dot_general(qm, k2, (((1,), (1,)), ((), ())), preferred_element_type=F32)
                s = s + bias_ref[2 * hp + half]
                if i == 0:
                    s = s + first_pen
                m = jnp.max(s, axis=-1, keepdims=True)
                p = jnp.exp2(s - m)
                dens.append(jnp.broadcast_to(jnp.sum(p, axis=-1, keepdims=True), (BLK, LANES)))
                ms.append(jnp.broadcast_to(m, (BLK, LANES)))
                pvs.append(jnp.dot(p.astype(BF16), v2, preferred_element_type=F32))
            den = jnp.where(lo, dens[0], dens[1])
            o_ref[rows, cols] = (jnp.where(lo, pvs[0], pvs[1]) / den).astype(BF16)
            l_ref[rows, cols] = jnp.where(lo, ms[0], ms[1]) + jnp.log2(den)


def _attention_group(qkv, bias, g):
    _, batch, r, length, _ = qkv.shape
    qb = min(512, length)
    nsub = qb // BLK

    def cur(which):
        return pl.BlockSpec((None, None, None, qb, GROUP_W), lambda b, c, n: (which, b, c, n, 0))

    def prev(which):
        return pl.BlockSpec((None, None, None, BLK, GROUP_W),
                            lambda b, c, n: (which, b, c, jnp.maximum(n * nsub - 1, 0), 0))

    out_spec = pl.BlockSpec((None, None, qb, GROUP_W), lambda b, c, n: (b, c, n, 0))
    return pl.pallas_call(
        functools.partial(_attn_body, nsub=nsub),
        grid=(batch, r, length // qb),
        in_specs=[cur(0), prev(1), cur(1), prev(2), cur(2),
                  pl.BlockSpec((HEADS, BLK, 2 * BLK), lambda b, c, n: (0, 0, 0))],
        out_specs=[out_spec, out_spec],
        out_shape=[
            jax.ShapeDtypeStruct((batch, r, length, GROUP_W), BF16),
            jax.ShapeDtypeStruct((batch, r, length, GROUP_W), F32),
        ],
        compiler_params=_cparams(("parallel", "parallel", "arbitrary")),
        name=f"attention_g{g}",
    )(qkv, qkv, qkv, qkv, qkv, bias)


def _ssm_tables(lam_re, lam_im, log_dt, b_re, b_im, c_re, c_im):
    lam = lax.complex(lam_re.astype(F32), lam_im.astype(F32))
    dt = jnp.exp(log_dt.astype(F32))[:, None]
    lam_dt = lam * dt
    lam_bar = jnp.exp(lam_dt)
    b = lax.complex(b_re.astype(F32), b_im.astype(F32))
    b_bar = ((lam_bar - 1.0) / lam)[..., None] * b
    half = CHUNK // 2
    t = jnp.arange(CHUNK, dtype=F32)

    def power(k):
        return jnp.exp(lam_dt[:, None, :] * jnp.reshape(jnp.asarray(k, F32), (1, -1, 1)))

    p_fwd = power(t - half)
    p_bwd = power(half - t)

    def in_pair(pw):
        return jnp.stack([jnp.concatenate([pw.real, pw.imag], axis=-1),
                          jnp.concatenate([-pw.imag, pw.real], axis=-1)], axis=1)

    def out_pair(pw):
        return jnp.stack([jnp.concatenate([pw.real, -pw.imag], axis=-1),
                          jnp.concatenate([-pw.imag, -pw.real], axis=-1)], axis=1)

    powers = jnp.stack([
        in_pair(p_bwd),
        in_pair(p_bwd * power(CHUNK - 1.0 - half)),
        out_pair(p_fwd),
        out_pair(p_fwd * power(half + 1.0)),
    ], axis=1)
    dup = lambda a: jnp.concatenate([a, a], axis=-1)
    b_cp = jnp.transpose(b_bar, (0, 2, 1))
    coefs = jnp.stack([dup(b_cp.real), dup(b_cp.imag), dup(c_re.astype(F32)), dup(c_im.astype(F32))],
                      axis=1)
    lc = power([float(CHUNK * 2 ** k) for k in range(SCAN_LEVELS)])
    l1 = jnp.concatenate([lc.real, lc.real], axis=-1)
    l2 = jnp.concatenate([-lc.imag, lc.imag], axis=-1)
    return powers, coefs, l1, l2


def _gelu_tanh(y):
    return y * jax.nn.sigmoid(1.5957691216057308 * (y + 0.044715 * (y * y * y)))


def _ssm_body(d_ref, u_ref, pw_ref, cf_ref, l1_ref, l2_ref, z_ref, m_scr, mask_scr, z_scr, a_scr, w_scr, *,
              ncb):
    g = pl.program_id(0)
    nc = u_ref.shape[0]
    width = SSM_CH * CHUNK
    cb = mask_scr.shape[1]

    @pl.when(g == 0)
    def _():
        s_idx = lax.broadcasted_iota(jnp.int32, (width, cb), 0) & (CHUNK - 1)
        t_idx = lax.broadcasted_iota(jnp.int32, (width, cb), 1) & (CHUNK - 1)
        mask_scr[...] = jnp.where(t_idx >= s_idx, -1, 0).astype(jnp.int32)

    def expand(kind, c1, c2, ch):
        return (cf_ref[c1, ch:ch + 1, :] * pw_ref[kind, 0] + cf_ref[c2, ch:ch + 1, :] * pw_ref[kind, 1])

    for ch in range(SSM_CH):
        rows = slice(ch * CHUNK, (ch + 1) * CHUNK)
        a_scr[rows, :] = expand(0, 0, 1, ch).astype(BF16)
        w_scr[rows, :] = expand(1, 0, 1, ch).astype(BF16)
    d_tiles = [expand(2, 2, 3, ch).T.astype(BF16) for ch in range(SSM_CH)]
    v_tiles = [expand(3, 2, 3, ch).T.astype(BF16) for ch in range(SSM_CH)]

    for k in range(width // cb):
        per = cb // CHUNK
        rhs = jnp.concatenate(d_tiles[k * per:(k + 1) * per], axis=1)
        mk = jnp.dot(a_scr[...], rhs, preferred_element_type=F32)
        kept = lax.bitcast_convert_type(mk, jnp.int32) & mask_scr[...]
        m_scr[:, k * cb:(k + 1) * cb] = lax.bitcast_convert_type(kept, F32).astype(BF16)

    u2 = u_ref.reshape(nc * SSM_CH, CHUNK)
    us =[u2[pl.ds(c, nc, stride=SSM_CH), :] for c in range(SSM_CH)]
    x = jnp.concatenate(us, axis=1).astype(BF16)

    acc = jnp.dot(x, w_scr[...], preferred_element_type=F32)
    rmod = lax.broadcasted_iota(jnp.int32, (nc, 2 * SSM_P), 0) & (ncb - 1)
    for k in range(ncb.bit_length() - 1):
        d = 1 << k
        sh = jnp.where(rmod >= d, pltpu.roll(acc, d, 0), 0.0)
        acc = acc + sh * l1_ref[k:k + 1, :] + pltpu.roll(sh, SSM_P, 1) * l2_ref[k:k + 1, :]
    x_in = jnp.where(rmod >= 1, pltpu.roll(acc, 1, 0), 0.0)

    y = jnp.dot(x, m_scr[...], preferred_element_type=F32)
    v_all = jnp.concatenate(v_tiles, axis=1)
    y = y + jnp.dot(x_in.astype(BF16), v_all, preferred_element_type=F32)
    for c in range(SSM_CH):
        yc = y[:, c * CHUNK:(c + 1) * CHUNK] + d_ref[g * SSM_CH + c] * us[c]
        z_scr[pl.ds(c, nc, stride=SSM_CH), :] = _gelu_tanh(yc)
    z_ref[...] = z_scr[...].reshape(nc, SSM_CH, CHUNK)


def _ssm_scan(u3, d_skip, tables, nbatch):
    powers, coefs, l1, l2 = tables
    nc = u3.shape[0]
    ncb = nc // nbatch
    assert ncb & (ncb - 1) == 0 and ncb <= 2 ** SCAN_LEVELS
    width = SSM_CH * CHUNK
    grid_spec = pltpu.PrefetchScalarGridSpec(
        num_scalar_prefetch=1,
        grid=(SSM_G,),
        in_specs=[
            pl.BlockSpec((nc, SSM_CH, CHUNK), lambda g, d: (0, g, 0)),
            pl.BlockSpec((None,) + powers.shape[1:], lambda g, d: (g, 0, 0, 0, 0)),
            pl.BlockSpec((None,) + coefs.shape[1:], lambda g, d: (g, 0, 0, 0)),
            pl.BlockSpec((None, SCAN_LEVELS, 2 * SSM_P), lambda g, d: (g, 0, 0)),
            pl.BlockSpec((None, SCAN_LEVELS, 2 * SSM_P), lambda g, d: (g, 0, 0)),
        ],
        out_specs=pl.BlockSpec((nc, SSM_CH, CHUNK), lambda g, d: (0, g, 0)),
        scratch_shapes=[pltpu.VMEM((width, width), BF16), pltpu.VMEM((width, 512), jnp.int32),
                        pltpu.VMEM((nc * SSM_CH, CHUNK), F32),
                        pltpu.VMEM((width, 2 * SSM_P), BF16), pltpu.VMEM((width, 2 * SSM_P), BF16)],
    )
    return pl.pallas_call(
        functools.partial(_ssm_body, ncb=ncb),
        grid_spec=grid_spec,
        out_shape=jax.ShapeDtypeStruct((nc, SSM_W, CHUNK), F32),
        compiler_params=_cparams(("arbitrary",)),
        name="ssm_scan",
    )(d_skip.astype(F32), u3, powers, coefs, l1, l2)


def _merge_body(o0, o1, o2, l0, l1, l2, zt_ref, ga_ref, gs_ref, x_ref,
                wglu_ref, bglu_ref, wab_ref, wsb_ref, wout_ref, out_ref, tok_scr):
    def token_major(ref, slot):
        r, rows, _ = ref.shape
        if r == 1:
            return lambda rs: ref[0, rs, :].astype(F32)
        nl = GROUP_W // LANES
        scrs = [tok_scr.at[slot * nl + k] for k in range(nl)]
        for c in range(r):
            sub = ref[c].astype(F32)
            for k in range(nl):
                scrs[k][pl.ds(c, rows, stride=r), :] = sub[:, k * LANES:(k + 1) * LANES]
        return lambda rs: jnp.concatenate([s[rs, :] for s in scrs], axis=1)

    lse = [token_major(l0, 0), token_major(l1, 0), token_major(l2, 1)]
    val = [token_major(o0, 0), token_major(o1, 2), token_major(o2, 3)]

    tm = x_ref.shape[0]
    rows = tm // MERGE_SPLIT
    for h in range(MERGE_SPLIT):
        rs = slice(h * rows, (h + 1) * rows)
        a0, a1, a2 = (f(rs) for f in lse)
        v0, v1, v2 = (f(rs) for f in val)
        mx = jnp.maximum(jnp.maximum(a0, a1), a2)
        e0, e1, e2 = jnp.exp2(a0 - mx), jnp.exp2(a1 - mx), jnp.exp2(a2 - mx)
        mix = (e0 * v0 + e1 * v1 + e2 * v2) / (e0 + e1 + e2)
        y_attn = jnp.dot(mix.astype(BF16), wab_ref[...], preferred_element_type=F32)

        chunks = range(h * rows // CHUNK, (h + 1) * rows // CHUNK)
        z = jnp.concatenate([zt_ref[k].T for k in chunks], axis=0).astype(BF16)
        gl = jnp.dot(z, wglu_ref[...], preferred_element_type=F32) + bglu_ref[...]
        sg = z.astype(F32) * jax.nn.sigmoid(gl)
        y_ssm = jnp.dot(sg.astype(BF16), wsb_ref[...], preferred_element_type=F32)

        merged = (jax.nn.sigmoid(ga_ref[rs, :].astype(F32)) * y_attn
                  + jax.nn.sigmoid(gs_ref[rs, :].astype(F32)) * y_ssm)
        out_ref[rs, :] = x_ref[rs, :] + jnp.dot(merged.astype(BF16), wout_ref[...],
                                                 preferred_element_type=F32)


def _merge(os_, ls_, zt, gates, x2, wglu, bglu, wab, wsb, wout, tm=512):
    n = x2.shape[0]
    tiles_per_seq = os_[0].shape[2] // tm
    row = lambda i: (i, 0)
    const = lambda i: (0, 0)

    def group_spec(a):
        r = a.shape[1]
        return pl.BlockSpec((None, r, tm // r, GROUP_W),
                            lambda i: (i // tiles_per_seq, 0, i % tiles_per_seq, 0))

    in_specs = (
        [group_spec(a) for a in os_] + [group_spec(a) for a in ls_]
        + [
            pl.BlockSpec((tm // CHUNK, SSM_W, CHUNK), lambda i: (i, 0, 0)),
            pl.BlockSpec((tm, D_MODEL), lambda i: (i, 0)),
            pl.BlockSpec((tm, D_MODEL), lambda i: (i, 1)),
            pl.BlockSpec((tm, D_MODEL), row),
            pl.BlockSpec((SSM_W, SSM_W), const),
            pl.BlockSpec((1, SSM_W), const),
            pl.BlockSpec((GROUP_W, D_MODEL), const),
            pl.BlockSpec((SSM_W, D_MODEL), const),
            pl.BlockSpec((D_MODEL, D_MODEL), const),
        ]
    )
    return pl.pallas_call(
        _merge_body,
        grid=(n // tm,),
        in_specs=in_specs,
        out_specs=pl.BlockSpec((tm, D_MODEL), row),
        out_shape=jax.ShapeDtypeStruct((n, D_MODEL), F32),
        scratch_shapes=[pltpu.VMEM((4 * GROUP_W // LANES, tm, LANES), F32)],
        compiler_params=_cparams(("parallel",)),
        name="merge",
    )(*os_, *ls_, zt, gates, gates, x2, wglu, bglu, wab, wsb, wout)


def _ffn_body(x_ref, g_ref, wg_ref, wu_ref, wd_ref, *rest, tf):
    ncast = (len(rest) - 1) // 2
    o_ref = rest[ncast]
    for src, dst in zip(rest[:ncast], rest[ncast + 1:]):
        dst[...] = src[...].astype(BF16)
    h = _rms(x_ref[...], g_ref[...]).astype(BF16)
    for f in range(wg_ref.shape[1] // tf):
        cols = slice(f * tf, (f + 1) * tf)
        a = jnp.dot(h, wg_ref[:, cols], preferred_element_type=F32)
        b = jnp.dot(h, wu_ref[:, cols], preferred_element_type=F32)
        act = (a * jax.nn.sigmoid(a) * b).astype(BF16)
        part = jnp.dot(act, wd_ref[cols, :], preferred_element_type=F32)
        if f == 0:
            o_ref[...] = x_ref[...] + part
        else:
            o_ref[...] += part


def _dense_ffn(x2, g, wg, wu, wd, to_cast=(), tm=512, tf=256):
    n = x2.shape[0]
    dff = wg.shape[1]
    steps = n // tm
    assert dff % tf == 0 and all(a.shape[0] % (16 * steps) == 0 for a in to_cast)
    resident = lambda shape: pl.BlockSpec(shape, lambda i: (0, 0), pipeline_mode=pl.Buffered(1))
    slabs = [pl.BlockSpec((a.shape[0] // steps, a.shape[1]), lambda i: (i, 0)) for a in to_cast]
    out, *casts = pl.pallas_call(
        functools.partial(_ffn_body, tf=tf),
        grid=(steps,),
        in_specs=[
            pl.BlockSpec((tm, D_MODEL), lambda i: (i, 0)),
            pl.BlockSpec((1, D_MODEL), lambda i: (0, 0)),
            resident((D_MODEL, dff)),
            resident((D_MODEL, dff)),
            resident((dff, D_MODEL)),
        ] + slabs,
        out_specs=[pl.BlockSpec((tm, D_MODEL), lambda i: (i, 0))] + slabs,
        out_shape=[jax.ShapeDtypeStruct((n, D_MODEL), F32)]
        + [jax.ShapeDtypeStruct(a.shape, BF16) for a in to_cast],
        compiler_params=_cparams(("parallel",)),
        name="dense_ffn",
    )(x2, g, wg, wu, wd, *to_cast)
    return out, casts


def _router_body(x_ref, g_ref, wr_ref, idx_ref, gate_ref, idxt_ref):
    h = _rms(x_ref[...], g_ref[...])
    w = wr_ref[...]
    h_hi, w_hi = h.astype(BF16), w.astype(BF16)
    h_lo = (h - h_hi.astype(F32)).astype(BF16)
    w_lo = (w - w_hi.astype(F32)).astype(BF16)
    logits = (jnp.dot(h_hi, w_hi, preferred_element_type=F32)
              + (jnp.dot(h_hi, w_lo, preferred_element_type=F32)
                 + jnp.dot(h_lo, w_hi, preferred_element_type=F32)))
    lane = lax.broadcasted_iota(jnp.int32, logits.shape, 1)
    lane_f = lane.astype(F32)
    logits = jnp.where(lane < N_EXPERTS, logits, -jnp.inf)
    v1 = jnp.max(logits, axis=-1, keepdims=True)
    i1 = jnp.min(jnp.where(logits == v1, lane_f, float(LANES)), axis=-1, keepdims=True)
    rest = jnp.where(lane_f == i1, -jnp.inf, logits)
    v2 = jnp.max(rest, axis=-1, keepdims=True)
    i2 = jnp.min(jnp.where(rest == v2, lane_f, float(LANES)), axis=-1, keepdims=True)
    e = jnp.exp(v2 - v1)
    g1 = 1.0 / (1.0 + e)
    g2 = e / (1.0 + e)
    idx_f = jnp.where(lane == 0, i1, jnp.where(lane == 1, i2, 0.0))
    idx_ref[...] = idx_f.astype(jnp.int32)
    gate_ref[...] = jnp.where(lane == 0, g1, jnp.where(lane == 1, g2, 0.0))
    idxt_ref[...] = idx_f.T[:idxt_ref.shape[0], :].astype(jnp.int32)


def _router(x2, g, wr_pad, tm=1024):
    n = x2.shape[0]
    return pl.pallas_call(
        _router_body,
        grid=(n // tm,),
        in_specs=[
            pl.BlockSpec((tm, D_MODEL), lambda i: (i, 0)),
            pl.BlockSpec((1, D_MODEL), lambda i: (0, 0)),
            pl.BlockSpec((D_MODEL, LANES), lambda i: (0, 0)),
        ],
        out_specs=[
            pl.BlockSpec((tm, LANES), lambda i: (i, 0)),
            pl.BlockSpec((tm, LANES), lambda i: (i, 0)),
            pl.BlockSpec((8, tm), lambda i: (0, i)),
        ],
        out_shape=[
            jax.ShapeDtypeStruct((n, LANES), jnp.int32),
            jax.ShapeDtypeStruct((n, LANES), F32),
            jax.ShapeDtypeStruct((8, n), jnp.int32),
        ],
        compiler_params=_cparams(("parallel",)),
        name="router",
    )(x2, g, wr_pad)


def _pack_bf16_pairs(hb):
    half = hb.shape[1] // 2
    lo = lax.bitcast_convert_type(hb[:, :half].astype(F32), jnp.uint32)
    hi = lax.bitcast_convert_type(hb[:, half:].astype(F32), jnp.uint32)
    return (hi & jnp.uint32(0xFFFF0000)) | (lo >> 16)


def _unpack_bf16_pairs(xu):
    lo = lax.bitcast_convert_type(xu << 16, F32).astype(BF16)
    hi = lax.bitcast_convert_type(xu & jnp.uint32(0xFFFF0000), F32).astype(BF16)
    return lo, hi


def _for_each_run_piece(i, start_ref, loff_ref, len_ref, fn):
    for e in range(N_EXPERTS):
        j = i * N_EXPERTS + e
        length, boff, soff = len_ref[j], loff_ref[j], start_ref[j]
        done = 0
        for p in RUN_PIECES:
            cond = (length & p) != 0
            fn(cond, pl.multiple_of(boff + done, RUN_ALIGN), pl.multiple_of(soff + done, RUN_ALIGN), p)
            done = done + jnp.where(cond, p, 0)


def _dispatch_body(start_ref, loff_ref, len_ref, tail_ref, x_ref, g_ref, idxt_ref, off_ref, cnt0_ref,
                   xs_ref, cbuf, tri_scr, zero_scr, sem, zsem):
    i = pl.program_id(0)
    tt = x_ref.shape[0]

    @pl.when(i == 0)
    def _():
        r = lax.broadcasted_iota(jnp.int32, (tt, tt), 0)
        c = lax.broadcasted_iota(jnp.int32, (tt, tt), 1)
        tri_scr[...] = jnp.where(r < c, 1.0, 0.0).astype(BF16)
        zero_scr[...] = jnp.zeros_like(zero_scr)

        def fill(e):
            row = pl.multiple_of(jnp.maximum(tail_ref[e], 0), MOE_BM)
            return pltpu.make_async_copy(zero_scr, xs_ref.at[pl.ds(row, MOE_BM)], zsem)

        for e in range(tail_ref.shape[0]):
            pl.when(tail_ref[e] >= 0)(lambda e=e: fill(e).start())
        for e in range(tail_ref.shape[0]):
            pl.when(tail_ref[e] >= 0)(lambda e=e: fill(e).wait())

    hb = _rms(x_ref[...], g_ref[...]).astype(BF16)

    sub = lax.broadcasted_iota(jnp.int32, (N_EXPERTS, tt), 0)
    pos = []
    for k in range(2):
        oh = jnp.where(sub == idxt_ref[k:k + 1, :], 1.0, 0.0)
        rank = jnp.dot(oh.astype(BF16), tri_scr[...], preferred_element_type=F32)
        base = off_ref[:, :1] if k == 0 else off_ref[:, :1] + cnt0_ref[:, :1]
        pos.append(jnp.sum(oh * (base + rank), axis=0, keepdims=True))

    rows = lax.broadcasted_iota(jnp.int32, (cbuf.shape[1], tt), 0).astype(F32)
    perm = (jnp.where(rows == pos[0], 1.0, 0.0) + jnp.where(rows == pos[1], 1.0, 0.0)).astype(BF16)
    slot = i % 2
    cbuf[slot] = _pack_bf16_pairs(jnp.dot(perm, hb, preferred_element_type=F32).astype(BF16))

    def piece(op, buf):
        def fn(cond, brow, srow, p):
            cp = pltpu.make_async_copy(cbuf.at[buf, pl.ds(brow, p)], xs_ref.at[pl.ds(srow, p)], sem.at[buf])
            pl.when(cond)(getattr(cp, op))
        return fn

    _for_each_run_piece(i, start_ref, loff_ref, len_ref, piece("start", slot))

    @pl.when(i > 0)
    def _():
        _for_each_run_piece(i - 1, start_ref, loff_ref, len_ref, piece("wait", 1 - slot))

    @pl.when(i == pl.num_programs(0) - 1)
    def _():
        _for_each_run_piece(i, start_ref, loff_ref, len_ref, piece("wait", slot))


def _dispatch(plan, x2, g, idxt):
    n = x2.shape[0]
    tt = MOE_TT
    smem = lambda i, *_: (i, 0, 0)
    grid_spec = pltpu.PrefetchScalarGridSpec(
        num_scalar_prefetch=4,
        grid=(n // tt,),
        in_specs=[
            pl.BlockSpec((tt, D_MODEL), lambda i, *_: (i, 0)),
            pl.BlockSpec((1, D_MODEL), lambda i, *_: (0, 0)),
            pl.BlockSpec((8, tt), lambda i, *_: (0, i)),
            pl.BlockSpec((None, N_EXPERTS, LANES), smem),
            pl.BlockSpec((None, N_EXPERTS, LANES), smem),
        ],
        out_specs=pl.BlockSpec(memory_space=pl.ANY),
        scratch_shapes=[
            pltpu.VMEM((2, CBUF_ROWS, D_MODEL // 2), jnp.uint32),
            pltpu.VMEM((tt, tt), BF16),
            pltpu.VMEM((MOE_BM, D_MODEL // 2), jnp.uint32),
            pltpu.SemaphoreType.DMA((2,)),
            pltpu.SemaphoreType.DMA(()),
        ],
    )
    return pl.pallas_call(
        _dispatch_body,
        grid_spec=grid_spec,
        out_shape=jax.ShapeDtypeStruct((plan["n_slots"], D_MODEL // 2), jnp.uint32),
        compiler_params=_cparams(("arbitrary",)),
        name="moe_dispatch",
    )(plan["start"], plan["loff"], plan["len8"], plan["tail_rows"], x2, g, idxt,
      plan["off_lanes"], plan["cnt0_lanes"])


def _experts_body(be_ref, nu_ref, xs_ref, wg_ref, wu_ref, wd_ref, ys_ref, acc_scr, *, tf):
    del be_ref
    i = pl.program_id(0)
    half = D_MODEL // 2

    @pl.when(i < nu_ref[0])
    def _():
        lo, hi = _unpack_bf16_pairs(xs_ref[...])
        for f in range(wg_ref.shape[1] // tf):
            cols = slice(f * tf, (f + 1) * tf)
            a = (jnp.dot(lo, wg_ref[:half, cols], preferred_element_type=F32)
                 + jnp.dot(hi, wg_ref[half:, cols], preferred_element_type=F32))
            b = (jnp.dot(lo, wu_ref[:half, cols], preferred_element_type=F32)
                 + jnp.dot(hi, wu_ref[half:, cols], preferred_element_type=F32))
            act = (a * jax.nn.sigmoid(a) * b).astype(BF16)
            part = jnp.dot(act, wd_ref[cols, :], preferred_element_type=F32)
            if f == 0:
                acc_scr[...] = part
            else:
                acc_scr[...] += part
        ys_ref[...] = _pack_bf16_pairs(acc_scr[...].astype(BF16))

    @pl.when(i >= nu_ref[0])
    def _():
        ys_ref[...] = jnp.zeros_like(ys_ref)


def _experts(block_e, n_used, xs, wg, wu, wd, tf=512):
    n_slots = xs.shape[0]
    n_blocks = n_slots // MOE_BM
    dff = wg.shape[2]

    def blk(i, nu):
        return jnp.minimum(i, nu[0] - 1)

    def wspec(rows, cols):
        return pl.BlockSpec((None, rows, cols), lambda i, be, nu: (be[blk(i, nu)], 0, 0),
                            pipeline_mode=pl.Buffered(1))

    grid_spec = pltpu.PrefetchScalarGridSpec(
        num_scalar_prefetch=2,
        grid=(n_blocks,),
        in_specs=[
            pl.BlockSpec((MOE_BM, D_MODEL // 2), lambda i, be, nu: (blk(i, nu), 0)),
            wspec(D_MODEL, dff),
            wspec(D_MODEL, dff),
            wspec(dff, D_MODEL),
        ],
        out_specs=pl.BlockSpec((MOE_BM, D_MODEL // 2), lambda i, be, nu: (i, 0)),
        scratch_shapes=[pltpu.VMEM((MOE_BM, D_MODEL), F32)],
    )
    return pl.pallas_call(
        functools.partial(_experts_body, tf=tf),
        grid_spec=grid_spec,
        out_shape=jax.ShapeDtypeStruct((n_slots, D_MODEL // 2), jnp.uint32),
        compiler_params=_cparams(("arbitrary",)),
        name="moe_experts",
    )(block_e, n_used, xs, wg, wu, wd)


def _combine_body(start_ref, loff_ref, len_ref, x_ref, idx_ref, gate_ref, meta_ref, g_ref, ys_ref, o_ref,
                  ybuf, tri_scr, sem):
    i = pl.program_id(0)
    tt = x_ref.shape[0]
    slot = i % 2

    def piece(op, buf):
        def fn(cond, brow, srow, p):
            cp = pltpu.make_async_copy(ys_ref.at[pl.ds(srow, p)], ybuf.at[buf, pl.ds(brow, p)], sem.at[buf])
            pl.when(cond)(getattr(cp, op))
        return fn

    @pl.when(i == 0)
    def _():
        r = lax.broadcasted_iota(jnp.int32, (tt, tt), 0)
        c = lax.broadcasted_iota(jnp.int32, (tt, tt), 1)
        tri_scr[...] = jnp.where(c < r, 1.0, 0.0).astype(BF16)
        ybuf[...] = jnp.zeros_like(ybuf)
        _for_each_run_piece(i, start_ref, loff_ref, len_ref, piece("start", slot))

    @pl.when(i + 1 < pl.num_programs(0))
    def _():
        _for_each_run_piece(i + 1, start_ref, loff_ref, len_ref, piece("start", 1 - slot))

    lane = lax.broadcasted_iota(jnp.int32, (tt, LANES), 1)
    idx = idx_ref[...]
    gt = gate_ref[...]
    cols = lax.broadcasted_iota(jnp.int32, (tt, ybuf.shape[1]), 1).astype(F32)
    sel = None
    for k in range(2):
        oh = jnp.where(lane == idx[:, k:k + 1], 1.0, 0.0)
        rank = jnp.dot(tri_scr[...], oh.astype(BF16), preferred_element_type=F32)
        base = meta_ref[0:1, :] if k == 0 else meta_ref[0:1, :] + meta_ref[1:2, :]
        pos = jnp.sum(oh * (base + rank), axis=1, keepdims=True)
        term = jnp.where(cols == pos, gt[:, k:k + 1], 0.0)
        sel = term if sel is None else sel + term
    sel = sel.astype(BF16)

    _for_each_run_piece(i, start_ref, loff_ref, len_ref, piece("wait", slot))

    lo, hi = _unpack_bf16_pairs(ybuf[slot])
    y = jnp.concatenate([jnp.dot(sel, lo, preferred_element_type=F32),
                         jnp.dot(sel, hi, preferred_element_type=F32)], axis=1)
    o_ref[...] = _rms(x_ref[...] + y, g_ref[...])


def _combine(plan, x2, idx, gates, g_final, ys):
    n = x2.shape[0]
    tt = MOE_TT
    grid_spec = pltpu.PrefetchScalarGridSpec(
        num_scalar_prefetch=3,
        grid=(n // tt,),
        in_specs=[
            pl.BlockSpec((tt, D_MODEL), lambda i, *_: (i, 0)),
            pl.BlockSpec((tt, LANES), lambda i, *_: (i, 0)),
            pl.BlockSpec((tt, LANES), lambda i, *_: (i, 0)),
            pl.BlockSpec((None, 8, LANES), lambda i, *_: (i, 0, 0)),
            pl.BlockSpec((1, D_MODEL), lambda i, *_: (0, 0)),
            pl.BlockSpec(memory_space=pl.ANY),
        ],
        out_specs=pl.BlockSpec((tt, D_MODEL), lambda i, *_: (i, 0)),
        scratch_shapes=[
            pltpu.VMEM((2, CBUF_ROWS, D_MODEL // 2), jnp.uint32),
            pltpu.VMEM((tt, tt), BF16),
            pltpu.SemaphoreType.DMA((2,)),
        ],
    )
    return pl.pallas_call(
        _combine_body,
        grid_spec=grid_spec,
        out_shape=jax.ShapeDtypeStruct((n, D_MODEL), F32),
        compiler_params=_cparams(("arbitrary",)),
        name="moe_combine",
    )(plan["start"], plan["loff"], plan["len8"], x2, idx, gates, plan["meta_rows"], g_final, ys)


def _route_plan(idxt, n):
    nt = n // MOE_TT
    e2 = idxt[:2].reshape(2, nt, MOE_TT)
    oh = (e2[..., None] == jnp.arange(N_EXPERTS, dtype=jnp.int32)).astype(jnp.int32)
    cnt = jnp.sum(oh, axis=2)
    cnt0 = cnt[0]
    len8 = (cnt[0] + cnt[1] + RUN_ALIGN - 1) // RUN_ALIGN * RUN_ALIGN
    loff = jnp.cumsum(len8, axis=1) - len8
    region = jnp.sum(len8, axis=0)
    padded = (region + MOE_BM - 1) // MOE_BM * MOE_BM
    pad_end = jnp.cumsum(padded)
    start = (pad_end - padded)[None, :] + jnp.cumsum(len8, axis=0) - len8
    n_blocks = (2 * n + nt * N_EXPERTS * (RUN_ALIGN - 1) + MOE_BM - 1) // MOE_BM + N_EXPERTS
    starts = jnp.arange(n_blocks, dtype=jnp.int32) * MOE_BM
    block_e = jnp.sum((starts[:, None] >= pad_end[None, :]).astype(jnp.int32), axis=1)
    block_e = jnp.minimum(block_e, N_EXPERTS - 1).astype(jnp.int32)
    n_used = (pad_end[-1] // MOE_BM).astype(jnp.int32).reshape(1)
    tails = jnp.where(padded > 0, pad_end - MOE_BM, -1)
    spare = pad_end[-1] + jnp.arange(n_blocks - (2 * n) // MOE_BM, dtype=pad_end.dtype) * MOE_BM
    spare = jnp.where(spare < n_blocks * MOE_BM, spare, -1)
    lanes = lambda a: jnp.broadcast_to(a.astype(F32)[:, :, None], (nt, N_EXPERTS, LANES))
    meta_rows = jnp.zeros((nt, 8, LANES), F32)
    meta_rows = meta_rows.at[:, 0, :N_EXPERTS].set(loff.astype(F32)).at[:, 1, :N_EXPERTS].set(cnt0.astype(F32))
    flat = lambda a: a.reshape(-1).astype(jnp.int32)
    return dict(start=flat(start), loff=flat(loff), len8=flat(len8), block_e=block_e, n_used=n_used,
                tail_rows=jnp.concatenate([tails, spare]).astype(jnp.int32),
                off_lanes=lanes(loff), cnt0_lanes=lanes(cnt0), meta_rows=meta_rows,
                n_slots=n_blocks * MOE_BM)


def _mixer_layer(x2, batch, seq, rel_bias, norm_g, w_in, ssm, d_skip, w_glu, b_glu,
                 w_attn_br, w_ssm_br, w_out):
    u_lo = 3 * ATTN_W
    blocks = [w_in[:, which * ATTN_W + g * GROUP_W: which * ATTN_W + (g + 1) * GROUP_W]
              for g in range(N_GROUPS) for which in range(3)]
    col_scale = np.ones((1, 3 * ATTN_W + 2 * D_MODEL), np.float32)
    for g in range(N_GROUPS):
        col_scale[:, 3 * g * GROUP_W:(3 * g + 1) * GROUP_W] = HEAD_DIM ** -0.5 * LOG2E
    w_bf = (jnp.concatenate(blocks + [w_in[:, u_lo + SSM_W:]], axis=1) * col_scale).astype(BF16)
    wu_bf = w_in[:, u_lo:u_lo + SSM_W].astype(BF16)
    *qkvs, gates, ut = _in_projection(x2, norm_g.reshape(1, D_MODEL), w_bf, wu_bf, batch, seq)

    os_, ls_ = [], []
    for g, (window, dilation) in enumerate(ATTN_GROUPS):
        bias = _band_bias(rel_bias[:, g * HEADS:(g + 1) * HEADS], window, dilation)
        o, l = _attention_group(qkvs[g], bias, g)
        os_.append(o)
        ls_.append(l)

    zt = _ssm_scan(ut, d_skip, _ssm_tables(*ssm), batch)

    return _merge(os_, ls_, zt, gates, x2, w_glu.astype(BF16), b_glu.reshape(1, SSM_W).astype(F32),
                  w_attn_br.astype(BF16), w_ssm_br.astype(BF16), w_out.astype(BF16))


def kernel(x, rel_bias, norm1_g, w_in, ssm_lam_re, ssm_lam_im, ssm_log_dt, ssm_b_re, ssm_b_im, ssm_c_re, ssm_c_im, ssm_d, w_glu, b_glu, w_attn_br, w_ssm_br, w_out, norm2_g, ffn_w_gate, ffn_w_up, ffn_w_down, moe_router, moe_w_gate, moe_w_up, moe_w_down, final_norm_g):
    batch, seq, d = x.shape
    assert d == D_MODEL and norm1_g.shape[0] == 2 and seq % (16 * BLK) == 0
    n = batch * seq
    x2 = x.reshape(n, d)

    def mixer(x2, l):
        ssm = (ssm_lam_re[l], ssm_lam_im[l], ssm_log_dt[l], ssm_b_re[l], ssm_b_im[l],
               ssm_c_re[l], ssm_c_im[l])
        return _mixer_layer(x2, batch, seq, rel_bias, norm1_g[l], w_in[l], ssm, ssm_d[l], w_glu[l],
                            b_glu[l], w_attn_br[l], w_ssm_br[l], w_out[l])

    x2 = mixer(x2, 0)
    moe_w = (moe_w_gate[0], moe_w_up[0], moe_w_down[0])
    x2, moe_bf = _dense_ffn(x2, norm2_g[0].reshape(1, d), ffn_w_gate[0].astype(BF16),
                            ffn_w_up[0].astype(BF16), ffn_w_down[0].astype(BF16),
                            to_cast=[w.reshape(-1, w.shape[2]) for w in moe_w])
    moe_bf = [b.reshape(w.shape) for b, w in zip(moe_bf, moe_w)]

    x2 = mixer(x2, 1)
    g2 = norm2_g[1].reshape(1, d)
    wr_pad = jnp.zeros((d, LANES), F32).at[:, :N_EXPERTS].set(moe_router[0].astype(F32))
    idx, gates, idxt = _router(x2, g2, wr_pad)
    plan = _route_plan(idxt, n)
    xs = _dispatch(plan, x2, g2, idxt)
    ys = _experts(plan["block_e"], plan["n_used"], xs, *moe_bf)
    out = _combine(plan, x2, idx, gates, final_norm_g.reshape(1, d), ys)
    return out.reshape(batch, seq, d)
```

```python
import functools

import numpy as np
import jax
import jax.numpy as jnp
from jax import lax
from jax.experimental import pallas as pl
from jax.experimental.pallas import tpu as pltpu

F32 = jnp.float32
BF16 = jnp.bfloat16

D_MODEL = 1024
HEAD_DIM = 64
ATTN_GROUPS = ((128, 1), (512, 4), (2048, 16))
N_GROUPS = 3
HEADS = 8
GROUP_W = HEADS * HEAD_DIM
ATTN_W = N_GROUPS * GROUP_W
BLK = 128
REL_BUCKETS = 32
REL_MAX_DIST = 2048
NEG_INF = -1e30
LOG2E = 1.4426950408889634
SSM_CH = 16
SSM_W = D_MODEL // 2
SSM_G = SSM_W // SSM_CH
SSM_P = 64
PROJ_W = 3 * ATTN_W + SSM_W + 2 * D_MODEL
N_EXPERTS = 8
MOE_BM = 512
MOE_TT = 512
RUN_ALIGN = 8
RUN_PIECES = tuple(1 << b for b in range(MOE_TT.bit_length() - 1, RUN_ALIGN.bit_length() - 2, -1))
CBUF_ROWS = -(-(2 * MOE_TT + N_EXPERTS * (RUN_ALIGN - 1)) // 16) * 16
EPS = 1e-6
CHUNK = 128
SCAN_LEVELS = 8

MERGE_SPLIT = 1
LANES = 128
VMEM_LIMIT = 56 * 1024 * 1024


def _cparams(sem):
    return pltpu.CompilerParams(dimension_semantics=sem, vmem_limit_bytes=VMEM_LIMIT)


def _rms(x, g):
    return x * lax.rsqrt(jnp.mean(x * x, axis=-1, keepdims=True) + EPS) * g


def _proj_body(x_ref, g_ref, w_ref, qkv0_ref, qkv1_ref, qkv2_ref, gate_ref, ut_ref, d_scr):
    tm = x_ref.shape[0]
    u_lo = 3 * ATTN_W
    h = _rms(x_ref[...], g_ref[...])
    hb = h.astype(BF16)
    u = jnp.dot(hb, w_ref[:, u_lo:u_lo + SSM_W], preferred_element_type=F32)
    for k in range(ut_ref.shape[0]):
        ut_ref[k] = u[k * CHUNK:(k + 1) * CHUNK, :].T

    nl = D_MODEL // LANES
    for k in range(nl):
        d_scr[k] = h[:, k * LANES:(k + 1) * LANES]

    def by_subsequence(r):
        blocks = [jnp.concatenate([d_scr.at[k][pl.ds(c, tm // r, stride=r), :] for k in range(nl)], axis=1)
                  for c in range(r)]
        return jnp.concatenate(blocks, axis=0).astype(BF16)

    cw = 2 * LANES
    scale = (HEAD_DIM ** -0.5 * LOG2E, None, None)
    for g, (out_ref, (_, r)) in enumerate(zip((qkv0_ref, qkv1_ref, qkv2_ref), ATTN_GROUPS)):
        lhs = hb if r == 1 else by_subsequence(r)
        for which in range(3):
            for lo in range(0, GROUP_W, cw):
                col = which * ATTN_W + g * GROUP_W + lo
                res = jnp.dot(lhs, w_ref[:, col:col + cw], preferred_element_type=F32)
                if scale[which] is not None:
                    res = res * scale[which]
                res = res.astype(BF16)
                for c in range(r):
                    out_ref[which, c, :, lo:lo + cw] = res[c * (tm // r):(c + 1) * (tm // r), :]
    for lo in range(0, 2 * D_MODEL, cw):
        col = u_lo + SSM_W + lo
        gate_ref[:, lo:lo + cw] = jnp.dot(hb, w_ref[:, col:col + cw], preferred_element_type=F32).astype(BF16)


def _in_projection(x2, g, w_bf, batch, seq, tm=512):
    n = x2.shape[0]
    tiles_per_seq = seq // tm
    wcols = w_bf.shape[1]

    def qkv_spec(r):
        return pl.BlockSpec((3, None, r, tm // r, GROUP_W),
                            lambda i: (0, i // tiles_per_seq, 0, i % tiles_per_seq, 0))

    return pl.pallas_call(
        _proj_body,
        grid=(n // tm,),
        in_specs=[
            pl.BlockSpec((tm, D_MODEL), lambda i: (i, 0)),
            pl.BlockSpec((1, D_MODEL), lambda i: (0, 0)),
            pl.BlockSpec((D_MODEL, wcols), lambda i: (0, 0), pipeline_mode=pl.Buffered(1)),
        ],
        out_specs=[qkv_spec(r) for _, r in ATTN_GROUPS] + [
            pl.BlockSpec((tm, 2 * D_MODEL), lambda i: (i, 0)),
            pl.BlockSpec((tm // CHUNK, SSM_W, CHUNK), lambda i: (i, 0, 0)),
        ],
        out_shape=[jax.ShapeDtypeStruct((3, batch, r, seq // r, GROUP_W), BF16) for _, r in ATTN_GROUPS] + [
            jax.ShapeDtypeStruct((n, 2 * D_MODEL), BF16),
            jax.ShapeDtypeStruct((n // CHUNK, SSM_W, CHUNK), F32),
        ],
        scratch_shapes=[pltpu.VMEM((D_MODEL // LANES, tm, LANES), F32)],
        compiler_params=_cparams(("parallel",)),
        name="in_projection",
    )(x2, g, w_bf)


def _t5_bucket(dist):
    max_exact = REL_BUCKETS // 2
    d = np.maximum(dist, 1).astype(np.float64)
    large = max_exact + (
        np.log(d / max_exact) / np.log(REL_MAX_DIST / max_exact) * (REL_BUCKETS - max_exact)
    ).astype(np.int32)
    large = np.minimum(large, REL_BUCKETS - 1)
    return np.where(dist < max_exact, dist, large).astype(np.int32)


def _band_bias(table, window, dilation):
    steps = window // dilation
    qi = np.arange(BLK)[:, None]
    kj = np.arange(2 * BLK)[None, :]
    delta = BLK + qi - kj
    band = (delta >= 0) & (delta <= steps)
    bucket = _t5_bucket(np.clip(delta, 0, steps) * dilation)
    onehot = np.eye(REL_BUCKETS, dtype=np.float32)[bucket]
    bias = jnp.einsum("qkb,bh->hqk", onehot, table.astype(F32), precision=lax.Precision.HIGHEST)
    return jnp.where(band[None], bias * LOG2E, NEG_INF)


def _attn_body(q_ref, kp_ref, kc_ref, vp_ref, vc_ref, bias_ref, o_ref, l_ref, *, nsub):
    lane = lax.broadcasted_iota(jnp.int32, (BLK, LANES), 1)
    lo = lane < HEAD_DIM
    keep_lo = jnp.where(lo, 1.0, 0.0).astype(BF16)
    keep_hi = jnp.where(lo, 0.0, 1.0).astype(BF16)
    col = lax.broadcasted_iota(jnp.int32, (BLK, 2 * BLK), 1)
    first_pen = jnp.where(col < BLK, jnp.where(pl.program_id(2) == 0, NEG_INF, 0.0), 0.0)
    for i in range(nsub):
        rows = slice(i * BLK, (i + 1) * BLK)
        q = q_ref[rows, :]
        if i == 0:
            kw = jnp.concatenate([kp_ref[...], kc_ref[0:BLK, :]], axis=0)
            vw = jnp.concatenate([vp_ref[...], vc_ref[0:BLK, :]], axis=0)
        else:
            kw = kc_ref[(i - 1) * BLK:(i + 1) * BLK, :]
            vw = vc_ref[(i - 1) * BLK:(i + 1) * BLK, :]
        for hp in range(HEADS // 2):
            cols = slice(hp * LANES, (hp + 1) * LANES)
            q2, k2, v2 = q[:, cols], kw[:, cols], vw[:, cols]
            pvs, ms, dens = [], [], []
            for half in range(2):
                qm = q2 * (keep_lo, keep_hi)[half]
                s = lax.dot_general(qm, k2, (((1,), (1,)), ((), ())), preferred_element_type=F32)
                s = s + bias_ref[2 * hp + half]
                if i == 0:
                    s = s + first_pen
                m = jnp.max(s, axis=-1, keepdims=True)
                p = jnp.exp2(s - m)
                dens.append(jnp.broadcast_to(jnp.sum(p, axis=-1, keepdims=True), (BLK, LANES)))
                ms.append(jnp.broadcast_to(m, (BLK, LANES)))
                pvs.append(jnp.dot(p.astype(BF16), v2, preferred_element_type=F32))
            den = jnp.where(lo, dens[0], dens[1])
            o_ref[rows, cols] = (jnp.where(lo, pvs[0], pvs[1]) / den).astype(BF16)
            l_ref[rows, cols] = jnp.where(lo, ms[0], ms[1]) + jnp.log2(den)


def _attention_group(qkv, bias, g):
    _, batch, r, length, _ = qkv.shape
    qb = min(512, length)
    nsub = qb // BLK

    def cur(which):
        return pl.BlockSpec((None, None, None, qb, GROUP_W), lambda b, c, n: (which, b, c, n, 0))

    def prev(which):
        return pl.BlockSpec((None, None, None, BLK, GROUP_W),
                            lambda b, c, n: (which, b, c, jnp.maximum(n * nsub - 1, 0), 0))

    out_spec = pl.BlockSpec((None, None, qb, GROUP_W), lambda b, c, n: (b, c, n, 0))
    return pl.pallas_call(
        functools.partial(_attn_body, nsub=nsub),
        grid=(batch, r, length // qb),
        in_specs=[cur(0), prev(1), cur(1), prev(2), cur(2),
                  pl.BlockSpec((HEADS, BLK, 2 * BLK), lambda b, c, n: (0, 0, 0))],
        out_specs=[out_spec, out_spec],
        out_shape=[
            jax.ShapeDtypeStruct((batch, r, length, GROUP_W), BF16),
            jax.ShapeDtypeStruct((batch, r, length, GROUP_W), F32),
        ],
        compiler_params=_cparams(("parallel", "parallel", "arbitrary")),
        name=f"attention_g{g}",
    )(qkv, qkv, qkv, qkv, qkv, bias)


def _ssm_tables(lam_re, lam_im, log_dt, b_re, b_im, c_re, c_im):
    lam = lax.complex(lam_re.astype(F32), lam_im.astype(F32))
    dt = jnp.exp(log_dt.astype(F32))[:, None]
    lam_dt = lam * dt
    lam_bar = jnp.exp(lam_dt)
    b = lax.complex(b_re.astype(F32), b_im.astype(F32))
    b_bar = ((lam_bar - 1.0) / lam)[..., None] * b
    half = CHUNK // 2
    t = jnp.arange(CHUNK, dtype=F32)

    def power(k):
        return jnp.exp(lam_dt[:, None, :] * jnp.reshape(jnp.asarray(k, F32), (1, -1, 1)))

    p_fwd = power(t - half)
    p_bwd = power(half - t)

    def in_pair(pw):
        return jnp.stack([jnp.concatenate([pw.real, pw.imag], axis=-1),
                          jnp.concatenate([-pw.imag, pw.real], axis=-1)], axis=1)

    def out_pair(pw):
        return jnp.stack([jnp.concatenate([pw.real, -pw.imag], axis=-1),
                          jnp.concatenate([-pw.imag, -pw.real], axis=-1)], axis=1)

    powers = jnp.stack([
        in_pair(p_bwd),
        in_pair(p_bwd * power(CHUNK - 1.0 - half)),
        out_pair(p_fwd),
        out_pair(p_fwd * power(half + 1.0)),
    ], axis=1)
    dup = lambda a: jnp.concatenate([a, a], axis=-1)
    b_cp = jnp.transpose(b_bar, (0, 2, 1))
    coefs = jnp.stack([dup(b_cp.real), dup(b_cp.imag), dup(c_re.astype(F32)), dup(c_im.astype(F32))],
                      axis=1)
    lc = power([float(CHUNK * 2 ** k) for k in range(SCAN_LEVELS)])
    l1 = jnp.concatenate([lc.real, lc.real], axis=-1)
    l2 = jnp.concatenate([-lc.imag, lc.imag], axis=-1)
    return powers, coefs, l1, l2


def _gelu_tanh(y):
    return y * jax.nn.sigmoid(1.5957691216057308 * (y + 0.044715 * (y * y * y)))


def _ssm_body(d_ref, u_ref, pw_ref, cf_ref, l1_ref, l2_ref, z_ref, m_scr, mask_scr, z_scr, a_scr, w_scr, *,
              ncb):
    g = pl.program_id(0)
    nc = u_ref.shape[0]
    width = SSM_CH * CHUNK
    cb = mask_scr.shape[1]

    @pl.when(g == 0)
    def _():
        s_idx = lax.broadcasted_iota(jnp.int32, (width, cb), 0) & (CHUNK - 1)
        t_idx = lax.broadcasted_iota(jnp.int32, (width, cb), 1) & (CHUNK - 1)
        mask_scr[...] = jnp.where(t_idx >= s_idx, -1, 0).astype(jnp.int32)

    def expand(kind, c1, c2, ch):
        return (cf_ref[c1, ch:ch + 1, :] * pw_ref[kind, 0] + cf_ref[c2, ch:ch + 1, :] * pw_ref[kind, 1])

    for ch in range(SSM_CH):
        rows = slice(ch * CHUNK, (ch + 1) * CHUNK)
        a_scr[rows, :] = expand(0, 0, 1, ch).astype(BF16)
        w_scr[rows, :] = expand(1, 0, 1, ch).astype(BF16)
    d_tiles = [expand(2, 2, 3, ch).T.astype(BF16) for ch in range(SSM_CH)]
    v_tiles = [expand(3, 2, 3, ch).T.astype(BF16) for ch in range(SSM_CH)]

    for k in range(width // cb):
        per = cb // CHUNK
        rhs = jnp.concatenate(d_tiles[k * per:(k + 1) * per], axis=1)
        mk = jnp.dot(a_scr[...], rhs, preferred_element_type=F32)
        kept = lax.bitcast_convert_type(mk, jnp.int32) & mask_scr[...]
        m_scr[:, k * cb:(k + 1) * cb] = lax.bitcast_convert_type(kept, F32).astype(BF16)

    u2 = u_ref.reshape(nc * SSM_CH, CHUNK)
    us =[u2[pl.ds(c, nc, stride=SSM_CH), :] for c in range(SSM_CH)]
    x = jnp.concatenate(us, axis=1).astype(BF16)

    acc = jnp.dot(x, w_scr[...], preferred_element_type=F32)
    rmod = lax.broadcasted_iota(jnp.int32, (nc, 2 * SSM_P), 0) & (ncb - 1)
    for k in range(ncb.bit_length() - 1):
        d = 1 << k
        sh = jnp.where(rmod >= d, pltpu.roll(acc, d, 0), 0.0)
        acc = acc + sh * l1_ref[k:k + 1, :] + pltpu.roll(sh, SSM_P, 1) * l2_ref[k:k + 1, :]
    x_in = jnp.where(rmod >= 1, pltpu.roll(acc, 1, 0), 0.0)

    y = jnp.dot(x, m_scr[...], preferred_element_type=F32)
    v_all = jnp.concatenate(v_tiles, axis=1)
    y = y + jnp.dot(x_in.astype(BF16), v_all, preferred_element_type=F32)
    for c in range(SSM_CH):
        yc = y[:, c * CHUNK:(c + 1) * CHUNK] + d_ref[g * SSM_CH + c] * us[c]
        z_scr[pl.ds(c, nc, stride=SSM_CH), :] = _gelu_tanh(yc)
    z_ref[...] = z_scr[...].reshape(nc, SSM_CH, CHUNK)


def _ssm_scan(u3, d_skip, tables, nbatch):
    powers, coefs, l1, l2 = tables
    nc = u3.shape[0]
    ncb = nc // nbatch
    assert ncb & (ncb - 1) == 0 and ncb <= 2 ** SCAN_LEVELS
    width = SSM_CH * CHUNK
    grid_spec = pltpu.PrefetchScalarGridSpec(
        num_scalar_prefetch=1,
        grid=(SSM_G,),
        in_specs=[
            pl.BlockSpec((nc, SSM_CH, CHUNK), lambda g, d: (0, g, 0)),
            pl.BlockSpec((None,) + powers.shape[1:], lambda g, d: (g, 0, 0, 0, 0)),
            pl.BlockSpec((None,) + coefs.shape[1:], lambda g, d: (g, 0, 0, 0)),
            pl.BlockSpec((None, SCAN_LEVELS, 2 * SSM_P), lambda g, d: (g, 0, 0)),
            pl.BlockSpec((None, SCAN_LEVELS, 2 * SSM_P), lambda g, d: (g, 0, 0)),
        ],
        out_specs=pl.BlockSpec((nc, SSM_CH, CHUNK), lambda g, d: (0, g, 0)),
        scratch_shapes=[pltpu.VMEM((width, width), BF16), pltpu.VMEM((width, 512), jnp.int32),
                        pltpu.VMEM((nc * SSM_CH, CHUNK), F32),
                        pltpu.VMEM((width, 2 * SSM_P), BF16), pltpu.VMEM((width, 2 * SSM_P), BF16)],
    )
    return pl.pallas_call(
        functools.partial(_ssm_body, ncb=ncb),
        grid_spec=grid_spec,
        out_shape=jax.ShapeDtypeStruct((nc, SSM_W, CHUNK), F32),
        compiler_params=_cparams(("arbitrary",)),
        name="ssm_scan",
    )(d_skip.astype(F32), u3, powers, coefs, l1, l2)


def _merge_body(o0, o1, o2, l0, l1, l2, zt_ref, ga_ref, gs_ref, x_ref,
                wglu_ref, bglu_ref, wab_ref, wsb_ref, wout_ref, out_ref, tok_scr):
    def token_major(ref, slot):
        r, rows, _ = ref.shape
        if r == 1:
            return lambda rs: ref[0, rs, :].astype(F32)
        nl = GROUP_W // LANES
        scrs = [tok_scr.at[slot * nl + k] for k in range(nl)]
        for c in range(r):
            sub = ref[c].astype(F32)
            for k in range(nl):
                scrs[k][pl.ds(c, rows, stride=r), :] = sub[:, k * LANES:(k + 1) * LANES]
        return lambda rs: jnp.concatenate([s[rs, :] for s in scrs], axis=1)

    lse = [token_major(l0, 0), token_major(l1, 0), token_major(l2, 1)]
    val = [token_major(o0, 0), token_major(o1, 2), token_major(o2, 3)]

    tm = x_ref.shape[0]
    rows = tm // MERGE_SPLIT
    for h in range(MERGE_SPLIT):
        rs = slice(h * rows, (h + 1) * rows)
        a0, a1, a2 = (f(rs) for f in lse)
        v0, v1, v2 = (f(rs) for f in val)
        mx = jnp.maximum(jnp.maximum(a0, a1), a2)
        e0, e1, e2 = jnp.exp2(a0 - mx), jnp.exp2(a1 - mx), jnp.exp2(a2 - mx)
        mix = (e0 * v0 + e1 * v1 + e2 * v2) / (e0 + e1 + e2)
        y_attn = jnp.dot(mix.astype(BF16), wab_ref[...], preferred_element_type=F32)

        chunks = range(h * rows // CHUNK, (h + 1) * rows // CHUNK)
        z = jnp.concatenate([zt_ref[k].T for k in chunks], axis=0).astype(BF16)
        gl = jnp.dot(z, wglu_ref[...], preferred_element_type=F32) + bglu_ref[...]
        sg = z.astype(F32) * jax.nn.sigmoid(gl)
        y_ssm = jnp.dot(sg.astype(BF16), wsb_ref[...], preferred_element_type=F32)

        merged = (jax.nn.sigmoid(ga_ref[rs, :].astype(F32)) * y_attn
                  + jax.nn.sigmoid(gs_ref[rs, :].astype(F32)) * y_ssm)
        out_ref[rs, :] = x_ref[rs, :] + jnp.dot(merged.astype(BF16), wout_ref[...],
                                                 preferred_element_type=F32)


def _merge(os_, ls_, zt, gates, x2, wglu, bglu, wab, wsb, wout, tm=512):
    n = x2.shape[0]
    tiles_per_seq = os_[0].shape[2] // tm
    row = lambda i: (i, 0)
    const = lambda i: (0, 0)

    def group_spec(a):
        r = a.shape[1]
        return pl.BlockSpec((None, r, tm // r, GROUP_W),
                            lambda i: (i // tiles_per_seq, 0, i % tiles_per_seq, 0))

    in_specs = (
        [group_spec(a) for a in os_] + [group_spec(a) for a in ls_]
        + [
            pl.BlockSpec((tm // CHUNK, SSM_W, CHUNK), lambda i: (i, 0, 0)),
            pl.BlockSpec((tm, D_MODEL), lambda i: (i, 0)),
            pl.BlockSpec((tm, D_MODEL), lambda i: (i, 1)),
            pl.BlockSpec((tm, D_MODEL), row),
            pl.BlockSpec((SSM_W, SSM_W), const),
            pl.BlockSpec((1, SSM_W), const),
            pl.BlockSpec((GROUP_W, D_MODEL), const),
            pl.BlockSpec((SSM_W, D_MODEL), const),
            pl.BlockSpec((D_MODEL, D_MODEL), const),
        ]
    )
    return pl.pallas_call(
        _merge_body,
        grid=(n // tm,),
        in_specs=in_specs,
        out_specs=pl.BlockSpec((tm, D_MODEL), row),
        out_shape=jax.ShapeDtypeStruct((n, D_MODEL), F32),
        scratch_shapes=[pltpu.VMEM((4 * GROUP_W // LANES, tm, LANES), F32)],
        compiler_params=_cparams(("parallel",)),
        name="merge",
    )(*os_, *ls_, zt, gates, gates, x2, wglu, bglu, wab, wsb, wout)


def _ffn_body(x_ref, g_ref, wg_ref, wu_ref, wd_ref, *rest, tf):
    ncast = (len(rest) - 1) // 2
    o_ref = rest[ncast]
    for src, dst in zip(rest[:ncast], rest[ncast + 1:]):
        dst[...] = src[...].astype(BF16)
    h = _rms(x_ref[...], g_ref[...]).astype(BF16)
    for f in range(wg_ref.shape[1] // tf):
        cols = slice(f * tf, (f + 1) * tf)
        a = jnp.dot(h, wg_ref[:, cols], preferred_element_type=F32)
        b = jnp.dot(h, wu_ref[:, cols], preferred_element_type=F32)
        act = (a * jax.nn.sigmoid(a) * b).astype(BF16)
        part = jnp.dot(act, wd_ref[cols, :], preferred_element_type=F32)
        if f == 0:
            o_ref[...] = x_ref[...] + part
        else:
            o_ref[...] += part


def _dense_ffn(x2, g, wg, wu, wd, to_cast=(), tm=512, tf=256):
    n = x2.shape[0]
    dff = wg.shape[1]
    steps = n // tm
    assert dff % tf == 0 and all(a.shape[0] % (16 * steps) == 0 for a in to_cast)
    resident = lambda shape: pl.BlockSpec(shape, lambda i: (0, 0), pipeline_mode=pl.Buffered(1))
    slabs = [pl.BlockSpec((a.shape[0] // steps, a.shape[1]), lambda i: (i, 0)) for a in to_cast]
    out, *casts = pl.pallas_call(
        functools.partial(_ffn_body, tf=tf),
        grid=(steps,),
        in_specs=[
            pl.BlockSpec((tm, D_MODEL), lambda i: (i, 0)),
            pl.BlockSpec((1, D_MODEL), lambda i: (0, 0)),
            resident((D_MODEL, dff)),
            resident((D_MODEL, dff)),
            resident((dff, D_MODEL)),
        ] + slabs,
        out_specs=[pl.BlockSpec((tm, D_MODEL), lambda i: (i, 0))] + slabs,
        out_shape=[jax.ShapeDtypeStruct((n, D_MODEL), F32)]
        + [jax.ShapeDtypeStruct(a.shape, BF16) for a in to_cast],
        compiler_params=_cparams(("parallel",)),
        name="dense_ffn",
    )(x2, g, wg, wu, wd, *to_cast)
    return out, casts


def _route_top2(h, w, idx_ref, gate_ref, idxt_ref):
    h_hi, w_hi = h.astype(BF16), w.astype(BF16)
    h_lo = (h - h_hi.astype(F32)).astype(BF16)
    w_lo = (w - w_hi.astype(F32)).astype(BF16)
    logits = (jnp.dot(h_hi, w_hi, preferred_element_type=F32)
              + (jnp.dot(h_hi, w_lo, preferred_element_type=F32)
                 + jnp.dot(h_lo, w_hi, preferred_element_type=F32)))
    lane = lax.broadcasted_iota(jnp.int32, logits.shape, 1)
    lane_f = lane.astype(F32)
    logits = jnp.where(lane < N_EXPERTS, logits, -jnp.inf)
    v1 = jnp.max(logits, axis=-1, keepdims=True)
    i1 = jnp.min(jnp.where(logits == v1, lane_f, float(LANES)), axis=-1, keepdims=True)
    rest = jnp.where(lane_f == i1, -jnp.inf, logits)
    v2 = jnp.max(rest, axis=-1, keepdims=True)
    i2 = jnp.min(jnp.where(rest == v2, lane_f, float(LANES)), axis=-1, keepdims=True)
    e = jnp.exp(v2 - v1)
    g1 = 1.0 / (1.0 + e)
    g2 = e / (1.0 + e)
    idx_f = jnp.where(lane == 0, i1, jnp.where(lane == 1, i2, 0.0))
    idx_ref[...] = idx_f.astype(jnp.int32)
    gate_ref[...] = jnp.where(lane == 0, g1, jnp.where(lane == 1, g2, 0.0))
    idxt_ref[...] = idx_f.T[:idxt_ref.shape[0], :].astype(jnp.int32)


def _router_body(x_ref, g_ref, wr_ref, idx_ref, gate_ref, idxt_ref):
    _route_top2(_rms(x_ref[...], g_ref[...]), wr_ref[...], idx_ref, gate_ref, idxt_ref)


def _router(x2, g, wr_pad, tm=1024):
    n = x2.shape[0]
    return pl.pallas_call(
        _router_body,
        grid=(n // tm,),
        in_specs=[
            pl.BlockSpec((tm, D_MODEL), lambda i: (i, 0)),
            pl.BlockSpec((1, D_MODEL), lambda i: (0, 0)),
            pl.BlockSpec((D_MODEL, LANES), lambda i: (0, 0)),
        ],
        out_specs=[
            pl.BlockSpec((tm, LANES), lambda i: (i, 0)),
            pl.BlockSpec((tm, LANES), lambda i: (i, 0)),
            pl.BlockSpec((8, tm), lambda i: (0, i)),
        ],
        out_shape=[
            jax.ShapeDtypeStruct((n, LANES), jnp.int32),
            jax.ShapeDtypeStruct((n, LANES), F32),
            jax.ShapeDtypeStruct((8, n), jnp.int32),
        ],
        compiler_params=_cparams(("parallel",)),
        name="router",
    )(x2, g, wr_pad)


def _pack_bf16_pairs(hb):
    half = hb.shape[1] // 2
    lo = lax.bitcast_convert_type(hb[:, :half].astype(F32), jnp.uint32)
    hi = lax.bitcast_convert_type(hb[:, half:].astype(F32), jnp.uint32)
    return (hi & jnp.uint32(0xFFFF0000)) | (lo >> 16)


def _unpack_bf16_pairs(xu):
    lo = lax.bitcast_convert_type(xu << 16, F32).astype(BF16)
    hi = lax.bitcast_convert_type(xu & jnp.uint32(0xFFFF0000), F32).astype(BF16)
    return lo, hi


def _for_each_run_piece(i, start_ref, loff_ref, len_ref, fn):
    for e in range(N_EXPERTS):
        j = i * N_EXPERTS + e
        length, boff, soff = len_ref[j], loff_ref[j], start_ref[j]
        done = 0
        for p in RUN_PIECES:
            cond = (length & p) != 0
            fn(cond, pl.multiple_of(boff + done, RUN_ALIGN), pl.multiple_of(soff + done, RUN_ALIGN), p)
            done = done + jnp.where(cond, p, 0)


def _dispatch_body(start_ref, loff_ref, len_ref, tail_ref, x_ref, g_ref, idxt_ref, off_ref, cnt0_ref,
                   xs_ref, cbuf, tri_scr, zero_scr, sem, zsem):
    i = pl.program_id(0)
    tt = x_ref.shape[0]

    @pl.when(i == 0)
    def _():
        r = lax.broadcasted_iota(jnp.int32, (tt, tt), 0)
        c = lax.broadcasted_iota(jnp.int32, (tt, tt), 1)
        tri_scr[...] = jnp.where(r < c, 1.0, 0.0).astype(BF16)
        zero_scr[...] = jnp.zeros_like(zero_scr)

        def fill(e):
            row = pl.multiple_of(jnp.maximum(tail_ref[e], 0), MOE_BM)
            return pltpu.make_async_copy(zero_scr, xs_ref.at[pl.ds(row, MOE_BM)], zsem)

        for e in range(tail_ref.shape[0]):
            pl.when(tail_ref[e] >= 0)(lambda e=e: fill(e).start())
        for e in range(tail_ref.shape[0]):
            pl.when(tail_ref[e] >= 0)(lambda e=e: fill(e).wait())

    hb = _rms(x_ref[...], g_ref[...]).astype(BF16)

    sub = lax.broadcasted_iota(jnp.int32, (N_EXPERTS, tt), 0)
    pos = []
    for k in range(2):
        oh = jnp.where(sub == idxt_ref[k:k + 1, :], 1.0, 0.0)
        rank = jnp.dot(oh.astype(BF16), tri_scr[...], preferred_element_type=F32)
        base = off_ref[:, :1] if k == 0 else off_ref[:, :1] + cnt0_ref[:, :1]
        pos.append(jnp.sum(oh * (base + rank), axis=0, keepdims=True))

    rows = lax.broadcasted_iota(jnp.int32, (cbuf.shape[1], tt), 0).astype(F32)
    perm = (jnp.where(rows == pos[0], 1.0, 0.0) + jnp.where(rows == pos[1], 1.0, 0.0)).astype(BF16)
    slot = i % 2
    cbuf[slot] = _pack_bf16_pairs(jnp.dot(perm, hb, preferred_element_type=F32).astype(BF16))

    def piece(op, buf):
        def fn(cond, brow, srow, p):
            cp = pltpu.make_async_copy(cbuf.at[buf, pl.ds(brow, p)], xs_ref.at[pl.ds(srow, p)], sem.at[buf])
            pl.when(cond)(getattr(cp, op))
        return fn

    _for_each_run_piece(i, start_ref, loff_ref, len_ref, piece("start", slot))

    @pl.when(i > 0)
    def _():
        _for_each_run_piece(i - 1, start_ref, loff_ref, len_ref, piece("wait", 1 - slot))

    @pl.when(i == pl.num_programs(0) - 1)
    def _():
        _for_each_run_piece(i, start_ref, loff_ref, len_ref, piece("wait", slot))


def _dispatch(plan, x2, g, idxt):
    n = x2.shape[0]
    tt = MOE_TT
    smem = lambda i, *_: (i, 0, 0)
    grid_spec = pltpu.PrefetchScalarGridSpec(
        num_scalar_prefetch=4,
        grid=(n // tt,),
        in_specs=[
            pl.BlockSpec((tt, D_MODEL), lambda i, *_: (i, 0)),
            pl.BlockSpec((1, D_MODEL), lambda i, *_: (0, 0)),
            pl.BlockSpec((8, tt), lambda i, *_: (0, i)),
            pl.BlockSpec((None, N_EXPERTS, LANES), smem),
            pl.BlockSpec((None, N_EXPERTS, LANES), smem),
        ],
        out_specs=pl.BlockSpec(memory_space=pl.ANY),
        scratch_shapes=[
            pltpu.VMEM((2, CBUF_ROWS, D_MODEL // 2), jnp.uint32),
            pltpu.VMEM((tt, tt), BF16),
            pltpu.VMEM((MOE_BM, D_MODEL // 2), jnp.uint32),
            pltpu.SemaphoreType.DMA((2,)),
            pltpu.SemaphoreType.DMA(()),
        ],
    )
    return pl.pallas_call(
        _dispatch_body,
        grid_spec=grid_spec,
        out_shape=jax.ShapeDtypeStruct((plan["n_slots"], D_MODEL // 2), jnp.uint32),
        compiler_params=_cparams(("arbitrary",)),
        name="moe_dispatch",
    )(plan["start"], plan["loff"], plan["len8"], plan["tail_rows"], x2, g, idxt,
      plan["off_lanes"], plan["cnt0_lanes"])


def _experts_body(be_ref, nu_ref, xs_ref, wg_ref, wu_ref, wd_ref, ys_ref, acc_scr, *, tf):
    del be_ref
    i = pl.program_id(0)
    half = D_MODEL // 2

    @pl.when(i < nu_ref[0])
    def _():
        lo, hi = _unpack_bf16_pairs(xs_ref[...])
        for f in range(wg_ref.shape[1] // tf):
            cols = slice(f * tf, (f + 1) * tf)
            a = (jnp.dot(lo, wg_ref[:half, cols], preferred_element_type=F32)
                 + jnp.dot(hi, wg_ref[half:, cols], preferred_element_type=F32))
            b = (jnp.dot(lo, wu_ref[:half, cols], preferred_element_type=F32)
                 + jnp.dot(hi, wu_ref[half:, cols], preferred_element_type=F32))
            act = (a * jax.nn.sigmoid(a) * b).astype(BF16)
            part = jnp.dot(act, wd_ref[cols, :], preferred_element_type=F32)
            if f == 0:
                acc_scr[...] = part
            else:
                acc_scr[...] += part
        ys_ref[...] = _pack_bf16_pairs(acc_scr[...].astype(BF16))

    @pl.when(i >= nu_ref[0])
    def _():
        ys_ref[...] = jnp.zeros_like(ys_ref)


def _experts(block_e, n_used, xs, wg, wu, wd, tf=512):
    n_slots = xs.shape[0]
    n_blocks = n_slots // MOE_BM
    dff = wg.shape[2]

    def blk(i, nu):
        return jnp.minimum(i, nu[0] - 1)

    def wspec(rows, cols):
        return pl.BlockSpec((None, rows, cols), lambda i, be, nu: (be[blk(i, nu)], 0, 0),
                            pipeline_mode=pl.Buffered(1))

    grid_spec = pltpu.PrefetchScalarGridSpec(
        num_scalar_prefetch=2,
        grid=(n_blocks,),
        in_specs=[
            pl.BlockSpec((MOE_BM, D_MODEL // 2), lambda i, be, nu: (blk(i, nu), 0)),
            wspec(D_MODEL, dff),
            wspec(D_MODEL, dff),
            wspec(dff, D_MODEL),
        ],
        out_specs=pl.BlockSpec((MOE_BM, D_MODEL // 2), lambda i, be, nu: (i, 0)),
        scratch_shapes=[pltpu.VMEM((MOE_BM, D_MODEL), F32)],
    )
    return pl.pallas_call(
        functools.partial(_experts_body, tf=tf),
        grid_spec=grid_spec,
        out_shape=jax.ShapeDtypeStruct((n_slots, D_MODEL // 2), jnp.uint32),
        compiler_params=_cparams(("arbitrary",)),
        name="moe_experts",
    )(block_e, n_used, xs, wg, wu, wd)


def _combine_body(start_ref, loff_ref, len_ref, x_ref, idx_ref, gate_ref, meta_ref, g_ref, ys_ref, o_ref,
                  ybuf, tri_scr, sem):
    i = pl.program_id(0)
    tt = x_ref.shape[0]
    slot = i % 2

    def piece(op, buf):
        def fn(cond, brow, srow, p):
            cp = pltpu.make_async_copy(ys_ref.at[pl.ds(srow, p)], ybuf.at[buf, pl.ds(brow, p)], sem.at[buf])
            pl.when(cond)(getattr(cp, op))
        return fn

    @pl.when(i == 0)
    def _():
        r = lax.broadcasted_iota(jnp.int32, (tt, tt), 0)
        c = lax.broadcasted_iota(jnp.int32, (tt, tt), 1)
        tri_scr[...] = jnp.where(c < r, 1.0, 0.0).astype(BF16)
        ybuf[...] = jnp.zeros_like(ybuf)
        _for_each_run_piece(i, start_ref, loff_ref, len_ref, piece("start", slot))

    @pl.when(i + 1 < pl.num_programs(0))
    def _():
        _for_each_run_piece(i + 1, start_ref, loff_ref, len_ref, piece("start", 1 - slot))

    lane = lax.broadcasted_iota(jnp.int32, (tt, LANES), 1)
    idx = idx_ref[...]
    gt = gate_ref[...]
    cols = lax.broadcasted_iota(jnp.int32, (tt, ybuf.shape[1]), 1).astype(F32)
    sel = None
    for k in range(2):
        oh = jnp.where(lane == idx[:, k:k + 1], 1.0, 0.0)
        rank = jnp.dot(tri_scr[...], oh.astype(BF16), preferred_element_type=F32)
        base = meta_ref[0:1, :] if k == 0 else meta_ref[0:1, :] + meta_ref[1:2, :]
        pos = jnp.sum(oh * (base + rank), axis=1, keepdims=True)
        term = jnp.where(cols == pos, gt[:, k:k + 1], 0.0)
        sel = term if sel is None else sel + term
    sel = sel.astype(BF16)

    _for_each_run_piece(i, start_ref, loff_ref, len_ref, piece("wait", slot))

    lo, hi = _unpack_bf16_pairs(ybuf[slot])
    y = jnp.concatenate([jnp.dot(sel, lo, preferred_element_type=F32),
                         jnp.dot(sel, hi, preferred_element_type=F32)], axis=1)
    o_ref[...] = _rms(x_ref[...] + y, g_ref[...])


def _combine(plan, x2, idx, gates, g_final, ys):
    n = x2.shape[0]
    tt = MOE_TT
    grid_spec = pltpu.PrefetchScalarGridSpec(
        num_scalar_prefetch=3,
        grid=(n // tt,),
        in_specs=[
            pl.BlockSpec((tt, D_MODEL), lambda i, *_: (i, 0)),
            pl.BlockSpec((tt, LANES), lambda i, *_: (i, 0)),
            pl.BlockSpec((tt, LANES), lambda i, *_: (i, 0)),
            pl.BlockSpec((None, 8, LANES), lambda i, *_: (i, 0, 0)),
            pl.BlockSpec((1, D_MODEL), lambda i, *_: (0, 0)),
            pl.BlockSpec(memory_space=pl.ANY),
        ],
        out_specs=pl.BlockSpec((tt, D_MODEL), lambda i, *_: (i, 0)),
        scratch_shapes=[
            pltpu.VMEM((2, CBUF_ROWS, D_MODEL // 2), jnp.uint32),
            pltpu.VMEM((tt, tt), BF16),
            pltpu.SemaphoreType.DMA((2,)),
        ],
    )
    return pl.pallas_call(
        _combine_body,
        grid_spec=grid_spec,
        out_shape=jax.ShapeDtypeStruct((n, D_MODEL), F32),
        compiler_params=_cparams(("arbitrary",)),
        name="moe_combine",
    )(plan["start"], plan["loff"], plan["len8"], x2, idx, gates, plan["meta_rows"], g_final, ys)


def _route_plan(idxt, n):
    nt = n // MOE_TT
    e2 = idxt[:2].reshape(2, nt, MOE_TT)
    oh = (e2[..., None] == jnp.arange(N_EXPERTS, dtype=jnp.int32)).astype(jnp.int32)
    cnt = jnp.sum(oh, axis=2)
    cnt0 = cnt[0]
    len8 = (cnt[0] + cnt[1] + RUN_ALIGN - 1) // RUN_ALIGN * RUN_ALIGN
    loff = jnp.cumsum(len8, axis=1) - len8
    region = jnp.sum(len8, axis=0)
    padded = (region + MOE_BM - 1) // MOE_BM * MOE_BM
    pad_end = jnp.cumsum(padded)
    start = (pad_end - padded)[None, :] + jnp.cumsum(len8, axis=0) - len8
    n_blocks = (2 * n + nt * N_EXPERTS * (RUN_ALIGN - 1) + MOE_BM - 1) // MOE_BM + N_EXPERTS
    starts = jnp.arange(n_blocks, dtype=jnp.int32) * MOE_BM
    block_e = jnp.sum((starts[:, None] >= pad_end[None, :]).astype(jnp.int32), axis=1)
    block_e = jnp.minimum(block_e, N_EXPERTS - 1).astype(jnp.int32)
    n_used = (pad_end[-1] // MOE_BM).astype(jnp.int32).reshape(1)
    tails = jnp.where(padded > 0, pad_end - MOE_BM, -1)
    spare = pad_end[-1] + jnp.arange(n_blocks - (2 * n) // MOE_BM, dtype=pad_end.dtype) * MOE_BM
    spare = jnp.where(spare < n_blocks * MOE_BM, spare, -1)
    lanes = lambda a: jnp.broadcast_to(a.astype(F32)[:, :, None], (nt, N_EXPERTS, LANES))
    meta_rows = jnp.zeros((nt, 8, LANES), F32)
    meta_rows = meta_rows.at[:, 0, :N_EXPERTS].set(loff.astype(F32)).at[:, 1, :N_EXPERTS].set(cnt0.astype(F32))
    flat = lambda a: a.reshape(-1).astype(jnp.int32)
    return dict(start=flat(start), loff=flat(loff), len8=flat(len8), block_e=block_e, n_used=n_used,
                tail_rows=jnp.concatenate([tails, spare]).astype(jnp.int32),
                off_lanes=lanes(loff), cnt0_lanes=lanes(cnt0), meta_rows=meta_rows,
                n_slots=n_blocks * MOE_BM)


def _mixer_layer(x2, batch, seq, rel_bias, norm_g, w_in_bf, ssm, d_skip, w_glu, b_glu,
                 w_attn_br, w_ssm_br, w_out):
    *qkvs, gates, ut = _in_projection(x2, norm_g.reshape(1, D_MODEL), w_in_bf, batch, seq)

    os_, ls_ = [], []
    for g, (window, dilation) in enumerate(ATTN_GROUPS):
        bias = _band_bias(rel_bias[:, g * HEADS:(g + 1) * HEADS], window, dilation)
        o, l = _attention_group(qkvs[g], bias, g)
        os_.append(o)
        ls_.append(l)

    zt = _ssm_scan(ut, d_skip, _ssm_tables(*ssm), batch)

    return _merge(os_, ls_, zt, gates, x2, w_glu.astype(BF16), b_glu.reshape(1, SSM_W).astype(F32),
                  w_attn_br.astype(BF16), w_ssm_br.astype(BF16), w_out.astype(BF16))


def kernel(x, rel_bias, norm1_g, w_in, ssm_lam_re, ssm_lam_im, ssm_log_dt, ssm_b_re, ssm_b_im, ssm_c_re, ssm_c_im, ssm_d, w_glu, b_glu, w_attn_br, w_ssm_br, w_out, norm2_g, ffn_w_gate, ffn_w_up, ffn_w_down, moe_router, moe_w_gate, moe_w_up, moe_w_down, final_norm_g):
    batch, seq, d = x.shape
    assert d == D_MODEL and norm1_g.shape[0] == 2 and seq % (16 * BLK) == 0
    n = batch * seq
    x2 = x.reshape(n, d)

    def mixer(x2, l, w_in_bf):
        ssm = (ssm_lam_re[l], ssm_lam_im[l], ssm_log_dt[l], ssm_b_re[l], ssm_b_im[l],
               ssm_c_re[l], ssm_c_im[l])
        return _mixer_layer(x2, batch, seq, rel_bias, norm1_g[l], w_in_bf, ssm, ssm_d[l], w_glu[l],
                            b_glu[l], w_attn_br[l], w_ssm_br[l], w_out[l])

    x2 = mixer(x2, 0, w_in[0].astype(BF16))
    moe_w = (moe_w_gate[0], moe_w_up[0], moe_w_down[0])
    x2, (w_in1_bf, *moe_bf) = _dense_ffn(
        x2, norm2_g[0].reshape(1, d), ffn_w_gate[0].astype(BF16), ffn_w_up[0].astype(BF16),
        ffn_w_down[0].astype(BF16), to_cast=[w_in[1]] + [w.reshape(-1, w.shape[2]) for w in moe_w])
    moe_bf = [b.reshape(w.shape) for b, w in zip(moe_bf, moe_w)]

    x2 = mixer(x2, 1, w_in1_bf)
    g2 = norm2_g[1].reshape(1, d)
    wr_pad = jnp.zeros((d, LANES), F32).at[:, :N_EXPERTS].set(moe_router[0].astype(F32))
    idx, gates, idxt = _router(x2, g2, wr_pad)
    plan = _route_plan(idxt, n)
    xs = _dispatch(plan, x2, g2, idxt)
    ys = _experts(plan["block_e"], plan["n_used"], xs, *moe_bf)
    out = _combine(plan, x2, idx, gates, final_norm_g.reshape(1, d), ys)
    return out.reshape(batch, seq, d)
```

```python
import functools

import numpy as np
import jax
import jax.numpy as jnp
from jax import lax
from jax.experimental import pallas as pl
from jax.experimental.pallas import tpu as pltpu

F32 = jnp.float32
BF16 = jnp.bfloat16

D_MODEL = 1024
HEAD_DIM = 64
ATTN_GROUPS = ((128, 1), (512, 4), (2048, 16))
N_GROUPS = 3
HEADS = 8
GROUP_W = HEADS * HEAD_DIM
ATTN_W = N_GROUPS * GROUP_W
BLK = 128
REL_BUCKETS = 32
REL_MAX_DIST = 2048
NEG_INF = -1e30
LOG2E = 1.4426950408889634
SSM_CH = 16
SSM_W = D_MODEL // 2
SSM_G = SSM_W // SSM_CH
SSM_P = 64
PROJ_W = 3 * ATTN_W + SSM_W + 2 * D_MODEL
N_EXPERTS = 8
MOE_BM = 512
MOE_TT = 512
RUN_ALIGN = 8
RUN_PIECES = tuple(1 << b for b in range(MOE_TT.bit_length() - 1, RUN_ALIGN.bit_length() - 2, -1))
CBUF_ROWS = -(-(2 * MOE_TT + N_EXPERTS * (RUN_ALIGN - 1)) // 16) * 16
EPS = 1e-6
CHUNK = 128
SCAN_LEVELS = 8

MERGE_SPLIT = 1
LANES = 128
VMEM_LIMIT = 56 * 1024 * 1024


def _cparams(sem):
    return pltpu.CompilerParams(dimension_semantics=sem, vmem_limit_bytes=VMEM_LIMIT)


def _rms(x, g):
    return x * lax.rsqrt(jnp.mean(x * x, axis=-1, keepdims=True) + EPS) * g


def _proj_body(x_ref, g_ref, w_ref, qkv0_ref, qkv1_ref, qkv2_ref, gate_ref, ut_ref, d_scr):
    tm = x_ref.shape[0]
    u_lo = 3 * ATTN_W
    h = _rms(x_ref[...], g_ref[...])
    hb = h.astype(BF16)
    u = jnp.dot(hb, w_ref[:, u_lo:u_lo + SSM_W], preferred_element_type=F32)
    for k in range(ut_ref.shape[0]):
        ut_ref[k] = u[k * CHUNK:(k + 1) * CHUNK, :].T

    nl = D_MODEL // LANES
    for k in range(nl):
        d_scr[k] = h[:, k * LANES:(k + 1) * LANES]

    def by_subsequence(r):
        blocks = [jnp.concatenate([d_scr.at[k][pl.ds(c, tm // r, stride=r), :] for k in range(nl)], axis=1)
                  for c in range(r)]
        return jnp.concatenate(blocks, axis=0).astype(BF16)

    cw = 2 * LANES
    scale = (HEAD_DIM ** -0.5 * LOG2E, None, None)
    for g, (out_ref, (_, r)) in enumerate(zip((qkv0_ref, qkv1_ref, qkv2_ref), ATTN_GROUPS)):
        lhs = hb if r == 1 else by_subsequence(r)
        for which in range(3):
            for lo in range(0, GROUP_W, cw):
                col = which * ATTN_W + g * GROUP_W + lo
                res = jnp.dot(lhs, w_ref[:, col:col + cw], preferred_element_type=F32)
                if scale[which] is not None:
                    res = res * scale[which]
                res = res.astype(BF16)
                for c in range(r):
                    out_ref[which, c, :, lo:lo + cw] = res[c * (tm // r):(c + 1) * (tm // r), :]
    for lo in range(0, 2 * D_MODEL, cw):
        col = u_lo + SSM_W + lo
        gate_ref[:, lo:lo + cw] = jnp.dot(hb, w_ref[:, col:col + cw], preferred_element_type=F32).astype(BF16)


def _in_projection(x2, g, w_bf, batch, seq, tm=512):
    n = x2.shape[0]
    tiles_per_seq = seq // tm
    wcols = w_bf.shape[1]

    def qkv_spec(r):
        return pl.BlockSpec((3, None, r, tm // r, GROUP_W),
                            lambda i: (0, i // tiles_per_seq, 0, i % tiles_per_seq, 0))

    return pl.pallas_call(
        _proj_body,
        grid=(n // tm,),
        in_specs=[
            pl.BlockSpec((tm, D_MODEL), lambda i: (i, 0)),
            pl.BlockSpec((1, D_MODEL), lambda i: (0, 0)),
            pl.BlockSpec((D_MODEL, wcols), lambda i: (0, 0), pipeline_mode=pl.Buffered(1)),
        ],
        out_specs=[qkv_spec(r) for _, r in ATTN_GROUPS] + [
            pl.BlockSpec((tm, 2 * D_MODEL), lambda i: (i, 0)),
            pl.BlockSpec((tm // CHUNK, SSM_W, CHUNK), lambda i: (i, 0, 0)),
        ],
        out_shape=[jax.ShapeDtypeStruct((3, batch, r, seq // r, GROUP_W), BF16) for _, r in ATTN_GROUPS] + [
            jax.ShapeDtypeStruct((n, 2 * D_MODEL), BF16),
            jax.ShapeDtypeStruct((n // CHUNK, SSM_W, CHUNK), F32),
        ],
        scratch_shapes=[pltpu.VMEM((D_MODEL // LANES, tm, LANES), F32)],
        compiler_params=_cparams(("parallel",)),
        name="in_projection",
    )(x2, g, w_bf)


def _t5_bucket(dist):
    max_exact = REL_BUCKETS // 2
    d = np.maximum(dist, 1).astype(np.float64)
    large = max_exact + (
        np.log(d / max_exact) / np.log(REL_MAX_DIST / max_exact) * (REL_BUCKETS - max_exact)
    ).astype(np.int32)
    large = np.minimum(large, REL_BUCKETS - 1)
    return np.where(dist < max_exact, dist, large).astype(np.int32)


def _band_bias(table, window, dilation):
    steps = window // dilation
    qi = np.arange(BLK)[:, None]
    kj = np.arange(2 * BLK)[None, :]
    delta = BLK + qi - kj
    band = (delta >= 0) & (delta <= steps)
    bucket = _t5_bucket(np.clip(delta, 0, steps) * dilation)
    onehot = np.eye(REL_BUCKETS, dtype=np.float32)[bucket]
    bias = jnp.einsum("qkb,bh->hqk", onehot, table.astype(F32), precision=lax.Precision.HIGHEST)
    return jnp.where(band[None], bias * LOG2E, NEG_INF)


def _attn_body(q_ref, kp_ref, kc_ref, vp_ref, vc_ref, bias_ref, o_ref, l_ref, *, nsub):
    lane = lax.broadcasted_iota(jnp.int32, (BLK, LANES), 1)
    lo = lane < HEAD_DIM
    keep_lo = jnp.where(lo, 1.0, 0.0).astype(BF16)
    keep_hi = jnp.where(lo, 0.0, 1.0).astype(BF16)
    col = lax.broadcasted_iota(jnp.int32, (BLK, 2 * BLK), 1)
    first_pen = jnp.where(col < BLK, jnp.where(pl.program_id(2) == 0, NEG_INF, 0.0), 0.0)
    for i in range(nsub):
        rows = slice(i * BLK, (i + 1) * BLK)
        q = q_ref[rows, :]
        if i == 0:
            kw = jnp.concatenate([kp_ref[...], kc_ref[0:BLK, :]], axis=0)
            vw = jnp.concatenate([vp_ref[...], vc_ref[0:BLK, :]], axis=0)
        else:
            kw = kc_ref[(i - 1) * BLK:(i + 1) * BLK, :]
            vw = vc_ref[(i - 1) * BLK:(i + 1) * BLK, :]
        for hp in range(HEADS // 2):
            cols = slice(hp * LANES, (hp + 1) * LANES)
            q2, k2, v2 = q[:, cols], kw[:, cols], vw[:, cols]
            pvs, ms, dens = [], [], []
            for half in range(2):
                qm = q2 * (keep_lo, keep_hi)[half]
                s = lax.dot_general(qm, k2, (((1,), (1,)), ((), ())), preferred_element_type=F32)
                s = s + bias_ref[2 * hp + half]
                if i == 0:
                    s = s + first_pen
                m = jnp.max(s, axis=-1, keepdims=True)
                p = jnp.exp2(s - m)
                dens.append(jnp.broadcast_to(jnp.sum(p, axis=-1, keepdims=True), (BLK, LANES)))
                ms.append(jnp.broadcast_to(m, (BLK, LANES)))
                pvs.append(jnp.dot(p.astype(BF16), v2, preferred_element_type=F32))
            den = jnp.where(lo, dens[0], dens[1])
            o_ref[rows, cols] = (jnp.where(lo, pvs[0], pvs[1]) / den).astype(BF16)
            l_ref[rows, cols] = jnp.where(lo, ms[0], ms[1]) + jnp.log2(den)


def _attention_group(qkv, bias, g):
    _, batch, r, length, _ = qkv.shape
    qb = min(512, length)
    nsub = qb // BLK

    def cur(which):
        return pl.BlockSpec((None, None, None, qb, GROUP_W), lambda b, c, n: (which, b, c, n, 0))

    def prev(which):
        return pl.BlockSpec((None, None, None, BLK, GROUP_W),
                            lambda b, c, n: (which, b, c, jnp.maximum(n * nsub - 1, 0), 0))

    out_spec = pl.BlockSpec((None, None, qb, GROUP_W), lambda b, c, n: (b, c, n, 0))
    return pl.pallas_call(
        functools.partial(_attn_body, nsub=nsub),
        grid=(batch, r, length // qb),
        in_specs=[cur(0), prev(1), cur(1), prev(2), cur(2),
                  pl.BlockSpec((HEADS, BLK, 2 * BLK), lambda b, c, n: (0, 0, 0))],
        out_specs=[out_spec, out_spec],
        out_shape=[
            jax.ShapeDtypeStruct((batch, r, length, GROUP_W), BF16),
            jax.ShapeDtypeStruct((batch, r, length, GROUP_W), F32),
        ],
        compiler_params=_cparams(("parallel", "parallel", "arbitrary")),
        name=f"attention_g{g}",
    )(qkv, qkv, qkv, qkv, qkv, bias)


def _ssm_tables(lam_re, lam_im, log_dt, b_re, b_im, c_re, c_im):
    lam = lax.complex(lam_re.astype(F32), lam_im.astype(F32))
    dt = jnp.exp(log_dt.astype(F32))[:, None]
    lam_dt = lam * dt
    lam_bar = jnp.exp(lam_dt)
    b = lax.complex(b_re.astype(F32), b_im.astype(F32))
    b_bar = ((lam_bar - 1.0) / lam)[..., None] * b
    half = CHUNK // 2
    t = jnp.arange(CHUNK, dtype=F32)

    def power(k):
        return jnp.exp(lam_dt[:, None, :] * jnp.reshape(jnp.asarray(k, F32), (1, -1, 1)))

    p_fwd = power(t - half)
    p_bwd = power(half - t)

    def in_pair(pw):
        return jnp.stack([jnp.concatenate([pw.real, pw.imag], axis=-1),
                          jnp.concatenate([-pw.imag, pw.real], axis=-1)], axis=1)

    def out_pair(pw):
        return jnp.stack([jnp.concatenate([pw.real, -pw.imag], axis=-1),
                          jnp.concatenate([-pw.imag, -pw.real], axis=-1)], axis=1)

    powers = jnp.stack([
        in_pair(p_bwd),
        in_pair(p_bwd * power(CHUNK - 1.0 - half)),
        out_pair(p_fwd),
        out_pair(p_fwd * power(half + 1.0)),
    ], axis=1)
    dup = lambda a: jnp.concatenate([a, a], axis=-1)
    b_cp = jnp.transpose(b_bar, (0, 2, 1))
    coefs = jnp.stack([dup(b_cp.real), dup(b_cp.imag), dup(c_re.astype(F32)), dup(c_im.astype(F32))],
                      axis=1)
    lc = power([float(CHUNK * 2 ** k) for k in range(SCAN_LEVELS)])
    l1 = jnp.concatenate([lc.real, lc.real], axis=-1)
    l2 = jnp.concatenate([-lc.imag, lc.imag], axis=-1)
    return powers, coefs, l1, l2


def _gelu_tanh(y):
    return y * jax.nn.sigmoid(1.5957691216057308 * (y + 0.044715 * (y * y * y)))


def _ssm_body(d_ref, u_ref, pw_ref, cf_ref, l1_ref, l2_ref, z_ref, mask_scr, z_scr, a_scr, w_scr, *,
              ncb):
    g = pl.program_id(0)
    nc = u_ref.shape[0]
    width = SSM_CH * CHUNK
    cb = mask_scr.shape[1]

    @pl.when(g == 0)
    def _():
        s_idx = lax.broadcasted_iota(jnp.int32, (width, cb), 0) & (CHUNK - 1)
        t_idx = lax.broadcasted_iota(jnp.int32, (width, cb), 1) & (CHUNK - 1)
        mask_scr[...] = jnp.where(t_idx >= s_idx, -1, 0).astype(jnp.int32)

    def expand(kind, c1, c2, ch):
        return (cf_ref[c1, ch:ch + 1, :] * pw_ref[kind, 0] + cf_ref[c2, ch:ch + 1, :] * pw_ref[kind, 1])

    for ch in range(SSM_CH):
        rows = slice(ch * CHUNK, (ch + 1) * CHUNK)
        a_scr[rows, :] = expand(0, 0, 1, ch).astype(BF16)
        w_scr[rows, :] = expand(1, 0, 1, ch).astype(BF16)

    u2 = u_ref.reshape(nc * SSM_CH, CHUNK)
    us =[u2[pl.ds(c, nc, stride=SSM_CH), :] for c in range(SSM_CH)]
    x = jnp.concatenate(us, axis=1).astype(BF16)

    acc = jnp.dot(x, w_scr[...], preferred_element_type=F32)
    rmod = lax.broadcasted_iota(jnp.int32, (nc, 2 * SSM_P), 0) & (ncb - 1)
    for k in range(ncb.bit_length() - 1):
        d = 1 << k
        sh = jnp.where(rmod >= d, pltpu.roll(acc, d, 0), 0.0)
        acc = acc + sh * l1_ref[k:k + 1, :] + pltpu.roll(sh, SSM_P, 1) * l2_ref[k:k + 1, :]
    x_in = jnp.where(rmod >= 1, pltpu.roll(acc, 1, 0), 0.0).astype(BF16)

    per = cb // CHUNK
    for k in range(width // cb):
        chans = range(k * per, (k + 1) * per)
        d_rhs = jnp.concatenate([expand(2, 2, 3, ch).T.astype(BF16) for ch in chans], axis=1)
        v_rhs = jnp.concatenate([expand(3, 2, 3, ch).T.astype(BF16) for ch in chans], axis=1)
        mk = jnp.dot(a_scr[...], d_rhs, preferred_element_type=F32)
        kept = lax.bitcast_convert_type(mk, jnp.int32) & mask_scr[...]
        m_k = lax.bitcast_convert_type(kept, F32).astype(BF16)
        y = (jnp.dot(x, m_k, preferred_element_type=F32)
             + jnp.dot(x_in, v_rhs, preferred_element_type=F32))
        for j, c in enumerate(chans):
            yc = y[:, j * CHUNK:(j + 1) * CHUNK] + d_ref[g * SSM_CH + c] * us[c]
            z_scr[pl.ds(c, nc, stride=SSM_CH), :] = _gelu_tanh(yc)
    z_ref[...] = z_scr[...].reshape(nc, SSM_CH, CHUNK)


def _ssm_scan(u3, d_skip, tables, nbatch):
    powers, coefs, l1, l2 = tables
    nc = u3.shape[0]
    ncb = nc // nbatch
    assert ncb & (ncb - 1) == 0 and ncb <= 2 ** SCAN_LEVELS
    width = SSM_CH * CHUNK
    grid_spec = pltpu.PrefetchScalarGridSpec(
        num_scalar_prefetch=1,
        grid=(SSM_G,),
        in_specs=[
            pl.BlockSpec((nc, SSM_CH, CHUNK), lambda g, d: (0, g, 0)),
            pl.BlockSpec((None,) + powers.shape[1:], lambda g, d: (g, 0, 0, 0, 0)),
            pl.BlockSpec((None,) + coefs.shape[1:], lambda g, d: (g, 0, 0, 0)),
            pl.BlockSpec((None, SCAN_LEVELS, 2 * SSM_P), lambda g, d: (g, 0, 0)),
            pl.BlockSpec((None, SCAN_LEVELS, 2 * SSM_P), lambda g, d: (g, 0, 0)),
        ],
        out_specs=pl.BlockSpec((nc, SSM_CH, CHUNK), lambda g, d: (0, g, 0)),
        scratch_shapes=[pltpu.VMEM((width, 4 * CHUNK), jnp.int32),
                        pltpu.VMEM((nc * SSM_CH, CHUNK), F32),
                        pltpu.VMEM((width, 2 * SSM_P), BF16), pltpu.VMEM((width, 2 * SSM_P), BF16)],
    )
    return pl.pallas_call(
        functools.partial(_ssm_body, ncb=ncb),
        grid_spec=grid_spec,
        out_shape=jax.ShapeDtypeStruct((nc, SSM_W, CHUNK), F32),
        compiler_params=_cparams(("arbitrary",)),
        name="ssm_scan",
    )(d_skip.astype(F32), u3, powers, coefs, l1, l2)


def _merge_body(o0, o1, o2, l0, l1, l2, zt_ref, ga_ref, gs_ref, x_ref,
                wglu_ref, bglu_ref, wab_ref, wsb_ref, wout_ref, out_ref, tok_scr):
    def token_major(ref, slot):
        r, rows, _ = ref.shape
        if r == 1:
            return lambda rs: ref[0, rs, :].astype(F32)
        nl = GROUP_W // LANES
        scrs = [tok_scr.at[slot * nl + k] for k in range(nl)]
        for c in range(r):
            sub = ref[c].astype(F32)
            for k in range(nl):
                scrs[k][pl.ds(c, rows, stride=r), :] = sub[:, k * LANES:(k + 1) * LANES]
        return lambda rs: jnp.concatenate([s[rs, :] for s in scrs], axis=1)

    lse = [token_major(l0, 0), token_major(l1, 0), token_major(l2, 1)]
    val = [token_major(o0, 0), token_major(o1, 2), token_major(o2, 3)]

    tm = x_ref.shape[0]
    rows = tm // MERGE_SPLIT
    for h in range(MERGE_SPLIT):
        rs = slice(h * rows, (h + 1) * rows)
        a0, a1, a2 = (f(rs) for f in lse)
        v0, v1, v2 = (f(rs) for f in val)
        mx = jnp.maximum(jnp.maximum(a0, a1), a2)
        e0, e1, e2 = jnp.exp2(a0 - mx), jnp.exp2(a1 - mx), jnp.exp2(a2 - mx)
        mix = (e0 * v0 + e1 * v1 + e2 * v2) / (e0 + e1 + e2)
        y_attn = jnp.dot(mix.astype(BF16), wab_ref[...], preferred_element_type=F32)

        chunks = range(h * rows // CHUNK, (h + 1) * rows // CHUNK)
        z = jnp.concatenate([zt_ref[k].T for k in chunks], axis=0).astype(BF16)
        gl = jnp.dot(z, wglu_ref[...], preferred_element_type=F32) + bglu_ref[...]
        sg = z.astype(F32) * jax.nn.sigmoid(gl)
        y_ssm = jnp.dot(sg.astype(BF16), wsb_ref[...], preferred_element_type=F32)

        merged = (jax.nn.sigmoid(ga_ref[rs, :].astype(F32)) * y_attn
                  + jax.nn.sigmoid(gs_ref[rs, :].astype(F32)) * y_ssm)
        out_ref[rs, :] = x_ref[rs, :] + jnp.dot(merged.astype(BF16), wout_ref[...],
                                                 preferred_element_type=F32)


def _merge(os_, ls_, zt, gates, x2, wglu, bglu, wab, wsb, wout, tm=512):
    n = x2.shape[0]
    tiles_per_seq = os_[0].shape[2] // tm
    row = lambda i: (i, 0)
    const = lambda i: (0, 0)

    def group_spec(a):
        r = a.shape[1]
        return pl.BlockSpec((None, r, tm // r, GROUP_W),
                            lambda i: (i // tiles_per_seq, 0, i % tiles_per_seq, 0))

    in_specs = (
        [group_spec(a) for a in os_] + [group_spec(a) for a in ls_]
        + [
            pl.BlockSpec((tm // CHUNK, SSM_W, CHUNK), lambda i: (i, 0, 0)),
            pl.BlockSpec((tm, D_MODEL), lambda i: (i, 0)),
            pl.BlockSpec((tm, D_MODEL), lambda i: (i, 1)),
            pl.BlockSpec((tm, D_MODEL), row),
            pl.BlockSpec((SSM_W, SSM_W), const),
            pl.BlockSpec((1, SSM_W), const),
            pl.BlockSpec((GROUP_W, D_MODEL), const),
            pl.BlockSpec((SSM_W, D_MODEL), const),
            pl.BlockSpec((D_MODEL, D_MODEL), const),
        ]
    )
    return pl.pallas_call(
        _merge_body,
        grid=(n // tm,),
        in_specs=in_specs,
        out_specs=pl.BlockSpec((tm, D_MODEL), row),
        out_shape=jax.ShapeDtypeStruct((n, D_MODEL), F32),
        scratch_shapes=[pltpu.VMEM((4 * GROUP_W // LANES, tm, LANES), F32)],
        compiler_params=_cparams(("parallel",)),
        name="merge",
    )(*os_, *ls_, zt, gates, gates, x2, wglu, bglu, wab, wsb, wout)


def _ffn_body(x_ref, g_ref, wg_ref, wu_ref, wd_ref, *rest, tf):
    ncast = (len(rest) - 1) // 2
    o_ref = rest[ncast]
    for src, dst in zip(rest[:ncast], rest[ncast + 1:]):
        dst[...] = src[...].astype(BF16)
    h = _rms(x_ref[...], g_ref[...]).astype(BF16)
    for f in range(wg_ref.shape[1] // tf):
        cols = slice(f * tf, (f + 1) * tf)
        a = jnp.dot(h, wg_ref[:, cols], preferred_element_type=F32)
        b = jnp.dot(h, wu_ref[:, cols], preferred_element_type=F32)
        act = (a * jax.nn.sigmoid(a) * b).astype(BF16)
        part = jnp.dot(act, wd_ref[cols, :], preferred_element_type=F32)
        if f == 0:
            o_ref[...] = x_ref[...] + part
        else:
            o_ref[...] += part


def _dense_ffn(x2, g, wg, wu, wd, to_cast=(), tm=512, tf=256):
    n = x2.shape[0]
    dff = wg.shape[1]
    steps = n // tm
    assert dff % tf == 0 and all(a.shape[0] % (16 * steps) == 0 for a in to_cast)
    resident = lambda shape: pl.BlockSpec(shape, lambda i: (0, 0), pipeline_mode=pl.Buffered(1))
    slabs = [pl.BlockSpec((a.shape[0] // steps, a.shape[1]), lambda i: (i, 0)) for a in to_cast]
    out, *casts = pl.pallas_call(
        functools.partial(_ffn_body, tf=tf),
        grid=(steps,),
        in_specs=[
            pl.BlockSpec((tm, D_MODEL), lambda i: (i, 0)),
            pl.BlockSpec((1, D_MODEL), lambda i: (0, 0)),
            resident((D_MODEL, dff)),
            resident((D_MODEL, dff)),
            resident((dff, D_MODEL)),
        ] + slabs,
        out_specs=[pl.BlockSpec((tm, D_MODEL), lambda i: (i, 0))] + slabs,
        out_shape=[jax.ShapeDtypeStruct((n, D_MODEL), F32)]
        + [jax.ShapeDtypeStruct(a.shape, BF16) for a in to_cast],
        compiler_params=_cparams(("parallel",)),
        name="dense_ffn",
    )(x2, g, wg, wu, wd, *to_cast)
    return out, casts


def _route_top2(h, w, idx_ref, gate_ref, idxt_ref):
    h_hi, w_hi = h.astype(BF16), w.astype(BF16)
    h_lo = (h - h_hi.astype(F32)).astype(BF16)
    w_lo = (w - w_hi.astype(F32)).astype(BF16)
    logits = (jnp.dot(h_hi, w_hi, preferred_element_type=F32)
              + (jnp.dot(h_hi, w_lo, preferred_element_type=F32)
                 + jnp.dot(h_lo, w_hi, preferred_element_type=F32)))
    lane = lax.broadcasted_iota(jnp.int32, logits.shape, 1)
    lane_f = lane.astype(F32)
    logits = jnp.where(lane < N_EXPERTS, logits, -jnp.inf)
    v1 = jnp.max(logits, axis=-1, keepdims=True)
    i1 = jnp.min(jnp.where(logits == v1, lane_f, float(LANES)), axis=-1, keepdims=True)
    rest = jnp.where(lane_f == i1, -jnp.inf, logits)
    v2 = jnp.max(rest, axis=-1, keepdims=True)
    i2 = jnp.min(jnp.where(rest == v2, lane_f, float(LANES)), axis=-1, keepdims=True)
    e = jnp.exp(v2 - v1)
    g1 = 1.0 / (1.0 + e)
    g2 = e / (1.0 + e)
    idx_f = jnp.where(lane == 0, i1, jnp.where(lane == 1, i2, 0.0))
    idx_ref[...] = idx_f.astype(jnp.int32)
    gate_ref[...] = jnp.where(lane == 0, g1, jnp.where(lane == 1, g2, 0.0))
    idxt_ref[...] = idx_f.T[:idxt_ref.shape[0], :].astype(jnp.int32)


def _router_body(x_ref, g_ref, wr_ref, idx_ref, gate_ref, idxt_ref):
    _route_top2(_rms(x_ref[...], g_ref[...]), wr_ref[...], idx_ref, gate_ref, idxt_ref)


def _router(x2, g, wr_pad, tm=1024):
    n = x2.shape[0]
    return pl.pallas_call(
        _router_body,
        grid=(n // tm,),
        in_specs=[
            pl.BlockSpec((tm, D_MODEL), lambda i: (i, 0)),
            pl.BlockSpec((1, D_MODEL), lambda i: (0, 0)),
            pl.BlockSpec((D_MODEL, LANES), lambda i: (0, 0)),
        ],
        out_specs=[
            pl.BlockSpec((tm, LANES), lambda i: (i, 0)),
            pl.BlockSpec((tm, LANES), lambda i: (i, 0)),
            pl.BlockSpec((8, tm), lambda i: (0, i)),
        ],
        out_shape=[
            jax.ShapeDtypeStruct((n, LANES), jnp.int32),
            jax.ShapeDtypeStruct((n, LANES), F32),
            jax.ShapeDtypeStruct((8, n), jnp.int32),
        ],
        compiler_params=_cparams(("parallel",)),
        name="router",
    )(x2, g, wr_pad)


def _pack_bf16_pairs(hb):
    half = hb.shape[1] // 2
    lo = lax.bitcast_convert_type(hb[:, :half].astype(F32), jnp.uint32)
    hi = lax.bitcast_convert_type(hb[:, half:].astype(F32), jnp.uint32)
    return (hi & jnp.uint32(0xFFFF0000)) | (lo >> 16)


def _unpack_bf16_pairs(xu):
    lo = lax.bitcast_convert_type(xu << 16, F32).astype(BF16)
    hi = lax.bitcast_convert_type(xu & jnp.uint32(0xFFFF0000), F32).astype(BF16)
    return lo, hi


def _for_each_run_piece(i, start_ref, loff_ref, len_ref, fn):
    for e in range(N_EXPERTS):
        j = i * N_EXPERTS + e
        length, boff, soff = len_ref[j], loff_ref[j], start_ref[j]
        done = 0
        for p in RUN_PIECES:
            cond = (length & p) != 0
            fn(cond, pl.multiple_of(boff + done, RUN_ALIGN), pl.multiple_of(soff + done, RUN_ALIGN), p)
            done = done + jnp.where(cond, p, 0)


def _dispatch_body(start_ref, loff_ref, len_ref, tail_ref, x_ref, g_ref, idxt_ref, off_ref, cnt0_ref,
                   xs_ref, cbuf, tri_scr, zero_scr, sem, zsem):
    i = pl.program_id(0)
    tt = x_ref.shape[0]

    @pl.when(i == 0)
    def _():
        r = lax.broadcasted_iota(jnp.int32, (tt, tt), 0)
        c = lax.broadcasted_iota(jnp.int32, (tt, tt), 1)
        tri_scr[...] = jnp.where(r < c, 1.0, 0.0).astype(BF16)
        zero_scr[...] = jnp.zeros_like(zero_scr)

        def fill(e):
            row = pl.multiple_of(jnp.maximum(tail_ref[e], 0), MOE_BM)
            return pltpu.make_async_copy(zero_scr, xs_ref.at[pl.ds(row, MOE_BM)], zsem)

        for e in range(tail_ref.shape[0]):
            pl.when(tail_ref[e] >= 0)(lambda e=e: fill(e).start())
        for e in range(tail_ref.shape[0]):
            pl.when(tail_ref[e] >= 0)(lambda e=e: fill(e).wait())

    hb = _rms(x_ref[...], g_ref[...]).astype(BF16)

    sub = lax.broadcasted_iota(jnp.int32, (N_EXPERTS, tt), 0)
    pos = []
    for k in range(2):
        oh = jnp.where(sub == idxt_ref[k:k + 1, :], 1.0, 0.0)
        rank = jnp.dot(oh.astype(BF16), tri_scr[...], preferred_element_type=F32)
        base = off_ref[:, :1] if k == 0 else off_ref[:, :1] + cnt0_ref[:, :1]
        pos.append(jnp.sum(oh * (base + rank), axis=0, keepdims=True))

    rows = lax.broadcasted_iota(jnp.int32, (cbuf.shape[1], tt), 0).astype(F32)
    perm = (jnp.where(rows == pos[0], 1.0, 0.0) + jnp.where(rows == pos[1], 1.0, 0.0)).astype(BF16)
    slot = i % 2
    cbuf[slot] = _pack_bf16_pairs(jnp.dot(perm, hb, preferred_element_type=F32).astype(BF16))

    def piece(op, buf):
        def fn(cond, brow, srow, p):
            cp = pltpu.make_async_copy(cbuf.at[buf, pl.ds(brow, p)], xs_ref.at[pl.ds(srow, p)], sem.at[buf])
            pl.when(cond)(getattr(cp, op))
        return fn

    _for_each_run_piece(i, start_ref, loff_ref, len_ref, piece("start", slot))

    @pl.when(i > 0)
    def _():
        _for_each_run_piece(i - 1, start_ref, loff_ref, len_ref, piece("wait", 1 - slot))

    @pl.when(i == pl.num_programs(0) - 1)
    def _():
        _for_each_run_piece(i, start_ref, loff_ref, len_ref, piece("wait", slot))


def _dispatch(plan, x2, g, idxt):
    n = x2.shape[0]
    tt = MOE_TT
    smem = lambda i, *_: (i, 0, 0)
    grid_spec = pltpu.PrefetchScalarGridSpec(
        num_scalar_prefetch=4,
        grid=(n // tt,),
        in_specs=[
            pl.BlockSpec((tt, D_MODEL), lambda i, *_: (i, 0)),
            pl.BlockSpec((1, D_MODEL), lambda i, *_: (0, 0)),
            pl.BlockSpec((8, tt), lambda i, *_: (0, i)),
            pl.BlockSpec((None, N_EXPERTS, LANES), smem),
            pl.BlockSpec((None, N_EXPERTS, LANES), smem),
        ],
        out_specs=pl.BlockSpec(memory_space=pl.ANY),
        scratch_shapes=[
            pltpu.VMEM((2, CBUF_ROWS, D_MODEL // 2), jnp.uint32),
            pltpu.VMEM((tt, tt), BF16),
            pltpu.VMEM((MOE_BM, D_MODEL // 2), jnp.uint32),
            pltpu.SemaphoreType.DMA((2,)),
            pltpu.SemaphoreType.DMA(()),
        ],
    )
    return pl.pallas_call(
        _dispatch_body,
        grid_spec=grid_spec,
        out_shape=jax.ShapeDtypeStruct((plan["n_slots"], D_MODEL // 2), jnp.uint32),
        compiler_params=_cparams(("arbitrary",)),
        name="moe_dispatch",
    )(plan["start"], plan["loff"], plan["len8"], plan["tail_rows"], x2, g, idxt,
      plan["off_lanes"], plan["cnt0_lanes"])


def _experts_body(be_ref, nu_ref, xs_ref, wg_ref, wu_ref, wd_ref, ys_ref, acc_scr, *, tf):
    del be_ref
    i = pl.program_id(0)
    half = D_MODEL // 2

    @pl.when(i < nu_ref[0])
    def _():
        lo, hi = _unpack_bf16_pairs(xs_ref[...])
        for f in range(wg_ref.shape[1] // tf):
            cols = slice(f * tf, (f + 1) * tf)
            a = (jnp.dot(lo, wg_ref[:half, cols], preferred_element_type=F32)
                 + jnp.dot(hi, wg_ref[half:, cols], preferred_element_type=F32))
            b = (jnp.dot(lo, wu_ref[:half, cols], preferred_element_type=F32)
                 + jnp.dot(hi, wu_ref[half:, cols], preferred_element_type=F32))
            act = (a * jax.nn.sigmoid(a) * b).astype(BF16)
            part = jnp.dot(act, wd_ref[cols, :], preferred_element_type=F32)
            if f == 0:
                acc_scr[...] = part
            else:
                acc_scr[...] += part
        ys_ref[...] = _pack_bf16_pairs(acc_scr[...].astype(BF16))

    @pl.when(i >= nu_ref[0])
    def _():
        ys_ref[...] = jnp.zeros_like(ys_ref)


def _experts(block_e, n_used, xs, wg, wu, wd, tf=512):
    n_slots = xs.shape[0]
    n_blocks = n_slots // MOE_BM
    dff = wg.shape[2]

    def blk(i, nu):
        return jnp.minimum(i, nu[0] - 1)

    def wspec(rows, cols):
        return pl.BlockSpec((None, rows, cols), lambda i, be, nu: (be[blk(i, nu)], 0, 0),
                            pipeline_mode=pl.Buffered(1))

    grid_spec = pltpu.PrefetchScalarGridSpec(
        num_scalar_prefetch=2,
        grid=(n_blocks,),
        in_specs=[
            pl.BlockSpec((MOE_BM, D_MODEL // 2), lambda i, be, nu: (blk(i, nu), 0)),
            wspec(D_MODEL, dff),
            wspec(D_MODEL, dff),
            wspec(dff, D_MODEL),
        ],
        out_specs=pl.BlockSpec((MOE_BM, D_MODEL // 2), lambda i, be, nu: (i, 0)),
        scratch_shapes=[pltpu.VMEM((MOE_BM, D_MODEL), F32)],
    )
    return pl.pallas_call(
        functools.partial(_experts_body, tf=tf),
        grid_spec=grid_spec,
        out_shape=jax.ShapeDtypeStruct((n_slots, D_MODEL // 2), jnp.uint32),
        compiler_params=_cparams(("arbitrary",)),
        name="moe_experts",
    )(block_e, n_used, xs, wg, wu, wd)


def _combine_body(start_ref, loff_ref, len_ref, x_ref, idx_ref, gate_ref, meta_ref, g_ref, ys_ref, o_ref,
                  ybuf, tri_scr, sem):
    i = pl.program_id(0)
    tt = x_ref.shape[0]
    slot = i % 2

    def piece(op, buf):
        def fn(cond, brow, srow, p):
            cp = pltpu.make_async_copy(ys_ref.at[pl.ds(srow, p)], ybuf.at[buf, pl.ds(brow, p)], sem.at[buf])
            pl.when(cond)(getattr(cp, op))
        return fn

    @pl.when(i == 0)
    def _():
        r = lax.broadcasted_iota(jnp.int32, (tt, tt), 0)
        c = lax.broadcasted_iota(jnp.int32, (tt, tt), 1)
        tri_scr[...] = jnp.where(c < r, 1.0, 0.0).astype(BF16)
        ybuf[...] = jnp.zeros_like(ybuf)
        _for_each_run_piece(i, start_ref, loff_ref, len_ref, piece("start", slot))

    @pl.when(i + 1 < pl.num_programs(0))
    def _():
        _for_each_run_piece(i + 1, start_ref, loff_ref, len_ref, piece("start", 1 - slot))

    lane = lax.broadcasted_iota(jnp.int32, (tt, LANES), 1)
    idx = idx_ref[...]
    gt = gate_ref[...]
    cols = lax.broadcasted_iota(jnp.int32, (tt, ybuf.shape[1]), 1).astype(F32)
    sel = None
    for k in range(2):
        oh = jnp.where(lane == idx[:, k:k + 1], 1.0, 0.0)
        rank = jnp.dot(tri_scr[...], oh.astype(BF16), preferred_element_type=F32)
        base = meta_ref[0:1, :] if k == 0 else meta_ref[0:1, :] + meta_ref[1:2, :]
        pos = jnp.sum(oh * (base + rank), axis=1, keepdims=True)
        term = jnp.where(cols == pos, gt[:, k:k + 1], 0.0)
        sel = term if sel is None else sel + term
    sel = sel.astype(BF16)

    _for_each_run_piece(i, start_ref, loff_ref, len_ref, piece("wait", slot))

    lo, hi = _unpack_bf16_pairs(ybuf[slot])
    y = jnp.concatenate([jnp.dot(sel, lo, preferred_element_type=F32),
                         jnp.dot(sel, hi, preferred_element_type=F32)], axis=1)
    o_ref[...] = _rms(x_ref[...] + y, g_ref[...])


def _combine(plan, x2, idx, gates, g_final, ys):
    n = x2.shape[0]
    tt = MOE_TT
    grid_spec = pltpu.PrefetchScalarGridSpec(
        num_scalar_prefetch=3,
        grid=(n // tt,),
        in_specs=[
            pl.BlockSpec((tt, D_MODEL), lambda i, *_: (i, 0)),
            pl.BlockSpec((tt, LANES), lambda i, *_: (i, 0)),
            pl.BlockSpec((tt, LANES), lambda i, *_: (i, 0)),
            pl.BlockSpec((None, 8, LANES), lambda i, *_: (i, 0, 0)),
            pl.BlockSpec((1, D_MODEL), lambda i, *_: (0, 0)),
            pl.BlockSpec(memory_space=pl.ANY),
        ],
        out_specs=pl.BlockSpec((tt, D_MODEL), lambda i, *_: (i, 0)),
        scratch_shapes=[
            pltpu.VMEM((2, CBUF_ROWS, D_MODEL // 2), jnp.uint32),
            pltpu.VMEM((tt, tt), BF16),
            pltpu.SemaphoreType.DMA((2,)),
        ],
    )
    return pl.pallas_call(
        _combine_body,
        grid_spec=grid_spec,
        out_shape=jax.ShapeDtypeStruct((n, D_MODEL), F32),
        compiler_params=_cparams(("arbitrary",)),
        name="moe_combine",
    )(plan["start"], plan["loff"], plan["len8"], x2, idx, gates, plan["meta_rows"], g_final, ys)


def _route_plan(idxt, n):
    nt = n // MOE_TT
    e2 = idxt[:2].reshape(2, nt, MOE_TT)
    oh = (e2[..., None] == jnp.arange(N_EXPERTS, dtype=jnp.int32)).astype(jnp.int32)
    cnt = jnp.sum(oh, axis=2)
    cnt0 = cnt[0]
    len8 = (cnt[0] + cnt[1] + RUN_ALIGN - 1) // RUN_ALIGN * RUN_ALIGN
    loff = jnp.cumsum(len8, axis=1) - len8
    region = jnp.sum(len8, axis=0)
    padded = (region + MOE_BM - 1) // MOE_BM * MOE_BM
    pad_end = jnp.cumsum(padded)
    start = (pad_end - padded)[None, :] + jnp.cumsum(len8, axis=0) - len8
    n_blocks = (2 * n + nt * N_EXPERTS * (RUN_ALIGN - 1) + MOE_BM - 1) // MOE_BM + N_EXPERTS
    starts = jnp.arange(n_blocks, dtype=jnp.int32) * MOE_BM
    block_e = jnp.sum((starts[:, None] >= pad_end[None, :]).astype(jnp.int32), axis=1)
    block_e = jnp.minimum(block_e, N_EXPERTS - 1).astype(jnp.int32)
    n_used = (pad_end[-1] // MOE_BM).astype(jnp.int32).reshape(1)
    tails = jnp.where(padded > 0, pad_end - MOE_BM, -1)
    spare = pad_end[-1] + jnp.arange(n_blocks - (2 * n) // MOE_BM, dtype=pad_end.dtype) * MOE_BM
    spare = jnp.where(spare < n_blocks * MOE_BM, spare, -1)
    lanes = lambda a: jnp.broadcast_to(a.astype(F32)[:, :, None], (nt, N_EXPERTS, LANES))
    meta_rows = jnp.zeros((nt, 8, LANES), F32)
    meta_rows = meta_rows.at[:, 0, :N_EXPERTS].set(loff.astype(F32)).at[:, 1, :N_EXPERTS].set(cnt0.astype(F32))
    flat = lambda a: a.reshape(-1).astype(jnp.int32)
    return dict(start=flat(start), loff=flat(loff), len8=flat(len8), block_e=block_e, n_used=n_used,
                tail_rows=jnp.concatenate([tails, spare]).astype(jnp.int32),
                off_lanes=lanes(loff), cnt0_lanes=lanes(cnt0), meta_rows=meta_rows,
                n_slots=n_blocks * MOE_BM)


def _mixer_layer(x2, batch, seq, rel_bias, norm_g, w_in_bf, ssm, d_skip, w_glu, b_glu,
                 w_attn_br, w_ssm_br, w_out):
    *qkvs, gates, ut = _in_projection(x2, norm_g.reshape(1, D_MODEL), w_in_bf, batch, seq)

    os_, ls_ = [], []
    for g, (window, dilation) in enumerate(ATTN_GROUPS):
        bias = _band_bias(rel_bias[:, g * HEADS:(g + 1) * HEADS], window, dilation)
        o, l = _attention_group(qkvs[g], bias, g)
        os_.append(o)
        ls_.append(l)

    zt = _ssm_scan(ut, d_skip, _ssm_tables(*ssm), batch)

    return _merge(os_, ls_, zt, gates, x2, w_glu.astype(BF16), b_glu.reshape(1, SSM_W).astype(F32),
                  w_attn_br.astype(BF16), w_ssm_br.astype(BF16), w_out.astype(BF16))


def kernel(x, rel_bias, norm1_g, w_in, ssm_lam_re, ssm_lam_im, ssm_log_dt, ssm_b_re, ssm_b_im, ssm_c_re, ssm_c_im, ssm_d, w_glu, b_glu, w_attn_br, w_ssm_br, w_out, norm2_g, ffn_w_gate, ffn_w_up, ffn_w_down, moe_router, moe_w_gate, moe_w_up, moe_w_down, final_norm_g):
    batch, seq, d = x.shape
    assert d == D_MODEL and norm1_g.shape[0] == 2 and seq % (16 * BLK) == 0
    n = batch * seq
    x2 = x.reshape(n, d)

    def mixer(x2, l, w_in_bf):
        ssm = (ssm_lam_re[l], ssm_lam_im[l], ssm_log_dt[l], ssm_b_re[l], ssm_b_im[l],
               ssm_c_re[l], ssm_c_im[l])
        return _mixer_layer(x2, batch, seq, rel_bias, norm1_g[l], w_in_bf, ssm, ssm_d[l], w_glu[l],
                            b_glu[l], w_attn_br[l], w_ssm_br[l], w_out[l])

    x2 = mixer(x2, 0, w_in[0].astype(BF16))
    moe_w = (moe_w_gate[0], moe_w_up[0], moe_w_down[0])
    x2, (w_in1_bf, *moe_bf) = _dense_ffn(
        x2, norm2_g[0].reshape(1, d), ffn_w_gate[0].astype(BF16), ffn_w_up[0].astype(BF16),
        ffn_w_down[0].astype(BF16), to_cast=[w_in[1]] + [w.reshape(-1, w.shape[2]) for w in moe_w])
    moe_bf = [b.reshape(w.shape) for b, w in zip(moe_bf, moe_w)]

    x2 = mixer(x2, 1, w_in1_bf)
    g2 = norm2_g[1].reshape(1, d)
    wr_pad = jnp.zeros((d, LANES), F32).at[:, :N_EXPERTS].set(moe_router[0].astype(F32))
    idx, gates, idxt = _router(x2, g2, wr_pad)
    plan = _route_plan(idxt, n)
    xs = _dispatch(plan, x2, g2, idxt)
    ys = _experts(plan["block_e"], plan["n_used"], xs, *moe_bf)
    out = _combine(plan, x2, idx, gates, final_norm_g.reshape(1, d), ys)
    return out.reshape(batch, seq, d)
```

```python
import functools

import numpy as np
import jax
import jax.numpy as jnp
from jax import lax
from jax.experimental import pallas as pl
from jax.experimental.pallas import tpu as pltpu

F32 = jnp.float32
BF16 = jnp.bfloat16

D_MODEL = 1024
HEAD_DIM = 64
ATTN_GROUPS = ((128, 1), (512, 4), (2048, 16))
N_GROUPS = 3
HEADS = 8
GROUP_W = HEADS * HEAD_DIM
ATTN_W = N_GROUPS * GROUP_W
BLK = 128
REL_BUCKETS = 32
REL_MAX_DIST = 2048
NEG_INF = -1e30
LOG2E = 1.4426950408889634
SSM_CH = 16
SSM_W = D_MODEL // 2
SSM_G = SSM_W // SSM_CH
SSM_P = 64
PROJ_W = 3 * ATTN_W + SSM_W + 2 * D_MODEL
N_EXPERTS = 8
MOE_BM = 512
MOE_TT = 512
RUN_ALIGN = 8
RUN_PIECES = tuple(1 << b for b in range(MOE_TT.bit_length() - 1, RUN_ALIGN.bit_length() - 2, -1))
CBUF_ROWS = -(-(2 * MOE_TT + N_EXPERTS * (RUN_ALIGN - 1)) // 16) * 16
FILL_ROWS = CBUF_ROWS - 2 * MOE_TT
FILL_PIECES = tuple(1 << b for b in range(FILL_ROWS.bit_length() - 1, RUN_ALIGN.bit_length() - 2, -1))
EPS = 1e-6
CHUNK = 128
SCAN_LEVELS = 8

MERGE_SPLIT = 1
LANES = 128
VMEM_LIMIT = 56 * 1024 * 1024


def _cparams(sem):
    return pltpu.CompilerParams(dimension_semantics=sem, vmem_limit_bytes=VMEM_LIMIT)


def _rms(x, g):
    return x * lax.rsqrt(jnp.mean(x * x, axis=-1, keepdims=True) + EPS) * g


def _proj_body(x_ref, g_ref, w_ref, qkv0_ref, qkv1_ref, qkv2_ref, gate_ref, ut_ref, d_scr):
    tm = x_ref.shape[0]
    u_lo = 3 * ATTN_W
    h = _rms(x_ref[...], g_ref[...])
    hb = h.astype(BF16)
    u = jnp.dot(hb, w_ref[:, u_lo:u_lo + SSM_W], preferred_element_type=F32)
    for k in range(ut_ref.shape[0]):
        ut_ref[k] = u[k * CHUNK:(k + 1) * CHUNK, :].T

    nl = D_MODEL // LANES
    for k in range(nl):
        d_scr[k] = h[:, k * LANES:(k + 1) * LANES]

    def by_subsequence(r):
        blocks = [jnp.concatenate([d_scr.at[k][pl.ds(c, tm // r, stride=r), :] for k in range(nl)], axis=1)
                  for c in range(r)]
        return jnp.concatenate(blocks, axis=0).astype(BF16)

    cw = 2 * LANES
    scale = (HEAD_DIM ** -0.5 * LOG2E, None, None)
    for g, (out_ref, (_, r)) in enumerate(zip((qkv0_ref, qkv1_ref, qkv2_ref), ATTN_GROUPS)):
        lhs = hb if r == 1 else by_subsequence(r)
        for which in range(3):
            for lo in range(0, GROUP_W, cw):
                col = which * ATTN_W + g * GROUP_W + lo
                res = jnp.dot(lhs, w_ref[:, col:col + cw], preferred_element_type=F32)
                if scale[which] is not None:
                    res = res * scale[which]
                res = res.astype(BF16)
                for c in range(r):
                    out_ref[which, c, :, lo:lo + cw] = res[c * (tm // r):(c + 1) * (tm // r), :]
    for lo in range(0, 2 * D_MODEL, cw):
        col = u_lo + SSM_W + lo
        gate_ref[:, lo:lo + cw] = jnp.dot(hb, w_ref[:, col:col + cw], preferred_element_type=F32).astype(BF16)


def _in_projection(x2, g, w_bf, batch, seq, tm=512):
    n = x2.shape[0]
    tiles_per_seq = seq // tm
    wcols = w_bf.shape[1]

    def qkv_spec(r):
        return pl.BlockSpec((3, None, r, tm // r, GROUP_W),
                            lambda i: (0, i // tiles_per_seq, 0, i % tiles_per_seq, 0))

    return pl.pallas_call(
        _proj_body,
        grid=(n // tm,),
        in_specs=[
            pl.BlockSpec((tm, D_MODEL), lambda i: (i, 0)),
            pl.BlockSpec((1, D_MODEL), lambda i: (0, 0)),
            pl.BlockSpec((D_MODEL, wcols), lambda i: (0, 0), pipeline_mode=pl.Buffered(1)),
        ],
        out_specs=[qkv_spec(r) for _, r in ATTN_GROUPS] + [
            pl.BlockSpec((tm, 2 * D_MODEL), lambda i: (i, 0)),
            pl.BlockSpec((tm // CHUNK, SSM_W, CHUNK), lambda i: (i, 0, 0)),
        ],
        out_shape=[jax.ShapeDtypeStruct((3, batch, r, seq // r, GROUP_W), BF16) for _, r in ATTN_GROUPS] + [
            jax.ShapeDtypeStruct((n, 2 * D_MODEL), BF16),
            jax.ShapeDtypeStruct((n // CHUNK, SSM_W, CHUNK), F32),
        ],
        scratch_shapes=[pltpu.VMEM((D_MODEL // LANES, tm, LANES), F32)],
        compiler_params=_cparams(("parallel",)),
        name="in_projection",
    )(x2, g, w_bf)


def _t5_bucket(dist):
    max_exact = REL_BUCKETS // 2
    d = np.maximum(dist, 1).astype(np.float64)
    large = max_exact + (
        np.log(d / max_exact) / np.log(REL_MAX_DIST / max_exact) * (REL_BUCKETS - max_exact)
    ).astype(np.int32)
    large = np.minimum(large, REL_BUCKETS - 1)
    return np.where(dist < max_exact, dist, large).astype(np.int32)


def _band_bias(table, window, dilation):
    steps = window // dilation
    qi = np.arange(BLK)[:, None]
    kj = np.arange(2 * BLK)[None, :]
    delta = BLK + qi - kj
    band = (delta >= 0) & (delta <= steps)
    bucket = _t5_bucket(np.clip(delta, 0, steps) * dilation)
    onehot = np.eye(REL_BUCKETS, dtype=np.float32)[bucket]
    bias = jnp.einsum("qkb,bh->hqk", onehot, table.astype(F32), precision=lax.Precision.HIGHEST)
    return jnp.where(band[None], bias * LOG2E, NEG_INF)


def _attn_body(q_ref, kp_ref, kc_ref, vp_ref, vc_ref, bias_ref, o_ref, l_ref, *, nsub):
    lane = lax.broadcasted_iota(jnp.int32, (BLK, LANES), 1)
    lo = lane < HEAD_DIM
    keep_lo = jnp.where(lo, 1.0, 0.0).astype(BF16)
    keep_hi = jnp.where(lo, 0.0, 1.0).astype(BF16)
    col = lax.broadcasted_iota(jnp.int32, (BLK, 2 * BLK), 1)
    first_pen = jnp.where(col < BLK, jnp.where(pl.program_id(2) == 0, NEG_INF, 0.0), 0.0)
    for i in range(nsub):
        rows = slice(i * BLK, (i + 1) * BLK)
        q = q_ref[rows, :]
        if i == 0:
            kw = jnp.concatenate([kp_ref[...], kc_ref[0:BLK, :]], axis=0)
            vw = jnp.concatenate([vp_ref[...], vc_ref[0:BLK, :]], axis=0)
        else:
            kw = kc_ref[(i - 1) * BLK:(i + 1) * BLK, :]
            vw = vc_ref[(i - 1) * BLK:(i + 1) * BLK, :]
        for hp in range(HEADS // 2):
            cols = slice(hp * LANES, (hp + 1) * LANES)
            q2, k2, v2 = q[:, cols], kw[:, cols], vw[:, cols]
            pvs, ms, dens = [], [], []
            for half in range(2):
                qm = q2 * (keep_lo, keep_hi)[half]
                s = lax.dot_general(qm, k2, (((1,), (1,)), ((), ())), preferred_element_type=F32)
                s = s + bias_ref[2 * hp + half]
                if i == 0:
                    s = s + first_pen
                m = jnp.max(s, axis=-1, keepdims=True)
                p = jnp.exp2(s - m)
                dens.append(jnp.broadcast_to(jnp.sum(p, axis=-1, keepdims=True), (BLK, LANES)))
                ms.append(jnp.broadcast_to(m, (BLK, LANES)))
                pvs.append(jnp.dot(p.astype(BF16), v2, preferred_element_type=F32))
            den = jnp.where(lo, dens[0], dens[1])
            o_ref[rows, cols] = (jnp.where(lo, pvs[0], pvs[1]) / den).astype(BF16)
            l_ref[rows, cols] = jnp.where(lo, ms[0], ms[1]) + jnp.log2(den)


def _attention_group(qkv, bias, g):
    _, batch, r, length, _ = qkv.shape
    qb = min(512, length)
    nsub = qb // BLK

    def cur(which):
        return pl.BlockSpec((None, None, None, qb, GROUP_W), lambda b, c, n: (which, b, c, n, 0))

    def prev(which):
        return pl.BlockSpec((None, None, None, BLK, GROUP_W),
                            lambda b, c, n: (which, b, c, jnp.maximum(n * nsub - 1, 0), 0))

    out_spec = pl.BlockSpec((None, None, qb, GROUP_W), lambda b, c, n: (b, c, n, 0))
    return pl.pallas_call(
        functools.partial(_attn_body, nsub=nsub),
        grid=(batch, r, length // qb),
        in_specs=[cur(0), prev(1), cur(1), prev(2), cur(2),
                  pl.BlockSpec((HEADS, BLK, 2 * BLK), lambda b, c, n: (0, 0, 0))],
        out_specs=[out_spec, out_spec],
        out_shape=[
            jax.ShapeDtypeStruct((batch, r, length, GROUP_W), BF16),
            jax.ShapeDtypeStruct((batch, r, length, GROUP_W), F32),
        ],
        compiler_params=_cparams(("parallel", "parallel", "arbitrary")),
        name=f"attention_g{g}",
    )(qkv, qkv, qkv, qkv, qkv, bias)


def _ssm_tables(lam_re, lam_im, log_dt, b_re, b_im, c_re, c_im):
    lam = lax.complex(lam_re.astype(F32), lam_im.astype(F32))
    dt = jnp.exp(log_dt.astype(F32))[:, None]
    lam_dt = lam * dt
    lam_bar = jnp.exp(lam_dt)
    b = lax.complex(b_re.astype(F32), b_im.astype(F32))
    b_bar = ((lam_bar - 1.0) / lam)[..., None] * b
    half = CHUNK // 2
    t = jnp.arange(CHUNK, dtype=F32)

    def power(k):
        return jnp.exp(lam_dt[:, None, :] * jnp.reshape(jnp.asarray(k, F32), (1, -1, 1)))

    p_fwd = power(t - half)
    p_bwd = power(half - t)

    def in_pair(pw):
        return jnp.stack([jnp.concatenate([pw.real, pw.imag], axis=-1),
                          jnp.concatenate([-pw.imag, pw.real], axis=-1)], axis=1)

    def out_pair(pw):
        return jnp.stack([jnp.concatenate([pw.real, -pw.imag], axis=-1),
                          jnp.concatenate([-pw.imag, -pw.real], axis=-1)], axis=1)

    powers = jnp.stack([
        in_pair(p_bwd),
        in_pair(p_bwd * power(CHUNK - 1.0 - half)),
        out_pair(p_fwd),
        out_pair(p_fwd * power(half + 1.0)),
    ], axis=1)
    dup = lambda a: jnp.concatenate([a, a], axis=-1)
    b_cp = jnp.transpose(b_bar, (0, 2, 1))
    coefs = jnp.stack([dup(b_cp.real), dup(b_cp.imag), dup(c_re.astype(F32)), dup(c_im.astype(F32))],
                      axis=1)
    lc = power([float(CHUNK * 2 ** k) for k in range(SCAN_LEVELS)])
    l1 = jnp.concatenate([lc.real, lc.real], axis=-1)
    l2 = jnp.concatenate([-lc.imag, lc.imag], axis=-1)
    return powers, coefs, l1, l2


def _gelu_tanh(y):
    return y * jax.nn.sigmoid(1.5957691216057308 * (y + 0.044715 * (y * y * y)))


def _ssm_body(d_ref, u_ref, pw_ref, cf_ref, l1_ref, l2_ref, z_ref, mask_scr, z_scr, a_scr, w_scr, *,
              ncb):
    g = pl.program_id(0)
    nc = u_ref.shape[0]
    width = SSM_CH * CHUNK
    cb = mask_scr.shape[1]

    @pl.when(g == 0)
    def _():
        s_idx = lax.broadcasted_iota(jnp.int32, (width, cb), 0) & (CHUNK - 1)
        t_idx = lax.broadcasted_iota(jnp.int32, (width, cb), 1) & (CHUNK - 1)
        mask_scr[...] = jnp.where(t_idx >= s_idx, -1, 0).astype(jnp.int32)

    def expand(kind, c1, c2, ch):
        return (cf_ref[c1, ch:ch + 1, :] * pw_ref[kind, 0] + cf_ref[c2, ch:ch + 1, :] * pw_ref[kind, 1])

    for ch in range(SSM_CH):
        rows = slice(ch * CHUNK, (ch + 1) * CHUNK)
        a_scr[rows, :] = expand(0, 0, 1, ch).astype(BF16)
        w_scr[rows, :] = expand(1, 0, 1, ch).astype(BF16)

    u2 = u_ref.reshape(nc * SSM_CH, CHUNK)
    us =[u2[pl.ds(c, nc, stride=SSM_CH), :] for c in range(SSM_CH)]
    x = jnp.concatenate(us, axis=1).astype(BF16)

    acc = jnp.dot(x, w_scr[...], preferred_element_type=F32)
    rmod = lax.broadcasted_iota(jnp.int32, (nc, 2 * SSM_P), 0) & (ncb - 1)
    for k in range(ncb.bit_length() - 1):
        d = 1 << k
        sh = jnp.where(rmod >= d, pltpu.roll(acc, d, 0), 0.0)
        acc = acc + sh * l1_ref[k:k + 1, :] + pltpu.roll(sh, SSM_P, 1) * l2_ref[k:k + 1, :]
    x_in = jnp.where(rmod >= 1, pltpu.roll(acc, 1, 0), 0.0).astype(BF16)

    per = cb // CHUNK
    for k in range(width // cb):
        chans = range(k * per, (k + 1) * per)
        d_rhs = jnp.concatenate([expand(2, 2, 3, ch).T.astype(BF16) for ch in chans], axis=1)
        v_rhs = jnp.concatenate([expand(3, 2, 3, ch).T.astype(BF16) for ch in chans], axis=1)
        mk = jnp.dot(a_scr[...], d_rhs, preferred_element_type=F32)
        kept = lax.bitcast_convert_type(mk, jnp.int32) & mask_scr[...]
        m_k = lax.bitcast_convert_type(kept, F32).astype(BF16)
        y = (jnp.dot(x, m_k, preferred_element_type=F32)
             + jnp.dot(x_in, v_rhs, preferred_element_type=F32))
        for j, c in enumerate(chans):
            yc = y[:, j * CHUNK:(j + 1) * CHUNK] + d_ref[g * SSM_CH + c] * us[c]
            z_scr[pl.ds(c, nc, stride=SSM_CH), :] = _gelu_tanh(yc)
    z_ref[...] = z_scr[...].reshape(nc, SSM_CH, CHUNK)


def _ssm_scan(u3, d_skip, tables, nbatch):
    powers, coefs, l1, l2 = tables
    nc = u3.shape[0]
    ncb = nc // nbatch
    assert ncb & (ncb - 1) == 0 and ncb <= 2 ** SCAN_LEVELS
    width = SSM_CH * CHUNK
    grid_spec = pltpu.PrefetchScalarGridSpec(
        num_scalar_prefetch=1,
        grid=(SSM_G,),
        in_specs=[
            pl.BlockSpec((nc, SSM_CH, CHUNK), lambda g, d: (0, g, 0)),
            pl.BlockSpec((None,) + powers.shape[1:], lambda g, d: (g, 0, 0, 0, 0)),
            pl.BlockSpec((None,) + coefs.shape[1:], lambda g, d: (g, 0, 0, 0)),
            pl.BlockSpec((None, SCAN_LEVELS, 2 * SSM_P), lambda g, d: (g, 0, 0)),
            pl.BlockSpec((None, SCAN_LEVELS, 2 * SSM_P), lambda g, d: (g, 0, 0)),
        ],
        out_specs=pl.BlockSpec((nc, SSM_CH, CHUNK), lambda g, d: (0, g, 0)),
        scratch_shapes=[pltpu.VMEM((width, 4 * CHUNK), jnp.int32),
                        pltpu.VMEM((nc * SSM_CH, CHUNK), F32),
                        pltpu.VMEM((width, 2 * SSM_P), BF16), pltpu.VMEM((width, 2 * SSM_P), BF16)],
    )
    return pl.pallas_call(
        functools.partial(_ssm_body, ncb=ncb),
        grid_spec=grid_spec,
        out_shape=jax.ShapeDtypeStruct((nc, SSM_W, CHUNK), F32),
        compiler_params=_cparams(("arbitrary",)),
        name="ssm_scan",
    )(d_skip.astype(F32), u3, powers, coefs, l1, l2)


def _merge_body(o0, o1, o2, l0, l1, l2, zt_ref, ga_ref, gs_ref, x_ref,
                wglu_ref, bglu_ref, wab_ref, wsb_ref, wout_ref, out_ref, tok_scr):
    def token_major(ref, slot):
        r, rows, _ = ref.shape
        if r == 1:
            return lambda rs: ref[0, rs, :].astype(F32)
        nl = GROUP_W // LANES
        scrs = [tok_scr.at[slot * nl + k] for k in range(nl)]
        for c in range(r):
            sub = ref[c].astype(F32)
            for k in range(nl):
                scrs[k][pl.ds(c, rows, stride=r), :] = sub[:, k * LANES:(k + 1) * LANES]
        return lambda rs: jnp.concatenate([s[rs, :] for s in scrs], axis=1)

    lse = [token_major(l0, 0), token_major(l1, 0), token_major(l2, 1)]
    val = [token_major(o0, 0), token_major(o1, 2), token_major(o2, 3)]

    tm = x_ref.shape[0]
    rows = tm // MERGE_SPLIT
    for h in range(MERGE_SPLIT):
        rs = slice(h * rows, (h + 1) * rows)
        a0, a1, a2 = (f(rs) for f in lse)
        v0, v1, v2 = (f(rs) for f in val)
        mx = jnp.maximum(jnp.maximum(a0, a1), a2)
        e0, e1, e2 = jnp.exp2(a0 - mx), jnp.exp2(a1 - mx), jnp.exp2(a2 - mx)
        mix = (e0 * v0 + e1 * v1 + e2 * v2) / (e0 + e1 + e2)
        y_attn = jnp.dot(mix.astype(BF16), wab_ref[...], preferred_element_type=F32)

        chunks = range(h * rows // CHUNK, (h + 1) * rows // CHUNK)
        z = jnp.concatenate([zt_ref[k].T for k in chunks], axis=0).astype(BF16)
        gl = jnp.dot(z, wglu_ref[...], preferred_element_type=F32) + bglu_ref[...]
        sg = z.astype(F32) * jax.nn.sigmoid(gl)
        y_ssm = jnp.dot(sg.astype(BF16), wsb_ref[...], preferred_element_type=F32)

        merged = (jax.nn.sigmoid(ga_ref[rs, :].astype(F32)) * y_attn
                  + jax.nn.sigmoid(gs_ref[rs, :].astype(F32)) * y_ssm)
        out_ref[rs, :] = x_ref[rs, :] + jnp.dot(merged.astype(BF16), wout_ref[...],
                                                 preferred_element_type=F32)


def _merge(os_, ls_, zt, gates, x2, wglu, bglu, wab, wsb, wout, tm=512):
    n = x2.shape[0]
    tiles_per_seq = os_[0].shape[2] // tm
    row = lambda i: (i, 0)
    const = lambda i: (0, 0)

    def group_spec(a):
        r = a.shape[1]
        return pl.BlockSpec((None, r, tm // r, GROUP_W),
                            lambda i: (i // tiles_per_seq, 0, i % tiles_per_seq, 0))

    in_specs = (
        [group_spec(a) for a in os_] + [group_spec(a) for a in ls_]
        + [
            pl.BlockSpec((tm // CHUNK, SSM_W, CHUNK), lambda i: (i, 0, 0)),
            pl.BlockSpec((tm, D_MODEL), lambda i: (i, 0)),
            pl.BlockSpec((tm, D_MODEL), lambda i: (i, 1)),
            pl.BlockSpec((tm, D_MODEL), row),
            pl.BlockSpec((SSM_W, SSM_W), const),
            pl.BlockSpec((1, SSM_W), const),
            pl.BlockSpec((GROUP_W, D_MODEL), const),
            pl.BlockSpec((SSM_W, D_MODEL), const),
            pl.BlockSpec((D_MODEL, D_MODEL), const),
        ]
    )
    return pl.pallas_call(
        _merge_body,
        grid=(n // tm,),
        in_specs=in_specs,
        out_specs=pl.BlockSpec((tm, D_MODEL), row),
        out_shape=jax.ShapeDtypeStruct((n, D_MODEL), F32),
        scratch_shapes=[pltpu.VMEM((4 * GROUP_W // LANES, tm, LANES), F32)],
        compiler_params=_cparams(("parallel",)),
        name="merge",
    )(*os_, *ls_, zt, gates, gates, x2, wglu, bglu, wab, wsb, wout)


def _ffn_body(x_ref, g_ref, wg_ref, wu_ref, wd_ref, *rest, tf):
    ncast = (len(rest) - 1) // 2
    o_ref = rest[ncast]
    for src, dst in zip(rest[:ncast], rest[ncast + 1:]):
        dst[...] = src[...].astype(BF16)
    h = _rms(x_ref[...], g_ref[...]).astype(BF16)
    for f in range(wg_ref.shape[1] // tf):
        cols = slice(f * tf, (f + 1) * tf)
        a = jnp.dot(h, wg_ref[:, cols], preferred_element_type=F32)
        b = jnp.dot(h, wu_ref[:, cols], preferred_element_type=F32)
        act = (a * jax.nn.sigmoid(a) * b).astype(BF16)
        part = jnp.dot(act, wd_ref[cols, :], preferred_element_type=F32)
        if f == 0:
            o_ref[...] = x_ref[...] + part
        else:
            o_ref[...] += part


def _dense_ffn(x2, g, wg, wu, wd, to_cast=(), tm=512, tf=256):
    n = x2.shape[0]
    dff = wg.shape[1]
    steps = n // tm
    assert dff % tf == 0 and all(a.shape[0] % (16 * steps) == 0 for a in to_cast)
    resident = lambda shape: pl.BlockSpec(shape, lambda i: (0, 0), pipeline_mode=pl.Buffered(1))
    slabs = [pl.BlockSpec((a.shape[0] // steps, a.shape[1]), lambda i: (i, 0)) for a in to_cast]
    out, *casts = pl.pallas_call(
        functools.partial(_ffn_body, tf=tf),
        grid=(steps,),
        in_specs=[
            pl.BlockSpec((tm, D_MODEL), lambda i: (i, 0)),
            pl.BlockSpec((1, D_MODEL), lambda i: (0, 0)),
            resident((D_MODEL, dff)),
            resident((D_MODEL, dff)),
            resident((dff, D_MODEL)),
        ] + slabs,
        out_specs=[pl.BlockSpec((tm, D_MODEL), lambda i: (i, 0))] + slabs,
        out_shape=[jax.ShapeDtypeStruct((n, D_MODEL), F32)]
        + [jax.ShapeDtypeStruct(a.shape, BF16) for a in to_cast],
        compiler_params=_cparams(("parallel",)),
        name="dense_ffn",
    )(x2, g, wg, wu, wd, *to_cast)
    return out, casts


def _route_top2(h, w, idx_ref, gate_ref, idxt_ref):
    h_hi, w_hi = h.astype(BF16), w.astype(BF16)
    h_lo = (h - h_hi.astype(F32)).astype(BF16)
    w_lo = (w - w_hi.astype(F32)).astype(BF16)
    both = jnp.dot(h_hi, jnp.concatenate([w_hi, w_lo], axis=1), preferred_element_type=F32)
    logits = both[:, :LANES] + (both[:, LANES:] + jnp.dot(h_lo, w_hi, preferred_element_type=F32))
    lane = lax.broadcasted_iota(jnp.int32, logits.shape, 1)
    lane_f = lane.astype(F32)
    logits = jnp.where(lane < N_EXPERTS, logits, -jnp.inf)
    v1 = jnp.max(logits, axis=-1, keepdims=True)
    i1 = jnp.min(jnp.where(logits == v1, lane_f, float(LANES)), axis=-1, keepdims=True)
    rest = jnp.where(lane_f == i1, -jnp.inf, logits)
    v2 = jnp.max(rest, axis=-1, keepdims=True)
    i2 = jnp.min(jnp.where(rest == v2, lane_f, float(LANES)), axis=-1, keepdims=True)
    e = jnp.exp(v2 - v1)
    g1 = 1.0 / (1.0 + e)
    g2 = e / (1.0 + e)
    idx_f = jnp.where(lane == 0, i1, jnp.where(lane == 1, i2, 0.0))
    idx_ref[...] = idx_f.astype(jnp.int32)
    gate_ref[...] = jnp.where(lane == 0, g1, jnp.where(lane == 1, g2, 0.0))
    idxt_ref[...] = idx_f.T[:idxt_ref.shape[0], :].astype(jnp.int32)


def _router_body(x_ref, g_ref, wr_ref, idx_ref, gate_ref, idxt_ref):
    _route_top2(_rms(x_ref[...], g_ref[...]), wr_ref[...], idx_ref, gate_ref, idxt_ref)


def _router(x2, g, wr_pad, tm=1024):
    n = x2.shape[0]
    return pl.pallas_call(
        _router_body,
        grid=(n // tm,),
        in_specs=[
            pl.BlockSpec((tm, D_MODEL), lambda i: (i, 0)),
            pl.BlockSpec((1, D_MODEL), lambda i: (0, 0)),
            pl.BlockSpec((D_MODEL, LANES), lambda i: (0, 0)),
        ],
        out_specs=[
            pl.BlockSpec((tm, LANES), lambda i: (i, 0)),
            pl.BlockSpec((tm, LANES), lambda i: (i, 0)),
            pl.BlockSpec((8, tm), lambda i: (0, i)),
        ],
        out_shape=[
            jax.ShapeDtypeStruct((n, LANES), jnp.int32),
            jax.ShapeDtypeStruct((n, LANES), F32),
            jax.ShapeDtypeStruct((8, n), jnp.int32),
        ],
        compiler_params=_cparams(("parallel",)),
        name="router",
    )(x2, g, wr_pad)


def _pack_bf16_pairs(hb):
    half = hb.shape[1] // 2
    lo = lax.bitcast_convert_type(hb[:, :half].astype(F32), jnp.uint32)
    hi = lax.bitcast_convert_type(hb[:, half:].astype(F32), jnp.uint32)
    return (hi & jnp.uint32(0xFFFF0000)) | (lo >> 16)


def _unpack_bf16_pairs(xu):
    lo = lax.bitcast_convert_type(xu << 16, F32).astype(BF16)
    hi = lax.bitcast_convert_type(xu & jnp.uint32(0xFFFF0000), F32).astype(BF16)
    return lo, hi


def _for_each_run_piece(i, start_ref, loff_ref, len_ref, fn, fill):
    for e in range(N_EXPERTS):
        j = i * N_EXPERTS + e
        length, boff, soff = len_ref[j], loff_ref[j], start_ref[j]
        done = 0
        for p in RUN_PIECES:
            cond = (length & p) != 0
            fn(cond, pl.multiple_of(boff + done, RUN_ALIGN), pl.multiple_of(soff + done, RUN_ALIGN), p)
            done = done + jnp.where(cond, p, 0)
    last = i * N_EXPERTS + N_EXPERTS - 1
    used = loff_ref[last] + len_ref[last]
    rest, done = CBUF_ROWS - used, 0
    for p in FILL_PIECES:
        cond = (rest & p) != 0
        fill(cond, pl.multiple_of(used + done, RUN_ALIGN), pl.multiple_of(done, RUN_ALIGN), p)
        done = done + jnp.where(cond, p, 0)


def _dispatch_body(start_ref, loff_ref, len_ref, tail_ref, x_ref, g_ref, idxt_ref, off_ref, cnt0_ref,
                   xs_ref, cbuf, tri_scr, zero_scr, sem, zsem):
    i = pl.program_id(0)
    tt = x_ref.shape[0]

    @pl.when(i == 0)
    def _():
        r = lax.broadcasted_iota(jnp.int32, (tt, tt), 0)
        c = lax.broadcasted_iota(jnp.int32, (tt, tt), 1)
        tri_scr[...] = jnp.where(r < c, 1.0, 0.0).astype(BF16)
        zero_scr[...] = jnp.zeros_like(zero_scr)

        def fill(e):
            row = pl.multiple_of(jnp.maximum(tail_ref[e], 0), MOE_BM)
            return pltpu.make_async_copy(zero_scr, xs_ref.at[pl.ds(row, MOE_BM)], zsem)

        for e in range(tail_ref.shape[0]):
            pl.when(tail_ref[e] >= 0)(lambda e=e: fill(e).start())
        for e in range(tail_ref.shape[0]):
            pl.when(tail_ref[e] >= 0)(lambda e=e: fill(e).wait())
        nspare = 2 * FILL_ROWS
        spare_fill = pltpu.make_async_copy(zero_scr.at[pl.ds(0, nspare)],
                                           xs_ref.at[pl.ds(xs_ref.shape[0] - nspare, nspare)], zsem)
        spare_fill.start()
        spare_fill.wait()

    hb = _rms(x_ref[...], g_ref[...]).astype(BF16)

    sub = lax.broadcasted_iota(jnp.int32, (N_EXPERTS, tt), 0)
    pos = []
    for k in range(2):
        oh = jnp.where(sub == idxt_ref[k:k + 1, :], 1.0, 0.0)
        rank = jnp.dot(oh.astype(BF16), tri_scr[...], preferred_element_type=F32)
        base = off_ref[:, :1] if k == 0 else off_ref[:, :1] + cnt0_ref[:, :1]
        pos.append(jnp.sum(oh * (base + rank), axis=0, keepdims=True))

    rows = lax.broadcasted_iota(jnp.int32, (cbuf.shape[1], tt), 0).astype(F32)
    perm = (jnp.where(rows == pos[0], 1.0, 0.0) + jnp.where(rows == pos[1], 1.0, 0.0)).astype(BF16)
    slot = i % 2
    cbuf[slot] = _pack_bf16_pairs(jnp.dot(perm, hb, preferred_element_type=F32).astype(BF16))

    spare = xs_ref.shape[0] - 2 * FILL_ROWS

    def run_piece(cond, brow, srow, p):
        cp = pltpu.make_async_copy(cbuf.at[slot, pl.ds(brow, p)], xs_ref.at[pl.ds(srow, p)], sem.at[slot])
        pl.when(cond)(cp.start)

    def fill_piece(cond, brow, frow, p):
        run_piece(cond, brow, pl.multiple_of(spare + slot * FILL_ROWS + frow, RUN_ALIGN), p)

    _for_each_run_piece(i, start_ref, loff_ref, len_ref, run_piece, fill_piece)

    def drain(buf):
        pltpu.make_async_copy(cbuf.at[buf], xs_ref.at[pl.ds(0, CBUF_ROWS)], sem.at[buf]).wait()

    pl.when(i > 0)(lambda: drain(1 - slot))
    pl.when(i == pl.num_programs(0) - 1)(lambda: drain(slot))


def _dispatch(plan, x2, g, idxt):
    n = x2.shape[0]
    tt = MOE_TT
    smem = lambda i, *_: (i, 0, 0)
    grid_spec = pltpu.PrefetchScalarGridSpec(
        num_scalar_prefetch=4,
        grid=(n // tt,),
        in_specs=[
            pl.BlockSpec((tt, D_MODEL), lambda i, *_: (i, 0)),
            pl.BlockSpec((1, D_MODEL), lambda i, *_: (0, 0)),
            pl.BlockSpec((8, tt), lambda i, *_: (0, i)),
            pl.BlockSpec((None, N_EXPERTS, LANES), smem),
            pl.BlockSpec((None, N_EXPERTS, LANES), smem),
        ],
        out_specs=pl.BlockSpec(memory_space=pl.ANY),
        scratch_shapes=[
            pltpu.VMEM((2, CBUF_ROWS, D_MODEL // 2), jnp.uint32),
            pltpu.VMEM((tt, tt), BF16),
            pltpu.VMEM((MOE_BM, D_MODEL // 2), jnp.uint32),
            pltpu.SemaphoreType.DMA((2,)),
            pltpu.SemaphoreType.DMA(()),
        ],
    )
    return pl.pallas_call(
        _dispatch_body,
        grid_spec=grid_spec,
        out_shape=jax.ShapeDtypeStruct((plan["n_slots"] + 2 * FILL_ROWS, D_MODEL // 2), jnp.uint32),
        compiler_params=_cparams(("arbitrary",)),
        name="moe_dispatch",
    )(plan["start"], plan["loff"], plan["len8"], plan["tail_rows"], x2, g, idxt,
      plan["off_lanes"], plan["cnt0_lanes"])


def _experts_body(be_ref, nu_ref, xs_ref, wg_ref, wu_ref, wd_ref, ys_ref, acc_scr, *, tf):
    del be_ref
    i = pl.program_id(0)
    half = D_MODEL // 2

    @pl.when(i < nu_ref[0])
    def _():
        lo, hi = _unpack_bf16_pairs(xs_ref[...])
        for f in range(wg_ref.shape[1] // tf):
            cols = slice(f * tf, (f + 1) * tf)
            a = (jnp.dot(lo, wg_ref[:half, cols], preferred_element_type=F32)
                 + jnp.dot(hi, wg_ref[half:, cols], preferred_element_type=F32))
            b = (jnp.dot(lo, wu_ref[:half, cols], preferred_element_type=F32)
                 + jnp.dot(hi, wu_ref[half:, cols], preferred_element_type=F32))
            act = (a * jax.nn.sigmoid(a) * b).astype(BF16)
            part = jnp.dot(act, wd_ref[cols, :], preferred_element_type=F32)
            if f == 0:
                acc_scr[...] = part
            else:
                acc_scr[...] += part
        ys_ref[...] = _pack_bf16_pairs(acc_scr[...].astype(BF16))

    @pl.when(i >= nu_ref[0])
    def _():
        ys_ref[...] = jnp.zeros_like(ys_ref)


def _experts(block_e, n_used, xs, wg, wu, wd, tf=512):
    n_blocks = xs.shape[0] // MOE_BM
    n_slots = n_blocks * MOE_BM
    dff = wg.shape[2]

    def blk(i, nu):
        return jnp.minimum(i, nu[0] - 1)

    def wspec(rows, cols):
        return pl.BlockSpec((None, rows, cols), lambda i, be, nu: (be[blk(i, nu)], 0, 0))

    grid_spec = pltpu.PrefetchScalarGridSpec(
        num_scalar_prefetch=2,
        grid=(n_blocks,),
        in_specs=[
            pl.BlockSpec((MOE_BM, D_MODEL // 2), lambda i, be, nu: (blk(i, nu), 0)),
            wspec(D_MODEL, dff),
            wspec(D_MODEL, dff),
            wspec(dff, D_MODEL),
        ],
        out_specs=pl.BlockSpec((MOE_BM, D_MODEL // 2), lambda i, be, nu: (i, 0)),
        scratch_shapes=[pltpu.VMEM((MOE_BM, D_MODEL), F32)],
    )
    return pl.pallas_call(
        functools.partial(_experts_body, tf=tf),
        grid_spec=grid_spec,
        out_shape=jax.ShapeDtypeStruct((n_slots, D_MODEL // 2), jnp.uint32),
        compiler_params=_cparams(("arbitrary",)),
        name="moe_experts",
    )(block_e, n_used, xs, wg, wu, wd)


def _combine_body(start_ref, loff_ref, len_ref, x_ref, idx_ref, gate_ref, meta_ref, g_ref, ys_ref, o_ref,
                  ybuf, tri_scr, sem):
    i = pl.program_id(0)
    tt = x_ref.shape[0]
    slot = i % 2

    def fetch(tile, buf):
        def run_piece(cond, brow, srow, p):
            cp = pltpu.make_async_copy(ys_ref.at[pl.ds(srow, p)], ybuf.at[buf, pl.ds(brow, p)], sem.at[buf])
            pl.when(cond)(cp.start)

        _for_each_run_piece(tile, start_ref, loff_ref, len_ref, run_piece, run_piece)

    @pl.when(i == 0)
    def _():
        r = lax.broadcasted_iota(jnp.int32, (tt, tt), 0)
        c = lax.broadcasted_iota(jnp.int32, (tt, tt), 1)
        tri_scr[...] = jnp.where(c < r, 1.0, 0.0).astype(BF16)
        fetch(i, slot)

    pl.when(i + 1 < pl.num_programs(0))(lambda: fetch(i + 1, 1 - slot))

    lane = lax.broadcasted_iota(jnp.int32, (tt, LANES), 1)
    idx = idx_ref[...]
    gt = gate_ref[...]
    cols = lax.broadcasted_iota(jnp.int32, (tt, ybuf.shape[1]), 1).astype(F32)
    sel = None
    for k in range(2):
        oh = jnp.where(lane == idx[:, k:k + 1], 1.0, 0.0)
        rank = jnp.dot(tri_scr[...], oh.astype(BF16), preferred_element_type=F32)
        base = meta_ref[0:1, :] if k == 0 else meta_ref[0:1, :] + meta_ref[1:2, :]
        pos = jnp.sum(oh * (base + rank), axis=1, keepdims=True)
        term = jnp.where(cols == pos, gt[:, k:k + 1], 0.0)
        sel = term if sel is None else sel + term
    sel = sel.astype(BF16)

    pltpu.make_async_copy(ys_ref.at[pl.ds(0, CBUF_ROWS)], ybuf.at[slot], sem.at[slot]).wait()

    lo, hi = _unpack_bf16_pairs(ybuf[slot])
    y = jnp.concatenate([jnp.dot(sel, lo, preferred_element_type=F32),
                         jnp.dot(sel, hi, preferred_element_type=F32)], axis=1)
    o_ref[...] = _rms(x_ref[...] + y, g_ref[...])


def _combine(plan, x2, idx, gates, g_final, ys):
    n = x2.shape[0]
    tt = MOE_TT
    grid_spec = pltpu.PrefetchScalarGridSpec(
        num_scalar_prefetch=3,
        grid=(n // tt,),
        in_specs=[
            pl.BlockSpec((tt, D_MODEL), lambda i, *_: (i, 0)),
            pl.BlockSpec((tt, LANES), lambda i, *_: (i, 0)),
            pl.BlockSpec((tt, LANES), lambda i, *_: (i, 0)),
            pl.BlockSpec((None, 8, LANES), lambda i, *_: (i, 0, 0)),
            pl.BlockSpec((1, D_MODEL), lambda i, *_: (0, 0)),
            pl.BlockSpec(memory_space=pl.ANY),
        ],
        out_specs=pl.BlockSpec((tt, D_MODEL), lambda i, *_: (i, 0)),
        scratch_shapes=[
            pltpu.VMEM((2, CBUF_ROWS, D_MODEL // 2), jnp.uint32),
            pltpu.VMEM((tt, tt), BF16),
            pltpu.SemaphoreType.DMA((2,)),
        ],
    )
    return pl.pallas_call(
        _combine_body,
        grid_spec=grid_spec,
        out_shape=jax.ShapeDtypeStruct((n, D_MODEL), F32),
        compiler_params=_cparams(("arbitrary",)),
        name="moe_combine",
    )(plan["start"], plan["loff"], plan["len8"], x2, idx, gates, plan["meta_rows"], g_final, ys)


def _route_plan(idxt, n):
    nt = n // MOE_TT
    e2 = idxt[:2].reshape(2, nt, MOE_TT)
    oh = (e2[..., None] == jnp.arange(N_EXPERTS, dtype=jnp.int32)).astype(jnp.int32)
    cnt = jnp.sum(oh, axis=2)
    cnt0 = cnt[0]
    len8 = (cnt[0] + cnt[1] + RUN_ALIGN - 1) // RUN_ALIGN * RUN_ALIGN
    loff = jnp.cumsum(len8, axis=1) - len8
    region = jnp.sum(len8, axis=0)
    padded = (region + MOE_BM - 1) // MOE_BM * MOE_BM
    pad_end = jnp.cumsum(padded)
    start = (pad_end - padded)[None, :] + jnp.cumsum(len8, axis=0) - len8
    n_blocks = (2 * n + nt * N_EXPERTS * (RUN_ALIGN - 1) + MOE_BM - 1) // MOE_BM + N_EXPERTS
    starts = jnp.arange(n_blocks, dtype=jnp.int32) * MOE_BM
    block_e = jnp.sum((starts[:, None] >= pad_end[None, :]).astype(jnp.int32), axis=1)
    block_e = jnp.minimum(block_e, N_EXPERTS - 1).astype(jnp.int32)
    n_used = (pad_end[-1] // MOE_BM).astype(jnp.int32).reshape(1)
    tails = jnp.where(padded > 0, pad_end - MOE_BM, -1)
    spare = pad_end[-1] + jnp.arange(n_blocks - (2 * n) // MOE_BM, dtype=pad_end.dtype) * MOE_BM
    spare = jnp.where(spare < n_blocks * MOE_BM, spare, -1)
    lanes = lambda a: jnp.broadcast_to(a.astype(F32)[:, :, None], (nt, N_EXPERTS, LANES))
    meta_rows = jnp.zeros((nt, 8, LANES), F32)
    meta_rows = meta_rows.at[:, 0, :N_EXPERTS].set(loff.astype(F32)).at[:, 1, :N_EXPERTS].set(cnt0.astype(F32))
    flat = lambda a: a.reshape(-1).astype(jnp.int32)
    return dict(start=flat(start), loff=flat(loff), len8=flat(len8), block_e=block_e, n_used=n_used,
                tail_rows=jnp.concatenate([tails, spare]).astype(jnp.int32),
                off_lanes=lanes(loff), cnt0_lanes=lanes(cnt0), meta_rows=meta_rows,
                n_slots=n_blocks * MOE_BM)


def _mixer_layer(x2, batch, seq, rel_bias, norm_g, w_in_bf, ssm, d_skip, w_glu, b_glu,
                 w_attn_br, w_ssm_br, w_out):
    *qkvs, gates, ut = _in_projection(x2, norm_g.reshape(1, D_MODEL), w_in_bf, batch, seq)

    os_, ls_ = [], []
    for g, (window, dilation) in enumerate(ATTN_GROUPS):
        bias = _band_bias(rel_bias[:, g * HEADS:(g + 1) * HEADS], window, dilation)
        o, l = _attention_group(qkvs[g], bias, g)
        os_.append(o)
        ls_.append(l)

    zt = _ssm_scan(ut, d_skip, _ssm_tables(*ssm), batch)

    return _merge(os_, ls_, zt, gates, x2, w_glu.astype(BF16), b_glu.reshape(1, SSM_W).astype(F32),
                  w_attn_br.astype(BF16), w_ssm_br.astype(BF16), w_out.astype(BF16))


def kernel(x, rel_bias, norm1_g, w_in, ssm_lam_re, ssm_lam_im, ssm_log_dt, ssm_b_re, ssm_b_im, ssm_c_re, ssm_c_im, ssm_d, w_glu, b_glu, w_attn_br, w_ssm_br, w_out, norm2_g, ffn_w_gate, ffn_w_up, ffn_w_down, moe_router, moe_w_gate, moe_w_up, moe_w_down, final_norm_g):
    batch, seq, d = x.shape
    assert d == D_MODEL and norm1_g.shape[0] == 2 and seq % (16 * BLK) == 0
    n = batch * seq
    x2 = x.reshape(n, d)

    def mixer(x2, l, w_in_bf):
        ssm = (ssm_lam_re[l], ssm_lam_im[l], ssm_log_dt[l], ssm_b_re[l], ssm_b_im[l],
               ssm_c_re[l], ssm_c_im[l])
        return _mixer_layer(x2, batch, seq, rel_bias, norm1_g[l], w_in_bf, ssm, ssm_d[l], w_glu[l],
                            b_glu[l], w_attn_br[l], w_ssm_br[l], w_out[l])

    x2 = mixer(x2, 0, w_in[0].astype(BF16))
    moe_w = (moe_w_gate[0], moe_w_up[0], moe_w_down[0])
    x2, (w_in1_bf, *moe_bf) = _dense_ffn(
        x2, norm2_g[0].reshape(1, d), ffn_w_gate[0].astype(BF16), ffn_w_up[0].astype(BF16),
        ffn_w_down[0].astype(BF16), to_cast=[w_in[1]] + [w.reshape(-1, w.shape[2]) for w in moe_w])
    moe_bf = [b.reshape(w.shape) for b, w in zip(moe_bf, moe_w)]

    x2 = mixer(x2, 1, w_in1_bf)
    g2 = norm2_g[1].reshape(1, d)
    wr_pad = jnp.zeros((d, LANES), F32).at[:, :N_EXPERTS].set(moe_router[0].astype(F32))
    idx, gates, idxt = _router(x2, g2, wr_pad)
    plan = _route_plan(idxt, n)
    xs = _dispatch(plan, x2, g2, idxt)
    ys = _experts(plan["block_e"], plan["n_used"], xs, *moe_bf)
    out = _combine(plan, x2, idx, gates, final_norm_g.reshape(1, d), ys)
    return out.reshape(batch, seq, d)
```

```python
import functools

import numpy as np
import jax
import jax.numpy as jnp
from jax import lax
from jax.experimental import pallas as pl
from jax.experimental.pallas import tpu as pltpu

F32 = jnp.float32
BF16 = jnp.bfloat16

LANES = 128
BF16_ROWS = 16

D_MODEL = 1024
HEAD_DIM = 64
ATTN_GROUPS = ((128, 1), (512, 4), (2048, 16))
N_GROUPS = 3
HEADS = 8
GROUP_W = HEADS * HEAD_DIM
ATTN_W = N_GROUPS * GROUP_W
BLK = 128
REL_BUCKETS = 32
REL_MAX_DIST = 2048
NEG_INF = -1e30
LOG2E = 1.4426950408889634
SSM_CH = 16
SSM_W = D_MODEL // 2
SSM_G = SSM_W // SSM_CH
SSM_P = 64
N_EXPERTS = 8
MOE_BM = 512
MOE_TT = 512
RUN_ALIGN = 8
RUN_PIECES = tuple(1 << b for b in range(MOE_TT.bit_length() - 1, RUN_ALIGN.bit_length() - 2, -1))
CBUF_ROWS = -(-(2 * MOE_TT + N_EXPERTS * (RUN_ALIGN - 1)) // BF16_ROWS) * BF16_ROWS
FILL_ROWS = CBUF_ROWS - 2 * MOE_TT
FILL_PIECES = tuple(1 << b for b in range(FILL_ROWS.bit_length() - 1, RUN_ALIGN.bit_length() - 2, -1))
EPS = 1e-6
CHUNK = 128
SCAN_LEVELS = 8

VMEM_LIMIT = 56 * 1024 * 1024


def _cparams(sem):
    return pltpu.CompilerParams(dimension_semantics=sem, vmem_limit_bytes=VMEM_LIMIT)


def _rms(x, g):
    return x * lax.rsqrt(jnp.mean(x * x, axis=-1, keepdims=True) + EPS) * g


def _proj_body(x_ref, g_ref, w_ref, qkv0_ref, qkv1_ref, qkv2_ref, gate_ref, ut_ref, d_scr):
    tm = x_ref.shape[0]
    u_lo = 3 * ATTN_W
    h = _rms(x_ref[...], g_ref[...])
    hb = h.astype(BF16)
    u = jnp.dot(hb, w_ref[:, u_lo:u_lo + SSM_W], preferred_element_type=F32)
    for k in range(ut_ref.shape[0]):
        ut_ref[k] = u[k * CHUNK:(k + 1) * CHUNK, :].T

    nl = D_MODEL // LANES
    for k in range(nl):
        d_scr[k] = h[:, k * LANES:(k + 1) * LANES]

    def by_subsequence(r):
        blocks = [jnp.concatenate([d_scr.at[k][pl.ds(c, tm // r, stride=r), :] for k in range(nl)], axis=1)
                  for c in range(r)]
        return jnp.concatenate(blocks, axis=0).astype(BF16)

    cw = 2 * LANES
    scale = (HEAD_DIM ** -0.5 * LOG2E, None, None)
    for g, (out_ref, (_, r)) in enumerate(zip((qkv0_ref, qkv1_ref, qkv2_ref), ATTN_GROUPS)):
        lhs = hb if r == 1 else by_subsequence(r)
        for which in range(3):
            for lo in range(0, GROUP_W, cw):
                col = which * ATTN_W + g * GROUP_W + lo
                res = jnp.dot(lhs, w_ref[:, col:col + cw], preferred_element_type=F32)
                if scale[which] is not None:
                    res = res * scale[which]
                res = res.astype(BF16)
                for c in range(r):
                    out_ref[which, c, :, lo:lo + cw] = res[c * (tm // r):(c + 1) * (tm // r), :]
    for lo in range(0, 2 * D_MODEL, cw):
        col = u_lo + SSM_W + lo
        gate_ref[:, lo:lo + cw] = jnp.dot(hb, w_ref[:, col:col + cw], preferred_element_type=F32).astype(BF16)


def _in_projection(x2, g, w_bf, batch, seq, tm=512):
    n = x2.shape[0]
    tiles_per_seq = seq // tm
    wcols = w_bf.shape[1]

    def qkv_spec(r):
        return pl.BlockSpec((3, None, r, tm // r, GROUP_W),
                            lambda i: (0, i // tiles_per_seq, 0, i % tiles_per_seq, 0))

    return pl.pallas_call(
        _proj_body,
        grid=(n // tm,),
        in_specs=[
            pl.BlockSpec((tm, D_MODEL), lambda i: (i, 0)),
            pl.BlockSpec((1, D_MODEL), lambda i: (0, 0)),
            pl.BlockSpec((D_MODEL, wcols), lambda i: (0, 0), pipeline_mode=pl.Buffered(1)),
        ],
        out_specs=[qkv_spec(r) for _, r in ATTN_GROUPS] + [
            pl.BlockSpec((tm, 2 * D_MODEL), lambda i: (i, 0)),
            pl.BlockSpec((tm // CHUNK, SSM_W, CHUNK), lambda i: (i, 0, 0)),
        ],
        out_shape=[jax.ShapeDtypeStruct((3, batch, r, seq // r, GROUP_W), BF16) for _, r in ATTN_GROUPS] + [
            jax.ShapeDtypeStruct((n, 2 * D_MODEL), BF16),
            jax.ShapeDtypeStruct((n // CHUNK, SSM_W, CHUNK), F32),
        ],
        scratch_shapes=[pltpu.VMEM((D_MODEL // LANES, tm, LANES), F32)],
        compiler_params=_cparams(("parallel",)),
        name="in_projection",
    )(x2, g, w_bf)


def _t5_bucket(dist):
    max_exact = REL_BUCKETS // 2
    d = np.maximum(dist, 1).astype(np.float64)
    large = max_exact + (
        np.log(d / max_exact) / np.log(REL_MAX_DIST / max_exact) * (REL_BUCKETS - max_exact)
    ).astype(np.int32)
    large = np.minimum(large, REL_BUCKETS - 1)
    return np.where(dist < max_exact, dist, large).astype(np.int32)


def _band_bias(table, window, dilation):
    steps = window // dilation
    qi = np.arange(BLK)[:, None]
    kj = np.arange(2 * BLK)[None, :]
    delta = BLK + qi - kj
    band = (delta >= 0) & (delta <= steps)
    bucket = _t5_bucket(np.clip(delta, 0, steps) * dilation)
    onehot = np.eye(REL_BUCKETS, dtype=np.float32)[bucket]
    bias = jnp.einsum("qkb,bh->hqk", onehot, table.astype(F32), precision=lax.Precision.HIGHEST)
    return jnp.where(band[None], bias * LOG2E, NEG_INF)


def _attn_body(q_ref, kp_ref, kc_ref, vp_ref, vc_ref, bias_ref, o_ref, l_ref, *, nsub):
    lane = lax.broadcasted_iota(jnp.int32, (BLK, LANES), 1)
    lo = lane < HEAD_DIM
    keep_lo = jnp.where(lo, 1.0, 0.0).astype(BF16)
    keep_hi = jnp.where(lo, 0.0, 1.0).astype(BF16)
    col = lax.broadcasted_iota(jnp.int32, (BLK, 2 * BLK), 1)
    first_pen = jnp.where(col < BLK, jnp.where(pl.program_id(2) == 0, NEG_INF, 0.0), 0.0)
    for i in range(nsub):
        rows = slice(i * BLK, (i + 1) * BLK)
        q = q_ref[rows, :]
        if i == 0:
            kw = jnp.concatenate([kp_ref[...], kc_ref[0:BLK, :]], axis=0)
            vw = jnp.concatenate([vp_ref[...], vc_ref[0:BLK, :]], axis=0)
        else:
            kw = kc_ref[(i - 1) * BLK:(i + 1) * BLK, :]
            vw = vc_ref[(i - 1) * BLK:(i + 1) * BLK, :]
        for hp in range(HEADS // 2):
            cols = slice(hp * LANES, (hp + 1) * LANES)
            q2, k2, v2 = q[:, cols], kw[:, cols], vw[:, cols]
            pvs, ms, dens = [], [], []
            for half in range(2):
                qm = q2 * (keep_lo, keep_hi)[half]
                s = lax.dot_general(qm, k2, (((1,), (1,)), ((), ())), preferred_element_type=F32)
                s = s + bias_ref[2 * hp + half]
                if i == 0:
                    s = s + first_pen
                m = jnp.max(s, axis=-1, keepdims=True)
                p = jnp.exp2(s - m)
                dens.append(jnp.broadcast_to(jnp.sum(p, axis=-1, keepdims=True), (BLK, LANES)))
                ms.append(jnp.broadcast_to(m, (BLK, LANES)))
                pvs.append(jnp.dot(p.astype(BF16), v2, preferred_element_type=F32))
            den = jnp.where(lo, dens[0], dens[1])
            o_ref[rows, cols] = (jnp.where(lo, pvs[0], pvs[1]) / den).astype(BF16)
            l_ref[rows, cols] = jnp.where(lo, ms[0], ms[1]) + jnp.log2(den)


def _attention_group(qkv, bias, g):
    _, batch, r, length, _ = qkv.shape
    qb = min(2048, length)
    nsub = qb // BLK

    def cur(which):
        return pl.BlockSpec((None, None, None, qb, GROUP_W), lambda b, c, n: (which, b, c, n, 0))

    def prev(which):
        return pl.BlockSpec((None, None, None, BLK, GROUP_W),
                            lambda b, c, n: (which, b, c, jnp.maximum(n * nsub - 1, 0), 0))

    out_spec = pl.BlockSpec((None, None, qb, GROUP_W), lambda b, c, n: (b, c, n, 0))
    return pl.pallas_call(
        functools.partial(_attn_body, nsub=nsub),
        grid=(batch, r, length // qb),
        in_specs=[cur(0), prev(1), cur(1), prev(2), cur(2),
                  pl.BlockSpec((HEADS, BLK, 2 * BLK), lambda b, c, n: (0, 0, 0))],
        out_specs=[out_spec, out_spec],
        out_shape=[
            jax.ShapeDtypeStruct((batch, r, length, GROUP_W), BF16),
            jax.ShapeDtypeStruct((batch, r, length, GROUP_W), F32),
        ],
        compiler_params=_cparams(("parallel", "parallel", "arbitrary")),
        name=f"attention_g{g}",
    )(qkv, qkv, qkv, qkv, qkv, bias)


def _ssm_tables(lam_re, lam_im, log_dt, b_re, b_im, c_re, c_im):
    lam = lax.complex(lam_re.astype(F32), lam_im.astype(F32))
    dt = jnp.exp(log_dt.astype(F32))[:, None]
    lam_dt = lam * dt
    lam_bar = jnp.exp(lam_dt)
    b = lax.complex(b_re.astype(F32), b_im.astype(F32))
    b_bar = ((lam_bar - 1.0) / lam)[..., None] * b
    half = CHUNK // 2
    t = jnp.arange(CHUNK, dtype=F32)

    def power(k):
        return jnp.exp(lam_dt[:, None, :] * jnp.reshape(jnp.asarray(k, F32), (1, -1, 1)))

    p_fwd = power(t - half)
    p_bwd = power(half - t)

    def in_pair(pw):
        return jnp.stack([jnp.concatenate([pw.real, pw.imag], axis=-1),
                          jnp.concatenate([-pw.imag, pw.real], axis=-1)], axis=1)

    def out_pair(pw):
        return jnp.stack([jnp.concatenate([pw.real, -pw.imag], axis=-1),
                          jnp.concatenate([-pw.imag, -pw.real], axis=-1)], axis=1)

    powers = jnp.stack([
        in_pair(p_bwd),
        in_pair(p_bwd * power(CHUNK - 1.0 - half)),
        out_pair(p_fwd),
        out_pair(p_fwd * power(half + 1.0)),
    ], axis=1)
    dup = lambda a: jnp.concatenate([a, a], axis=-1)
    b_cp = jnp.transpose(b_bar, (0, 2, 1))
    coefs = jnp.stack([dup(b_cp.real), dup(b_cp.imag), dup(c_re.astype(F32)), dup(c_im.astype(F32))],
                      axis=1)
    lc = power([float(CHUNK * 2 ** k) for k in range(SCAN_LEVELS)])
    l1 = jnp.concatenate([lc.real, lc.real], axis=-1)
    l2 = jnp.concatenate([-lc.imag, lc.imag], axis=-1)
    return powers, coefs, l1, l2


def _gelu_tanh(y):
    return y * jax.nn.sigmoid(1.5957691216057308 * (y + 0.044715 * (y * y * y)))


def _ssm_body(d_ref, u_ref, pw_ref, cf_ref, l1_ref, l2_ref, z_ref, mask_scr, z_scr, a_scr, w_scr, *,
              ncb):
    g = pl.program_id(0)
    nc = u_ref.shape[0]
    width = SSM_CH * CHUNK
    cb = mask_scr.shape[1]

    @pl.when(g == 0)
    def _():
        s_idx = lax.broadcasted_iota(jnp.int32, (width, cb), 0) & (CHUNK - 1)
        t_idx = lax.broadcasted_iota(jnp.int32, (width, cb), 1) & (CHUNK - 1)
        mask_scr[...] = jnp.where(t_idx >= s_idx, -1, 0).astype(jnp.int32)

    def expand(kind, c1, c2, ch):
        return (cf_ref[c1, ch:ch + 1, :] * pw_ref[kind, 0] + cf_ref[c2, ch:ch + 1, :] * pw_ref[kind, 1])

    for ch in range(SSM_CH):
        rows = slice(ch * CHUNK, (ch + 1) * CHUNK)
        a_scr[rows, :] = expand(0, 0, 1, ch).astype(BF16)
        w_scr[rows, :] = expand(1, 0, 1, ch).astype(BF16)

    u2 = u_ref.reshape(nc * SSM_CH, CHUNK)
    us =[u2[pl.ds(c, nc, stride=SSM_CH), :] for c in range(SSM_CH)]
    x = jnp.concatenate(us, axis=1).astype(BF16)

    acc = jnp.dot(x, w_scr[...], preferred_element_type=F32)
    rmod = lax.broadcasted_iota(jnp.int32, (nc, 2 * SSM_P), 0) & (ncb - 1)
    for k in range(ncb.bit_length() - 1):
        d = 1 << k
        sh = jnp.where(rmod >= d, pltpu.roll(acc, d, 0), 0.0)
        acc = acc + sh * l1_ref[k:k + 1, :] + pltpu.roll(sh, SSM_P, 1) * l2_ref[k:k + 1, :]
    x_in = jnp.where(rmod >= 1, pltpu.roll(acc, 1, 0), 0.0).astype(BF16)

    per = cb // CHUNK
    for k in range(width // cb):
        chans = range(k * per, (k + 1) * per)
        d_rhs = jnp.concatenate([expand(2, 2, 3, ch).T.astype(BF16) for ch in chans], axis=1)
        v_rhs = jnp.concatenate([expand(3, 2, 3, ch).T.astype(BF16) for ch in chans], axis=1)
        mk = jnp.dot(a_scr[...], d_rhs, preferred_element_type=F32)
        kept = lax.bitcast_convert_type(mk, jnp.int32) & mask_scr[...]
        m_k = lax.bitcast_convert_type(kept, F32).astype(BF16)
        y = (jnp.dot(x, m_k, preferred_element_type=F32)
             + jnp.dot(x_in, v_rhs, preferred_element_type=F32))
        for j, c in enumerate(chans):
            yc = y[:, j * CHUNK:(j + 1) * CHUNK] + d_ref[g * SSM_CH + c] * us[c]
            z_scr[pl.ds(c, nc, stride=SSM_CH), :] = _gelu_tanh(yc)
    z_ref[...] = z_scr[...].reshape(nc, SSM_CH, CHUNK)


def _ssm_scan(u3, d_skip, tables, nbatch):
    powers, coefs, l1, l2 = tables
    nc = u3.shape[0]
    ncb = nc // nbatch
    assert ncb & (ncb - 1) == 0 and ncb <= 2 ** SCAN_LEVELS
    width = SSM_CH * CHUNK
    grid_spec = pltpu.PrefetchScalarGridSpec(
        num_scalar_prefetch=1,
        grid=(SSM_G,),
        in_specs=[
            pl.BlockSpec((nc, SSM_CH, CHUNK), lambda g, d: (0, g, 0)),
            pl.BlockSpec((None,) + powers.shape[1:], lambda g, d: (g, 0, 0, 0, 0)),
            pl.BlockSpec((None,) + coefs.shape[1:], lambda g, d: (g, 0, 0, 0)),
            pl.BlockSpec((None, SCAN_LEVELS, 2 * SSM_P), lambda g, d: (g, 0, 0)),
            pl.BlockSpec((None, SCAN_LEVELS, 2 * SSM_P), lambda g, d: (g, 0, 0)),
        ],
        out_specs=pl.BlockSpec((nc, SSM_CH, CHUNK), lambda g, d: (0, g, 0)),
        scratch_shapes=[pltpu.VMEM((width, 4 * CHUNK), jnp.int32),
                        pltpu.VMEM((nc * SSM_CH, CHUNK), F32),
                        pltpu.VMEM((width, 2 * SSM_P), BF16), pltpu.VMEM((width, 2 * SSM_P), BF16)],
    )
    return pl.pallas_call(
        functools.partial(_ssm_body, ncb=ncb),
        grid_spec=grid_spec,
        out_shape=jax.ShapeDtypeStruct((nc, SSM_W, CHUNK), F32),
        compiler_params=_cparams(("arbitrary",)),
        name="ssm_scan",
    )(d_skip.astype(F32), u3, powers, coefs, l1, l2)


def _merge_body(o0, o1, o2, l0, l1, l2, zt_ref, ga_ref, gs_ref, x_ref,
                wglu_ref, bglu_ref, wab_ref, wsb_ref, wout_ref, out_ref, tok_scr):
    def token_major(ref, slot):
        r, rows, _ = ref.shape
        if r == 1:
            return ref[0].astype(F32)
        nl = GROUP_W // LANES
        scrs = [tok_scr.at[slot * nl + k] for k in range(nl)]
        for c in range(r):
            sub = ref[c].astype(F32)
            for k in range(nl):
                scrs[k][pl.ds(c, rows, stride=r), :] = sub[:, k * LANES:(k + 1) * LANES]
        return jnp.concatenate([s[...] for s in scrs], axis=1)

    a0, a1, a2 = token_major(l0, 0), token_major(l1, 0), token_major(l2, 1)
    v0, v1, v2 = token_major(o0, 0), token_major(o1, 2), token_major(o2, 3)
    mx = jnp.maximum(jnp.maximum(a0, a1), a2)
    e0, e1, e2 = jnp.exp2(a0 - mx), jnp.exp2(a1 - mx), jnp.exp2(a2 - mx)
    mix = (e0 * v0 + e1 * v1 + e2 * v2) / (e0 + e1 + e2)
    y_attn = jnp.dot(mix.astype(BF16), wab_ref[...], preferred_element_type=F32)

    z = jnp.concatenate([zt_ref[k].T for k in range(zt_ref.shape[0])], axis=0).astype(BF16)
    gl = jnp.dot(z, wglu_ref[...], preferred_element_type=F32) + bglu_ref[...]
    sg = z.astype(F32) * jax.nn.sigmoid(gl)
    y_ssm = jnp.dot(sg.astype(BF16), wsb_ref[...], preferred_element_type=F32)

    merged = (jax.nn.sigmoid(ga_ref[...].astype(F32)) * y_attn
              + jax.nn.sigmoid(gs_ref[...].astype(F32)) * y_ssm)
    out_ref[...] = x_ref[...] + jnp.dot(merged.astype(BF16), wout_ref[...], preferred_element_type=F32)


def _merge(os_, ls_, zt, gates, x2, wglu, bglu, wab, wsb, wout, tm=512):
    n = x2.shape[0]
    tiles_per_seq = os_[0].shape[2] // tm
    row = lambda i: (i, 0)
    const = lambda i: (0, 0)

    def group_spec(a):
        r = a.shape[1]
        return pl.BlockSpec((None, r, tm // r, GROUP_W),
                            lambda i: (i // tiles_per_seq, 0, i % tiles_per_seq, 0))

    in_specs = (
        [group_spec(a) for a in os_] + [group_spec(a) for a in ls_]
        + [
            pl.BlockSpec((tm // CHUNK, SSM_W, CHUNK), lambda i: (i, 0, 0)),
            pl.BlockSpec((tm, D_MODEL), lambda i: (i, 0)),
            pl.BlockSpec((tm, D_MODEL), lambda i: (i, 1)),
            pl.BlockSpec((tm, D_MODEL), row),
            pl.BlockSpec((SSM_W, SSM_W), const),
            pl.BlockSpec((1, SSM_W), const),
            pl.BlockSpec((GROUP_W, D_MODEL), const),
            pl.BlockSpec((SSM_W, D_MODEL), const),
            pl.BlockSpec((D_MODEL, D_MODEL), const),
        ]
    )
    return pl.pallas_call(
        _merge_body,
        grid=(n // tm,),
        in_specs=in_specs,
        out_specs=pl.BlockSpec((tm, D_MODEL), row),
        out_shape=jax.ShapeDtypeStruct((n, D_MODEL), F32),
        scratch_shapes=[pltpu.VMEM((4 * GROUP_W // LANES, tm, LANES), F32)],
        compiler_params=_cparams(("parallel",)),
        name="merge",
    )(*os_, *ls_, zt, gates, gates, x2, wglu, bglu, wab, wsb, wout)


def _ffn_body(x_ref, g_ref, wg_ref, wu_ref, wd_ref, *rest, tf):
    ncast = (len(rest) - 1) // 2
    o_ref = rest[ncast]
    for src, dst in zip(rest[:ncast], rest[ncast + 1:]):
        dst[...] = src[...].astype(BF16)
    h = _rms(x_ref[...], g_ref[...]).astype(BF16)
    for f in range(wg_ref.shape[1] // tf):
        cols = slice(f * tf, (f + 1) * tf)
        a = jnp.dot(h, wg_ref[:, cols], preferred_element_type=F32)
        b = jnp.dot(h, wu_ref[:, cols], preferred_element_type=F32)
        act = (a * jax.nn.sigmoid(a) * b).astype(BF16)
        part = jnp.dot(act, wd_ref[cols, :], preferred_element_type=F32)
        if f == 0:
            o_ref[...] = x_ref[...] + part
        else:
            o_ref[...] += part


def _dense_ffn(x2, g, wg, wu, wd, to_cast=(), tm=512, tf=256):
    n = x2.shape[0]
    dff = wg.shape[1]
    steps = n // tm
    assert dff % tf == 0 and all(a.shape[0] % (BF16_ROWS * steps) == 0 for a in to_cast)
    resident = lambda shape: pl.BlockSpec(shape, lambda i: (0, 0), pipeline_mode=pl.Buffered(1))
    slabs = [pl.BlockSpec((a.shape[0] // steps, a.shape[1]), lambda i: (i, 0)) for a in to_cast]
    out, *casts = pl.pallas_call(
        functools.partial(_ffn_body, tf=tf),
        grid=(steps,),
        in_specs=[
            pl.BlockSpec((tm, D_MODEL), lambda i: (i, 0)),
            pl.BlockSpec((1, D_MODEL), lambda i: (0, 0)),
            resident((D_MODEL, dff)),
            resident((D_MODEL, dff)),
            resident((dff, D_MODEL)),
        ] + slabs,
        out_specs=[pl.BlockSpec((tm, D_MODEL), lambda i: (i, 0))] + slabs,
        out_shape=[jax.ShapeDtypeStruct((n, D_MODEL), F32)]
        + [jax.ShapeDtypeStruct(a.shape, BF16) for a in to_cast],
        compiler_params=_cparams(("parallel",)),
        name="dense_ffn",
    )(x2, g, wg, wu, wd, *to_cast)
    return out, casts


def _route_top2(h, w, idx_ref, gate_ref, idxt_ref):
    h_hi, w_hi = h.astype(BF16), w.astype(BF16)
    h_lo = (h - h_hi.astype(F32)).astype(BF16)
    w_lo = (w - w_hi.astype(F32)).astype(BF16)
    both = jnp.dot(h_hi, jnp.concatenate([w_hi, w_lo], axis=1), preferred_element_type=F32)
    logits = both[:, :LANES] + (both[:, LANES:] + jnp.dot(h_lo, w_hi, preferred_element_type=F32))
    lane = lax.broadcasted_iota(jnp.int32, logits.shape, 1)
    lane_f = lane.astype(F32)
    logits = jnp.where(lane < N_EXPERTS, logits, -jnp.inf)
    v1 = jnp.max(logits, axis=-1, keepdims=True)
    i1 = jnp.min(jnp.where(logits == v1, lane_f, float(LANES)), axis=-1, keepdims=True)
    rest = jnp.where(lane_f == i1, -jnp.inf, logits)
    v2 = jnp.max(rest, axis=-1, keepdims=True)
    i2 = jnp.min(jnp.where(rest == v2, lane_f, float(LANES)), axis=-1, keepdims=True)
    e = jnp.exp(v2 - v1)
    g1 = 1.0 / (1.0 + e)
    g2 = e / (1.0 + e)
    idx_f = jnp.where(lane == 0, i1, jnp.where(lane == 1, i2, 0.0))
    idx_ref[...] = idx_f.astype(jnp.int32)
    gate_ref[...] = jnp.where(lane == 0, g1, jnp.where(lane == 1, g2, 0.0))
    idxt_ref[...] = idx_f.T[:idxt_ref.shape[0], :].astype(jnp.int32)


def _router_body(x_ref, g_ref, wr_ref, idx_ref, gate_ref, idxt_ref):
    _route_top2(_rms(x_ref[...], g_ref[...]), wr_ref[...], idx_ref, gate_ref, idxt_ref)


def _router(x2, g, wr_pad, tm=1024):
    n = x2.shape[0]
    return pl.pallas_call(
        _router_body,
        grid=(n // tm,),
        in_specs=[
            pl.BlockSpec((tm, D_MODEL), lambda i: (i, 0)),
            pl.BlockSpec((1, D_MODEL), lambda i: (0, 0)),
            pl.BlockSpec((D_MODEL, LANES), lambda i: (0, 0)),
        ],
        out_specs=[
            pl.BlockSpec((tm, LANES), lambda i: (i, 0)),
            pl.BlockSpec((tm, LANES), lambda i: (i, 0)),
            pl.BlockSpec((8, tm), lambda i: (0, i)),
        ],
        out_shape=[
            jax.ShapeDtypeStruct((n, LANES), jnp.int32),
            jax.ShapeDtypeStruct((n, LANES), F32),
            jax.ShapeDtypeStruct((8, n), jnp.int32),
        ],
        compiler_params=_cparams(("parallel",)),
        name="router",
    )(x2, g, wr_pad)


def _pack_bf16_pairs(hb):
    half = hb.shape[1] // 2
    lo = lax.bitcast_convert_type(hb[:, :half].astype(F32), jnp.uint32)
    hi = lax.bitcast_convert_type(hb[:, half:].astype(F32), jnp.uint32)
    return (hi & jnp.uint32(0xFFFF0000)) | (lo >> 16)


def _unpack_bf16_pairs(xu):
    lo = lax.bitcast_convert_type(xu << 16, F32).astype(BF16)
    hi = lax.bitcast_convert_type(xu & jnp.uint32(0xFFFF0000), F32).astype(BF16)
    return lo, hi


def _for_each_run_piece(i, start_ref, loff_ref, len_ref, fn, fill):
    for e in range(N_EXPERTS):
        j = i * N_EXPERTS + e
        length, boff, soff = len_ref[j], loff_ref[j], start_ref[j]
        done = 0
        for p in RUN_PIECES:
            cond = (length & p) != 0
            fn(cond, pl.multiple_of(boff + done, RUN_ALIGN), pl.multiple_of(soff + done, RUN_ALIGN), p)
            done = done + jnp.where(cond, p, 0)
    last = i * N_EXPERTS + N_EXPERTS - 1
    used = loff_ref[last] + len_ref[last]
    rest, done = CBUF_ROWS - used, 0
    for p in FILL_PIECES:
        cond = (rest & p) != 0
        fill(cond, pl.multiple_of(used + done, RUN_ALIGN), pl.multiple_of(done, RUN_ALIGN), p)
        done = done + jnp.where(cond, p, 0)


def _dispatch_body(start_ref, loff_ref, len_ref, tail_ref, x_ref, g_ref, idxt_ref, off_ref, cnt0_ref,
                   xs_ref, cbuf, tri_scr, zero_scr, sem, zsem):
    i = pl.program_id(0)
    tt = x_ref.shape[0]

    @pl.when(i == 0)
    def _():
        r = lax.broadcasted_iota(jnp.int32, (tt, tt), 0)
        c = lax.broadcasted_iota(jnp.int32, (tt, tt), 1)
        tri_scr[...] = jnp.where(r < c, 1.0, 0.0).astype(BF16)
        zero_scr[...] = jnp.zeros_like(zero_scr)

        def fill(e):
            row = pl.multiple_of(jnp.maximum(tail_ref[e], 0), MOE_BM)
            return pltpu.make_async_copy(zero_scr, xs_ref.at[pl.ds(row, MOE_BM)], zsem)

        for e in range(tail_ref.shape[0]):
            pl.when(tail_ref[e] >= 0)(lambda e=e: fill(e).start())
        for e in range(tail_ref.shape[0]):
            pl.when(tail_ref[e] >= 0)(lambda e=e: fill(e).wait())
        nspare = 2 * FILL_ROWS
        spare_fill = pltpu.make_async_copy(zero_scr.at[pl.ds(0, nspare)],
                                           xs_ref.at[pl.ds(xs_ref.shape[0] - nspare, nspare)], zsem)
        spare_fill.start()
        spare_fill.wait()

    hb = _rms(x_ref[...], g_ref[...]).astype(BF16)

    sub = lax.broadcasted_iota(jnp.int32, (N_EXPERTS, tt), 0)
    pos = []
    for k in range(2):
        oh = jnp.where(sub == idxt_ref[k:k + 1, :], 1.0, 0.0)
        rank = jnp.dot(oh.astype(BF16), tri_scr[...], preferred_element_type=F32)
        base = off_ref[:, :1] if k == 0 else off_ref[:, :1] + cnt0_ref[:, :1]
        pos.append(jnp.sum(oh * (base + rank), axis=0, keepdims=True))

    rows = lax.broadcasted_iota(jnp.int32, (cbuf.shape[1], tt), 0).astype(F32)
    perm = (jnp.where(rows == pos[0], 1.0, 0.0) + jnp.where(rows == pos[1], 1.0, 0.0)).astype(BF16)
    slot = i % 2
    cbuf[slot] = _pack_bf16_pairs(jnp.dot(perm, hb, preferred_element_type=F32).astype(BF16))

    spare = xs_ref.shape[0] - 2 * FILL_ROWS

    def run_piece(cond, brow, srow, p):
        cp = pltpu.make_async_copy(cbuf.at[slot, pl.ds(brow, p)], xs_ref.at[pl.ds(srow, p)], sem.at[slot])
        pl.when(cond)(cp.start)

    def fill_piece(cond, brow, frow, p):
        run_piece(cond, brow, pl.multiple_of(spare + slot * FILL_ROWS + frow, RUN_ALIGN), p)

    _for_each_run_piece(i, start_ref, loff_ref, len_ref, run_piece, fill_piece)

    def drain(buf):
        pltpu.make_async_copy(cbuf.at[buf], xs_ref.at[pl.ds(0, CBUF_ROWS)], sem.at[buf]).wait()

    pl.when(i > 0)(lambda: drain(1 - slot))
    pl.when(i == pl.num_programs(0) - 1)(lambda: drain(slot))


def _dispatch(plan, x2, g, idxt):
    n = x2.shape[0]
    tt = MOE_TT
    smem = lambda i, *_: (i, 0, 0)
    grid_spec = pltpu.PrefetchScalarGridSpec(
        num_scalar_prefetch=4,
        grid=(n // tt,),
        in_specs=[
            pl.BlockSpec((tt, D_MODEL), lambda i, *_: (i, 0)),
            pl.BlockSpec((1, D_MODEL), lambda i, *_: (0, 0)),
            pl.BlockSpec((8, tt), lambda i, *_: (0, i)),
            pl.BlockSpec((None, N_EXPERTS, LANES), smem),
            pl.BlockSpec((None, N_EXPERTS, LANES), smem),
        ],
        out_specs=pl.BlockSpec(memory_space=pl.ANY),
        scratch_shapes=[
            pltpu.VMEM((2, CBUF_ROWS, D_MODEL // 2), jnp.uint32),
            pltpu.VMEM((tt, tt), BF16),
            pltpu.VMEM((MOE_BM, D_MODEL // 2), jnp.uint32),
            pltpu.SemaphoreType.DMA((2,)),
            pltpu.SemaphoreType.DMA(()),
        ],
    )
    return pl.pallas_call(
        _dispatch_body,
        grid_spec=grid_spec,
        out_shape=jax.ShapeDtypeStruct((plan["n_slots"] + 2 * FILL_ROWS, D_MODEL // 2), jnp.uint32),
        compiler_params=_cparams(("arbitrary",)),
        name="moe_dispatch",
    )(plan["start"], plan["loff"], plan["len8"], plan["tail_rows"], x2, g, idxt,
      plan["off_lanes"], plan["cnt0_lanes"])


def _experts_body(be_ref, nu_ref, xs_ref, wg_ref, wu_ref, wd_ref, ys_ref, acc_scr, *, tf):
    del be_ref
    i = pl.program_id(0)
    half = D_MODEL // 2

    @pl.when(i < nu_ref[0])
    def _():
        lo, hi = _unpack_bf16_pairs(xs_ref[...])
        for f in range(wg_ref.shape[1] // tf):
            cols = slice(f * tf, (f + 1) * tf)
            a = (jnp.dot(lo, wg_ref[:half, cols], preferred_element_type=F32)
                 + jnp.dot(hi, wg_ref[half:, cols], preferred_element_type=F32))
            b = (jnp.dot(lo, wu_ref[:half, cols], preferred_element_type=F32)
                 + jnp.dot(hi, wu_ref[half:, cols], preferred_element_type=F32))
            act = (a * jax.nn.sigmoid(a) * b).astype(BF16)
            part = jnp.dot(act, wd_ref[cols, :], preferred_element_type=F32)
            if f == 0:
                acc_scr[...] = part
            else:
                acc_scr[...] += part
        ys_ref[...] = _pack_bf16_pairs(acc_scr[...].astype(BF16))

    @pl.when(i >= nu_ref[0])
    def _():
        ys_ref[...] = jnp.zeros_like(ys_ref)


def _experts(block_e, n_used, xs, wg, wu, wd, tf=512):
    n_blocks = xs.shape[0] // MOE_BM
    n_slots = n_blocks * MOE_BM
    dff = wg.shape[2]

    def blk(i, nu):
        return jnp.minimum(i, nu[0] - 1)

    def wspec(rows, cols):
        return pl.BlockSpec((None, rows, cols), lambda i, be, nu: (be[blk(i, nu)], 0, 0))

    grid_spec = pltpu.PrefetchScalarGridSpec(
        num_scalar_prefetch=2,
        grid=(n_blocks,),
        in_specs=[
            pl.BlockSpec((MOE_BM, D_MODEL // 2), lambda i, be, nu: (blk(i, nu), 0)),
            wspec(D_MODEL, dff),
            wspec(D_MODEL, dff),
            wspec(dff, D_MODEL),
        ],
        out_specs=pl.BlockSpec((MOE_BM, D_MODEL // 2), lambda i, be, nu: (i, 0)),
        scratch_shapes=[pltpu.VMEM((MOE_BM, D_MODEL), F32)],
    )
    return pl.pallas_call(
        functools.partial(_experts_body, tf=tf),
        grid_spec=grid_spec,
        out_shape=jax.ShapeDtypeStruct((n_slots, D_MODEL // 2), jnp.uint32),
        compiler_params=_cparams(("arbitrary",)),
        name="moe_experts",
    )(block_e, n_used, xs, wg, wu, wd)


def _combine_body(start_ref, loff_ref, len_ref, x_ref, idx_ref, gate_ref, meta_ref, g_ref, ys_ref, o_ref,
                  ybuf, tri_scr, sem):
    i = pl.program_id(0)
    tt = x_ref.shape[0]
    slot = i % 2

    def fetch(tile, buf):
        def run_piece(cond, brow, srow, p):
            cp = pltpu.make_async_copy(ys_ref.at[pl.ds(srow, p)], ybuf.at[buf, pl.ds(brow, p)], sem.at[buf])
            pl.when(cond)(cp.start)

        _for_each_run_piece(tile, start_ref, loff_ref, len_ref, run_piece, run_piece)

    @pl.when(i == 0)
    def _():
        r = lax.broadcasted_iota(jnp.int32, (tt, tt), 0)
        c = lax.broadcasted_iota(jnp.int32, (tt, tt), 1)
        tri_scr[...] = jnp.where(c < r, 1.0, 0.0).astype(BF16)
        fetch(i, slot)

    pl.when(i + 1 < pl.num_programs(0))(lambda: fetch(i + 1, 1 - slot))

    lane = lax.broadcasted_iota(jnp.int32, (tt, LANES), 1)
    idx = idx_ref[...]
    gt = gate_ref[...]
    cols = lax.broadcasted_iota(jnp.int32, (tt, ybuf.shape[1]), 1).astype(F32)
    sel = None
    for k in range(2):
        oh = jnp.where(lane == idx[:, k:k + 1], 1.0, 0.0)
        rank = jnp.dot(tri_scr[...], oh.astype(BF16), preferred_element_type=F32)
        base = meta_ref[0:1, :] if k == 0 else meta_ref[0:1, :] + meta_ref[1:2, :]
        pos = jnp.sum(oh * (base + rank), axis=1, keepdims=True)
        term = jnp.where(cols == pos, gt[:, k:k + 1], 0.0)
        sel = term if sel is None else sel + term
    sel = sel.astype(BF16)

    pltpu.make_async_copy(ys_ref.at[pl.ds(0, CBUF_ROWS)], ybuf.at[slot], sem.at[slot]).wait()

    lo, hi = _unpack_bf16_pairs(ybuf[slot])
    y = jnp.concatenate([jnp.dot(sel, lo, preferred_element_type=F32),
                         jnp.dot(sel, hi, preferred_element_type=F32)], axis=1)
    o_ref[...] = _rms(x_ref[...] + y, g_ref[...])


def _combine(plan, x2, idx, gates, g_final, ys):
    n = x2.shape[0]
    tt = MOE_TT
    grid_spec = pltpu.PrefetchScalarGridSpec(
        num_scalar_prefetch=3,
        grid=(n // tt,),
        in_specs=[
            pl.BlockSpec((tt, D_MODEL), lambda i, *_: (i, 0)),
            pl.BlockSpec((tt, LANES), lambda i, *_: (i, 0)),
            pl.BlockSpec((tt, LANES), lambda i, *_: (i, 0)),
            pl.BlockSpec((None, 8, LANES), lambda i, *_: (i, 0, 0)),
            pl.BlockSpec((1, D_MODEL), lambda i, *_: (0, 0)),
            pl.BlockSpec(memory_space=pl.ANY),
        ],
        out_specs=pl.BlockSpec((tt, D_MODEL), lambda i, *_: (i, 0)),
        scratch_shapes=[
            pltpu.VMEM((2, CBUF_ROWS, D_MODEL // 2), jnp.uint32),
            pltpu.VMEM((tt, tt), BF16),
            pltpu.SemaphoreType.DMA((2,)),
        ],
    )
    return pl.pallas_call(
        _combine_body,
        grid_spec=grid_spec,
        out_shape=jax.ShapeDtypeStruct((n, D_MODEL), F32),
        compiler_params=_cparams(("arbitrary",)),
        name="moe_combine",
    )(plan["start"], plan["loff"], plan["len8"], x2, idx, gates, plan["meta_rows"], g_final, ys)


def _route_plan(idxt, n):
    nt = n // MOE_TT
    e2 = idxt[:2].reshape(2, nt, MOE_TT)
    oh = (e2[..., None] == jnp.arange(N_EXPERTS, dtype=jnp.int32)).astype(jnp.int32)
    cnt = jnp.sum(oh, axis=2)
    cnt0 = cnt[0]
    len8 = (cnt[0] + cnt[1] + RUN_ALIGN - 1) // RUN_ALIGN * RUN_ALIGN
    loff = jnp.cumsum(len8, axis=1) - len8
    region = jnp.sum(len8, axis=0)
    padded = (region + MOE_BM - 1) // MOE_BM * MOE_BM
    pad_end = jnp.cumsum(padded)
    start = (pad_end - padded)[None, :] + jnp.cumsum(len8, axis=0) - len8
    n_blocks = (2 * n + nt * N_EXPERTS * (RUN_ALIGN - 1) + MOE_BM - 1) // MOE_BM + N_EXPERTS
    starts = jnp.arange(n_blocks, dtype=jnp.int32) * MOE_BM
    block_e = jnp.sum((starts[:, None] >= pad_end[None, :]).astype(jnp.int32), axis=1)
    block_e = jnp.minimum(block_e, N_EXPERTS - 1).astype(jnp.int32)
    n_used = (pad_end[-1] // MOE_BM).astype(jnp.int32).reshape(1)
    tails = jnp.where(padded > 0, pad_end - MOE_BM, -1)
    spare = pad_end[-1] + jnp.arange(n_blocks - (2 * n) // MOE_BM, dtype=pad_end.dtype) * MOE_BM
    spare = jnp.where(spare < n_blocks * MOE_BM, spare, -1)
    lanes = lambda a: jnp.broadcast_to(a.astype(F32)[:, :, None], (nt, N_EXPERTS, LANES))
    meta_rows = jnp.zeros((nt, 8, LANES), F32)
    meta_rows = meta_rows.at[:, 0, :N_EXPERTS].set(loff.astype(F32)).at[:, 1, :N_EXPERTS].set(cnt0.astype(F32))
    flat = lambda a: a.reshape(-1).astype(jnp.int32)
    return dict(start=flat(start), loff=flat(loff), len8=flat(len8), block_e=block_e, n_used=n_used,
                tail_rows=jnp.concatenate([tails, spare]).astype(jnp.int32),
                off_lanes=lanes(loff), cnt0_lanes=lanes(cnt0), meta_rows=meta_rows,
                n_slots=n_blocks * MOE_BM)


def _mixer_layer(x2, batch, seq, rel_bias, norm_g, w_in_bf, ssm, d_skip, w_glu, b_glu,
                 w_attn_br, w_ssm_br, w_out):
    *qkvs, gates, ut = _in_projection(x2, norm_g.reshape(1, D_MODEL), w_in_bf, batch, seq)

    os_, ls_ = [], []
    for g, (window, dilation) in enumerate(ATTN_GROUPS):
        bias = _band_bias(rel_bias[:, g * HEADS:(g + 1) * HEADS], window, dilation)
        o, l = _attention_group(qkvs[g], bias, g)
        os_.append(o)
        ls_.append(l)

    zt = _ssm_scan(ut, d_skip, _ssm_tables(*ssm), batch)

    return _merge(os_, ls_, zt, gates, x2, w_glu.astype(BF16), b_glu.reshape(1, SSM_W).astype(F32),
                  w_attn_br.astype(BF16), w_ssm_br.astype(BF16), w_out.astype(BF16))


def kernel(x, rel_bias, norm1_g, w_in, ssm_lam_re, ssm_lam_im, ssm_log_dt, ssm_b_re, ssm_b_im, ssm_c_re, ssm_c_im, ssm_d, w_glu, b_glu, w_attn_br, w_ssm_br, w_out, norm2_g, ffn_w_gate, ffn_w_up, ffn_w_down, moe_router, moe_w_gate, moe_w_up, moe_w_down, final_norm_g):
    batch, seq, d = x.shape
    assert d == D_MODEL and norm1_g.shape[0] == 2 and seq % (16 * BLK) == 0
    n = batch * seq
    x2 = x.reshape(n, d)

    def mixer(x2, l, w_in_bf):
        ssm = (ssm_lam_re[l], ssm_lam_im[l], ssm_log_dt[l], ssm_b_re[l], ssm_b_im[l],
               ssm_c_re[l], ssm_c_im[l])
        return _mixer_layer(x2, batch, seq, rel_bias, norm1_g[l], w_in_bf, ssm, ssm_d[l], w_glu[l],
                            b_glu[l], w_attn_br[l], w_ssm_br[l], w_out[l])

    x2 = mixer(x2, 0, w_in[0].astype(BF16))
    moe_w = (moe_w_gate[0], moe_w_up[0], moe_w_down[0])
    x2, (w_in1_bf, *moe_bf) = _dense_ffn(
        x2, norm2_g[0].reshape(1, d), ffn_w_gate[0].astype(BF16), ffn_w_up[0].astype(BF16),
        ffn_w_down[0].astype(BF16), to_cast=[w_in[1]] + [w.reshape(-1, w.shape[2]) for w in moe_w])
    moe_bf = [b.reshape(w.shape) for b, w in zip(moe_bf, moe_w)]

    x2 = mixer(x2, 1, w_in1_bf)
    g2 = norm2_g[1].reshape(1, d)
    wr_pad = jnp.zeros((d, LANES), F32).at[:, :N_EXPERTS].set(moe_router[0].astype(F32))
    idx, gates, idxt = _router(x2, g2, wr_pad)
    plan = _route_plan(idxt, n)
    xs = _dispatch(plan, x2, g2, idxt)
    ys = _experts(plan["block_e"], plan["n_used"], xs, *moe_bf)
    out = _combine(plan, x2, idx, gates, final_norm_g.reshape(1, d), ys)
    return out.reshape(batch, seq, d)
```

```python
import functools

import numpy as np
import jax
import jax.numpy as jnp
from jax import lax
from jax.experimental import pallas as pl
from jax.experimental.pallas import tpu as pltpu

F32 = jnp.float32
BF16 = jnp.bfloat16

LANES = 128
BF16_ROWS = 16

D_MODEL = 1024
HEAD_DIM = 64
ATTN_GROUPS = ((128, 1), (512, 4), (2048, 16))
N_GROUPS = 3
HEADS = 8
GROUP_W = HEADS * HEAD_DIM
ATTN_W = N_GROUPS * GROUP_W
BLK = 128
ATTN_ROWS = 2048
REL_BUCKETS = 32
REL_MAX_DIST = 2048
NEG_INF = -1e30
LOG2E = 1.4426950408889634
SSM_CH = 16
SSM_W = D_MODEL // 2
SSM_G = SSM_W // SSM_CH
SSM_P = 64
N_EXPERTS = 8
MOE_BM = 512
MOE_TT = 512
RUN_ALIGN = 8
RUN_PIECES = tuple(1 << b for b in range(MOE_TT.bit_length() - 1, RUN_ALIGN.bit_length() - 2, -1))
CBUF_ROWS = -(-(2 * MOE_TT + N_EXPERTS * (RUN_ALIGN - 1)) // BF16_ROWS) * BF16_ROWS
FILL_ROWS = CBUF_ROWS - 2 * MOE_TT
FILL_PIECES = tuple(1 << b for b in range(FILL_ROWS.bit_length() - 1, RUN_ALIGN.bit_length() - 2, -1))
EPS = 1e-6
CHUNK = 128
SCAN_LEVELS = 8

VMEM_LIMIT = 56 * 1024 * 1024


def _cparams(sem):
    return pltpu.CompilerParams(dimension_semantics=sem, vmem_limit_bytes=VMEM_LIMIT)


def _rms(x, g):
    return x * lax.rsqrt(jnp.mean(x * x, axis=-1, keepdims=True) + EPS) * g


def _proj_body(x_ref, g_ref, w_ref, qkv0_ref, qkv1_ref, qkv2_ref, gate_ref, ut_ref, d_scr):
    tm = x_ref.shape[0]
    u_lo = 3 * ATTN_W
    h = _rms(x_ref[...], g_ref[...])
    hb = h.astype(BF16)
    u = jnp.dot(hb, w_ref[:, u_lo:u_lo + SSM_W], preferred_element_type=F32)
    for k in range(ut_ref.shape[0]):
        ut_ref[k] = u[k * CHUNK:(k + 1) * CHUNK, :].T

    nl = D_MODEL // LANES
    for k in range(nl):
        d_scr[k] = h[:, k * LANES:(k + 1) * LANES]

    def by_subsequence(r):
        blocks = [jnp.concatenate([d_scr.at[k][pl.ds(c, tm // r, stride=r), :] for k in range(nl)], axis=1)
                  for c in range(r)]
        return jnp.concatenate(blocks, axis=0).astype(BF16)

    cw = 2 * LANES
    scale = (HEAD_DIM ** -0.5 * LOG2E, None, None)
    for g, (out_ref, (_, r)) in enumerate(zip((qkv0_ref, qkv1_ref, qkv2_ref), ATTN_GROUPS)):
        lhs = hb if r == 1 else by_subsequence(r)
        for which in range(3):
            for lo in range(0, GROUP_W, cw):
                col = which * ATTN_W + g * GROUP_W + lo
                res = jnp.dot(lhs, w_ref[:, col:col + cw], preferred_element_type=F32)
                if scale[which] is not None:
                    res = res * scale[which]
                res = res.astype(BF16)
                for c in range(r):
                    out_ref[which, c, :, lo:lo + cw] = res[c * (tm // r):(c + 1) * (tm // r), :]
    for lo in range(0, 2 * D_MODEL, cw):
        col = u_lo + SSM_W + lo
        gate_ref[:, lo:lo + cw] = jnp.dot(hb, w_ref[:, col:col + cw], preferred_element_type=F32).astype(BF16)


def _in_projection(x2, g, w_bf, batch, seq, tm=512):
    n = x2.shape[0]
    tiles_per_seq = seq // tm
    wcols = w_bf.shape[1]

    def qkv_spec(r):
        return pl.BlockSpec((3, None, r, tm // r, GROUP_W),
                            lambda i: (0, i // tiles_per_seq, 0, i % tiles_per_seq, 0))

    return pl.pallas_call(
        _proj_body,
        grid=(n // tm,),
        in_specs=[
            pl.BlockSpec((tm, D_MODEL), lambda i: (i, 0)),
            pl.BlockSpec((1, D_MODEL), lambda i: (0, 0)),
            pl.BlockSpec((D_MODEL, wcols), lambda i: (0, 0), pipeline_mode=pl.Buffered(1)),
        ],
        out_specs=[qkv_spec(r) for _, r in ATTN_GROUPS] + [
            pl.BlockSpec((tm, 2 * D_MODEL), lambda i: (i, 0)),
            pl.BlockSpec((tm // CHUNK, SSM_W, CHUNK), lambda i: (i, 0, 0)),
        ],
        out_shape=[jax.ShapeDtypeStruct((3, batch, r, seq // r, GROUP_W), BF16) for _, r in ATTN_GROUPS] + [
            jax.ShapeDtypeStruct((n, 2 * D_MODEL), BF16),
            jax.ShapeDtypeStruct((n // CHUNK, SSM_W, CHUNK), F32),
        ],
        scratch_shapes=[pltpu.VMEM((D_MODEL // LANES, tm, LANES), F32)],
        compiler_params=_cparams(("parallel",)),
        name="in_projection",
    )(x2, g, w_bf)


def _t5_bucket(dist):
    max_exact = REL_BUCKETS // 2
    d = np.maximum(dist, 1).astype(np.float64)
    large = max_exact + (
        np.log(d / max_exact) / np.log(REL_MAX_DIST / max_exact) * (REL_BUCKETS - max_exact)
    ).astype(np.int32)
    large = np.minimum(large, REL_BUCKETS - 1)
    return np.where(dist < max_exact, dist, large).astype(np.int32)


def _band_bias(table, window, dilation):
    steps = window // dilation
    qi = np.arange(BLK)[:, None]
    kj = np.arange(2 * BLK)[None, :]
    delta = BLK + qi - kj
    band = (delta >= 0) & (delta <= steps)
    bucket = _t5_bucket(np.clip(delta, 0, steps) * dilation)
    onehot = np.eye(REL_BUCKETS, dtype=np.float32)[bucket]
    bias = jnp.einsum("qkb,bh->hqk", onehot, table.astype(F32), precision=lax.Precision.HIGHEST)
    return jnp.where(band[None], bias * LOG2E, NEG_INF)


def _attn_body(q_ref, kp_ref, kc_ref, vp_ref, vc_ref, bias_ref, o_ref, l_ref, *, nsub):
    lane = lax.broadcasted_iota(jnp.int32, (BLK, LANES), 1)
    lo = lane < HEAD_DIM
    keep_lo = jnp.where(lo, 1.0, 0.0).astype(BF16)
    keep_hi = jnp.where(lo, 0.0, 1.0).astype(BF16)
    col = lax.broadcasted_iota(jnp.int32, (BLK, 2 * BLK), 1)
    first_pen = jnp.where(col < BLK, jnp.where(pl.program_id(2) == 0, NEG_INF, 0.0), 0.0)
    for c, i in [(c, i) for c in range(q_ref.shape[0]) for i in range(nsub)]:
        rows = (c, slice(i * BLK, (i + 1) * BLK))
        q = q_ref[rows]
        if i == 0:
            kw = jnp.concatenate([kp_ref[c], kc_ref[c, 0:BLK, :]], axis=0)
            vw = jnp.concatenate([vp_ref[c], vc_ref[c, 0:BLK, :]], axis=0)
        else:
            kw = kc_ref[c, (i - 1) * BLK:(i + 1) * BLK, :]
            vw = vc_ref[c, (i - 1) * BLK:(i + 1) * BLK, :]
        for hp in range(HEADS // 2):
            cols = slice(hp * LANES, (hp + 1) * LANES)
            q2, k2, v2 = q[:, cols], kw[:, cols], vw[:, cols]
            pvs, ms, dens = [], [], []
            for half in range(2):
                qm = q2 * (keep_lo, keep_hi)[half]
                s = lax.dot_general(qm, k2, (((1,), (1,)), ((), ())), preferred_element_type=F32)
                s = s + bias_ref[2 * hp + half]
                if i == 0:
                    s = s + first_pen
                m = jnp.max(s, axis=-1, keepdims=True)
                p = jnp.exp2(s - m)
                dens.append(jnp.broadcast_to(jnp.sum(p, axis=-1, keepdims=True), (BLK, LANES)))
                ms.append(jnp.broadcast_to(m, (BLK, LANES)))
                pvs.append(jnp.dot(p.astype(BF16), v2, preferred_element_type=F32))
            den = jnp.where(lo, dens[0], dens[1])
            o_ref[rows + (cols,)] = (jnp.where(lo, pvs[0], pvs[1]) / den).astype(BF16)
            l_ref[rows + (cols,)] = jnp.where(lo, ms[0], ms[1]) + jnp.log2(den)


def _attention_group(qkv, bias, g):
    _, batch, r, length, _ = qkv.shape
    qb = min(ATTN_ROWS, length)
    cb = min(ATTN_ROWS // qb, r)
    nsub = qb // BLK

    def cur(which):
        return pl.BlockSpec((None, None, cb, qb, GROUP_W), lambda b, c, n: (which, b, c, n, 0))

    def prev(which):
        return pl.BlockSpec((None, None, cb, BLK, GROUP_W),
                            lambda b, c, n: (which, b, c, jnp.maximum(n * nsub - 1, 0), 0))

    out_spec = pl.BlockSpec((None, cb, qb, GROUP_W), lambda b, c, n: (b, c, n, 0))
    return pl.pallas_call(
        functools.partial(_attn_body, nsub=nsub),
        grid=(batch, r // cb, length // qb),
        in_specs=[cur(0), prev(1), cur(1), prev(2), cur(2),
                  pl.BlockSpec((HEADS, BLK, 2 * BLK), lambda b, c, n: (0, 0, 0))],
        out_specs=[out_spec, out_spec],
        out_shape=[
            jax.ShapeDtypeStruct((batch, r, length, GROUP_W), BF16),
            jax.ShapeDtypeStruct((batch, r, length, GROUP_W), F32),
        ],
        compiler_params=_cparams(("parallel", "parallel", "arbitrary")),
        name=f"attention_g{g}",
    )(qkv, qkv, qkv, qkv, qkv, bias)


def _ssm_tables(lam_re, lam_im, log_dt, b_re, b_im, c_re, c_im):
    lam = lax.complex(lam_re.astype(F32), lam_im.astype(F32))
    dt = jnp.exp(log_dt.astype(F32))[:, None]
    lam_dt = lam * dt
    lam_bar = jnp.exp(lam_dt)
    b = lax.complex(b_re.astype(F32), b_im.astype(F32))
    b_bar = ((lam_bar - 1.0) / lam)[..., None] * b
    half = CHUNK // 2
    t = jnp.arange(CHUNK, dtype=F32)

    def power(k):
        return jnp.exp(lam_dt[:, None, :] * jnp.reshape(jnp.asarray(k, F32), (1, -1, 1)))

    p_fwd = power(t - half)
    p_bwd = power(half - t)

    def in_pair(pw):
        return jnp.stack([jnp.concatenate([pw.real, pw.imag], axis=-1),
                          jnp.concatenate([-pw.imag, pw.real], axis=-1)], axis=1)

    def out_pair(pw):
        return jnp.stack([jnp.concatenate([pw.real, -pw.imag], axis=-1),
                          jnp.concatenate([-pw.imag, -pw.real], axis=-1)], axis=1)

    powers = jnp.stack([
        in_pair(p_bwd),
        in_pair(p_bwd * power(CHUNK - 1.0 - half)),
        out_pair(p_fwd),
        out_pair(p_fwd * power(half + 1.0)),
    ], axis=1)
    dup = lambda a: jnp.concatenate([a, a], axis=-1)
    b_cp = jnp.transpose(b_bar, (0, 2, 1))
    coefs = jnp.stack([dup(b_cp.real), dup(b_cp.imag), dup(c_re.astype(F32)), dup(c_im.astype(F32))],
                      axis=1)
    lc = power([float(CHUNK * 2 ** k) for k in range(SCAN_LEVELS)])
    l1 = jnp.concatenate([lc.real, lc.real], axis=-1)
    l2 = jnp.concatenate([-lc.imag, lc.imag], axis=-1)
    return powers, coefs, l1, l2


def _gelu_tanh(y):
    return y * jax.nn.sigmoid(1.5957691216057308 * (y + 0.044715 * (y * y * y)))


def _ssm_body(d_ref, u_ref, pw_ref, cf_ref, l1_ref, l2_ref, z_ref, mask_scr, z_scr, a_scr, w_scr, *,
              ncb):
    g = pl.program_id(0)
    nc = u_ref.shape[0]
    width = SSM_CH * CHUNK
    cb = mask_scr.shape[1]

    @pl.when(g == 0)
    def _():
        s_idx = lax.broadcasted_iota(jnp.int32, (width, cb), 0) & (CHUNK - 1)
        t_idx = lax.broadcasted_iota(jnp.int32, (width, cb), 1) & (CHUNK - 1)
        mask_scr[...] = jnp.where(t_idx >= s_idx, -1, 0).astype(jnp.int32)

    def expand(kind, c1, c2, ch):
        return (cf_ref[c1, ch:ch + 1, :] * pw_ref[kind, 0] + cf_ref[c2, ch:ch + 1, :] * pw_ref[kind, 1])

    for ch in range(SSM_CH):
        rows = slice(ch * CHUNK, (ch + 1) * CHUNK)
        a_scr[rows, :] = expand(0, 0, 1, ch).astype(BF16)
        w_scr[rows, :] = expand(1, 0, 1, ch).astype(BF16)

    u2 = u_ref.reshape(nc * SSM_CH, CHUNK)
    us =[u2[pl.ds(c, nc, stride=SSM_CH), :] for c in range(SSM_CH)]
    x = jnp.concatenate(us, axis=1).astype(BF16)

    acc = jnp.dot(x, w_scr[...], preferred_element_type=F32)
    rmod = lax.broadcasted_iota(jnp.int32, (nc, 2 * SSM_P), 0) & (ncb - 1)
    for k in range(ncb.bit_length() - 1):
        d = 1 << k
        sh = jnp.where(rmod >= d, pltpu.roll(acc, d, 0), 0.0)
        acc = acc + sh * l1_ref[k:k + 1, :] + pltpu.roll(sh, SSM_P, 1) * l2_ref[k:k + 1, :]
    x_in = jnp.where(rmod >= 1, pltpu.roll(acc, 1, 0), 0.0).astype(BF16)

    per = cb // CHUNK
    for k in range(width // cb):
        chans = range(k * per, (k + 1) * per)
        d_rhs = jnp.concatenate([expand(2, 2, 3, ch).T.astype(BF16) for ch in chans], axis=1)
        v_rhs = jnp.concatenate([expand(3, 2, 3, ch).T.astype(BF16) for ch in chans], axis=1)
        mk = jnp.dot(a_scr[...], d_rhs, preferred_element_type=F32)
        kept = lax.bitcast_convert_type(mk, jnp.int32) & mask_scr[...]
        m_k = lax.bitcast_convert_type(kept, F32).astype(BF16)
        y = (jnp.dot(x, m_k, preferred_element_type=F32)
             + jnp.dot(x_in, v_rhs, preferred_element_type=F32))
        for j, c in enumerate(chans):
            yc = y[:, j * CHUNK:(j + 1) * CHUNK] + d_ref[g * SSM_CH + c] * us[c]
            z_scr[pl.ds(c, nc, stride=SSM_CH), :] = _gelu_tanh(yc)
    z_ref[...] = z_scr[...].reshape(nc, SSM_CH, CHUNK)


def _ssm_scan(u3, d_skip, tables, nbatch):
    powers, coefs, l1, l2 = tables
    nc = u3.shape[0]
    ncb = nc // nbatch
    assert ncb & (ncb - 1) == 0 and ncb <= 2 ** SCAN_LEVELS
    width = SSM_CH * CHUNK
    grid_spec = pltpu.PrefetchScalarGridSpec(
        num_scalar_prefetch=1,
        grid=(SSM_G,),
        in_specs=[
            pl.BlockSpec((nc, SSM_CH, CHUNK), lambda g, d: (0, g, 0)),
            pl.BlockSpec((None,) + powers.shape[1:], lambda g, d: (g, 0, 0, 0, 0)),
            pl.BlockSpec((None,) + coefs.shape[1:], lambda g, d: (g, 0, 0, 0)),
            pl.BlockSpec((None, SCAN_LEVELS, 2 * SSM_P), lambda g, d: (g, 0, 0)),
            pl.BlockSpec((None, SCAN_LEVELS, 2 * SSM_P), lambda g, d: (g, 0, 0)),
        ],
        out_specs=pl.BlockSpec((nc, SSM_CH, CHUNK), lambda g, d: (0, g, 0)),
        scratch_shapes=[pltpu.VMEM((width, 4 * CHUNK), jnp.int32),
                        pltpu.VMEM((nc * SSM_CH, CHUNK), F32),
                        pltpu.VMEM((width, 2 * SSM_P), BF16), pltpu.VMEM((width, 2 * SSM_P), BF16)],
    )
    return pl.pallas_call(
        functools.partial(_ssm_body, ncb=ncb),
        grid_spec=grid_spec,
        out_shape=jax.ShapeDtypeStruct((nc, SSM_W, CHUNK), F32),
        compiler_params=_cparams(("arbitrary",)),
        name="ssm_scan",
    )(d_skip.astype(F32), u3, powers, coefs, l1, l2)


def _merge_body(o0, o1, o2, l0, l1, l2, zt_ref, ga_ref, gs_ref, x_ref,
                wglu_ref, bglu_ref, wab_ref, wsb_ref, wout_ref, out_ref, tok_scr):
    def token_major(ref, slot):
        r, rows, _ = ref.shape
        if r == 1:
            return ref[0].astype(F32)
        nl = GROUP_W // LANES
        scrs = [tok_scr.at[slot * nl + k] for k in range(nl)]
        for c in range(r):
            sub = ref[c].astype(F32)
            for k in range(nl):
                scrs[k][pl.ds(c, rows, stride=r), :] = sub[:, k * LANES:(k + 1) * LANES]
        return jnp.concatenate([s[...] for s in scrs], axis=1)

    a0, a1, a2 = token_major(l0, 0), token_major(l1, 0), token_major(l2, 1)
    v0, v1, v2 = token_major(o0, 0), token_major(o1, 2), token_major(o2, 3)
    mx = jnp.maximum(jnp.maximum(a0, a1), a2)
    e0, e1, e2 = jnp.exp2(a0 - mx), jnp.exp2(a1 - mx), jnp.exp2(a2 - mx)
    mix = (e0 * v0 + e1 * v1 + e2 * v2) / (e0 + e1 + e2)
    y_attn = jnp.dot(mix.astype(BF16), wab_ref[...], preferred_element_type=F32)

    z = jnp.concatenate([zt_ref[k].T for k in range(zt_ref.shape[0])], axis=0).astype(BF16)
    gl = jnp.dot(z, wglu_ref[...], preferred_element_type=F32) + bglu_ref[...]
    sg = z.astype(F32) * jax.nn.sigmoid(gl)
    y_ssm = jnp.dot(sg.astype(BF16), wsb_ref[...], preferred_element_type=F32)

    merged = (jax.nn.sigmoid(ga_ref[...].astype(F32)) * y_attn
              + jax.nn.sigmoid(gs_ref[...].astype(F32)) * y_ssm)
    out_ref[...] = x_ref[...] + jnp.dot(merged.astype(BF16), wout_ref[...], preferred_element_type=F32)


def _merge(os_, ls_, zt, gates, x2, wglu, bglu, wab, wsb, wout, tm=512):
    n = x2.shape[0]
    tiles_per_seq = os_[0].shape[2] // tm
    row = lambda i: (i, 0)
    const = lambda i: (0, 0)

    def group_spec(a):
        r = a.shape[1]
        return pl.BlockSpec((None, r, tm // r, GROUP_W),
                            lambda i: (i // tiles_per_seq, 0, i % tiles_per_seq, 0))

    in_specs = (
        [group_spec(a) for a in os_] + [group_spec(a) for a in ls_]
        + [
            pl.BlockSpec((tm // CHUNK, SSM_W, CHUNK), lambda i: (i, 0, 0)),
            pl.BlockSpec((tm, D_MODEL), lambda i: (i, 0)),
            pl.BlockSpec((tm, D_MODEL), lambda i: (i, 1)),
            pl.BlockSpec((tm, D_MODEL), row),
            pl.BlockSpec((SSM_W, SSM_W), const),
            pl.BlockSpec((1, SSM_W), const),
            pl.BlockSpec((GROUP_W, D_MODEL), const),
            pl.BlockSpec((SSM_W, D_MODEL), const),
            pl.BlockSpec((D_MODEL, D_MODEL), const),
        ]
    )
    return pl.pallas_call(
        _merge_body,
        grid=(n // tm,),
        in_specs=in_specs,
        out_specs=pl.BlockSpec((tm, D_MODEL), row),
        out_shape=jax.ShapeDtypeStruct((n, D_MODEL), F32),
        scratch_shapes=[pltpu.VMEM((4 * GROUP_W // LANES, tm, LANES), F32)],
        compiler_params=_cparams(("parallel",)),
        name="merge",
    )(*os_, *ls_, zt, gates, gates, x2, wglu, bglu, wab, wsb, wout)


def _ffn_body(x_ref, g_ref, wg_ref, wu_ref, wd_ref, *rest, tf):
    ncast = (len(rest) - 1) // 2
    o_ref = rest[ncast]
    for src, dst in zip(rest[:ncast], rest[ncast + 1:]):
        dst[...] = src[...].astype(BF16)
    h = _rms(x_ref[...], g_ref[...]).astype(BF16)
    for f in range(wg_ref.shape[1] // tf):
        cols = slice(f * tf, (f + 1) * tf)
        a = jnp.dot(h, wg_ref[:, cols], preferred_element_type=F32)
        b = jnp.dot(h, wu_ref[:, cols], preferred_element_type=F32)
        act = (a * jax.nn.sigmoid(a) * b).astype(BF16)
        part = jnp.dot(act, wd_ref[cols, :], preferred_element_type=F32)
        if f == 0:
            o_ref[...] = x_ref[...] + part
        else:
            o_ref[...] += part


def _dense_ffn(x2, g, wg, wu, wd, to_cast=(), tm=512, tf=256):
    n = x2.shape[0]
    dff = wg.shape[1]
    steps = n // tm
    assert dff % tf == 0 and all(a.shape[0] % (BF16_ROWS * steps) == 0 for a in to_cast)
    resident = lambda shape: pl.BlockSpec(shape, lambda i: (0, 0), pipeline_mode=pl.Buffered(1))
    slabs = [pl.BlockSpec((a.shape[0] // steps, a.shape[1]), lambda i: (i, 0)) for a in to_cast]
    out, *casts = pl.pallas_call(
        functools.partial(_ffn_body, tf=tf),
        grid=(steps,),
        in_specs=[
            pl.BlockSpec((tm, D_MODEL), lambda i: (i, 0)),
            pl.BlockSpec((1, D_MODEL), lambda i: (0, 0)),
            resident((D_MODEL, dff)),
            resident((D_MODEL, dff)),
            resident((dff, D_MODEL)),
        ] + slabs,
        out_specs=[pl.BlockSpec((tm, D_MODEL), lambda i: (i, 0))] + slabs,
        out_shape=[jax.ShapeDtypeStruct((n, D_MODEL), F32)]
        + [jax.ShapeDtypeStruct(a.shape, BF16) for a in to_cast],
        compiler_params=_cparams(("parallel",)),
        name="dense_ffn",
    )(x2, g, wg, wu, wd, *to_cast)
    return out, casts


def _route_top2(h, w, idx_ref, gate_ref, idxt_ref):
    h_hi, w_hi = h.astype(BF16), w.astype(BF16)
    h_lo = (h - h_hi.astype(F32)).astype(BF16)
    w_lo = (w - w_hi.astype(F32)).astype(BF16)
    both = jnp.dot(h_hi, jnp.concatenate([w_hi, w_lo], axis=1), preferred_element_type=F32)
    logits = both[:, :LANES] + (both[:, LANES:] + jnp.dot(h_lo, w_hi, preferred_element_type=F32))
    lane = lax.broadcasted_iota(jnp.int32, logits.shape, 1)
    lane_f = lane.astype(F32)
    logits = jnp.where(lane < N_EXPERTS, logits, -jnp.inf)
    v1 = jnp.max(logits, axis=-1, keepdims=True)
    i1 = jnp.min(jnp.where(logits == v1, lane_f, float(LANES)), axis=-1, keepdims=True)
    rest = jnp.where(lane_f == i1, -jnp.inf, logits)
    v2 = jnp.max(rest, axis=-1, keepdims=True)
    i2 = jnp.min(jnp.where(rest == v2, lane_f, float(LANES)), axis=-1, keepdims=True)
    e = jnp.exp(v2 - v1)
    g1 = 1.0 / (1.0 + e)
    g2 = e / (1.0 + e)
    idx_f = jnp.where(lane == 0, i1, jnp.where(lane == 1, i2, 0.0))
    idx_ref[...] = idx_f.astype(jnp.int32)
    gate_ref[...] = jnp.where(lane == 0, g1, jnp.where(lane == 1, g2, 0.0))
    idxt_ref[...] = idx_f.T[:idxt_ref.shape[0], :].astype(jnp.int32)


def _router_body(x_ref, g_ref, wr_ref, idx_ref, gate_ref, idxt_ref):
    _route_top2(_rms(x_ref[...], g_ref[...]), wr_ref[...], idx_ref, gate_ref, idxt_ref)


def _router(x2, g, wr_pad, tm=1024):
    n = x2.shape[0]
    return pl.pallas_call(
        _router_body,
        grid=(n // tm,),
        in_specs=[
            pl.BlockSpec((tm, D_MODEL), lambda i: (i, 0)),
            pl.BlockSpec((1, D_MODEL), lambda i: (0, 0)),
            pl.BlockSpec((D_MODEL, LANES), lambda i: (0, 0)),
        ],
        out_specs=[
            pl.BlockSpec((tm, LANES), lambda i: (i, 0)),
            pl.BlockSpec((tm, LANES), lambda i: (i, 0)),
            pl.BlockSpec((8, tm), lambda i: (0, i)),
        ],
        out_shape=[
            jax.ShapeDtypeStruct((n, LANES), jnp.int32),
            jax.ShapeDtypeStruct((n, LANES), F32),
            jax.ShapeDtypeStruct((8, n), jnp.int32),
        ],
        compiler_params=_cparams(("parallel",)),
        name="router",
    )(x2, g, wr_pad)


def _pack_bf16_pairs(hb):
    half = hb.shape[1] // 2
    lo = lax.bitcast_convert_type(hb[:, :half].astype(F32), jnp.uint32)
    hi = lax.bitcast_convert_type(hb[:, half:].astype(F32), jnp.uint32)
    return (hi & jnp.uint32(0xFFFF0000)) | (lo >> 16)


def _unpack_bf16_pairs(xu):
    lo = lax.bitcast_convert_type(xu << 16, F32).astype(BF16)
    hi = lax.bitcast_convert_type(xu & jnp.uint32(0xFFFF0000), F32).astype(BF16)
    return lo, hi


def _for_each_run_piece(i, start_ref, loff_ref, len_ref, fn, fill):
    for e in range(N_EXPERTS):
        j = i * N_EXPERTS + e
        length, boff, soff = len_ref[j], loff_ref[j], start_ref[j]
        done = 0
        for p in RUN_PIECES:
            cond = (length & p) != 0
            fn(cond, pl.multiple_of(boff + done, RUN_ALIGN), pl.multiple_of(soff + done, RUN_ALIGN), p)
            done = done + jnp.where(cond, p, 0)
    last = i * N_EXPERTS + N_EXPERTS - 1
    used = loff_ref[last] + len_ref[last]
    rest, done = CBUF_ROWS - used, 0
    for p in FILL_PIECES:
        cond = (rest & p) != 0
        fill(cond, pl.multiple_of(used + done, RUN_ALIGN), pl.multiple_of(done, RUN_ALIGN), p)
        done = done + jnp.where(cond, p, 0)


def _dispatch_body(start_ref, loff_ref, len_ref, tail_ref, x_ref, g_ref, idxt_ref, off_ref, cnt0_ref,
                   xs_ref, cbuf, tri_scr, zero_scr, sem, zsem):
    i = pl.program_id(0)
    tt = x_ref.shape[0]

    @pl.when(i == 0)
    def _():
        r = lax.broadcasted_iota(jnp.int32, (tt, tt), 0)
        c = lax.broadcasted_iota(jnp.int32, (tt, tt), 1)
        tri_scr[...] = jnp.where(r < c, 1.0, 0.0).astype(BF16)
        zero_scr[...] = jnp.zeros_like(zero_scr)

        def fill(e):
            row = pl.multiple_of(jnp.maximum(tail_ref[e], 0), MOE_BM)
            return pltpu.make_async_copy(zero_scr, xs_ref.at[pl.ds(row, MOE_BM)], zsem)

        for e in range(tail_ref.shape[0]):
            pl.when(tail_ref[e] >= 0)(lambda e=e: fill(e).start())
        for e in range(tail_ref.shape[0]):
            pl.when(tail_ref[e] >= 0)(lambda e=e: fill(e).wait())
        nspare = 2 * FILL_ROWS
        spare_fill = pltpu.make_async_copy(zero_scr.at[pl.ds(0, nspare)],
                                           xs_ref.at[pl.ds(xs_ref.shape[0] - nspare, nspare)], zsem)
        spare_fill.start()
        spare_fill.wait()

    hb = _rms(x_ref[...], g_ref[...]).astype(BF16)

    sub = lax.broadcasted_iota(jnp.int32, (N_EXPERTS, tt), 0)
    pos = []
    for k in range(2):
        oh = jnp.where(sub == idxt_ref[k:k + 1, :], 1.0, 0.0)
        rank = jnp.dot(oh.astype(BF16), tri_scr[...], preferred_element_type=F32)
        base = off_ref[:, :1] if k == 0 else off_ref[:, :1] + cnt0_ref[:, :1]
        pos.append(jnp.sum(oh * (base + rank), axis=0, keepdims=True))

    rows = lax.broadcasted_iota(jnp.int32, (cbuf.shape[1], tt), 0).astype(F32)
    perm = (jnp.where(rows == pos[0], 1.0, 0.0) + jnp.where(rows == pos[1], 1.0, 0.0)).astype(BF16)
    slot = i % 2
    cbuf[slot] = _pack_bf16_pairs(jnp.dot(perm, hb, preferred_element_type=F32).astype(BF16))

    spare = xs_ref.shape[0] - 2 * FILL_ROWS

    def run_piece(cond, brow, srow, p):
        cp = pltpu.make_async_copy(cbuf.at[slot, pl.ds(brow, p)], xs_ref.at[pl.ds(srow, p)], sem.at[slot])
        pl.when(cond)(cp.start)

    def fill_piece(cond, brow, frow, p):
        run_piece(cond, brow, pl.multiple_of(spare + slot * FILL_ROWS + frow, RUN_ALIGN), p)

    _for_each_run_piece(i, start_ref, loff_ref, len_ref, run_piece, fill_piece)

    def drain(buf):
        pltpu.make_async_copy(cbuf.at[buf], xs_ref.at[pl.ds(0, CBUF_ROWS)], sem.at[buf]).wait()

    pl.when(i > 0)(lambda: drain(1 - slot))
    pl.when(i == pl.num_programs(0) - 1)(lambda: drain(slot))


def _dispatch(plan, x2, g, idxt):
    n = x2.shape[0]
    tt = MOE_TT
    smem = lambda i, *_: (i, 0, 0)
    grid_spec = pltpu.PrefetchScalarGridSpec(
        num_scalar_prefetch=4,
        grid=(n // tt,),
        in_specs=[
            pl.BlockSpec((tt, D_MODEL), lambda i, *_: (i, 0)),
            pl.BlockSpec((1, D_MODEL), lambda i, *_: (0, 0)),
            pl.BlockSpec((8, tt), lambda i, *_: (0, i)),
            pl.BlockSpec((None, N_EXPERTS, LANES), smem),
            pl.BlockSpec((None, N_EXPERTS, LANES), smem),
        ],
        out_specs=pl.BlockSpec(memory_space=pl.ANY),
        scratch_shapes=[
            pltpu.VMEM((2, CBUF_ROWS, D_MODEL // 2), jnp.uint32),
            pltpu.VMEM((tt, tt), BF16),
            pltpu.VMEM((MOE_BM, D_MODEL // 2), jnp.uint32),
            pltpu.SemaphoreType.DMA((2,)),
            pltpu.SemaphoreType.DMA(()),
        ],
    )
    return pl.pallas_call(
        _dispatch_body,
        grid_spec=grid_spec,
        out_shape=jax.ShapeDtypeStruct((plan["n_slots"] + 2 * FILL_ROWS, D_MODEL // 2), jnp.uint32),
        compiler_params=_cparams(("arbitrary",)),
        name="moe_dispatch",
    )(plan["start"], plan["loff"], plan["len8"], plan["tail_rows"], x2, g, idxt,
      plan["off_lanes"], plan["cnt0_lanes"])


def _experts_body(be_ref, nu_ref, xs_ref, wg_ref, wu_ref, wd_ref, ys_ref, acc_scr, *, tf):
    del be_ref
    i = pl.program_id(0)
    half = D_MODEL // 2

    @pl.when(i < nu_ref[0])
    def _():
        lo, hi = _unpack_bf16_pairs(xs_ref[...])
        for f in range(wg_ref.shape[1] // tf):
            cols = slice(f * tf, (f + 1) * tf)
            a = (jnp.dot(lo, wg_ref[:half, cols], preferred_element_type=F32)
                 + jnp.dot(hi, wg_ref[half:, cols], preferred_element_type=F32))
            b = (jnp.dot(lo, wu_ref[:half, cols], preferred_element_type=F32)
                 + jnp.dot(hi, wu_ref[half:, cols], preferred_element_type=F32))
            act = (a * jax.nn.sigmoid(a) * b).astype(BF16)
            part = jnp.dot(act, wd_ref[cols, :], preferred_element_type=F32)
            if f == 0:
                acc_scr[...] = part
            else:
                acc_scr[...] += part
        ys_ref[...] = _pack_bf16_pairs(acc_scr[...].astype(BF16))

    @pl.when(i >= nu_ref[0])
    def _():
        ys_ref[...] = jnp.zeros_like(ys_ref)


def _experts(block_e, n_used, xs, wg, wu, wd, tf=512):
    n_blocks = xs.shape[0] // MOE_BM
    n_slots = n_blocks * MOE_BM
    dff = wg.shape[2]

    def blk(i, nu):
        return jnp.minimum(i, nu[0] - 1)

    def wspec(rows, cols):
        return pl.BlockSpec((None, rows, cols), lambda i, be, nu: (be[blk(i, nu)], 0, 0))

    grid_spec = pltpu.PrefetchScalarGridSpec(
        num_scalar_prefetch=2,
        grid=(n_blocks,),
        in_specs=[
            pl.BlockSpec((MOE_BM, D_MODEL // 2), lambda i, be, nu: (blk(i, nu), 0)),
            wspec(D_MODEL, dff),
            wspec(D_MODEL, dff),
            wspec(dff, D_MODEL),
        ],
        out_specs=pl.BlockSpec((MOE_BM, D_MODEL // 2), lambda i, be, nu: (i, 0)),
        scratch_shapes=[pltpu.VMEM((MOE_BM, D_MODEL), F32)],
    )
    return pl.pallas_call(
        functools.partial(_experts_body, tf=tf),
        grid_spec=grid_spec,
        out_shape=jax.ShapeDtypeStruct((n_slots, D_MODEL // 2), jnp.uint32),
        compiler_params=_cparams(("arbitrary",)),
        name="moe_experts",
    )(block_e, n_used, xs, wg, wu, wd)


def _combine_body(start_ref, loff_ref, len_ref, x_ref, idx_ref, gate_ref, meta_ref, g_ref, ys_ref, o_ref,
                  ybuf, tri_scr, sem):
    i = pl.program_id(0)
    tt = x_ref.shape[0]
    slot = i % 2

    def fetch(tile, buf):
        def run_piece(cond, brow, srow, p):
            cp = pltpu.make_async_copy(ys_ref.at[pl.ds(srow, p)], ybuf.at[buf, pl.ds(brow, p)], sem.at[buf])
            pl.when(cond)(cp.start)

        _for_each_run_piece(tile, start_ref, loff_ref, len_ref, run_piece, run_piece)

    @pl.when(i == 0)
    def _():
        r = lax.broadcasted_iota(jnp.int32, (tt, tt), 0)
        c = lax.broadcasted_iota(jnp.int32, (tt, tt), 1)
        tri_scr[...] = jnp.where(c < r, 1.0, 0.0).astype(BF16)
        fetch(i, slot)

    pl.when(i + 1 < pl.num_programs(0))(lambda: fetch(i + 1, 1 - slot))

    lane = lax.broadcasted_iota(jnp.int32, (tt, LANES), 1)
    idx = idx_ref[...]
    gt = gate_ref[...]
    cols = lax.broadcasted_iota(jnp.int32, (tt, ybuf.shape[1]), 1).astype(F32)
    sel = None
    for k in range(2):
        oh = jnp.where(lane == idx[:, k:k + 1], 1.0, 0.0)
        rank = jnp.dot(tri_scr[...], oh.astype(BF16), preferred_element_type=F32)
        base = meta_ref[0:1, :] if k == 0 else meta_ref[0:1, :] + meta_ref[1:2, :]
        pos = jnp.sum(oh * (base + rank), axis=1, keepdims=True)
        term = jnp.where(cols == pos, gt[:, k:k + 1], 0.0)
        sel = term if sel is None else sel + term
    sel = sel.astype(BF16)

    pltpu.make_async_copy(ys_ref.at[pl.ds(0, CBUF_ROWS)], ybuf.at[slot], sem.at[slot]).wait()

    lo, hi = _unpack_bf16_pairs(ybuf[slot])
    y = jnp.concatenate([jnp.dot(sel, lo, preferred_element_type=F32),
                         jnp.dot(sel, hi, preferred_element_type=F32)], axis=1)
    o_ref[...] = _rms(x_ref[...] + y, g_ref[...])


def _combine(plan, x2, idx, gates, g_final, ys):
    n = x2.shape[0]
    tt = MOE_TT
    grid_spec = pltpu.PrefetchScalarGridSpec(
        num_scalar_prefetch=3,
        grid=(n // tt,),
        in_specs=[
            pl.BlockSpec((tt, D_MODEL), lambda i, *_: (i, 0)),
            pl.BlockSpec((tt, LANES), lambda i, *_: (i, 0)),
            pl.BlockSpec((tt, LANES), lambda i, *_: (i, 0)),
            pl.BlockSpec((None, 8, LANES), lambda i, *_: (i, 0, 0)),
            pl.BlockSpec((1, D_MODEL), lambda i, *_: (0, 0)),
            pl.BlockSpec(memory_space=pl.ANY),
        ],
        out_specs=pl.BlockSpec((tt, D_MODEL), lambda i, *_: (i, 0)),
        scratch_shapes=[
            pltpu.VMEM((2, CBUF_ROWS, D_MODEL // 2), jnp.uint32),
            pltpu.VMEM((tt, tt), BF16),
            pltpu.SemaphoreType.DMA((2,)),
        ],
    )
    return pl.pallas_call(
        _combine_body,
        grid_spec=grid_spec,
        out_shape=jax.ShapeDtypeStruct((n, D_MODEL), F32),
        compiler_params=_cparams(("arbitrary",)),
        name="moe_combine",
    )(plan["start"], plan["loff"], plan["len8"], x2, idx, gates, plan["meta_rows"], g_final, ys)


def _route_plan(idxt, n):
    nt = n // MOE_TT
    e2 = idxt[:2].reshape(2, nt, MOE_TT)
    oh = (e2[..., None] == jnp.arange(N_EXPERTS, dtype=jnp.int32)).astype(jnp.int32)
    cnt = jnp.sum(oh, axis=2)
    cnt0 = cnt[0]
    len8 = (cnt[0] + cnt[1] + RUN_ALIGN - 1) // RUN_ALIGN * RUN_ALIGN
    loff = jnp.cumsum(len8, axis=1) - len8
    region = jnp.sum(len8, axis=0)
    padded = (region + MOE_BM - 1) // MOE_BM * MOE_BM
    pad_end = jnp.cumsum(padded)
    start = (pad_end - padded)[None, :] + jnp.cumsum(len8, axis=0) - len8
    n_blocks = (2 * n + nt * N_EXPERTS * (RUN_ALIGN - 1) + MOE_BM - 1) // MOE_BM + N_EXPERTS
    starts = jnp.arange(n_blocks, dtype=jnp.int32) * MOE_BM
    block_e = jnp.sum((starts[:, None] >= pad_end[None, :]).astype(jnp.int32), axis=1)
    block_e = jnp.minimum(block_e, N_EXPERTS - 1).astype(jnp.int32)
    n_used = (pad_end[-1] // MOE_BM).astype(jnp.int32).reshape(1)
    tails = jnp.where(padded > 0, pad_end - MOE_BM, -1)
    spare = pad_end[-1] + jnp.arange(n_blocks - (2 * n) // MOE_BM, dtype=pad_end.dtype) * MOE_BM
    spare = jnp.where(spare < n_blocks * MOE_BM, spare, -1)
    lanes = lambda a: jnp.broadcast_to(a.astype(F32)[:, :, None], (nt, N_EXPERTS, LANES))
    meta_rows = jnp.zeros((nt, 8, LANES), F32)
    meta_rows = meta_rows.at[:, 0, :N_EXPERTS].set(loff.astype(F32)).at[:, 1, :N_EXPERTS].set(cnt0.astype(F32))
    flat = lambda a: a.reshape(-1).astype(jnp.int32)
    return dict(start=flat(start), loff=flat(loff), len8=flat(len8), block_e=block_e, n_used=n_used,
                tail_rows=jnp.concatenate([tails, spare]).astype(jnp.int32),
                off_lanes=lanes(loff), cnt0_lanes=lanes(cnt0), meta_rows=meta_rows,
                n_slots=n_blocks * MOE_BM)


def _mixer_layer(x2, batch, seq, rel_bias, norm_g, w_in_bf, ssm, d_skip, w_glu, b_glu,
                 w_attn_br, w_ssm_br, w_out):
    *qkvs, gates, ut = _in_projection(x2, norm_g.reshape(1, D_MODEL), w_in_bf, batch, seq)

    os_, ls_ = [], []
    for g, (window, dilation) in enumerate(ATTN_GROUPS):
        bias = _band_bias(rel_bias[:, g * HEADS:(g + 1) * HEADS], window, dilation)
        o, l = _attention_group(qkvs[g], bias, g)
        os_.append(o)
        ls_.append(l)

    zt = _ssm_scan(ut, d_skip, _ssm_tables(*ssm), batch)

    return _merge(os_, ls_, zt, gates, x2, w_glu.astype(BF16), b_glu.reshape(1, SSM_W).astype(F32),
                  w_attn_br.astype(BF16), w_ssm_br.astype(BF16), w_out.astype(BF16))


def kernel(x, rel_bias, norm1_g, w_in, ssm_lam_re, ssm_lam_im, ssm_log_dt, ssm_b_re, ssm_b_im, ssm_c_re, ssm_c_im, ssm_d, w_glu, b_glu, w_attn_br, w_ssm_br, w_out, norm2_g, ffn_w_gate, ffn_w_up, ffn_w_down, moe_router, moe_w_gate, moe_w_up, moe_w_down, final_norm_g):
    batch, seq, d = x.shape
    assert d == D_MODEL and norm1_g.shape[0] == 2 and seq % (16 * BLK) == 0
    n = batch * seq
    x2 = x.reshape(n, d)

    def mixer(x2, l, w_in_bf):
        ssm = (ssm_lam_re[l], ssm_lam_im[l], ssm_log_dt[l], ssm_b_re[l], ssm_b_im[l],
               ssm_c_re[l], ssm_c_im[l])
        return _mixer_layer(x2, batch, seq, rel_bias, norm1_g[l], w_in_bf, ssm, ssm_d[l], w_glu[l],
                            b_glu[l], w_attn_br[l], w_ssm_br[l], w_out[l])

    x2 = mixer(x2, 0, w_in[0].astype(BF16))
    moe_w = (moe_w_gate[0], moe_w_up[0], moe_w_down[0])
    x2, (w_in1_bf, *moe_bf) = _dense_ffn(
        x2, norm2_g[0].reshape(1, d), ffn_w_gate[0].astype(BF16), ffn_w_up[0].astype(BF16),
        ffn_w_down[0].astype(BF16), to_cast=[w_in[1]] + [w.reshape(-1, w.shape[2]) for w in moe_w])
    moe_bf = [b.reshape(w.shape) for b, w in zip(moe_bf, moe_w)]

    x2 = mixer(x2, 1, w_in1_bf)
    g2 = norm2_g[1].reshape(1, d)
    wr_pad = jnp.zeros((d, LANES), F32).at[:, :N_EXPERTS].set(moe_router[0].astype(F32))
    idx, gates, idxt = _router(x2, g2, wr_pad)
    plan = _route_plan(idxt, n)
    xs = _dispatch(plan, x2, g2, idxt)
    ys = _experts(plan["block_e"], plan["n_used"], xs, *moe_bf)
    out = _combine(plan, x2, idx, gates, final_norm_g.reshape(1, d), ys)
    return out.reshape(batch, seq, d)
```

```python
import functools

import numpy as np
import jax
import jax.numpy as jnp
from jax import lax
from jax.experimental import pallas as pl
from jax.experimental.pallas import tpu as pltpu

F32 = jnp.float32
BF16 = jnp.bfloat16

LANES = 128
BF16_ROWS = 16

D_MODEL = 1024
HEAD_DIM = 64
ATTN_GROUPS = ((128, 1), (512, 4), (2048, 16))
N_GROUPS = 3
HEADS = 8
GROUP_W = HEADS * HEAD_DIM
ATTN_W = N_GROUPS * GROUP_W
BLK = 128
ATTN_ROWS = 2048
REL_BUCKETS = 32
REL_MAX_DIST = 2048
NEG_INF = -1e30
LOG2E = 1.4426950408889634
SSM_CH = 16
SSM_W = D_MODEL // 2
SSM_G = SSM_W // SSM_CH
SSM_P = 64
N_EXPERTS = 8
MOE_BM = 512
MOE_TT = 512
RUN_ALIGN = 8
RUN_PIECES = tuple(1 << b for b in range(MOE_TT.bit_length() - 1, RUN_ALIGN.bit_length() - 2, -1))
CBUF_ROWS = -(-(2 * MOE_TT + N_EXPERTS * (RUN_ALIGN - 1)) // BF16_ROWS) * BF16_ROWS
FILL_ROWS = CBUF_ROWS - 2 * MOE_TT
FILL_PIECES = tuple(1 << b for b in range(FILL_ROWS.bit_length() - 1, RUN_ALIGN.bit_length() - 2, -1))
EPS = 1e-6
CHUNK = 128
SCAN_LEVELS = 8

VMEM_LIMIT = 56 * 1024 * 1024


def _cparams(sem):
    return pltpu.CompilerParams(dimension_semantics=sem, vmem_limit_bytes=VMEM_LIMIT)


def _rms(x, g):
    return x * lax.rsqrt(jnp.mean(x * x, axis=-1, keepdims=True) + EPS) * g


def _proj_body(x_ref, g_ref, w_ref, qkv0_ref, qkv1_ref, qkv2_ref, gate_ref, ut_ref, d_scr):
    tm = x_ref.shape[0]
    u_lo = 3 * ATTN_W
    h = _rms(x_ref[...], g_ref[...])
    hb = h.astype(BF16)
    u = jnp.dot(hb, w_ref[:, u_lo:u_lo + SSM_W], preferred_element_type=F32)
    for k in range(ut_ref.shape[0]):
        ut_ref[k] = u[k * CHUNK:(k + 1) * CHUNK, :].T

    nl = D_MODEL // LANES
    for k in range(nl):
        d_scr[k] = h[:, k * LANES:(k + 1) * LANES]

    def by_subsequence(r):
        blocks = [jnp.concatenate([d_scr.at[k][pl.ds(c, tm // r, stride=r), :] for k in range(nl)], axis=1)
                  for c in range(r)]
        return jnp.concatenate(blocks, axis=0).astype(BF16)

    cw = 2 * LANES
    scale = (HEAD_DIM ** -0.5 * LOG2E, None, None)
    for g, (out_ref, (_, r)) in enumerate(zip((qkv0_ref, qkv1_ref, qkv2_ref), ATTN_GROUPS)):
        lhs = hb if r == 1 else by_subsequence(r)
        for which in range(3):
            for lo in range(0, GROUP_W, cw):
                col = which * ATTN_W + g * GROUP_W + lo
                res = jnp.dot(lhs, w_ref[:, col:col + cw], preferred_element_type=F32)
                if scale[which] is not None:
                    res = res * scale[which]
                res = res.astype(BF16)
                for c in range(r):
                    out_ref[which, c, :, lo:lo + cw] = res[c * (tm // r):(c + 1) * (tm // r), :]
    for lo in range(0, 2 * D_MODEL, cw):
        col = u_lo + SSM_W + lo
        gate_ref[:, lo:lo + cw] = jnp.dot(hb, w_ref[:, col:col + cw], preferred_element_type=F32).astype(BF16)


def _in_projection(x2, g, w_bf, batch, seq, tm=512):
    n = x2.shape[0]
    tiles_per_seq = seq // tm
    wcols = w_bf.shape[1]

    def qkv_spec(r):
        return pl.BlockSpec((3, None, r, tm // r, GROUP_W),
                            lambda i: (0, i // tiles_per_seq, 0, i % tiles_per_seq, 0))

    return pl.pallas_call(
        _proj_body,
        grid=(n // tm,),
        in_specs=[
            pl.BlockSpec((tm, D_MODEL), lambda i: (i, 0)),
            pl.BlockSpec((1, D_MODEL), lambda i: (0, 0)),
            pl.BlockSpec((D_MODEL, wcols), lambda i: (0, 0), pipeline_mode=pl.Buffered(1)),
        ],
        out_specs=[qkv_spec(r) for _, r in ATTN_GROUPS] + [
            pl.BlockSpec((tm, 2 * D_MODEL), lambda i: (i, 0)),
            pl.BlockSpec((tm // CHUNK, SSM_W, CHUNK), lambda i: (i, 0, 0)),
        ],
        out_shape=[jax.ShapeDtypeStruct((3, batch, r, seq // r, GROUP_W), BF16) for _, r in ATTN_GROUPS] + [
            jax.ShapeDtypeStruct((n, 2 * D_MODEL), BF16),
            jax.ShapeDtypeStruct((n // CHUNK, SSM_W, CHUNK), F32),
        ],
        scratch_shapes=[pltpu.VMEM((D_MODEL // LANES, tm, LANES), F32)],
        compiler_params=_cparams(("parallel",)),
        name="in_projection",
    )(x2, g, w_bf)


def _t5_bucket(dist):
    max_exact = REL_BUCKETS // 2
    d = np.maximum(dist, 1).astype(np.float64)
    large = max_exact + (
        np.log(d / max_exact) / np.log(REL_MAX_DIST / max_exact) * (REL_BUCKETS - max_exact)
    ).astype(np.int32)
    large = np.minimum(large, REL_BUCKETS - 1)
    return np.where(dist < max_exact, dist, large).astype(np.int32)


def _band_bias(table, window, dilation):
    steps = window // dilation
    qi = np.arange(BLK)[:, None]
    kj = np.arange(2 * BLK)[None, :]
    delta = BLK + qi - kj
    band = (delta >= 0) & (delta <= steps)
    bucket = _t5_bucket(np.clip(delta, 0, steps) * dilation)
    onehot = np.eye(REL_BUCKETS, dtype=np.float32)[bucket]
    bias = jnp.einsum("qkb,bh->hqk", onehot, table.astype(F32), precision=lax.Precision.HIGHEST)
    return jnp.where(band[None], bias * LOG2E, NEG_INF)


def _attn_body(q_ref, kp_ref, kc_ref, vp_ref, vc_ref, bias_ref, o_ref, l_ref, *, nsub):
    lane = lax.broadcasted_iota(jnp.int32, (BLK, LANES), 1)
    lo = lane < HEAD_DIM
    keep_lo = jnp.where(lo, 1.0, 0.0).astype(BF16)
    keep_hi = jnp.where(lo, 0.0, 1.0).astype(BF16)
    col = lax.broadcasted_iota(jnp.int32, (BLK, 2 * BLK), 1)
    first_pen = jnp.where(col < BLK, jnp.where(pl.program_id(2) == 0, NEG_INF, 0.0), 0.0)
    for c, i in [(c, i) for c in range(q_ref.shape[0]) for i in range(nsub)]:
        rows = (c, slice(i * BLK, (i + 1) * BLK))
        q = q_ref[rows]
        if i == 0:
            kw = jnp.concatenate([kp_ref[c], kc_ref[c, 0:BLK, :]], axis=0)
            vw = jnp.concatenate([vp_ref[c], vc_ref[c, 0:BLK, :]], axis=0)
        else:
            kw = kc_ref[c, (i - 1) * BLK:(i + 1) * BLK, :]
            vw = vc_ref[c, (i - 1) * BLK:(i + 1) * BLK, :]
        for hp in range(HEADS // 2):
            cols = slice(hp * LANES, (hp + 1) * LANES)
            q2, k2, v2 = q[:, cols], kw[:, cols], vw[:, cols]
            pvs, ms, dens = [], [], []
            for half in range(2):
                qm = q2 * (keep_lo, keep_hi)[half]
                s = lax.dot_general(qm, k2, (((1,), (1,)), ((), ())), preferred_element_type=F32)
                s = s + bias_ref[2 * hp + half]
                if i == 0:
                    s = s + first_pen
                m = jnp.max(s, axis=-1, keepdims=True)
                p = jnp.exp2(s - m)
                dens.append(jnp.broadcast_to(jnp.sum(p, axis=-1, keepdims=True), (BLK, LANES)))
                ms.append(jnp.broadcast_to(m, (BLK, LANES)))
                pvs.append(jnp.dot(p.astype(BF16), v2, preferred_element_type=F32))
            den = jnp.where(lo, dens[0], dens[1])
            o_ref[rows + (cols,)] = (jnp.where(lo, pvs[0], pvs[1]) / den).astype(BF16)
            l_ref[rows + (cols,)] = jnp.where(lo, ms[0], ms[1]) + jnp.log2(den)


def _attention_group(qkv, bias, g):
    _, batch, r, length, _ = qkv.shape
    qb = min(ATTN_ROWS, length)
    cb = min(ATTN_ROWS // qb, r)
    nsub = qb // BLK

    def cur(which):
        return pl.BlockSpec((None, None, cb, qb, GROUP_W), lambda b, c, n: (which, b, c, n, 0))

    def prev(which):
        return pl.BlockSpec((None, None, cb, BLK, GROUP_W),
                            lambda b, c, n: (which, b, c, jnp.maximum(n * nsub - 1, 0), 0))

    out_spec = pl.BlockSpec((None, cb, qb, GROUP_W), lambda b, c, n: (b, c, n, 0))
    return pl.pallas_call(
        functools.partial(_attn_body, nsub=nsub),
        grid=(batch, r // cb, length // qb),
        in_specs=[cur(0), prev(1), cur(1), prev(2), cur(2),
                  pl.BlockSpec((HEADS, BLK, 2 * BLK), lambda b, c, n: (0, 0, 0))],
        out_specs=[out_spec, out_spec],
        out_shape=[
            jax.ShapeDtypeStruct((batch, r, length, GROUP_W), BF16),
            jax.ShapeDtypeStruct((batch, r, length, GROUP_W), F32),
        ],
        compiler_params=_cparams(("parallel", "parallel", "arbitrary")),
        name=f"attention_g{g}",
    )(qkv, qkv, qkv, qkv, qkv, bias)


def _ssm_tables(lam_re, lam_im, log_dt, b_re, b_im, c_re, c_im):
    lam = lax.complex(lam_re.astype(F32), lam_im.astype(F32))
    dt = jnp.exp(log_dt.astype(F32))[:, None]
    lam_dt = lam * dt
    lam_bar = jnp.exp(lam_dt)
    b = lax.complex(b_re.astype(F32), b_im.astype(F32))
    b_bar = ((lam_bar - 1.0) / lam)[..., None] * b
    half = CHUNK // 2
    t = jnp.arange(CHUNK, dtype=F32)

    def power(k):
        return jnp.exp(lam_dt[:, None, :] * jnp.reshape(jnp.asarray(k, F32), (1, -1, 1)))

    p_fwd = power(t - half)
    p_bwd = power(half - t)

    def in_pair(pw):
        return jnp.stack([jnp.concatenate([pw.real, pw.imag], axis=-1),
                          jnp.concatenate([-pw.imag, pw.real], axis=-1)], axis=1)

    def out_pair(pw):
        return jnp.stack([jnp.concatenate([pw.real, -pw.imag], axis=-1),
                          jnp.concatenate([-pw.imag, -pw.real], axis=-1)], axis=1)

    powers = jnp.stack([
        in_pair(p_bwd),
        in_pair(p_bwd * power(CHUNK - 1.0 - half)),
        out_pair(p_fwd),
        out_pair(p_fwd * power(half + 1.0)),
    ], axis=1)
    dup = lambda a: jnp.concatenate([a, a], axis=-1)
    b_cp = jnp.transpose(b_bar, (0, 2, 1))
    coefs = jnp.stack([dup(b_cp.real), dup(b_cp.imag), dup(c_re.astype(F32)), dup(c_im.astype(F32))],
                      axis=1)
    lc = power([float(CHUNK * 2 ** k) for k in range(SCAN_LEVELS)])
    l1 = jnp.concatenate([lc.real, lc.real], axis=-1)
    l2 = jnp.concatenate([-lc.imag, lc.imag], axis=-1)
    return powers, coefs, l1, l2


def _gelu_tanh(y):
    return y * jax.nn.sigmoid(1.5957691216057308 * (y + 0.044715 * (y * y * y)))


def _ssm_body(d_ref, u_ref, pw_ref, cf_ref, l1_ref, l2_ref, z_ref, mask_scr, z_scr, a_scr, w_scr, *,
              ncb):
    g = pl.program_id(0)
    nc = u_ref.shape[0]
    width = SSM_CH * CHUNK
    cb = mask_scr.shape[1]

    @pl.when(g == 0)
    def _():
        s_idx = lax.broadcasted_iota(jnp.int32, (width, cb), 0) & (CHUNK - 1)
        t_idx = lax.broadcasted_iota(jnp.int32, (width, cb), 1) & (CHUNK - 1)
        mask_scr[...] = jnp.where(t_idx >= s_idx, -1, 0).astype(jnp.int32)

    def expand(kind, c1, c2, ch):
        return (cf_ref[c1, ch:ch + 1, :] * pw_ref[kind, 0] + cf_ref[c2, ch:ch + 1, :] * pw_ref[kind, 1])

    for ch in range(SSM_CH):
        rows = slice(ch * CHUNK, (ch + 1) * CHUNK)
        a_scr[rows, :] = expand(0, 0, 1, ch).astype(BF16)
        w_scr[rows, :] = expand(1, 0, 1, ch).astype(BF16)

    u2 = u_ref.reshape(nc * SSM_CH, CHUNK)
    us =[u2[pl.ds(c, nc, stride=SSM_CH), :] for c in range(SSM_CH)]
    x = jnp.concatenate(us, axis=1).astype(BF16)

    acc = jnp.dot(x, w_scr[...], preferred_element_type=F32)
    rmod = lax.broadcasted_iota(jnp.int32, (nc, 2 * SSM_P), 0) & (ncb - 1)
    for k in range(ncb.bit_length() - 1):
        d = 1 << k
        sh = jnp.where(rmod >= d, pltpu.roll(acc, d, 0), 0.0)
        acc = acc + sh * l1_ref[k:k + 1, :] + pltpu.roll(sh, SSM_P, 1) * l2_ref[k:k + 1, :]
    x_in = jnp.where(rmod >= 1, pltpu.roll(acc, 1, 0), 0.0).astype(BF16)

    per = cb // CHUNK
    for k in range(width // cb):
        chans = range(k * per, (k + 1) * per)
        d_rhs = jnp.concatenate([expand(2, 2, 3, ch).T.astype(BF16) for ch in chans], axis=1)
        v_rhs = jnp.concatenate([expand(3, 2, 3, ch).T.astype(BF16) for ch in chans], axis=1)
        mk = jnp.dot(a_scr[...], d_rhs, preferred_element_type=F32)
        kept = lax.bitcast_convert_type(mk, jnp.int32) & mask_scr[...]
        m_k = lax.bitcast_convert_type(kept, F32).astype(BF16)
        y = (jnp.dot(x, m_k, preferred_element_type=F32)
             + jnp.dot(x_in, v_rhs, preferred_element_type=F32))
        for j, c in enumerate(chans):
            yc = y[:, j * CHUNK:(j + 1) * CHUNK] + d_ref[g * SSM_CH + c] * us[c]
            z_scr[pl.ds(c, nc, stride=SSM_CH), :] = _gelu_tanh(yc)
    z_ref[...] = z_scr[...].reshape(nc, SSM_CH, CHUNK)


def _ssm_scan(u3, d_skip, tables, nbatch):
    powers, coefs, l1, l2 = tables
    nc = u3.shape[0]
    ncb = nc // nbatch
    assert ncb & (ncb - 1) == 0 and ncb <= 2 ** SCAN_LEVELS
    width = SSM_CH * CHUNK
    grid_spec = pltpu.PrefetchScalarGridSpec(
        num_scalar_prefetch=1,
        grid=(SSM_G,),
        in_specs=[
            pl.BlockSpec((nc, SSM_CH, CHUNK), lambda g, d: (0, g, 0)),
            pl.BlockSpec((None,) + powers.shape[1:], lambda g, d: (g, 0, 0, 0, 0)),
            pl.BlockSpec((None,) + coefs.shape[1:], lambda g, d: (g, 0, 0, 0)),
            pl.BlockSpec((None, SCAN_LEVELS, 2 * SSM_P), lambda g, d: (g, 0, 0)),
            pl.BlockSpec((None, SCAN_LEVELS, 2 * SSM_P), lambda g, d: (g, 0, 0)),
        ],
        out_specs=pl.BlockSpec((nc, SSM_CH, CHUNK), lambda g, d: (0, g, 0)),
        scratch_shapes=[pltpu.VMEM((width, 4 * CHUNK), jnp.int32),
                        pltpu.VMEM((nc * SSM_CH, CHUNK), F32),
                        pltpu.VMEM((width, 2 * SSM_P), BF16), pltpu.VMEM((width, 2 * SSM_P), BF16)],
    )
    return pl.pallas_call(
        functools.partial(_ssm_body, ncb=ncb),
        grid_spec=grid_spec,
        out_shape=jax.ShapeDtypeStruct((nc, SSM_W, CHUNK), F32),
        compiler_params=_cparams(("arbitrary",)),
        name="ssm_scan",
    )(d_skip.astype(F32), u3, powers, coefs, l1, l2)


def _merge_body(o0, o1, o2, l0, l1, l2, zt_ref, ga_ref, gs_ref, x_ref,
                wglu_ref, bglu_ref, wab_ref, wsb_ref, wout_ref, out_ref, tok_scr):
    def token_major(ref, slot):
        r, rows, _ = ref.shape
        if r == 1:
            return ref[0].astype(F32)
        nl = GROUP_W // LANES
        scrs = [tok_scr.at[slot * nl + k] for k in range(nl)]
        for c in range(r):
            sub = ref[c].astype(F32)
            for k in range(nl):
                scrs[k][pl.ds(c, rows, stride=r), :] = sub[:, k * LANES:(k + 1) * LANES]
        return jnp.concatenate([s[...] for s in scrs], axis=1)

    a0, a1, a2 = token_major(l0, 0), token_major(l1, 0), token_major(l2, 1)
    v0, v1, v2 = token_major(o0, 0), token_major(o1, 2), token_major(o2, 3)
    mx = jnp.maximum(jnp.maximum(a0, a1), a2)
    e0, e1, e2 = jnp.exp2(a0 - mx), jnp.exp2(a1 - mx), jnp.exp2(a2 - mx)
    mix = (e0 * v0 + e1 * v1 + e2 * v2) / (e0 + e1 + e2)
    y_attn = jnp.dot(mix.astype(BF16), wab_ref[...], preferred_element_type=F32)

    z = jnp.concatenate([zt_ref[k].T for k in range(zt_ref.shape[0])], axis=0).astype(BF16)
    gl = jnp.dot(z, wglu_ref[...], preferred_element_type=F32) + bglu_ref[...]
    sg = z.astype(F32) * jax.nn.sigmoid(gl)
    y_ssm = jnp.dot(sg.astype(BF16), wsb_ref[...], preferred_element_type=F32)

    merged = (jax.nn.sigmoid(ga_ref[...].astype(F32)) * y_attn
              + jax.nn.sigmoid(gs_ref[...].astype(F32)) * y_ssm)
    out_ref[...] = x_ref[...] + jnp.dot(merged.astype(BF16), wout_ref[...], preferred_element_type=F32)


def _merge(os_, ls_, zt, gates, x2, wglu, bglu, wab, wsb, wout, tm=512):
    n = x2.shape[0]
    tiles_per_seq = os_[0].shape[2] // tm
    row = lambda i: (i, 0)
    const = lambda i: (0, 0)

    def group_spec(a):
        r = a.shape[1]
        return pl.BlockSpec((None, r, tm // r, GROUP_W),
                            lambda i: (i // tiles_per_seq, 0, i % tiles_per_seq, 0))

    in_specs = (
        [group_spec(a) for a in os_] + [group_spec(a) for a in ls_]
        + [
            pl.BlockSpec((tm // CHUNK, SSM_W, CHUNK), lambda i: (i, 0, 0)),
            pl.BlockSpec((tm, D_MODEL), lambda i: (i, 0)),
            pl.BlockSpec((tm, D_MODEL), lambda i: (i, 1)),
            pl.BlockSpec((tm, D_MODEL), row),
            pl.BlockSpec((SSM_W, SSM_W), const),
            pl.BlockSpec((1, SSM_W), const),
            pl.BlockSpec((GROUP_W, D_MODEL), const),
            pl.BlockSpec((SSM_W, D_MODEL), const),
            pl.BlockSpec((D_MODEL, D_MODEL), const),
        ]
    )
    return pl.pallas_call(
        _merge_body,
        grid=(n // tm,),
        in_specs=in_specs,
        out_specs=pl.BlockSpec((tm, D_MODEL), row),
        out_shape=jax.ShapeDtypeStruct((n, D_MODEL), F32),
        scratch_shapes=[pltpu.VMEM((4 * GROUP_W // LANES, tm, LANES), F32)],
        compiler_params=_cparams(("parallel",)),
        name="merge",
    )(*os_, *ls_, zt, gates, gates, x2, wglu, bglu, wab, wsb, wout)


def _ffn_body(x_ref, g_ref, wg_ref, wu_ref, wd_ref, *rest, tf):
    ncast = (len(rest) - 1) // 2
    o_ref = rest[ncast]
    for src, dst in zip(rest[:ncast], rest[ncast + 1:]):
        dst[...] = src[...].astype(BF16)
    h = _rms(x_ref[...], g_ref[...]).astype(BF16)
    for f in range(wg_ref.shape[1] // tf):
        cols = slice(f * tf, (f + 1) * tf)
        a = jnp.dot(h, wg_ref[:, cols], preferred_element_type=F32)
        b = jnp.dot(h, wu_ref[:, cols], preferred_element_type=F32)
        act = (a * jax.nn.sigmoid(a) * b).astype(BF16)
        part = jnp.dot(act, wd_ref[cols, :], preferred_element_type=F32)
        if f == 0:
            o_ref[...] = x_ref[...] + part
        else:
            o_ref[...] += part


def _dense_ffn(x2, g, wg, wu, wd, to_cast=(), tm=512, tf=256):
    n = x2.shape[0]
    dff = wg.shape[1]
    steps = n // tm
    assert dff % tf == 0 and all(a.shape[0] % (BF16_ROWS * steps) == 0 for a in to_cast)
    resident = lambda shape: pl.BlockSpec(shape, lambda i: (0, 0), pipeline_mode=pl.Buffered(1))
    slabs = [pl.BlockSpec((a.shape[0] // steps, a.shape[1]), lambda i: (i, 0)) for a in to_cast]
    out, *casts = pl.pallas_call(
        functools.partial(_ffn_body, tf=tf),
        grid=(steps,),
        in_specs=[
            pl.BlockSpec((tm, D_MODEL), lambda i: (i, 0)),
            pl.BlockSpec((1, D_MODEL), lambda i: (0, 0)),
            resident((D_MODEL, dff)),
            resident((D_MODEL, dff)),
            resident((dff, D_MODEL)),
        ] + slabs,
        out_specs=[pl.BlockSpec((tm, D_MODEL), lambda i: (i, 0))] + slabs,
        out_shape=[jax.ShapeDtypeStruct((n, D_MODEL), F32)]
        + [jax.ShapeDtypeStruct(a.shape, BF16) for a in to_cast],
        compiler_params=_cparams(("parallel",)),
        name="dense_ffn",
    )(x2, g, wg, wu, wd, *to_cast)
    return out, casts


def _route_top2(h, w, idx_ref, gate_ref, idxt_ref):
    h_hi, w_hi = h.astype(BF16), w.astype(BF16)
    h_lo = (h - h_hi.astype(F32)).astype(BF16)
    w_lo = (w - w_hi.astype(F32)).astype(BF16)
    both = jnp.dot(h_hi, jnp.concatenate([w_hi, w_lo], axis=1), preferred_element_type=F32)
    logits = both[:, :LANES] + (both[:, LANES:] + jnp.dot(h_lo, w_hi, preferred_element_type=F32))
    lane = lax.broadcasted_iota(jnp.int32, logits.shape, 1)
    lane_f = lane.astype(F32)
    logits = jnp.where(lane < N_EXPERTS, logits, -jnp.inf)
    v1 = jnp.max(logits, axis=-1, keepdims=True)
    i1 = jnp.min(jnp.where(logits == v1, lane_f, float(LANES)), axis=-1, keepdims=True)
    rest = jnp.where(lane_f == i1, -jnp.inf, logits)
    v2 = jnp.max(rest, axis=-1, keepdims=True)
    i2 = jnp.min(jnp.where(rest == v2, lane_f, float(LANES)), axis=-1, keepdims=True)
    e = jnp.exp(v2 - v1)
    g1 = 1.0 / (1.0 + e)
    g2 = e / (1.0 + e)
    idx_f = jnp.where(lane == 0, i1, jnp.where(lane == 1, i2, 0.0))
    idx_ref[...] = idx_f.astype(jnp.int32)
    gate_ref[...] = jnp.where(lane == 0, g1, jnp.where(lane == 1, g2, 0.0))
    idxt_ref[...] = idx_f.T[:idxt_ref.shape[0], :].astype(jnp.int32)


def _router_body(x_ref, g_ref, wr_ref, idx_ref, gate_ref, idxt_ref):
    _route_top2(_rms(x_ref[...], g_ref[...]), wr_ref[...], idx_ref, gate_ref, idxt_ref)


def _router(x2, g, wr_pad, tm=1024):
    n = x2.shape[0]
    return pl.pallas_call(
        _router_body,
        grid=(n // tm,),
        in_specs=[
            pl.BlockSpec((tm, D_MODEL), lambda i: (i, 0)),
            pl.BlockSpec((1, D_MODEL), lambda i: (0, 0)),
            pl.BlockSpec((D_MODEL, LANES), lambda i: (0, 0)),
        ],
        out_specs=[
            pl.BlockSpec((tm, LANES), lambda i: (i, 0)),
            pl.BlockSpec((tm, LANES), lambda i: (i, 0)),
            pl.BlockSpec((8, tm), lambda i: (0, i)),
        ],
        out_shape=[
            jax.ShapeDtypeStruct((n, LANES), jnp.int32),
            jax.ShapeDtypeStruct((n, LANES), F32),
            jax.ShapeDtypeStruct((8, n), jnp.int32),
        ],
        compiler_params=_cparams(("parallel",)),
        name="router",
    )(x2, g, wr_pad)


def _pack_bf16_pairs(hb):
    half = hb.shape[1] // 2
    lo = lax.bitcast_convert_type(hb[:, :half].astype(F32), jnp.uint32)
    hi = lax.bitcast_convert_type(hb[:, half:].astype(F32), jnp.uint32)
    return (hi & jnp.uint32(0xFFFF0000)) | (lo >> 16)


def _unpack_bf16_pairs(xu):
    lo = lax.bitcast_convert_type(xu << 16, F32).astype(BF16)
    hi = lax.bitcast_convert_type(xu & jnp.uint32(0xFFFF0000), F32).astype(BF16)
    return lo, hi


def _for_each_run_piece(i, start_ref, loff_ref, len_ref, fn, fill):
    for e in range(N_EXPERTS):
        j = i * N_EXPERTS + e
        length, boff, soff = len_ref[j], loff_ref[j], start_ref[j]
        done = 0
        for p in RUN_PIECES:
            cond = (length & p) != 0
            fn(cond, pl.multiple_of(boff + done, RUN_ALIGN), pl.multiple_of(soff + done, RUN_ALIGN), p)
            done = done + jnp.where(cond, p, 0)
    last = i * N_EXPERTS + N_EXPERTS - 1
    used = loff_ref[last] + len_ref[last]
    rest, done = CBUF_ROWS - used, 0
    for p in FILL_PIECES:
        cond = (rest & p) != 0
        fill(cond, pl.multiple_of(used + done, RUN_ALIGN), pl.multiple_of(done, RUN_ALIGN), p)
        done = done + jnp.where(cond, p, 0)


def _dispatch_body(start_ref, loff_ref, len_ref, tail_ref, x_ref, g_ref, idxt_ref, off_ref, cnt0_ref,
                   xs_ref, cbuf, tri_scr, zero_scr, sem, zsem):
    i = pl.program_id(0)
    tt = x_ref.shape[0]

    @pl.when(i == 0)
    def _():
        r = lax.broadcasted_iota(jnp.int32, (tt, tt), 0)
        c = lax.broadcasted_iota(jnp.int32, (tt, tt), 1)
        tri_scr[...] = jnp.where(r < c, 1.0, 0.0).astype(BF16)
        zero_scr[...] = jnp.zeros_like(zero_scr)

        def fill(e):
            row = pl.multiple_of(jnp.maximum(tail_ref[e], 0), MOE_BM)
            return pltpu.make_async_copy(zero_scr, xs_ref.at[pl.ds(row, MOE_BM)], zsem)

        for e in range(tail_ref.shape[0]):
            pl.when(tail_ref[e] >= 0)(lambda e=e: fill(e).start())
        for e in range(tail_ref.shape[0]):
            pl.when(tail_ref[e] >= 0)(lambda e=e: fill(e).wait())
        nspare = 2 * FILL_ROWS
        spare_fill = pltpu.make_async_copy(zero_scr.at[pl.ds(0, nspare)],
                                           xs_ref.at[pl.ds(xs_ref.shape[0] - nspare, nspare)], zsem)
        spare_fill.start()
        spare_fill.wait()

    hb = _rms(x_ref[...], g_ref[...]).astype(BF16)

    sub = lax.broadcasted_iota(jnp.int32, (N_EXPERTS, tt), 0)
    pos = []
    for k in range(2):
        oh = jnp.where(sub == idxt_ref[k:k + 1, :], 1.0, 0.0)
        rank = jnp.dot(oh.astype(BF16), tri_scr[...], preferred_element_type=F32)
        base = off_ref[:, :1] if k == 0 else off_ref[:, :1] + cnt0_ref[:, :1]
        pos.append(jnp.sum(oh * (base + rank), axis=0, keepdims=True))

    rows = lax.broadcasted_iota(jnp.int32, (cbuf.shape[1], tt), 0).astype(F32)
    perm = (jnp.where(rows == pos[0], 1.0, 0.0) + jnp.where(rows == pos[1], 1.0, 0.0)).astype(BF16)
    slot = i % 2
    cbuf[slot] = _pack_bf16_pairs(jnp.dot(perm, hb, preferred_element_type=F32).astype(BF16))

    spare = xs_ref.shape[0] - 2 * FILL_ROWS

    def run_piece(cond, brow, srow, p):
        cp = pltpu.make_async_copy(cbuf.at[slot, pl.ds(brow, p)], xs_ref.at[pl.ds(srow, p)], sem.at[slot])
        pl.when(cond)(cp.start)

    def fill_piece(cond, brow, frow, p):
        run_piece(cond, brow, pl.multiple_of(spare + slot * FILL_ROWS + frow, RUN_ALIGN), p)

    _for_each_run_piece(i, start_ref, loff_ref, len_ref, run_piece, fill_piece)

    def drain(buf):
        pltpu.make_async_copy(cbuf.at[buf], xs_ref.at[pl.ds(0, CBUF_ROWS)], sem.at[buf]).wait()

    pl.when(i > 0)(lambda: drain(1 - slot))
    pl.when(i == pl.num_programs(0) - 1)(lambda: drain(slot))


def _dispatch(plan, x2, g, idxt):
    n = x2.shape[0]
    tt = MOE_TT
    smem = lambda i, *_: (i, 0, 0)
    grid_spec = pltpu.PrefetchScalarGridSpec(
        num_scalar_prefetch=4,
        grid=(n // tt,),
        in_specs=[
            pl.BlockSpec((tt, D_MODEL), lambda i, *_: (i, 0)),
            pl.BlockSpec((1, D_MODEL), lambda i, *_: (0, 0)),
            pl.BlockSpec((8, tt), lambda i, *_: (0, i)),
            pl.BlockSpec((None, N_EXPERTS, LANES), smem),
            pl.BlockSpec((None, N_EXPERTS, LANES), smem),
        ],
        out_specs=pl.BlockSpec(memory_space=pl.ANY),
        scratch_shapes=[
            pltpu.VMEM((2, CBUF_ROWS, D_MODEL // 2), jnp.uint32),
            pltpu.VMEM((tt, tt), BF16),
            pltpu.VMEM((MOE_BM, D_MODEL // 2), jnp.uint32),
            pltpu.SemaphoreType.DMA((2,)),
            pltpu.SemaphoreType.DMA(()),
        ],
    )
    return pl.pallas_call(
        _dispatch_body,
        grid_spec=grid_spec,
        out_shape=jax.ShapeDtypeStruct((plan["n_slots"] + 2 * FILL_ROWS, D_MODEL // 2), jnp.uint32),
        compiler_params=_cparams(("arbitrary",)),
        name="moe_dispatch",
    )(plan["start"], plan["loff"], plan["len8"], plan["tail_rows"], x2, g, idxt,
      plan["off_lanes"], plan["cnt0_lanes"])


def _experts_body(be_ref, nu_ref, xs_ref, wg_ref, wu_ref, wd_ref, ys_ref, acc_scr, *, tf):
    del be_ref
    i = pl.program_id(0)

    @pl.when(i < nu_ref[0])
    def _():
        x = jnp.concatenate(_unpack_bf16_pairs(xs_ref[...]), axis=1)
        for f in range(wg_ref.shape[1] // tf):
            cols = slice(f * tf, (f + 1) * tf)
            a = jnp.dot(x, wg_ref[:, cols], preferred_element_type=F32)
            b = jnp.dot(x, wu_ref[:, cols], preferred_element_type=F32)
            act = (a * jax.nn.sigmoid(a) * b).astype(BF16)
            part = jnp.dot(act, wd_ref[cols, :], preferred_element_type=F32)
            if f == 0:
                acc_scr[...] = part
            else:
                acc_scr[...] += part
        ys_ref[...] = _pack_bf16_pairs(acc_scr[...].astype(BF16))

    @pl.when(i >= nu_ref[0])
    def _():
        ys_ref[...] = jnp.zeros_like(ys_ref)


def _experts(block_e, n_used, xs, wg, wu, wd, tf=256):
    n_blocks = xs.shape[0] // MOE_BM
    n_slots = n_blocks * MOE_BM
    dff = wg.shape[2]

    def blk(i, nu):
        return jnp.minimum(i, nu[0] - 1)

    def wspec(rows, cols):
        return pl.BlockSpec((None, rows, cols), lambda i, be, nu: (be[blk(i, nu)], 0, 0))

    grid_spec = pltpu.PrefetchScalarGridSpec(
        num_scalar_prefetch=2,
        grid=(n_blocks,),
        in_specs=[
            pl.BlockSpec((MOE_BM, D_MODEL // 2), lambda i, be, nu: (blk(i, nu), 0)),
            wspec(D_MODEL, dff),
            wspec(D_MODEL, dff),
            wspec(dff, D_MODEL),
        ],
        out_specs=pl.BlockSpec((MOE_BM, D_MODEL // 2), lambda i, be, nu: (i, 0)),
        scratch_shapes=[pltpu.VMEM((MOE_BM, D_MODEL), F32)],
    )
    return pl.pallas_call(
        functools.partial(_experts_body, tf=tf),
        grid_spec=grid_spec,
        out_shape=jax.ShapeDtypeStruct((n_slots, D_MODEL // 2), jnp.uint32),
        compiler_params=_cparams(("arbitrary",)),
        name="moe_experts",
    )(block_e, n_used, xs, wg, wu, wd)


def _combine_body(start_ref, loff_ref, len_ref, x_ref, idx_ref, gate_ref, meta_ref, g_ref, ys_ref, o_ref,
                  ybuf, tri_scr, sem):
    i = pl.program_id(0)
    tt = x_ref.shape[0]
    slot = i % 2

    def fetch(tile, buf):
        def run_piece(cond, brow, srow, p):
            cp = pltpu.make_async_copy(ys_ref.at[pl.ds(srow, p)], ybuf.at[buf, pl.ds(brow, p)], sem.at[buf])
            pl.when(cond)(cp.start)

        _for_each_run_piece(tile, start_ref, loff_ref, len_ref, run_piece, run_piece)

    @pl.when(i == 0)
    def _():
        r = lax.broadcasted_iota(jnp.int32, (tt, tt), 0)
        c = lax.broadcasted_iota(jnp.int32, (tt, tt), 1)
        tri_scr[...] = jnp.where(c < r, 1.0, 0.0).astype(BF16)
        fetch(i, slot)

    pl.when(i + 1 < pl.num_programs(0))(lambda: fetch(i + 1, 1 - slot))

    lane = lax.broadcasted_iota(jnp.int32, (tt, LANES), 1)
    idx = idx_ref[...]
    gt = gate_ref[...]
    cols = lax.broadcasted_iota(jnp.int32, (tt, ybuf.shape[1]), 1).astype(F32)
    sel = None
    for k in range(2):
        oh = jnp.where(lane == idx[:, k:k + 1], 1.0, 0.0)
        rank = jnp.dot(tri_scr[...], oh.astype(BF16), preferred_element_type=F32)
        base = meta_ref[0:1, :] if k == 0 else meta_ref[0:1, :] + meta_ref[1:2, :]
        pos = jnp.sum(oh * (base + rank), axis=1, keepdims=True)
        term = jnp.where(cols == pos, gt[:, k:k + 1], 0.0)
        sel = term if sel is None else sel + term
    sel = sel.astype(BF16)

    pltpu.make_async_copy(ys_ref.at[pl.ds(0, CBUF_ROWS)], ybuf.at[slot], sem.at[slot]).wait()

    lo, hi = _unpack_bf16_pairs(ybuf[slot])
    y = jnp.concatenate([jnp.dot(sel, lo, preferred_element_type=F32),
                         jnp.dot(sel, hi, preferred_element_type=F32)], axis=1)
    o_ref[...] = _rms(x_ref[...] + y, g_ref[...])


def _combine(plan, x2, idx, gates, g_final, ys):
    n = x2.shape[0]
    tt = MOE_TT
    grid_spec = pltpu.PrefetchScalarGridSpec(
        num_scalar_prefetch=3,
        grid=(n // tt,),
        in_specs=[
            pl.BlockSpec((tt, D_MODEL), lambda i, *_: (i, 0)),
            pl.BlockSpec((tt, LANES), lambda i, *_: (i, 0)),
            pl.BlockSpec((tt, LANES), lambda i, *_: (i, 0)),
            pl.BlockSpec((None, 8, LANES), lambda i, *_: (i, 0, 0)),
            pl.BlockSpec((1, D_MODEL), lambda i, *_: (0, 0)),
            pl.BlockSpec(memory_space=pl.ANY),
        ],
        out_specs=pl.BlockSpec((tt, D_MODEL), lambda i, *_: (i, 0)),
        scratch_shapes=[
            pltpu.VMEM((2, CBUF_ROWS, D_MODEL // 2), jnp.uint32),
            pltpu.VMEM((tt, tt), BF16),
            pltpu.SemaphoreType.DMA((2,)),
        ],
    )
    return pl.pallas_call(
        _combine_body,
        grid_spec=grid_spec,
        out_shape=jax.ShapeDtypeStruct((n, D_MODEL), F32),
        compiler_params=_cparams(("arbitrary",)),
        name="moe_combine",
    )(plan["start"], plan["loff"], plan["len8"], x2, idx, gates, plan["meta_rows"], g_final, ys)


def _route_plan(idxt, n):
    nt = n // MOE_TT
    e2 = idxt[:2].reshape(2, nt, MOE_TT)
    oh = (e2[..., None] == jnp.arange(N_EXPERTS, dtype=jnp.int32)).astype(jnp.int32)
    cnt = jnp.sum(oh, axis=2)
    cnt0 = cnt[0]
    len8 = (cnt[0] + cnt[1] + RUN_ALIGN - 1) // RUN_ALIGN * RUN_ALIGN
    loff = jnp.cumsum(len8, axis=1) - len8
    region = jnp.sum(len8, axis=0)
    padded = (region + MOE_BM - 1) // MOE_BM * MOE_BM
    pad_end = jnp.cumsum(padded)
    start = (pad_end - padded)[None, :] + jnp.cumsum(len8, axis=0) - len8
    n_blocks = (2 * n + nt * N_EXPERTS * (RUN_ALIGN - 1) + MOE_BM - 1) // MOE_BM + N_EXPERTS
    starts = jnp.arange(n_blocks, dtype=jnp.int32) * MOE_BM
    block_e = jnp.sum((starts[:, None] >= pad_end[None, :]).astype(jnp.int32), axis=1)
    block_e = jnp.minimum(block_e, N_EXPERTS - 1).astype(jnp.int32)
    n_used = (pad_end[-1] // MOE_BM).astype(jnp.int32).reshape(1)
    tails = jnp.where(padded > 0, pad_end - MOE_BM, -1)
    spare = pad_end[-1] + jnp.arange(n_blocks - (2 * n) // MOE_BM, dtype=pad_end.dtype) * MOE_BM
    spare = jnp.where(spare < n_blocks * MOE_BM, spare, -1)
    lanes = lambda a: jnp.broadcast_to(a.astype(F32)[:, :, None], (nt, N_EXPERTS, LANES))
    meta_rows = jnp.zeros((nt, 8, LANES), F32)
    meta_rows = meta_rows.at[:, 0, :N_EXPERTS].set(loff.astype(F32)).at[:, 1, :N_EXPERTS].set(cnt0.astype(F32))
    flat = lambda a: a.reshape(-1).astype(jnp.int32)
    return dict(start=flat(start), loff=flat(loff), len8=flat(len8), block_e=block_e, n_used=n_used,
                tail_rows=jnp.concatenate([tails, spare]).astype(jnp.int32),
                off_lanes=lanes(loff), cnt0_lanes=lanes(cnt0), meta_rows=meta_rows,
                n_slots=n_blocks * MOE_BM)


def _mixer_layer(x2, batch, seq, rel_bias, norm_g, w_in_bf, ssm, d_skip, w_glu, b_glu,
                 w_attn_br, w_ssm_br, w_out):
    *qkvs, gates, ut = _in_projection(x2, norm_g.reshape(1, D_MODEL), w_in_bf, batch, seq)

    os_, ls_ = [], []
    for g, (window, dilation) in enumerate(ATTN_GROUPS):
        bias = _band_bias(rel_bias[:, g * HEADS:(g + 1) * HEADS], window, dilation)
        o, l = _attention_group(qkvs[g], bias, g)
        os_.append(o)
        ls_.append(l)

    zt = _ssm_scan(ut, d_skip, _ssm_tables(*ssm), batch)

    return _merge(os_, ls_, zt, gates, x2, w_glu.astype(BF16), b_glu.reshape(1, SSM_W).astype(F32),
                  w_attn_br.astype(BF16), w_ssm_br.astype(BF16), w_out.astype(BF16))


def kernel(x, rel_bias, norm1_g, w_in, ssm_lam_re, ssm_lam_im, ssm_log_dt, ssm_b_re, ssm_b_im, ssm_c_re, ssm_c_im, ssm_d, w_glu, b_glu, w_attn_br, w_ssm_br, w_out, norm2_g, ffn_w_gate, ffn_w_up, ffn_w_down, moe_router, moe_w_gate, moe_w_up, moe_w_down, final_norm_g):
    batch, seq, d = x.shape
    assert d == D_MODEL and norm1_g.shape[0] == 2 and seq % (16 * BLK) == 0
    n = batch * seq
    x2 = x.reshape(n, d)

    def mixer(x2, l, w_in_bf):
        ssm = (ssm_lam_re[l], ssm_lam_im[l], ssm_log_dt[l], ssm_b_re[l], ssm_b_im[l],
               ssm_c_re[l], ssm_c_im[l])
        return _mixer_layer(x2, batch, seq, rel_bias, norm1_g[l], w_in_bf, ssm, ssm_d[l], w_glu[l],
                            b_glu[l], w_attn_br[l], w_ssm_br[l], w_out[l])

    x2 = mixer(x2, 0, w_in[0].astype(BF16))
    moe_w = (moe_w_gate[0], moe_w_up[0], moe_w_down[0])
    x2, (w_in1_bf, *moe_bf) = _dense_ffn(
        x2, norm2_g[0].reshape(1, d), ffn_w_gate[0].astype(BF16), ffn_w_up[0].astype(BF16),
        ffn_w_down[0].astype(BF16), to_cast=[w_in[1]] + [w.reshape(-1, w.shape[2]) for w in moe_w])
    moe_bf = [b.reshape(w.shape) for b, w in zip(moe_bf, moe_w)]

    x2 = mixer(x2, 1, w_in1_bf)
    g2 = norm2_g[1].reshape(1, d)
    wr_pad = jnp.zeros((d, LANES), F32).at[:, :N_EXPERTS].set(moe_router[0].astype(F32))
    idx, gates, idxt = _router(x2, g2, wr_pad)
    plan = _route_plan(idxt, n)
    xs = _dispatch(plan, x2, g2, idxt)
    ys = _experts(plan["block_e"], plan["n_used"], xs, *moe_bf)
    out = _combine(plan, x2, idx, gates, final_norm_g.reshape(1, d), ys)
    return out.reshape(batch, seq, d)
```

```python
import functools

import numpy as np
import jax
import jax.numpy as jnp
from jax import lax
from jax.experimental import pallas as pl
from jax.experimental.pallas import tpu as pltpu

F32 = jnp.float32
BF16 = jnp.bfloat16

LANES = 128
BF16_ROWS = 16

D_MODEL = 1024
HEAD_DIM = 64
ATTN_GROUPS = ((128, 1), (512, 4), (2048, 16))
N_GROUPS = 3
HEADS = 8
GROUP_W = HEADS * HEAD_DIM
ATTN_W = N_GROUPS * GROUP_W
BLK = 128
ATTN_ROWS = 4096
REL_BUCKETS = 32
REL_MAX_DIST = 2048
NEG_INF = -1e30
LOG2E = 1.4426950408889634
SSM_CH = 16
SSM_W = D_MODEL // 2
SSM_G = SSM_W // SSM_CH
SSM_P = 64
N_EXPERTS = 8
MOE_BM = 512
MOE_TT = 512
RUN_ALIGN = 8
RUN_PIECES = tuple(1 << b for b in range(MOE_TT.bit_length() - 1, RUN_ALIGN.bit_length() - 2, -1))
CBUF_ROWS = -(-(2 * MOE_TT + N_EXPERTS * (RUN_ALIGN - 1)) // BF16_ROWS) * BF16_ROWS
FILL_ROWS = CBUF_ROWS - 2 * MOE_TT
FILL_PIECES = tuple(1 << b for b in range(FILL_ROWS.bit_length() - 1, RUN_ALIGN.bit_length() - 2, -1))
EPS = 1e-6
CHUNK = 128
SCAN_LEVELS = 8

VMEM_LIMIT = 56 * 1024 * 1024


def _cparams(sem):
    return pltpu.CompilerParams(dimension_semantics=sem, vmem_limit_bytes=VMEM_LIMIT)


def _rms(x, g):
    return x * lax.rsqrt(jnp.mean(x * x, axis=-1, keepdims=True) + EPS) * g


def _proj_body(x_ref, g_ref, w_ref, qkv0_ref, qkv1_ref, qkv2_ref, gate_ref, ut_ref, d_scr):
    tm = x_ref.shape[0]
    u_lo = 3 * ATTN_W
    h = _rms(x_ref[...], g_ref[...])
    hb = h.astype(BF16)
    u = jnp.dot(hb, w_ref[:, u_lo:u_lo + SSM_W], preferred_element_type=F32)
    for k in range(ut_ref.shape[0]):
        ut_ref[k] = u[k * CHUNK:(k + 1) * CHUNK, :].T

    nl = D_MODEL // LANES
    for k in range(nl):
        d_scr[k] = h[:, k * LANES:(k + 1) * LANES]

    def by_subsequence(r):
        blocks = [jnp.concatenate([d_scr.at[k][pl.ds(c, tm // r, stride=r), :] for k in range(nl)], axis=1)
                  for c in range(r)]
        return jnp.concatenate(blocks, axis=0).astype(BF16)

    cw = 2 * LANES
    scale = (HEAD_DIM ** -0.5 * LOG2E, None, None)
    for g, (out_ref, (_, r)) in enumerate(zip((qkv0_ref, qkv1_ref, qkv2_ref), ATTN_GROUPS)):
        lhs = hb if r == 1 else by_subsequence(r)
        for which in range(3):
            for lo in range(0, GROUP_W, cw):
                col = which * ATTN_W + g * GROUP_W + lo
                res = jnp.dot(lhs, w_ref[:, col:col + cw], preferred_element_type=F32)
                if scale[which] is not None:
                    res = res * scale[which]
                res = res.astype(BF16)
                for c in range(r):
                    out_ref[which, c, :, lo:lo + cw] = res[c * (tm // r):(c + 1) * (tm // r), :]
    for lo in range(0, 2 * D_MODEL, cw):
        col = u_lo + SSM_W + lo
        gate_ref[:, lo:lo + cw] = jnp.dot(hb, w_ref[:, col:col + cw], preferred_element_type=F32).astype(BF16)


def _in_projection(x2, g, w_bf, batch, seq, tm=512):
    n = x2.shape[0]
    tiles_per_seq = seq // tm
    wcols = w_bf.shape[1]

    def qkv_spec(r):
        return pl.BlockSpec((3, None, r, tm // r, GROUP_W),
                            lambda i: (0, i // tiles_per_seq, 0, i % tiles_per_seq, 0))

    return pl.pallas_call(
        _proj_body,
        grid=(n // tm,),
        in_specs=[
            pl.BlockSpec((tm, D_MODEL), lambda i: (i, 0)),
            pl.BlockSpec((1, D_MODEL), lambda i: (0, 0)),
            pl.BlockSpec((D_MODEL, wcols), lambda i: (0, 0), pipeline_mode=pl.Buffered(1)),
        ],
        out_specs=[qkv_spec(r) for _, r in ATTN_GROUPS] + [
            pl.BlockSpec((tm, 2 * D_MODEL), lambda i: (i, 0)),
            pl.BlockSpec((tm // CHUNK, SSM_W, CHUNK), lambda i: (i, 0, 0)),
        ],
        out_shape=[jax.ShapeDtypeStruct((3, batch, r, seq // r, GROUP_W), BF16) for _, r in ATTN_GROUPS] + [
            jax.ShapeDtypeStruct((n, 2 * D_MODEL), BF16),
            jax.ShapeDtypeStruct((n // CHUNK, SSM_W, CHUNK), F32),
        ],
        scratch_shapes=[pltpu.VMEM((D_MODEL // LANES, tm, LANES), F32)],
        compiler_params=_cparams(("parallel",)),
        name="in_projection",
    )(x2, g, w_bf)


def _t5_bucket(dist):
    max_exact = REL_BUCKETS // 2
    d = np.maximum(dist, 1).astype(np.float64)
    large = max_exact + (
        np.log(d / max_exact) / np.log(REL_MAX_DIST / max_exact) * (REL_BUCKETS - max_exact)
    ).astype(np.int32)
    large = np.minimum(large, REL_BUCKETS - 1)
    return np.where(dist < max_exact, dist, large).astype(np.int32)


def _band_bias(table, window, dilation):
    steps = window // dilation
    qi = np.arange(BLK)[:, None]
    kj = np.arange(2 * BLK)[None, :]
    delta = BLK + qi - kj
    band = (delta >= 0) & (delta <= steps)
    bucket = _t5_bucket(np.clip(delta, 0, steps) * dilation)
    onehot = np.eye(REL_BUCKETS, dtype=np.float32)[bucket]
    bias = jnp.einsum("qkb,bh->hqk", onehot, table.astype(F32), precision=lax.Precision.HIGHEST)
    return jnp.where(band[None], bias * LOG2E, NEG_INF)


def _attn_body(q_ref, kp_ref, kc_ref, vp_ref, vc_ref, bias_ref, o_ref, l_ref, *, nsub):
    lane = lax.broadcasted_iota(jnp.int32, (BLK, LANES), 1)
    lo = lane < HEAD_DIM
    keep_lo = jnp.where(lo, 1.0, 0.0).astype(BF16)
    keep_hi = jnp.where(lo, 0.0, 1.0).astype(BF16)
    col = lax.broadcasted_iota(jnp.int32, (BLK, 2 * BLK), 1)
    first_pen = jnp.where(col < BLK, jnp.where(pl.program_id(2) == 0, NEG_INF, 0.0), 0.0)
    for c, i in [(c, i) for c in range(q_ref.shape[0]) for i in range(nsub)]:
        rows = (c, slice(i * BLK, (i + 1) * BLK))
        q = q_ref[rows]
        if i == 0:
            kw = jnp.concatenate([kp_ref[c], kc_ref[c, 0:BLK, :]], axis=0)
            vw = jnp.concatenate([vp_ref[c], vc_ref[c, 0:BLK, :]], axis=0)
        else:
            kw = kc_ref[c, (i - 1) * BLK:(i + 1) * BLK, :]
            vw = vc_ref[c, (i - 1) * BLK:(i + 1) * BLK, :]
        for hp in range(HEADS // 2):
            cols = slice(hp * LANES, (hp + 1) * LANES)
            q2, k2, v2 = q[:, cols], kw[:, cols], vw[:, cols]
            pvs, ms, dens = [], [], []
            for half in range(2):
                qm = q2 * (keep_lo, keep_hi)[half]
                s = lax.dot_general(qm, k2, (((1,), (1,)), ((), ())), preferred_element_type=F32)
                s = s + bias_ref[2 * hp + half]
                if i == 0:
                    s = s + first_pen
                m = jnp.max(s, axis=-1, keepdims=True)
                p = jnp.exp2(s - m)
                dens.append(jnp.broadcast_to(jnp.sum(p, axis=-1, keepdims=True), (BLK, LANES)))
                ms.append(jnp.broadcast_to(m, (BLK, LANES)))
                pvs.append(jnp.dot(p.astype(BF16), v2, preferred_element_type=F32))
            den = jnp.where(lo, dens[0], dens[1])
            o_ref[rows + (cols,)] = (jnp.where(lo, pvs[0], pvs[1]) / den).astype(BF16)
            l_ref[rows + (cols,)] = jnp.where(lo, ms[0], ms[1]) + jnp.log2(den)


def _attention_group(qkv, bias, g):
    _, batch, r, length, _ = qkv.shape
    qb = min(ATTN_ROWS, length)
    cb = min(ATTN_ROWS // qb, r)
    nsub = qb // BLK

    def cur(which):
        return pl.BlockSpec((None, None, cb, qb, GROUP_W), lambda b, c, n: (which, b, c, n, 0))

    def prev(which):
        return pl.BlockSpec((None, None, cb, BLK, GROUP_W),
                            lambda b, c, n: (which, b, c, jnp.maximum(n * nsub - 1, 0), 0))

    out_spec = pl.BlockSpec((None, cb, qb, GROUP_W), lambda b, c, n: (b, c, n, 0))
    return pl.pallas_call(
        functools.partial(_attn_body, nsub=nsub),
        grid=(batch, r // cb, length // qb),
        in_specs=[cur(0), prev(1), cur(1), prev(2), cur(2),
                  pl.BlockSpec((HEADS, BLK, 2 * BLK), lambda b, c, n: (0, 0, 0))],
        out_specs=[out_spec, out_spec],
        out_shape=[
            jax.ShapeDtypeStruct((batch, r, length, GROUP_W), BF16),
            jax.ShapeDtypeStruct((batch, r, length, GROUP_W), F32),
        ],
        compiler_params=_cparams(("parallel", "parallel", "arbitrary")),
        name=f"attention_g{g}",
    )(qkv, qkv, qkv, qkv, qkv, bias)


def _ssm_tables(lam_re, lam_im, log_dt, b_re, b_im, c_re, c_im):
    lam = lax.complex(lam_re.astype(F32), lam_im.astype(F32))
    dt = jnp.exp(log_dt.astype(F32))[:, None]
    lam_dt = lam * dt
    lam_bar = jnp.exp(lam_dt)
    b = lax.complex(b_re.astype(F32), b_im.astype(F32))
    b_bar = ((lam_bar - 1.0) / lam)[..., None] * b
    half = CHUNK // 2
    t = jnp.arange(CHUNK, dtype=F32)

    def power(k):
        return jnp.exp(lam_dt[:, None, :] * jnp.reshape(jnp.asarray(k, F32), (1, -1, 1)))

    p_fwd = power(t - half)
    p_bwd = 1.0 / p_fwd

    def in_pair(pw):
        return jnp.stack([jnp.concatenate([pw.real, pw.imag], axis=-1),
                          jnp.concatenate([-pw.imag, pw.real], axis=-1)], axis=1)

    def out_pair(pw):
        return jnp.stack([jnp.concatenate([pw.real, -pw.imag], axis=-1),
                          jnp.concatenate([-pw.imag, -pw.real], axis=-1)], axis=1)

    powers = jnp.stack([
        in_pair(p_bwd),
        in_pair(p_bwd * power(CHUNK - 1.0 - half)),
        out_pair(p_fwd),
        out_pair(p_fwd * power(half + 1.0)),
    ], axis=1)
    dup = lambda a: jnp.concatenate([a, a], axis=-1)
    b_cp = jnp.transpose(b_bar, (0, 2, 1))
    coefs = jnp.stack([dup(b_cp.real), dup(b_cp.imag), dup(c_re.astype(F32)), dup(c_im.astype(F32))],
                      axis=1)
    lc = power([float(CHUNK * 2 ** k) for k in range(SCAN_LEVELS)])
    l1 = jnp.concatenate([lc.real, lc.real], axis=-1)
    l2 = jnp.concatenate([-lc.imag, lc.imag], axis=-1)
    return powers, coefs, l1, l2


def _gelu_tanh(y):
    return y * jax.nn.sigmoid(1.5957691216057308 * (y + 0.044715 * (y * y * y)))


def _ssm_body(d_ref, u_ref, pw_ref, cf_ref, l1_ref, l2_ref, z_ref, mask_scr, z_scr, a_scr, w_scr, *,
              ncb):
    g = pl.program_id(0)
    nc = u_ref.shape[0]
    width = SSM_CH * CHUNK
    cb = mask_scr.shape[1]

    @pl.when(g == 0)
    def _():
        s_idx = lax.broadcasted_iota(jnp.int32, (width, cb), 0) & (CHUNK - 1)
        t_idx = lax.broadcasted_iota(jnp.int32, (width, cb), 1) & (CHUNK - 1)
        mask_scr[...] = jnp.where(t_idx >= s_idx, -1, 0).astype(jnp.int32)

    def expand(kind, c1, c2, ch):
        return (cf_ref[c1, ch:ch + 1, :] * pw_ref[kind, 0] + cf_ref[c2, ch:ch + 1, :] * pw_ref[kind, 1])

    for ch in range(SSM_CH):
        rows = slice(ch * CHUNK, (ch + 1) * CHUNK)
        a_scr[rows, :] = expand(0, 0, 1, ch).astype(BF16)
        w_scr[rows, :] = expand(1, 0, 1, ch).astype(BF16)

    u2 = u_ref.reshape(nc * SSM_CH, CHUNK)
    us =[u2[pl.ds(c, nc, stride=SSM_CH), :] for c in range(SSM_CH)]
    x = jnp.concatenate(us, axis=1).astype(BF16)

    acc = jnp.dot(x, w_scr[...], preferred_element_type=F32)
    rmod = lax.broadcasted_iota(jnp.int32, (nc, 2 * SSM_P), 0) & (ncb - 1)
    for k in range(ncb.bit_length() - 1):
        d = 1 << k
        sh = jnp.where(rmod >= d, pltpu.roll(acc, d, 0), 0.0)
        acc = acc + sh * l1_ref[k:k + 1, :] + pltpu.roll(sh, SSM_P, 1) * l2_ref[k:k + 1, :]
    x_in = jnp.where(rmod >= 1, pltpu.roll(acc, 1, 0), 0.0).astype(BF16)

    per = cb // CHUNK
    for k in range(width // cb):
        chans = range(k * per, (k + 1) * per)
        d_rhs = jnp.concatenate([expand(2, 2, 3, ch).T.astype(BF16) for ch in chans], axis=1)
        v_rhs = jnp.concatenate([expand(3, 2, 3, ch).T.astype(BF16) for ch in chans], axis=1)
        mk = jnp.dot(a_scr[...], d_rhs, preferred_element_type=F32)
        kept = lax.bitcast_convert_type(mk, jnp.int32) & mask_scr[...]
        m_k = lax.bitcast_convert_type(kept, F32).astype(BF16)
        y = (jnp.dot(x, m_k, preferred_element_type=F32)
             + jnp.dot(x_in, v_rhs, preferred_element_type=F32))
        for j, c in enumerate(chans):
            yc = y[:, j * CHUNK:(j + 1) * CHUNK] + d_ref[g * SSM_CH + c] * us[c]
            z_scr[pl.ds(c, nc, stride=SSM_CH), :] = _gelu_tanh(yc)
    z_ref[...] = z_scr[...].reshape(nc, SSM_CH, CHUNK)


def _ssm_scan(u3, d_skip, tables, nbatch):
    powers, coefs, l1, l2 = tables
    nc = u3.shape[0]
    ncb = nc // nbatch
    assert ncb & (ncb - 1) == 0 and ncb <= 2 ** SCAN_LEVELS
    width = SSM_CH * CHUNK
    grid_spec = pltpu.PrefetchScalarGridSpec(
        num_scalar_prefetch=1,
        grid=(SSM_G,),
        in_specs=[
            pl.BlockSpec((nc, SSM_CH, CHUNK), lambda g, d: (0, g, 0)),
            pl.BlockSpec((None,) + powers.shape[1:], lambda g, d: (g, 0, 0, 0, 0)),
            pl.BlockSpec((None,) + coefs.shape[1:], lambda g, d: (g, 0, 0, 0)),
            pl.BlockSpec((None, SCAN_LEVELS, 2 * SSM_P), lambda g, d: (g, 0, 0)),
            pl.BlockSpec((None, SCAN_LEVELS, 2 * SSM_P), lambda g, d: (g, 0, 0)),
        ],
        out_specs=pl.BlockSpec((nc, SSM_CH, CHUNK), lambda g, d: (0, g, 0)),
        scratch_shapes=[pltpu.VMEM((width, 4 * CHUNK), jnp.int32),
                        pltpu.VMEM((nc * SSM_CH, CHUNK), F32),
                        pltpu.VMEM((width, 2 * SSM_P), BF16), pltpu.VMEM((width, 2 * SSM_P), BF16)],
    )
    return pl.pallas_call(
        functools.partial(_ssm_body, ncb=ncb),
        grid_spec=grid_spec,
        out_shape=jax.ShapeDtypeStruct((nc, SSM_W, CHUNK), F32),
        compiler_params=_cparams(("arbitrary",)),
        name="ssm_scan",
    )(d_skip.astype(F32), u3, powers, coefs, l1, l2)


def _merge_body(o0, o1, o2, l0, l1, l2, zt_ref, ga_ref, gs_ref, x_ref,
                wglu_ref, bglu_ref, wab_ref, wsb_ref, wout_ref, out_ref, tok_scr):
    def token_major(ref, slot):
        r, rows, _ = ref.shape
        if r == 1:
            return ref[0].astype(F32)
        nl = GROUP_W // LANES
        scrs = [tok_scr.at[slot * nl + k] for k in range(nl)]
        for c in range(r):
            sub = ref[c].astype(F32)
            for k in range(nl):
                scrs[k][pl.ds(c, rows, stride=r), :] = sub[:, k * LANES:(k + 1) * LANES]
        return jnp.concatenate([s[...] for s in scrs], axis=1)

    a0, a1, a2 = token_major(l0, 0), token_major(l1, 0), token_major(l2, 1)
    v0, v1, v2 = token_major(o0, 0), token_major(o1, 2), token_major(o2, 3)
    mx = jnp.maximum(jnp.maximum(a0, a1), a2)
    e0, e1, e2 = jnp.exp2(a0 - mx), jnp.exp2(a1 - mx), jnp.exp2(a2 - mx)
    mix = (e0 * v0 + e1 * v1 + e2 * v2) / (e0 + e1 + e2)
    y_attn = jnp.dot(mix.astype(BF16), wab_ref[...], preferred_element_type=F32)

    z = jnp.concatenate([zt_ref[k].T for k in range(zt_ref.shape[0])], axis=0).astype(BF16)
    gl = jnp.dot(z, wglu_ref[...], preferred_element_type=F32) + bglu_ref[...]
    sg = z.astype(F32) * jax.nn.sigmoid(gl)
    y_ssm = jnp.dot(sg.astype(BF16), wsb_ref[...], preferred_element_type=F32)

    merged = (jax.nn.sigmoid(ga_ref[...].astype(F32)) * y_attn
              + jax.nn.sigmoid(gs_ref[...].astype(F32)) * y_ssm)
    out_ref[...] = x_ref[...] + jnp.dot(merged.astype(BF16), wout_ref[...], preferred_element_type=F32)


def _merge(os_, ls_, zt, gates, x2, wglu, bglu, wab, wsb, wout, tm=512):
    n = x2.shape[0]
    tiles_per_seq = os_[0].shape[2] // tm
    row = lambda i: (i, 0)
    const = lambda i: (0, 0)

    def group_spec(a):
        r = a.shape[1]
        return pl.BlockSpec((None, r, tm // r, GROUP_W),
                            lambda i: (i // tiles_per_seq, 0, i % tiles_per_seq, 0))

    in_specs = (
        [group_spec(a) for a in os_] + [group_spec(a) for a in ls_]
        + [
            pl.BlockSpec((tm // CHUNK, SSM_W, CHUNK), lambda i: (i, 0, 0)),
            pl.BlockSpec((tm, D_MODEL), lambda i: (i, 0)),
            pl.BlockSpec((tm, D_MODEL), lambda i: (i, 1)),
            pl.BlockSpec((tm, D_MODEL), row),
            pl.BlockSpec((SSM_W, SSM_W), const),
            pl.BlockSpec((1, SSM_W), const),
            pl.BlockSpec((GROUP_W, D_MODEL), const),
            pl.BlockSpec((SSM_W, D_MODEL), const),
            pl.BlockSpec((D_MODEL, D_MODEL), const),
        ]
    )
    return pl.pallas_call(
        _merge_body,
        grid=(n // tm,),
        in_specs=in_specs,
        out_specs=pl.BlockSpec((tm, D_MODEL), row),
        out_shape=jax.ShapeDtypeStruct((n, D_MODEL), F32),
        scratch_shapes=[pltpu.VMEM((4 * GROUP_W // LANES, tm, LANES), F32)],
        compiler_params=_cparams(("parallel",)),
        name="merge",
    )(*os_, *ls_, zt, gates, gates, x2, wglu, bglu, wab, wsb, wout)


def _ffn_body(x_ref, g_ref, wg_ref, wu_ref, wd_ref, *rest, tf):
    ncast = (len(rest) - 1) // 2
    o_ref = rest[ncast]
    for src, dst in zip(rest[:ncast], rest[ncast + 1:]):
        dst[...] = src[...].astype(BF16)
    h = _rms(x_ref[...], g_ref[...]).astype(BF16)
    for f in range(wg_ref.shape[1] // tf):
        cols = slice(f * tf, (f + 1) * tf)
        a = jnp.dot(h, wg_ref[:, cols], preferred_element_type=F32)
        b = jnp.dot(h, wu_ref[:, cols], preferred_element_type=F32)
        act = (a * jax.nn.sigmoid(a) * b).astype(BF16)
        part = jnp.dot(act, wd_ref[cols, :], preferred_element_type=F32)
        if f == 0:
            o_ref[...] = x_ref[...] + part
        else:
            o_ref[...] += part


def _dense_ffn(x2, g, wg, wu, wd, to_cast=(), tm=512, tf=256):
    n = x2.shape[0]
    dff = wg.shape[1]
    steps = n // tm
    assert dff % tf == 0 and all(a.shape[0] % (BF16_ROWS * steps) == 0 for a in to_cast)
    resident = lambda shape: pl.BlockSpec(shape, lambda i: (0, 0), pipeline_mode=pl.Buffered(1))
    slabs = [pl.BlockSpec((a.shape[0] // steps, a.shape[1]), lambda i: (i, 0)) for a in to_cast]
    out, *casts = pl.pallas_call(
        functools.partial(_ffn_body, tf=tf),
        grid=(steps,),
        in_specs=[
            pl.BlockSpec((tm, D_MODEL), lambda i: (i, 0)),
            pl.BlockSpec((1, D_MODEL), lambda i: (0, 0)),
            resident((D_MODEL, dff)),
            resident((D_MODEL, dff)),
            resident((dff, D_MODEL)),
        ] + slabs,
        out_specs=[pl.BlockSpec((tm, D_MODEL), lambda i: (i, 0))] + slabs,
        out_shape=[jax.ShapeDtypeStruct((n, D_MODEL), F32)]
        + [jax.ShapeDtypeStruct(a.shape, BF16) for a in to_cast],
        compiler_params=_cparams(("parallel",)),
        name="dense_ffn",
    )(x2, g, wg, wu, wd, *to_cast)
    return out, casts


def _route_top2(h, w, idx_ref, gate_ref, idxt_ref):
    h_hi, w_hi = h.astype(BF16), w.astype(BF16)
    h_lo = (h - h_hi.astype(F32)).astype(BF16)
    w_lo = (w - w_hi.astype(F32)).astype(BF16)
    both = jnp.dot(h_hi, jnp.concatenate([w_hi, w_lo], axis=1), preferred_element_type=F32)
    logits = both[:, :LANES] + (both[:, LANES:] + jnp.dot(h_lo, w_hi, preferred_element_type=F32))
    lane = lax.broadcasted_iota(jnp.int32, logits.shape, 1)
    lane_f = lane.astype(F32)
    logits = jnp.where(lane < N_EXPERTS, logits, -jnp.inf)
    v1 = jnp.max(logits, axis=-1, keepdims=True)
    i1 = jnp.min(jnp.where(logits == v1, lane_f, float(LANES)), axis=-1, keepdims=True)
    rest = jnp.where(lane_f == i1, -jnp.inf, logits)
    v2 = jnp.max(rest, axis=-1, keepdims=True)
    i2 = jnp.min(jnp.where(rest == v2, lane_f, float(LANES)), axis=-1, keepdims=True)
    e = jnp.exp(v2 - v1)
    g1 = 1.0 / (1.0 + e)
    g2 = e / (1.0 + e)
    idx_f = jnp.where(lane == 0, i1, jnp.where(lane == 1, i2, 0.0))
    idx_ref[...] = idx_f.astype(jnp.int32)
    gate_ref[...] = jnp.where(lane == 0, g1, jnp.where(lane == 1, g2, 0.0))
    idxt_ref[...] = idx_f.T[:idxt_ref.shape[0], :].astype(jnp.int32)


def _router_body(x_ref, g_ref, wr_ref, idx_ref, gate_ref, idxt_ref):
    _route_top2(_rms(x_ref[...], g_ref[...]), wr_ref[...], idx_ref, gate_ref, idxt_ref)


def _router(x2, g, wr_pad, tm=1024):
    n = x2.shape[0]
    return pl.pallas_call(
        _router_body,
        grid=(n // tm,),
        in_specs=[
            pl.BlockSpec((tm, D_MODEL), lambda i: (i, 0)),
            pl.BlockSpec((1, D_MODEL), lambda i: (0, 0)),
            pl.BlockSpec((D_MODEL, LANES), lambda i: (0, 0)),
        ],
        out_specs=[
            pl.BlockSpec((tm, LANES), lambda i: (i, 0)),
            pl.BlockSpec((tm, LANES), lambda i: (i, 0)),
            pl.BlockSpec((8, tm), lambda i: (0, i)),
        ],
        out_shape=[
            jax.ShapeDtypeStruct((n, LANES), jnp.int32),
            jax.ShapeDtypeStruct((n, LANES), F32),
            jax.ShapeDtypeStruct((8, n), jnp.int32),
        ],
        compiler_params=_cparams(("parallel",)),
        name="router",
    )(x2, g, wr_pad)


def _pack_bf16_pairs(hb):
    half = hb.shape[1] // 2
    lo = lax.bitcast_convert_type(hb[:, :half].astype(F32), jnp.uint32)
    hi = lax.bitcast_convert_type(hb[:, half:].astype(F32), jnp.uint32)
    return (hi & jnp.uint32(0xFFFF0000)) | (lo >> 16)


def _unpack_bf16_pairs(xu):
    lo = lax.bitcast_convert_type(xu << 16, F32).astype(BF16)
    hi = lax.bitcast_convert_type(xu & jnp.uint32(0xFFFF0000), F32).astype(BF16)
    return lo, hi


def _for_each_run_piece(i, start_ref, loff_ref, len_ref, fn, fill):
    for e in range(N_EXPERTS):
        j = i * N_EXPERTS + e
        length, boff, soff = len_ref[j], loff_ref[j], start_ref[j]
        done = 0
        for p in RUN_PIECES:
            cond = (length & p) != 0
            fn(cond, pl.multiple_of(boff + done, RUN_ALIGN), pl.multiple_of(soff + done, RUN_ALIGN), p)
            done = done + jnp.where(cond, p, 0)
    last = i * N_EXPERTS + N_EXPERTS - 1
    used = loff_ref[last] + len_ref[last]
    rest, done = CBUF_ROWS - used, 0
    for p in FILL_PIECES:
        cond = (rest & p) != 0
        fill(cond, pl.multiple_of(used + done, RUN_ALIGN), pl.multiple_of(done, RUN_ALIGN), p)
        done = done + jnp.where(cond, p, 0)


def _dispatch_body(start_ref, loff_ref, len_ref, tail_ref, x_ref, g_ref, idxt_ref, off_ref, cnt0_ref,
                   xs_ref, cbuf, tri_scr, zero_scr, sem, zsem):
    i = pl.program_id(0)
    tt = x_ref.shape[0]

    @pl.when(i == 0)
    def _():
        r = lax.broadcasted_iota(jnp.int32, (tt, tt), 0)
        c = lax.broadcasted_iota(jnp.int32, (tt, tt), 1)
        tri_scr[...] = jnp.where(r < c, 1.0, 0.0).astype(BF16)
        zero_scr[...] = jnp.zeros_like(zero_scr)

        def fill(e):
            row = pl.multiple_of(jnp.maximum(tail_ref[e], 0), MOE_BM)
            return pltpu.make_async_copy(zero_scr, xs_ref.at[pl.ds(row, MOE_BM)], zsem)

        for e in range(tail_ref.shape[0]):
            pl.when(tail_ref[e] >= 0)(lambda e=e: fill(e).start())
        for e in range(tail_ref.shape[0]):
            pl.when(tail_ref[e] >= 0)(lambda e=e: fill(e).wait())
        nspare = 2 * FILL_ROWS
        spare_fill = pltpu.make_async_copy(zero_scr.at[pl.ds(0, nspare)],
                                           xs_ref.at[pl.ds(xs_ref.shape[0] - nspare, nspare)], zsem)
        spare_fill.start()
        spare_fill.wait()

    hb = _rms(x_ref[...], g_ref[...]).astype(BF16)

    sub = lax.broadcasted_iota(jnp.int32, (N_EXPERTS, tt), 0)
    pos = []
    for k in range(2):
        oh = jnp.where(sub == idxt_ref[k:k + 1, :], 1.0, 0.0)
        rank = jnp.dot(oh.astype(BF16), tri_scr[...], preferred_element_type=F32)
        base = off_ref[:, :1] if k == 0 else off_ref[:, :1] + cnt0_ref[:, :1]
        pos.append(jnp.sum(oh * (base + rank), axis=0, keepdims=True))

    rows = lax.broadcasted_iota(jnp.int32, (cbuf.shape[1], tt), 0).astype(F32)
    perm = (jnp.where(rows == pos[0], 1.0, 0.0) + jnp.where(rows == pos[1], 1.0, 0.0)).astype(BF16)
    slot = i % 2
    cbuf[slot] = _pack_bf16_pairs(jnp.dot(perm, hb, preferred_element_type=F32).astype(BF16))

    spare = xs_ref.shape[0] - 2 * FILL_ROWS

    def run_piece(cond, brow, srow, p):
        cp = pltpu.make_async_copy(cbuf.at[slot, pl.ds(brow, p)], xs_ref.at[pl.ds(srow, p)], sem.at[slot])
        pl.when(cond)(cp.start)

    def fill_piece(cond, brow, frow, p):
        run_piece(cond, brow, pl.multiple_of(spare + slot * FILL_ROWS + frow, RUN_ALIGN), p)

    _for_each_run_piece(i, start_ref, loff_ref, len_ref, run_piece, fill_piece)

    def drain(buf):
        pltpu.make_async_copy(cbuf.at[buf], xs_ref.at[pl.ds(0, CBUF_ROWS)], sem.at[buf]).wait()

    pl.when(i > 0)(lambda: drain(1 - slot))
    pl.when(i == pl.num_programs(0) - 1)(lambda: drain(slot))


def _dispatch(plan, x2, g, idxt):
    n = x2.shape[0]
    tt = MOE_TT
    smem = lambda i, *_: (i, 0, 0)
    grid_spec = pltpu.PrefetchScalarGridSpec(
        num_scalar_prefetch=4,
        grid=(n // tt,),
        in_specs=[
            pl.BlockSpec((tt, D_MODEL), lambda i, *_: (i, 0)),
            pl.BlockSpec((1, D_MODEL), lambda i, *_: (0, 0)),
            pl.BlockSpec((8, tt), lambda i, *_: (0, i)),
            pl.BlockSpec((None, N_EXPERTS, LANES), smem),
            pl.BlockSpec((None, N_EXPERTS, LANES), smem),
        ],
        out_specs=pl.BlockSpec(memory_space=pl.ANY),
        scratch_shapes=[
            pltpu.VMEM((2, CBUF_ROWS, D_MODEL // 2), jnp.uint32),
            pltpu.VMEM((tt, tt), BF16),
            pltpu.VMEM((MOE_BM, D_MODEL // 2), jnp.uint32),
            pltpu.SemaphoreType.DMA((2,)),
            pltpu.SemaphoreType.DMA(()),
        ],
    )
    return pl.pallas_call(
        _dispatch_body,
        grid_spec=grid_spec,
        out_shape=jax.ShapeDtypeStruct((plan["n_slots"] + 2 * FILL_ROWS, D_MODEL // 2), jnp.uint32),
        compiler_params=_cparams(("arbitrary",)),
        name="moe_dispatch",
    )(plan["start"], plan["loff"], plan["len8"], plan["tail_rows"], x2, g, idxt,
      plan["off_lanes"], plan["cnt0_lanes"])


def _experts_body(be_ref, nu_ref, xs_ref, wg_ref, wu_ref, wd_ref, ys_ref, acc_scr, *, tf):
    del be_ref
    i = pl.program_id(0)

    @pl.when(i < nu_ref[0])
    def _():
        x = jnp.concatenate(_unpack_bf16_pairs(xs_ref[...]), axis=1)
        for f in range(wg_ref.shape[1] // tf):
            cols = slice(f * tf, (f + 1) * tf)
            a = jnp.dot(x, wg_ref[:, cols], preferred_element_type=F32)
            b = jnp.dot(x, wu_ref[:, cols], preferred_element_type=F32)
            act = (a * jax.nn.sigmoid(a) * b).astype(BF16)
            part = jnp.dot(act, wd_ref[cols, :], preferred_element_type=F32)
            if f == 0:
                acc_scr[...] = part
            else:
                acc_scr[...] += part
        ys_ref[...] = _pack_bf16_pairs(acc_scr[...].astype(BF16))

    @pl.when(i >= nu_ref[0])
    def _():
        ys_ref[...] = jnp.zeros_like(ys_ref)


def _experts(block_e, n_used, xs, wg, wu, wd, tf=256):
    n_blocks = xs.shape[0] // MOE_BM
    n_slots = n_blocks * MOE_BM
    dff = wg.shape[2]

    def blk(i, nu):
        return jnp.minimum(i, nu[0] - 1)

    def wspec(rows, cols):
        return pl.BlockSpec((None, rows, cols), lambda i, be, nu: (be[blk(i, nu)], 0, 0))

    grid_spec = pltpu.PrefetchScalarGridSpec(
        num_scalar_prefetch=2,
        grid=(n_blocks,),
        in_specs=[
            pl.BlockSpec((MOE_BM, D_MODEL // 2), lambda i, be, nu: (blk(i, nu), 0)),
            wspec(D_MODEL, dff),
            wspec(D_MODEL, dff),
            wspec(dff, D_MODEL),
        ],
        out_specs=pl.BlockSpec((MOE_BM, D_MODEL // 2), lambda i, be, nu: (i, 0)),
        scratch_shapes=[pltpu.VMEM((MOE_BM, D_MODEL), F32)],
    )
    return pl.pallas_call(
        functools.partial(_experts_body, tf=tf),
        grid_spec=grid_spec,
        out_shape=jax.ShapeDtypeStruct((n_slots, D_MODEL // 2), jnp.uint32),
        compiler_params=_cparams(("arbitrary",)),
        name="moe_experts",
    )(block_e, n_used, xs, wg, wu, wd)


def _combine_body(start_ref, loff_ref, len_ref, x_ref, idx_ref, gate_ref, meta_ref, g_ref, ys_ref, o_ref,
                  ybuf, tri_scr, sem):
    i = pl.program_id(0)
    tt = x_ref.shape[0]
    slot = i % 2

    def fetch(tile, buf):
        def run_piece(cond, brow, srow, p):
            cp = pltpu.make_async_copy(ys_ref.at[pl.ds(srow, p)], ybuf.at[buf, pl.ds(brow, p)], sem.at[buf])
            pl.when(cond)(cp.start)

        _for_each_run_piece(tile, start_ref, loff_ref, len_ref, run_piece, run_piece)

    @pl.when(i == 0)
    def _():
        r = lax.broadcasted_iota(jnp.int32, (tt, tt), 0)
        c = lax.broadcasted_iota(jnp.int32, (tt, tt), 1)
        tri_scr[...] = jnp.where(c < r, 1.0, 0.0).astype(BF16)
        fetch(i, slot)

    pl.when(i + 1 < pl.num_programs(0))(lambda: fetch(i + 1, 1 - slot))

    lane = lax.broadcasted_iota(jnp.int32, (tt, LANES), 1)
    idx = idx_ref[...]
    gt = gate_ref[...]
    cols = lax.broadcasted_iota(jnp.int32, (tt, ybuf.shape[1]), 1).astype(F32)
    sel = None
    for k in range(2):
        oh = jnp.where(lane == idx[:, k:k + 1], 1.0, 0.0)
        rank = jnp.dot(tri_scr[...], oh.astype(BF16), preferred_element_type=F32)
        base = meta_ref[0:1, :] if k == 0 else meta_ref[0:1, :] + meta_ref[1:2, :]
        pos = jnp.sum(oh * (base + rank), axis=1, keepdims=True)
        term = jnp.where(cols == pos, gt[:, k:k + 1], 0.0)
        sel = term if sel is None else sel + term
    sel = sel.astype(BF16)

    pltpu.make_async_copy(ys_ref.at[pl.ds(0, CBUF_ROWS)], ybuf.at[slot], sem.at[slot]).wait()

    lo, hi = _unpack_bf16_pairs(ybuf[slot])
    y = jnp.concatenate([jnp.dot(sel, lo, preferred_element_type=F32),
                         jnp.dot(sel, hi, preferred_element_type=F32)], axis=1)
    o_ref[...] = _rms(x_ref[...] + y, g_ref[...])


def _combine(plan, x2, idx, gates, g_final, ys):
    n = x2.shape[0]
    tt = MOE_TT
    grid_spec = pltpu.PrefetchScalarGridSpec(
        num_scalar_prefetch=3,
        grid=(n // tt,),
        in_specs=[
            pl.BlockSpec((tt, D_MODEL), lambda i, *_: (i, 0)),
            pl.BlockSpec((tt, LANES), lambda i, *_: (i, 0)),
            pl.BlockSpec((tt, LANES), lambda i, *_: (i, 0)),
            pl.BlockSpec((None, 8, LANES), lambda i, *_: (i, 0, 0)),
            pl.BlockSpec((1, D_MODEL), lambda i, *_: (0, 0)),
            pl.BlockSpec(memory_space=pl.ANY),
        ],
        out_specs=pl.BlockSpec((tt, D_MODEL), lambda i, *_: (i, 0)),
        scratch_shapes=[
            pltpu.VMEM((2, CBUF_ROWS, D_MODEL // 2), jnp.uint32),
            pltpu.VMEM((tt, tt), BF16),
            pltpu.SemaphoreType.DMA((2,)),
        ],
    )
    return pl.pallas_call(
        _combine_body,
        grid_spec=grid_spec,
        out_shape=jax.ShapeDtypeStruct((n, D_MODEL), F32),
        compiler_params=_cparams(("arbitrary",)),
        name="moe_combine",
    )(plan["start"], plan["loff"], plan["len8"], x2, idx, gates, plan["meta_rows"], g_final, ys)


def _route_plan(idxt, n):
    nt = n // MOE_TT
    e2 = idxt[:2].reshape(2, nt, MOE_TT)
    oh = (e2[..., None] == jnp.arange(N_EXPERTS, dtype=jnp.int32)).astype(jnp.int32)
    cnt = jnp.sum(oh, axis=2)
    cnt0 = cnt[0]
    len8 = (cnt[0] + cnt[1] + RUN_ALIGN - 1) // RUN_ALIGN * RUN_ALIGN
    loff = jnp.cumsum(len8, axis=1) - len8
    region = jnp.sum(len8, axis=0)
    padded = (region + MOE_BM - 1) // MOE_BM * MOE_BM
    pad_end = jnp.cumsum(padded)
    start = (pad_end - padded)[None, :] + jnp.cumsum(len8, axis=0) - len8
    n_blocks = (2 * n + nt * N_EXPERTS * (RUN_ALIGN - 1) + MOE_BM - 1) // MOE_BM + N_EXPERTS
    starts = jnp.arange(n_blocks, dtype=jnp.int32) * MOE_BM
    block_e = jnp.sum((starts[:, None] >= pad_end[None, :]).astype(jnp.int32), axis=1)
    block_e = jnp.minimum(block_e, N_EXPERTS - 1).astype(jnp.int32)
    n_used = (pad_end[-1] // MOE_BM).astype(jnp.int32).reshape(1)
    tails = jnp.where(padded > 0, pad_end - MOE_BM, -1)
    spare = pad_end[-1] + jnp.arange(n_blocks - (2 * n) // MOE_BM, dtype=pad_end.dtype) * MOE_BM
    spare = jnp.where(spare < n_blocks * MOE_BM, spare, -1)
    lanes = lambda a: jnp.broadcast_to(a.astype(F32)[:, :, None], (nt, N_EXPERTS, LANES))
    meta_rows = jnp.zeros((nt, 8, LANES), F32)
    meta_rows = meta_rows.at[:, 0, :N_EXPERTS].set(loff.astype(F32)).at[:, 1, :N_EXPERTS].set(cnt0.astype(F32))
    flat = lambda a: a.reshape(-1).astype(jnp.int32)
    return dict(start=flat(start), loff=flat(loff), len8=flat(len8), block_e=block_e, n_used=n_used,
                tail_rows=jnp.concatenate([tails, spare]).astype(jnp.int32),
                off_lanes=lanes(loff), cnt0_lanes=lanes(cnt0), meta_rows=meta_rows,
                n_slots=n_blocks * MOE_BM)


def _mixer_layer(x2, batch, seq, rel_bias, norm_g, w_in_bf, ssm, d_skip, w_glu, b_glu,
                 w_attn_br, w_ssm_br, w_out):
    *qkvs, gates, ut = _in_projection(x2, norm_g.reshape(1, D_MODEL), w_in_bf, batch, seq)

    os_, ls_ = [], []
    for g, (window, dilation) in enumerate(ATTN_GROUPS):
        bias = _band_bias(rel_bias[:, g * HEADS:(g + 1) * HEADS], window, dilation)
        o, l = _attention_group(qkvs[g], bias, g)
        os_.append(o)
        ls_.append(l)

    zt = _ssm_scan(ut, d_skip, _ssm_tables(*ssm), batch)

    return _merge(os_, ls_, zt, gates, x2, w_glu.astype(BF16), b_glu.reshape(1, SSM_W).astype(F32),
                  w_attn_br.astype(BF16), w_ssm_br.astype(BF16), w_out.astype(BF16))


def kernel(x, rel_bias, norm1_g, w_in, ssm_lam_re, ssm_lam_im, ssm_log_dt, ssm_b_re, ssm_b_im, ssm_c_re, ssm_c_im, ssm_d, w_glu, b_glu, w_attn_br, w_ssm_br, w_out, norm2_g, ffn_w_gate, ffn_w_up, ffn_w_down, moe_router, moe_w_gate, moe_w_up, moe_w_down, final_norm_g):
    batch, seq, d = x.shape
    assert d == D_MODEL and norm1_g.shape[0] == 2 and seq % (16 * BLK) == 0
    n = batch * seq
    x2 = x.reshape(n, d)

    def mixer(x2, l, w_in_bf):
        ssm = (ssm_lam_re[l], ssm_lam_im[l], ssm_log_dt[l], ssm_b_re[l], ssm_b_im[l],
               ssm_c_re[l], ssm_c_im[l])
        return _mixer_layer(x2, batch, seq, rel_bias, norm1_g[l], w_in_bf, ssm, ssm_d[l], w_glu[l],
                            b_glu[l], w_attn_br[l], w_ssm_br[l], w_out[l])

    x2 = mixer(x2, 0, w_in[0].astype(BF16))
    moe_w = (moe_w_gate[0], moe_w_up[0], moe_w_down[0])
    x2, (w_in1_bf, *moe_bf) = _dense_ffn(
        x2, norm2_g[0].reshape(1, d), ffn_w_gate[0].astype(BF16), ffn_w_up[0].astype(BF16),
        ffn_w_down[0].astype(BF16), to_cast=[w_in[1]] + [w.reshape(-1, w.shape[2]) for w in moe_w])
    moe_bf = [b.reshape(w.shape) for b, w in zip(moe_bf, moe_w)]

    x2 = mixer(x2, 1, w_in1_bf)
    g2 = norm2_g[1].reshape(1, d)
    wr_pad = jnp.zeros((d, LANES), F32).at[:, :N_EXPERTS].set(moe_router[0].astype(F32))
    idx, gates, idxt = _router(x2, g2, wr_pad)
    plan = _route_plan(idxt, n)
    xs = _dispatch(plan, x2, g2, idxt)
    ys = _experts(plan["block_e"], plan["n_used"], xs, *moe_bf)
    out = _combine(plan, x2, idx, gates, final_norm_g.reshape(1, d), ys)
    return out.reshape(batch, seq, d)
```

```python
import functools

import numpy as np
import jax
import jax.numpy as jnp
from jax import lax
from jax.experimental import pallas as pl
from jax.experimental.pallas import tpu as pltpu

F32 = jnp.float32
BF16 = jnp.bfloat16

LANES = 128
BF16_ROWS = 16

D_MODEL = 1024
HEAD_DIM = 64
ATTN_GROUPS = ((128, 1), (512, 4), (2048, 16))
N_GROUPS = 3
HEADS = 8
GROUP_W = HEADS * HEAD_DIM
ATTN_W = N_GROUPS * GROUP_W
BLK = 128
ATTN_ROWS = 2048
REL_BUCKETS = 32
REL_MAX_DIST = 2048
NEG_INF = -1e30
LOG2E = 1.4426950408889634
SSM_CH = 16
SSM_W = D_MODEL // 2
SSM_G = SSM_W // SSM_CH
SSM_P = 64
N_EXPERTS = 8
MOE_BM = 512
MOE_TT = 512
RUN_ALIGN = 8
RUN_PIECES = tuple(1 << b for b in range(MOE_TT.bit_length() - 1, RUN_ALIGN.bit_length() - 2, -1))
CBUF_ROWS = -(-(2 * MOE_TT + N_EXPERTS * (RUN_ALIGN - 1)) // BF16_ROWS) * BF16_ROWS
FILL_ROWS = CBUF_ROWS - 2 * MOE_TT
FILL_PIECES = tuple(1 << b for b in range(FILL_ROWS.bit_length() - 1, RUN_ALIGN.bit_length() - 2, -1))
EPS = 1e-6
CHUNK = 128
SCAN_LEVELS = 8

VMEM_LIMIT = 56 * 1024 * 1024


def _cparams(sem):
    return pltpu.CompilerParams(dimension_semantics=sem, vmem_limit_bytes=VMEM_LIMIT)


def _rms(x, g):
    return x * lax.rsqrt(jnp.mean(x * x, axis=-1, keepdims=True) + EPS) * g


def _proj_body(x_ref, g_ref, w_ref, qkv0_ref, qkv1_ref, qkv2_ref, gate_ref, ut_ref, d_scr):
    tm = x_ref.shape[0]
    u_lo = 3 * ATTN_W
    h = _rms(x_ref[...], g_ref[...])
    hb = h.astype(BF16)
    u = jnp.dot(hb, w_ref[:, u_lo:u_lo + SSM_W], preferred_element_type=F32)
    for k in range(ut_ref.shape[0]):
        ut_ref[k] = u[k * CHUNK:(k + 1) * CHUNK, :].T

    nl = D_MODEL // LANES
    for k in range(nl):
        d_scr[k] = h[:, k * LANES:(k + 1) * LANES]

    def by_subsequence(r):
        blocks = [jnp.concatenate([d_scr.at[k][pl.ds(c, tm // r, stride=r), :] for k in range(nl)], axis=1)
                  for c in range(r)]
        return jnp.concatenate(blocks, axis=0).astype(BF16)

    cw = 2 * LANES
    scale = (HEAD_DIM ** -0.5 * LOG2E, None, None)
    for g, (out_ref, (_, r)) in enumerate(zip((qkv0_ref, qkv1_ref, qkv2_ref), ATTN_GROUPS)):
        lhs = hb if r == 1 else by_subsequence(r)
        for which in range(3):
            for lo in range(0, GROUP_W, cw):
                col = which * ATTN_W + g * GROUP_W + lo
                res = jnp.dot(lhs, w_ref[:, col:col + cw], preferred_element_type=F32)
                if scale[which] is not None:
                    res = res * scale[which]
                res = res.astype(BF16)
                for c in range(r):
                    out_ref[which, c, :, lo:lo + cw] = res[c * (tm // r):(c + 1) * (tm // r), :]
    for lo in range(0, 2 * D_MODEL, cw):
        col = u_lo + SSM_W + lo
        gate_ref[:, lo:lo + cw] = jnp.dot(hb, w_ref[:, col:col + cw], preferred_element_type=F32).astype(BF16)


def _in_projection(x2, g, w_bf, batch, seq, tm=512):
    n = x2.shape[0]
    tiles_per_seq = seq // tm
    wcols = w_bf.shape[1]

    def qkv_spec(r):
        return pl.BlockSpec((3, None, r, tm // r, GROUP_W),
                            lambda i: (0, i // tiles_per_seq, 0, i % tiles_per_seq, 0))

    return pl.pallas_call(
        _proj_body,
        grid=(n // tm,),
        in_specs=[
            pl.BlockSpec((tm, D_MODEL), lambda i: (i, 0)),
            pl.BlockSpec((1, D_MODEL), lambda i: (0, 0)),
            pl.BlockSpec((D_MODEL, wcols), lambda i: (0, 0), pipeline_mode=pl.Buffered(1)),
        ],
        out_specs=[qkv_spec(r) for _, r in ATTN_GROUPS] + [
            pl.BlockSpec((tm, 2 * D_MODEL), lambda i: (i, 0)),
            pl.BlockSpec((tm // CHUNK, SSM_W, CHUNK), lambda i: (i, 0, 0)),
        ],
        out_shape=[jax.ShapeDtypeStruct((3, batch, r, seq // r, GROUP_W), BF16) for _, r in ATTN_GROUPS] + [
            jax.ShapeDtypeStruct((n, 2 * D_MODEL), BF16),
            jax.ShapeDtypeStruct((n // CHUNK, SSM_W, CHUNK), F32),
        ],
        scratch_shapes=[pltpu.VMEM((D_MODEL // LANES, tm, LANES), F32)],
        compiler_params=_cparams(("parallel",)),
        name="in_projection",
    )(x2, g, w_bf)


def _t5_bucket(dist):
    max_exact = REL_BUCKETS // 2
    d = np.maximum(dist, 1).astype(np.float64)
    large = max_exact + (
        np.log(d / max_exact) / np.log(REL_MAX_DIST / max_exact) * (REL_BUCKETS - max_exact)
    ).astype(np.int32)
    large = np.minimum(large, REL_BUCKETS - 1)
    return np.where(dist < max_exact, dist, large).astype(np.int32)


def _band_bias(table, window, dilation):
    steps = window // dilation
    qi = np.arange(BLK)[:, None]
    kj = np.arange(2 * BLK)[None, :]
    delta = BLK + qi - kj
    band = (delta >= 0) & (delta <= steps)
    bucket = _t5_bucket(np.clip(delta, 0, steps) * dilation)
    onehot = np.eye(REL_BUCKETS, dtype=np.float32)[bucket]
    bias = jnp.einsum("qkb,bh->hqk", onehot, table.astype(F32), precision=lax.Precision.HIGHEST)
    return jnp.where(band[None], bias * LOG2E, NEG_INF)


def _attn_body(q_ref, kp_ref, kc_ref, vp_ref, vc_ref, bias_ref, o_ref, l_ref, *, nsub):
    lane = lax.broadcasted_iota(jnp.int32, (BLK, LANES), 1)
    lo = lane < HEAD_DIM
    keep_lo = jnp.where(lo, 1.0, 0.0).astype(BF16)
    keep_hi = jnp.where(lo, 0.0, 1.0).astype(BF16)
    col = lax.broadcasted_iota(jnp.int32, (BLK, 2 * BLK), 1)
    first_pen = jnp.where(col < BLK, jnp.where(pl.program_id(2) == 0, NEG_INF, 0.0), 0.0)
    for c, i in [(c, i) for c in range(q_ref.shape[0]) for i in range(nsub)]:
        rows = (c, slice(i * BLK, (i + 1) * BLK))
        q = q_ref[rows]
        if i == 0:
            kw = jnp.concatenate([kp_ref[c], kc_ref[c, 0:BLK, :]], axis=0)
            vw = jnp.concatenate([vp_ref[c], vc_ref[c, 0:BLK, :]], axis=0)
        else:
            kw = kc_ref[c, (i - 1) * BLK:(i + 1) * BLK, :]
            vw = vc_ref[c, (i - 1) * BLK:(i + 1) * BLK, :]
        for hp in range(HEADS // 2):
            cols = slice(hp * LANES, (hp + 1) * LANES)
            q2, k2, v2 = q[:, cols], kw[:, cols], vw[:, cols]
            pvs, ms, dens = [], [], []
            for half in range(2):
                qm = q2 * (keep_lo, keep_hi)[half]
                s = lax.dot_general(qm, k2, (((1,), (1,)), ((), ())), preferred_element_type=F32)
                s = s + bias_ref[2 * hp + half]
                if i == 0:
                    s = s + first_pen
                m = jnp.max(s, axis=-1, keepdims=True)
                p = jnp.exp2(s - m)
                dens.append(jnp.broadcast_to(jnp.sum(p, axis=-1, keepdims=True), (BLK, LANES)))
                ms.append(jnp.broadcast_to(m, (BLK, LANES)))
                pvs.append(jnp.dot(p.astype(BF16), v2, preferred_element_type=F32))
            den = jnp.where(lo, dens[0], dens[1])
            o_ref[rows + (cols,)] = (jnp.where(lo, pvs[0], pvs[1]) / den).astype(BF16)
            l_ref[rows + (cols,)] = jnp.where(lo, ms[0], ms[1]) + jnp.log2(den)


def _attention_group(qkv, bias, g):
    _, batch, r, length, _ = qkv.shape
    qb = min(ATTN_ROWS, length)
    cb = min(ATTN_ROWS // qb, r)
    nsub = qb // BLK

    def cur(which):
        return pl.BlockSpec((None, None, cb, qb, GROUP_W), lambda b, c, n: (which, b, c, n, 0))

    def prev(which):
        return pl.BlockSpec((None, None, cb, BLK, GROUP_W),
                            lambda b, c, n: (which, b, c, jnp.maximum(n * nsub - 1, 0), 0))

    out_spec = pl.BlockSpec((None, cb, qb, GROUP_W), lambda b, c, n: (b, c, n, 0))
    return pl.pallas_call(
        functools.partial(_attn_body, nsub=nsub),
        grid=(batch, r // cb, length // qb),
        in_specs=[cur(0), prev(1), cur(1), prev(2), cur(2),
                  pl.BlockSpec((HEADS, BLK, 2 * BLK), lambda b, c, n: (0, 0, 0))],
        out_specs=[out_spec, out_spec],
        out_shape=[
            jax.ShapeDtypeStruct((batch, r, length, GROUP_W), BF16),
            jax.ShapeDtypeStruct((batch, r, length, GROUP_W), F32),
        ],
        compiler_params=_cparams(("parallel", "parallel", "arbitrary")),
        name=f"attention_g{g}",
    )(qkv, qkv, qkv, qkv, qkv, bias)


def _ssm_tables(lam_re, lam_im, log_dt, b_re, b_im, c_re, c_im):
    lam = lax.complex(lam_re.astype(F32), lam_im.astype(F32))
    dt = jnp.exp(log_dt.astype(F32))[:, None]
    lam_dt = lam * dt
    lam_bar = jnp.exp(lam_dt)
    b = lax.complex(b_re.astype(F32), b_im.astype(F32))
    b_bar = ((lam_bar - 1.0) / lam)[..., None] * b
    half = CHUNK // 2
    t = jnp.arange(CHUNK, dtype=F32)

    def power(k):
        return jnp.exp(lam_dt[:, None, :] * jnp.reshape(jnp.asarray(k, F32), (1, -1, 1)))

    p_fwd = power(t - half)
    p_bwd = 1.0 / p_fwd

    def in_pair(pw):
        return jnp.stack([jnp.concatenate([pw.real, pw.imag], axis=-1),
                          jnp.concatenate([-pw.imag, pw.real], axis=-1)], axis=1)

    def out_pair(pw):
        return jnp.stack([jnp.concatenate([pw.real, -pw.imag], axis=-1),
                          jnp.concatenate([-pw.imag, -pw.real], axis=-1)], axis=1)

    powers = jnp.stack([
        in_pair(p_bwd),
        in_pair(p_bwd * power(CHUNK - 1.0 - half)),
        out_pair(p_fwd),
        out_pair(p_fwd * power(half + 1.0)),
    ], axis=1)
    dup = lambda a: jnp.concatenate([a, a], axis=-1)
    b_cp = jnp.transpose(b_bar, (0, 2, 1))
    coefs = jnp.stack([dup(b_cp.real), dup(b_cp.imag), dup(c_re.astype(F32)), dup(c_im.astype(F32))],
                      axis=1)
    lc = power([float(CHUNK * 2 ** k) for k in range(SCAN_LEVELS)])
    l1 = jnp.concatenate([lc.real, lc.real], axis=-1)
    l2 = jnp.concatenate([-lc.imag, lc.imag], axis=-1)
    return powers, coefs, l1, l2


def _gelu_tanh(y):
    return y * jax.nn.sigmoid(1.5957691216057308 * (y + 0.044715 * (y * y * y)))


def _ssm_body(d_ref, u_ref, pw_ref, cf_ref, l1_ref, l2_ref, z_ref, mask_scr, z_scr, a_scr, w_scr, *,
              ncb):
    g = pl.program_id(0)
    nc = u_ref.shape[0]
    width = SSM_CH * CHUNK
    cb = mask_scr.shape[1]

    @pl.when(g == 0)
    def _():
        s_idx = lax.broadcasted_iota(jnp.int32, (width, cb), 0) & (CHUNK - 1)
        t_idx = lax.broadcasted_iota(jnp.int32, (width, cb), 1) & (CHUNK - 1)
        mask_scr[...] = jnp.where(t_idx >= s_idx, -1, 0).astype(jnp.int32)

    def expand(kind, c1, c2, ch):
        return (cf_ref[c1, ch:ch + 1, :] * pw_ref[kind, 0] + cf_ref[c2, ch:ch + 1, :] * pw_ref[kind, 1])

    for ch in range(SSM_CH):
        rows = slice(ch * CHUNK, (ch + 1) * CHUNK)
        a_scr[rows, :] = expand(0, 0, 1, ch).astype(BF16)
        w_scr[rows, :] = expand(1, 0, 1, ch).astype(BF16)

    u2 = u_ref.reshape(nc * SSM_CH, CHUNK)
    us =[u2[pl.ds(c, nc, stride=SSM_CH), :] for c in range(SSM_CH)]
    x = jnp.concatenate(us, axis=1).astype(BF16)

    acc = jnp.dot(x, w_scr[...], preferred_element_type=F32)
    rmod = lax.broadcasted_iota(jnp.int32, (nc, 2 * SSM_P), 0) & (ncb - 1)
    for k in range(ncb.bit_length() - 1):
        d = 1 << k
        sh = jnp.where(rmod >= d, pltpu.roll(acc, d, 0), 0.0)
        acc = acc + sh * l1_ref[k:k + 1, :] + pltpu.roll(sh, SSM_P, 1) * l2_ref[k:k + 1, :]
    x_in = jnp.where(rmod >= 1, pltpu.roll(acc, 1, 0), 0.0).astype(BF16)

    per = cb // CHUNK
    for k in range(width // cb):
        chans = range(k * per, (k + 1) * per)
        d_rhs = jnp.concatenate([expand(2, 2, 3, ch).T.astype(BF16) for ch in chans], axis=1)
        v_rhs = jnp.concatenate([expand(3, 2, 3, ch).T.astype(BF16) for ch in chans], axis=1)
        mk = jnp.dot(a_scr[...], d_rhs, preferred_element_type=F32)
        kept = lax.bitcast_convert_type(mk, jnp.int32) & mask_scr[...]
        m_k = lax.bitcast_convert_type(kept, F32).astype(BF16)
        y = (jnp.dot(x, m_k, preferred_element_type=F32)
             + jnp.dot(x_in, v_rhs, preferred_element_type=F32))
        for j, c in enumerate(chans):
            yc = y[:, j * CHUNK:(j + 1) * CHUNK] + d_ref[g * SSM_CH + c] * us[c]
            z_scr[pl.ds(c, nc, stride=SSM_CH), :] = _gelu_tanh(yc)
    z_ref[...] = z_scr[...].reshape(nc, SSM_CH, CHUNK)


def _ssm_scan(u3, d_skip, tables, nbatch):
    powers, coefs, l1, l2 = tables
    nc = u3.shape[0]
    ncb = nc // nbatch
    assert ncb & (ncb - 1) == 0 and ncb <= 2 ** SCAN_LEVELS
    width = SSM_CH * CHUNK
    grid_spec = pltpu.PrefetchScalarGridSpec(
        num_scalar_prefetch=1,
        grid=(SSM_G,),
        in_specs=[
            pl.BlockSpec((nc, SSM_CH, CHUNK), lambda g, d: (0, g, 0)),
            pl.BlockSpec((None,) + powers.shape[1:], lambda g, d: (g, 0, 0, 0, 0)),
            pl.BlockSpec((None,) + coefs.shape[1:], lambda g, d: (g, 0, 0, 0)),
            pl.BlockSpec((None, SCAN_LEVELS, 2 * SSM_P), lambda g, d: (g, 0, 0)),
            pl.BlockSpec((None, SCAN_LEVELS, 2 * SSM_P), lambda g, d: (g, 0, 0)),
        ],
        out_specs=pl.BlockSpec((nc, SSM_CH, CHUNK), lambda g, d: (0, g, 0)),
        scratch_shapes=[pltpu.VMEM((width, 4 * CHUNK), jnp.int32),
                        pltpu.VMEM((nc * SSM_CH, CHUNK), F32),
                        pltpu.VMEM((width, 2 * SSM_P), BF16), pltpu.VMEM((width, 2 * SSM_P), BF16)],
    )
    return pl.pallas_call(
        functools.partial(_ssm_body, ncb=ncb),
        grid_spec=grid_spec,
        out_shape=jax.ShapeDtypeStruct((nc, SSM_W, CHUNK), F32),
        compiler_params=_cparams(("arbitrary",)),
        name="ssm_scan",
    )(d_skip.astype(F32), u3, powers, coefs, l1, l2)


def _merge_body(o0, o1, o2, l0, l1, l2, zt_ref, ga_ref, gs_ref, x_ref,
                wglu_ref, bglu_ref, wab_ref, wsb_ref, wout_ref, out_ref, tok_scr):
    def token_major(ref, slot):
        r, rows, _ = ref.shape
        if r == 1:
            return ref[0].astype(F32)
        nl = GROUP_W // LANES
        scrs = [tok_scr.at[slot * nl + k] for k in range(nl)]
        for c in range(r):
            sub = ref[c].astype(F32)
            for k in range(nl):
                scrs[k][pl.ds(c, rows, stride=r), :] = sub[:, k * LANES:(k + 1) * LANES]
        return jnp.concatenate([s[...] for s in scrs], axis=1)

    a0, a1, a2 = token_major(l0, 0), token_major(l1, 0), token_major(l2, 1)
    v0, v1, v2 = token_major(o0, 0), token_major(o1, 2), token_major(o2, 3)
    mx = jnp.maximum(jnp.maximum(a0, a1), a2)
    e0, e1, e2 = jnp.exp2(a0 - mx), jnp.exp2(a1 - mx), jnp.exp2(a2 - mx)
    mix = (e0 * v0 + e1 * v1 + e2 * v2) / (e0 + e1 + e2)
    y_attn = jnp.dot(mix.astype(BF16), wab_ref[...], preferred_element_type=F32)

    z = jnp.concatenate([zt_ref[k].T for k in range(zt_ref.shape[0])], axis=0).astype(BF16)
    gl = jnp.dot(z, wglu_ref[...], preferred_element_type=F32) + bglu_ref[...]
    sg = z.astype(F32) * jax.nn.sigmoid(gl)
    y_ssm = jnp.dot(sg.astype(BF16), wsb_ref[...], preferred_element_type=F32)

    merged = (jax.nn.sigmoid(ga_ref[...].astype(F32)) * y_attn
              + jax.nn.sigmoid(gs_ref[...].astype(F32)) * y_ssm)
    out_ref[...] = x_ref[...] + jnp.dot(merged.astype(BF16), wout_ref[...], preferred_element_type=F32)


def _merge(os_, ls_, zt, gates, x2, wglu, bglu, wab, wsb, wout, tm=512):
    n = x2.shape[0]
    tiles_per_seq = os_[0].shape[2] // tm
    row = lambda i: (i, 0)
    const = lambda i: (0, 0)

    def group_spec(a):
        r = a.shape[1]
        return pl.BlockSpec((None, r, tm // r, GROUP_W),
                            lambda i: (i // tiles_per_seq, 0, i % tiles_per_seq, 0))

    in_specs = (
        [group_spec(a) for a in os_] + [group_spec(a) for a in ls_]
        + [
            pl.BlockSpec((tm // CHUNK, SSM_W, CHUNK), lambda i: (i, 0, 0)),
            pl.BlockSpec((tm, D_MODEL), lambda i: (i, 0)),
            pl.BlockSpec((tm, D_MODEL), lambda i: (i, 1)),
            pl.BlockSpec((tm, D_MODEL), row),
            pl.BlockSpec((SSM_W, SSM_W), const),
            pl.BlockSpec((1, SSM_W), const),
            pl.BlockSpec((GROUP_W, D_MODEL), const),
            pl.BlockSpec((SSM_W, D_MODEL), const),
            pl.BlockSpec((D_MODEL, D_MODEL), const),
        ]
    )
    return pl.pallas_call(
        _merge_body,
        grid=(n // tm,),
        in_specs=in_specs,
        out_specs=pl.BlockSpec((tm, D_MODEL), row),
        out_shape=jax.ShapeDtypeStruct((n, D_MODEL), F32),
        scratch_shapes=[pltpu.VMEM((4 * GROUP_W // LANES, tm, LANES), F32)],
        compiler_params=_cparams(("parallel",)),
        name="merge",
    )(*os_, *ls_, zt, gates, gates, x2, wglu, bglu, wab, wsb, wout)


def _ffn_body(x_ref, g_ref, wg_ref, wu_ref, wd_ref, *rest, tf):
    ncast = (len(rest) - 1) // 2
    o_ref = rest[ncast]
    for src, dst in zip(rest[:ncast], rest[ncast + 1:]):
        dst[...] = src[...].astype(BF16)
    h = _rms(x_ref[...], g_ref[...]).astype(BF16)
    for f in range(wg_ref.shape[1] // tf):
        cols = slice(f * tf, (f + 1) * tf)
        a = jnp.dot(h, wg_ref[:, cols], preferred_element_type=F32)
        b = jnp.dot(h, wu_ref[:, cols], preferred_element_type=F32)
        act = (a * jax.nn.sigmoid(a) * b).astype(BF16)
        part = jnp.dot(act, wd_ref[cols, :], preferred_element_type=F32)
        if f == 0:
            o_ref[...] = x_ref[...] + part
        else:
            o_ref[...] += part


def _dense_ffn(x2, g, wg, wu, wd, to_cast=(), tm=512, tf=256):
    n = x2.shape[0]
    dff = wg.shape[1]
    steps = n // tm
    assert dff % tf == 0 and all(a.shape[0] % (BF16_ROWS * steps) == 0 for a in to_cast)
    resident = lambda shape: pl.BlockSpec(shape, lambda i: (0, 0), pipeline_mode=pl.Buffered(1))
    slabs = [pl.BlockSpec((a.shape[0] // steps, a.shape[1]), lambda i: (i, 0)) for a in to_cast]
    out, *casts = pl.pallas_call(
        functools.partial(_ffn_body, tf=tf),
        grid=(steps,),
        in_specs=[
            pl.BlockSpec((tm, D_MODEL), lambda i: (i, 0)),
            pl.BlockSpec((1, D_MODEL), lambda i: (0, 0)),
            resident((D_MODEL, dff)),
            resident((D_MODEL, dff)),
            resident((dff, D_MODEL)),
        ] + slabs,
        out_specs=[pl.BlockSpec((tm, D_MODEL), lambda i: (i, 0))] + slabs,
        out_shape=[jax.ShapeDtypeStruct((n, D_MODEL), F32)]
        + [jax.ShapeDtypeStruct(a.shape, BF16) for a in to_cast],
        compiler_params=_cparams(("parallel",)),
        name="dense_ffn",
    )(x2, g, wg, wu, wd, *to_cast)
    return out, casts


def _route_top2(h, w, idx_ref, gate_ref, idxt_ref):
    h_hi, w_hi = h.astype(BF16), w.astype(BF16)
    h_lo = (h - h_hi.astype(F32)).astype(BF16)
    w_lo = (w - w_hi.astype(F32)).astype(BF16)
    both = jnp.dot(h_hi, jnp.concatenate([w_hi, w_lo], axis=1), preferred_element_type=F32)
    logits = both[:, :LANES] + (both[:, LANES:] + jnp.dot(h_lo, w_hi, preferred_element_type=F32))
    lane = lax.broadcasted_iota(jnp.int32, logits.shape, 1)
    lane_f = lane.astype(F32)
    logits = jnp.where(lane < N_EXPERTS, logits, -jnp.inf)
    v1 = jnp.max(logits, axis=-1, keepdims=True)
    i1 = jnp.min(jnp.where(logits == v1, lane_f, float(LANES)), axis=-1, keepdims=True)
    rest = jnp.where(lane_f == i1, -jnp.inf, logits)
    v2 = jnp.max(rest, axis=-1, keepdims=True)
    i2 = jnp.min(jnp.where(rest == v2, lane_f, float(LANES)), axis=-1, keepdims=True)
    e = jnp.exp(v2 - v1)
    g1 = 1.0 / (1.0 + e)
    g2 = e / (1.0 + e)
    idx_f = jnp.where(lane == 0, i1, jnp.where(lane == 1, i2, 0.0))
    idx_ref[...] = idx_f.astype(jnp.int32)
    gate_ref[...] = jnp.where(lane == 0, g1, jnp.where(lane == 1, g2, 0.0))
    idxt_ref[...] = idx_f.T[:idxt_ref.shape[0], :].astype(jnp.int32)


def _router_body(x_ref, g_ref, wr_ref, idx_ref, gate_ref, idxt_ref):
    _route_top2(_rms(x_ref[...], g_ref[...]), wr_ref[...], idx_ref, gate_ref, idxt_ref)


def _router(x2, g, wr_pad, tm=1024):
    n = x2.shape[0]
    return pl.pallas_call(
        _router_body,
        grid=(n // tm,),
        in_specs=[
            pl.BlockSpec((tm, D_MODEL), lambda i: (i, 0)),
            pl.BlockSpec((1, D_MODEL), lambda i: (0, 0)),
            pl.BlockSpec((D_MODEL, LANES), lambda i: (0, 0)),
        ],
        out_specs=[
            pl.BlockSpec((tm, LANES), lambda i: (i, 0)),
            pl.BlockSpec((tm, LANES), lambda i: (i, 0)),
            pl.BlockSpec((8, tm), lambda i: (0, i)),
        ],
        out_shape=[
            jax.ShapeDtypeStruct((n, LANES), jnp.int32),
            jax.ShapeDtypeStruct((n, LANES), F32),
            jax.ShapeDtypeStruct((8, n), jnp.int32),
        ],
        compiler_params=_cparams(("parallel",)),
        name="router",
    )(x2, g, wr_pad)


def _pack_bf16_pairs(hb):
    half = hb.shape[1] // 2
    lo = lax.bitcast_convert_type(hb[:, :half].astype(F32), jnp.uint32)
    hi = lax.bitcast_convert_type(hb[:, half:].astype(F32), jnp.uint32)
    return (hi & jnp.uint32(0xFFFF0000)) | (lo >> 16)


def _unpack_bf16_pairs(xu):
    lo = lax.bitcast_convert_type(xu << 16, F32).astype(BF16)
    hi = lax.bitcast_convert_type(xu & jnp.uint32(0xFFFF0000), F32).astype(BF16)
    return lo, hi


def _for_each_run_piece(i, start_ref, loff_ref, len_ref, fn, fill):
    for e in range(N_EXPERTS):
        j = i * N_EXPERTS + e
        length, boff, soff = len_ref[j], loff_ref[j], start_ref[j]
        done = 0
        for p in RUN_PIECES:
            cond = (length & p) != 0
            fn(cond, pl.multiple_of(boff + done, RUN_ALIGN), pl.multiple_of(soff + done, RUN_ALIGN), p)
            done = done + jnp.where(cond, p, 0)
    last = i * N_EXPERTS + N_EXPERTS - 1
    used = loff_ref[last] + len_ref[last]
    rest, done = CBUF_ROWS - used, 0
    for p in FILL_PIECES:
        cond = (rest & p) != 0
        fill(cond, pl.multiple_of(used + done, RUN_ALIGN), pl.multiple_of(done, RUN_ALIGN), p)
        done = done + jnp.where(cond, p, 0)


def _dispatch_body(start_ref, loff_ref, len_ref, tail_ref, x_ref, g_ref, idxt_ref, off_ref, cnt0_ref,
                   xs_ref, cbuf, tri_scr, zero_scr, sem, zsem):
    i = pl.program_id(0)
    tt = x_ref.shape[0]

    @pl.when(i == 0)
    def _():
        r = lax.broadcasted_iota(jnp.int32, (tt, tt), 0)
        c = lax.broadcasted_iota(jnp.int32, (tt, tt), 1)
        tri_scr[...] = jnp.where(r < c, 1.0, 0.0).astype(BF16)
        zero_scr[...] = jnp.zeros_like(zero_scr)

        def fill(e):
            row = pl.multiple_of(jnp.maximum(tail_ref[e], 0), MOE_BM)
            return pltpu.make_async_copy(zero_scr, xs_ref.at[pl.ds(row, MOE_BM)], zsem)

        for e in range(tail_ref.shape[0]):
            pl.when(tail_ref[e] >= 0)(lambda e=e: fill(e).start())
        for e in range(tail_ref.shape[0]):
            pl.when(tail_ref[e] >= 0)(lambda e=e: fill(e).wait())
        nspare = 2 * FILL_ROWS
        spare_fill = pltpu.make_async_copy(zero_scr.at[pl.ds(0, nspare)],
                                           xs_ref.at[pl.ds(xs_ref.shape[0] - nspare, nspare)], zsem)
        spare_fill.start()
        spare_fill.wait()

    hb = _rms(x_ref[...], g_ref[...]).astype(BF16)

    sub = lax.broadcasted_iota(jnp.int32, (N_EXPERTS, tt), 0)
    pos = []
    for k in range(2):
        oh = jnp.where(sub == idxt_ref[k:k + 1, :], 1.0, 0.0)
        rank = jnp.dot(oh.astype(BF16), tri_scr[...], preferred_element_type=F32)
        base = off_ref[:, :1] if k == 0 else off_ref[:, :1] + cnt0_ref[:, :1]
        pos.append(jnp.sum(oh * (base + rank), axis=0, keepdims=True))

    rows = lax.broadcasted_iota(jnp.int32, (cbuf.shape[1], tt), 0).astype(F32)
    perm = (jnp.where(rows == pos[0], 1.0, 0.0) + jnp.where(rows == pos[1], 1.0, 0.0)).astype(BF16)
    slot = i % 2
    cbuf[slot] = _pack_bf16_pairs(jnp.dot(perm, hb, preferred_element_type=F32).astype(BF16))

    spare = xs_ref.shape[0] - 2 * FILL_ROWS

    def run_piece(cond, brow, srow, p):
        cp = pltpu.make_async_copy(cbuf.at[slot, pl.ds(brow, p)], xs_ref.at[pl.ds(srow, p)], sem.at[slot])
        pl.when(cond)(cp.start)

    def fill_piece(cond, brow, frow, p):
        run_piece(cond, brow, pl.multiple_of(spare + slot * FILL_ROWS + frow, RUN_ALIGN), p)

    _for_each_run_piece(i, start_ref, loff_ref, len_ref, run_piece, fill_piece)

    def drain(buf):
        pltpu.make_async_copy(cbuf.at[buf], xs_ref.at[pl.ds(0, CBUF_ROWS)], sem.at[buf]).wait()

    pl.when(i > 0)(lambda: drain(1 - slot))
    pl.when(i == pl.num_programs(0) - 1)(lambda: drain(slot))


def _dispatch(plan, x2, g, idxt):
    n = x2.shape[0]
    tt = MOE_TT
    smem = lambda i, *_: (i, 0, 0)
    grid_spec = pltpu.PrefetchScalarGridSpec(
        num_scalar_prefetch=4,
        grid=(n // tt,),
        in_specs=[
            pl.BlockSpec((tt, D_MODEL), lambda i, *_: (i, 0)),
            pl.BlockSpec((1, D_MODEL), lambda i, *_: (0, 0)),
            pl.BlockSpec((8, tt), lambda i, *_: (0, i)),
            pl.BlockSpec((None, N_EXPERTS, LANES), smem),
            pl.BlockSpec((None, N_EXPERTS, LANES), smem),
        ],
        out_specs=pl.BlockSpec(memory_space=pl.ANY),
        scratch_shapes=[
            pltpu.VMEM((2, CBUF_ROWS, D_MODEL // 2), jnp.uint32),
            pltpu.VMEM((tt, tt), BF16),
            pltpu.VMEM((MOE_BM, D_MODEL // 2), jnp.uint32),
            pltpu.SemaphoreType.DMA((2,)),
            pltpu.SemaphoreType.DMA(()),
        ],
    )
    return pl.pallas_call(
        _dispatch_body,
        grid_spec=grid_spec,
        out_shape=jax.ShapeDtypeStruct((plan["n_slots"] + 2 * FILL_ROWS, D_MODEL // 2), jnp.uint32),
        compiler_params=_cparams(("arbitrary",)),
        name="moe_dispatch",
    )(plan["start"], plan["loff"], plan["len8"], plan["tail_rows"], x2, g, idxt,
      plan["off_lanes"], plan["cnt0_lanes"])


def _experts_body(be_ref, nu_ref, xs_ref, wg_ref, wu_ref, wd_ref, ys_ref, acc_scr, *, tf):
    del be_ref
    i = pl.program_id(0)

    @pl.when(i < nu_ref[0])
    def _():
        x = jnp.concatenate(_unpack_bf16_pairs(xs_ref[...]), axis=1)
        for f in range(wg_ref.shape[1] // tf):
            cols = slice(f * tf, (f + 1) * tf)
            a = jnp.dot(x, wg_ref[:, cols], preferred_element_type=F32)
            b = jnp.dot(x, wu_ref[:, cols], preferred_element_type=F32)
            act = (a * jax.nn.sigmoid(a) * b).astype(BF16)
            part = jnp.dot(act, wd_ref[cols, :], preferred_element_type=F32)
            if f == 0:
                acc_scr[...] = part
            else:
                acc_scr[...] += part
        ys_ref[...] = _pack_bf16_pairs(acc_scr[...].astype(BF16))

    @pl.when(i >= nu_ref[0])
    def _():
        ys_ref[...] = jnp.zeros_like(ys_ref)


def _experts(block_e, n_used, xs, wg, wu, wd, tf=256):
    n_blocks = xs.shape[0] // MOE_BM
    n_slots = n_blocks * MOE_BM
    dff = wg.shape[2]

    def blk(i, nu):
        return jnp.minimum(i, nu[0] - 1)

    def wspec(rows, cols):
        return pl.BlockSpec((None, rows, cols), lambda i, be, nu: (be[blk(i, nu)], 0, 0))

    grid_spec = pltpu.PrefetchScalarGridSpec(
        num_scalar_prefetch=2,
        grid=(n_blocks,),
        in_specs=[
            pl.BlockSpec((MOE_BM, D_MODEL // 2), lambda i, be, nu: (blk(i, nu), 0)),
            wspec(D_MODEL, dff),
            wspec(D_MODEL, dff),
            wspec(dff, D_MODEL),
        ],
        out_specs=pl.BlockSpec((MOE_BM, D_MODEL // 2), lambda i, be, nu: (i, 0)),
        scratch_shapes=[pltpu.VMEM((MOE_BM, D_MODEL), F32)],
    )
    return pl.pallas_call(
        functools.partial(_experts_body, tf=tf),
        grid_spec=grid_spec,
        out_shape=jax.ShapeDtypeStruct((n_slots, D_MODEL // 2), jnp.uint32),
        compiler_params=_cparams(("arbitrary",)),
        name="moe_experts",
    )(block_e, n_used, xs, wg, wu, wd)


def _combine_body(start_ref, loff_ref, len_ref, x_ref, idx_ref, gate_ref, meta_ref, g_ref, ys_ref, o_ref,
                  ybuf, tri_scr, sem):
    i = pl.program_id(0)
    tt = x_ref.shape[0]
    slot = i % 2

    def fetch(tile, buf):
        def run_piece(cond, brow, srow, p):
            cp = pltpu.make_async_copy(ys_ref.at[pl.ds(srow, p)], ybuf.at[buf, pl.ds(brow, p)], sem.at[buf])
            pl.when(cond)(cp.start)

        _for_each_run_piece(tile, start_ref, loff_ref, len_ref, run_piece, run_piece)

    @pl.when(i == 0)
    def _():
        r = lax.broadcasted_iota(jnp.int32, (tt, tt), 0)
        c = lax.broadcasted_iota(jnp.int32, (tt, tt), 1)
        tri_scr[...] = jnp.where(c < r, 1.0, 0.0).astype(BF16)
        fetch(i, slot)

    pl.when(i + 1 < pl.num_programs(0))(lambda: fetch(i + 1, 1 - slot))

    lane = lax.broadcasted_iota(jnp.int32, (tt, LANES), 1)
    idx = idx_ref[...]
    gt = gate_ref[...]
    cols = lax.broadcasted_iota(jnp.int32, (tt, ybuf.shape[1]), 1).astype(F32)
    sel = None
    for k in range(2):
        oh = jnp.where(lane == idx[:, k:k + 1], 1.0, 0.0)
        rank = jnp.dot(tri_scr[...], oh.astype(BF16), preferred_element_type=F32)
        base = meta_ref[0:1, :] if k == 0 else meta_ref[0:1, :] + meta_ref[1:2, :]
        pos = jnp.sum(oh * (base + rank), axis=1, keepdims=True)
        term = jnp.where(cols == pos, gt[:, k:k + 1], 0.0)
        sel = term if sel is None else sel + term
    sel = sel.astype(BF16)

    pltpu.make_async_copy(ys_ref.at[pl.ds(0, CBUF_ROWS)], ybuf.at[slot], sem.at[slot]).wait()

    lo, hi = _unpack_bf16_pairs(ybuf[slot])
    y = jnp.concatenate([jnp.dot(sel, lo, preferred_element_type=F32),
                         jnp.dot(sel, hi, preferred_element_type=F32)], axis=1)
    o_ref[...] = _rms(x_ref[...] + y, g_ref[...])


def _combine(plan, x2, idx, gates, g_final, ys):
    n = x2.shape[0]
    tt = MOE_TT
    grid_spec = pltpu.PrefetchScalarGridSpec(
        num_scalar_prefetch=3,
        grid=(n // tt,),
        in_specs=[
            pl.BlockSpec((tt, D_MODEL), lambda i, *_: (i, 0)),
            pl.BlockSpec((tt, LANES), lambda i, *_: (i, 0)),
            pl.BlockSpec((tt, LANES), lambda i, *_: (i, 0)),
            pl.BlockSpec((None, 8, LANES), lambda i, *_: (i, 0, 0)),
            pl.BlockSpec((1, D_MODEL), lambda i, *_: (0, 0)),
            pl.BlockSpec(memory_space=pl.ANY),
        ],
        out_specs=pl.BlockSpec((tt, D_MODEL), lambda i, *_: (i, 0)),
        scratch_shapes=[
            pltpu.VMEM((2, CBUF_ROWS, D_MODEL // 2), jnp.uint32),
            pltpu.VMEM((tt, tt), BF16),
            pltpu.SemaphoreType.DMA((2,)),
        ],
    )
    return pl.pallas_call(
        _combine_body,
        grid_spec=grid_spec,
        out_shape=jax.ShapeDtypeStruct((n, D_MODEL), F32),
        compiler_params=_cparams(("arbitrary",)),
        name="moe_combine",
    )(plan["start"], plan["loff"], plan["len8"], x2, idx, gates, plan["meta_rows"], g_final, ys)


def _route_plan(idxt, n):
    nt = n // MOE_TT
    e2 = idxt[:2].reshape(2, nt, MOE_TT)
    oh = (e2[..., None] == jnp.arange(N_EXPERTS, dtype=jnp.int32)).astype(jnp.int32)
    cnt = jnp.sum(oh, axis=2)
    cnt0 = cnt[0]
    len8 = (cnt[0] + cnt[1] + RUN_ALIGN - 1) // RUN_ALIGN * RUN_ALIGN
    loff = jnp.cumsum(len8, axis=1) - len8
    region = jnp.sum(len8, axis=0)
    padded = (region + MOE_BM - 1) // MOE_BM * MOE_BM
    pad_end = jnp.cumsum(padded)
    start = (pad_end - padded)[None, :] + jnp.cumsum(len8, axis=0) - len8
    n_blocks = (2 * n + nt * N_EXPERTS * (RUN_ALIGN - 1) + MOE_BM - 1) // MOE_BM + N_EXPERTS
    starts = jnp.arange(n_blocks, dtype=jnp.int32) * MOE_BM
    block_e = jnp.sum((starts[:, None] >= pad_end[None, :]).astype(jnp.int32), axis=1)
    block_e = jnp.minimum(block_e, N_EXPERTS - 1).astype(jnp.int32)
    n_used = (pad_end[-1] // MOE_BM).astype(jnp.int32).reshape(1)
    tails = jnp.where(padded > 0, pad_end - MOE_BM, -1)
    spare = pad_end[-1] + jnp.arange(n_blocks - (2 * n) // MOE_BM, dtype=pad_end.dtype) * MOE_BM
    spare = jnp.where(spare < n_blocks * MOE_BM, spare, -1)
    lanes = lambda a: jnp.broadcast_to(a.astype(F32)[:, :, None], (nt, N_EXPERTS, LANES))
    meta_rows = jnp.zeros((nt, 8, LANES), F32)
    meta_rows = meta_rows.at[:, 0, :N_EXPERTS].set(loff.astype(F32)).at[:, 1, :N_EXPERTS].set(cnt0.astype(F32))
    flat = lambda a: a.reshape(-1).astype(jnp.int32)
    return dict(start=flat(start), loff=flat(loff), len8=flat(len8), block_e=block_e, n_used=n_used,
                tail_rows=jnp.concatenate([tails, spare]).astype(jnp.int32),
                off_lanes=lanes(loff), cnt0_lanes=lanes(cnt0), meta_rows=meta_rows,
                n_slots=n_blocks * MOE_BM)


def _mixer_layer(x2, batch, seq, rel_bias, norm_g, w_in_bf, ssm, d_skip, w_glu, b_glu,
                 w_attn_br, w_ssm_br, w_out):
    *qkvs, gates, ut = _in_projection(x2, norm_g.reshape(1, D_MODEL), w_in_bf, batch, seq)

    os_, ls_ = [], []
    for g, (window, dilation) in enumerate(ATTN_GROUPS):
        bias = _band_bias(rel_bias[:, g * HEADS:(g + 1) * HEADS], window, dilation)
        o, l = _attention_group(qkvs[g], bias, g)
        os_.append(o)
        ls_.append(l)

    zt = _ssm_scan(ut, d_skip, _ssm_tables(*ssm), batch)

    return _merge(os_, ls_, zt, gates, x2, w_glu.astype(BF16), b_glu.reshape(1, SSM_W).astype(F32),
                  w_attn_br.astype(BF16), w_ssm_br.astype(BF16), w_out.astype(BF16))


def kernel(x, rel_bias, norm1_g, w_in, ssm_lam_re, ssm_lam_im, ssm_log_dt, ssm_b_re, ssm_b_im, ssm_c_re, ssm_c_im, ssm_d, w_glu, b_glu, w_attn_br, w_ssm_br, w_out, norm2_g, ffn_w_gate, ffn_w_up, ffn_w_down, moe_router, moe_w_gate, moe_w_up, moe_w_down, final_norm_g):
    batch, seq, d = x.shape
    assert d == D_MODEL and norm1_g.shape[0] == 2 and seq % (16 * BLK) == 0
    n = batch * seq
    x2 = x.reshape(n, d)

    def mixer(x2, l, w_in_bf):
        ssm = (ssm_lam_re[l], ssm_lam_im[l], ssm_log_dt[l], ssm_b_re[l], ssm_b_im[l],
               ssm_c_re[l], ssm_c_im[l])
        return _mixer_layer(x2, batch, seq, rel_bias, norm1_g[l], w_in_bf, ssm, ssm_d[l], w_glu[l],
                            b_glu[l], w_attn_br[l], w_ssm_br[l], w_out[l])

    x2 = mixer(x2, 0, w_in[0].astype(BF16))
    moe_w = (moe_w_gate[0], moe_w_up[0], moe_w_down[0])
    x2, (w_in1_bf, *moe_bf) = _dense_ffn(
        x2, norm2_g[0].reshape(1, d), ffn_w_gate[0].astype(BF16), ffn_w_up[0].astype(BF16),
        ffn_w_down[0].astype(BF16), to_cast=[w_in[1]] + [w.reshape(-1, w.shape[2]) for w in moe_w])
    moe_bf = [b.reshape(w.shape) for b, w in zip(moe_bf, moe_w)]

    x2 = mixer(x2, 1, w_in1_bf)
    g2 = norm2_g[1].reshape(1, d)
    wr_pad = jnp.zeros((d, LANES), F32).at[:, :N_EXPERTS].set(moe_router[0].astype(F32))
    idx, gates, idxt = _router(x2, g2, wr_pad)
    plan = _route_plan(idxt, n)
    xs = _dispatch(plan, x2, g2, idxt)
    ys = _experts(plan["block_e"], plan["n_used"], xs, *moe_bf)
    out = _combine(plan, x2, idx, gates, final_norm_g.reshape(1, d), ys)
    return out.reshape(batch, seq, d)
```
